```python
import math
import jax, jax.numpy as jnp
from jax import lax
import numpy as np

D_MODEL = 2048
BATCH = 4
SEQ = 2048
DEPTH = 2
DEC_BATCH = 128
DEC_SEQ = 4
PAST_LEN = 16384
PAGE_SIZE = 128

N_MIX_LAYERS = (DEPTH + 1) // 2
N_SSM_LAYERS = DEPTH // 2
GLA_HEADS = 4
GLA_DK = D_MODEL // 16
GLA_DV = D_MODEL // 8
GLA_RANK = 16
GLA_LOGIT_NORM = 16.0
RET_HEADS = 4
RET_DK = D_MODEL // 16
RET_DV = D_MODEL // 8
GLA_KEY = GLA_HEADS * GLA_DK
GLA_VAL = GLA_HEADS * GLA_DV
RET_KEY = RET_HEADS * RET_DK
RET_VAL = RET_HEADS * RET_DV
D_MIX = GLA_VAL + RET_VAL
IN_COLS = 2 * GLA_KEY + 2 * GLA_VAL + GLA_RANK + 2 * RET_KEY + 2 * RET_VAL
CHUNK = 16
ROPE_BASE = 10000.0
S5_GROUP = 16
S5_GROUPS = D_MODEL // S5_GROUP
S5_STATE = 64
D_FF = 4 * D_MODEL
EPS = 1e-6

kernel_name = "hybrid_gla_retnet_s5_adaln_step"


def rmsnorm(x, g):
    xf = x.astype(jnp.float32)
    y = xf * lax.rsqrt(jnp.mean(xf * xf, axis=-1, keepdims=True) + EPS) * g.astype(jnp.float32)
    return y.astype(x.dtype)


def split_heads(a, n):
    b, t, _ = a.shape
    return a.reshape(b, t, n, -1).transpose(0, 2, 1, 3)


def merge_heads(a):
    b, h, t, d = a.shape
    return a.transpose(0, 2, 1, 3).reshape(b, t, h * d)


def rotary(x, pos):
    half = x.shape[-1] // 2
    inv = ROPE_BASE ** (-jnp.arange(half, dtype=jnp.float32) / half)
    ang = pos.astype(jnp.float32)[:, None] * inv[None, :]
    cos, sin = jnp.cos(ang), jnp.sin(ang)
    x1, x2 = x[..., :half], x[..., half:]
    return jnp.concatenate([x1 * cos - x2 * sin, x1 * sin + x2 * cos], axis=-1)


def chunk_decay_linear_attention(q, k, v, g, s0):
    b_, h_, t_, dk = q.shape
    dv = v.shape[-1]
    c = math.gcd(t_, CHUNK)
    n = t_ // c

    def blocks(a):
        return jnp.moveaxis(a.astype(jnp.float32).reshape(b_, h_, n, c, a.shape[-1]), 2, 0)

    causal = jnp.tril(jnp.ones((c, c), dtype=bool))

    def step(s, blk):
        qb, kb, vb, gb = blk
        bcum = jnp.cumsum(gb, axis=2)
        b_last = bcum[:, :, -1:, :]
        q_in = qb * jnp.exp(bcum)
        k_in = kb * jnp.exp(-bcum)
        k_out = kb * jnp.exp(b_last - bcum)
        att = jnp.where(causal, jnp.einsum('bhid,bhjd->bhij', q_in, k_in), 0.0)
        o = jnp.einsum('bhij,bhjv->bhiv', att, vb) + jnp.einsum('bhid,bhdv->bhiv', q_in, s)
        s = jnp.exp(b_last[:, :, 0, :, None]) * s + jnp.einsum('bhjd,bhjv->bhdv', k_out, vb)
        return s, o

    s_fin, o = lax.scan(step, s0.astype(jnp.float32), (blocks(q), blocks(k), blocks(v), blocks(g)))
    o = jnp.moveaxis(o, 0, 2).reshape(b_, h_, t_, dv)
    return o, s_fin


def head_rmsnorm(o, g):
    return o * lax.rsqrt(jnp.mean(o * o, axis=-1, keepdims=True) + EPS) * g[None, :, None, :]


def head_groupnorm(o, g):
    mu = jnp.mean(o, axis=-1, keepdims=True)
    oc = o - mu
    return oc * lax.rsqrt(jnp.mean(oc * oc, axis=-1, keepdims=True) + EPS) * g[None, :, None, :]


def gla_retnet_mixer(h, pos, s_gla, s_ret, w_in, w_gk, b_gk, gla_norm, ret_norm, w_out):
    proj = h @ w_in
    sizes = (GLA_KEY, GLA_KEY, GLA_VAL, GLA_VAL, GLA_RANK, RET_KEY, RET_KEY, RET_VAL, RET_VAL)
    idx = [int(i) for i in np.cumsum(sizes)[:-1]]
    gq, gk, gv, gg, glr, rq, rk, rv, rg = jnp.split(proj, idx, axis=-1)
    logit = (glr @ w_gk + b_gk).astype(jnp.float32)
    glog = jax.nn.log_sigmoid(logit) / GLA_LOGIT_NORM
    o_gla, s_gla_new = chunk_decay_linear_attention(
        split_heads(gq, GLA_HEADS) * GLA_DK ** -0.5, split_heads(gk, GLA_HEADS),
        split_heads(gv, GLA_HEADS), split_heads(glog, GLA_HEADS), s_gla)
    o_gla = head_rmsnorm(o_gla, gla_norm) * jax.nn.silu(split_heads(gg, GLA_HEADS).astype(jnp.float32))
    gamma_log = jnp.log1p(-jnp.power(2.0, -5.0 - jnp.arange(RET_HEADS, dtype=jnp.float32)))
    qr = rotary(split_heads(rq, RET_HEADS).astype(jnp.float32), pos)
    kr = rotary(split_heads(rk, RET_HEADS).astype(jnp.float32), pos) * RET_DK ** -0.5
    g_ret = jnp.broadcast_to(gamma_log[None, :, None, None], kr.shape)
    o_ret, s_ret_new = chunk_decay_linear_attention(qr, kr, split_heads(rv, RET_HEADS), g_ret, s_ret)
    o_ret = head_groupnorm(o_ret, ret_norm) * jax.nn.silu(split_heads(rg, RET_HEADS).astype(jnp.float32))
    merged = jnp.concatenate([merge_heads(o_gla), merge_heads(o_ret)], axis=-1).astype(h.dtype)
    return merged @ w_out, s_gla_new, s_ret_new


def s5_mixer(h, s_re, s_im, lam_re, lam_im, log_dt, b_re, b_im, c_re, c_im, d_skip, w_glu_a, w_glu_b):
    bsz, t_, _ = h.shape
    u = h.astype(jnp.float32).reshape(bsz, t_, S5_GROUPS, S5_GROUP)
    dt = jnp.exp(log_dt.astype(jnp.float32))[:, None]
    lr, li = lam_re.astype(jnp.float32), lam_im.astype(jnp.float32)
    mag = jnp.exp(lr * dt)
    lb_re, lb_im = mag * jnp.cos(li * dt), mag * jnp.sin(li * dt)
    nr, ni = lb_re - 1.0, lb_im
    den = lr * lr + li * li
    f_re = (nr * lr + ni * li) / den
    f_im = (ni * lr - nr * li) / den
    br, bi = b_re.astype(jnp.float32), b_im.astype(jnp.float32)
    bb_re = f_re[..., None] * br - f_im[..., None] * bi
    bb_im = f_re[..., None] * bi + f_im[..., None] * br
    bu_re = jnp.einsum('btgc,gpc->btgp', u, bb_re)
    bu_im = jnp.einsum('btgc,gpc->btgp', u, bb_im)
    bu_re = bu_re.at[:, 0].add(lb_re * s_re - lb_im * s_im)
    bu_im = bu_im.at[:, 0].add(lb_re * s_im + lb_im * s_re)
    a_re = jnp.broadcast_to(lb_re, bu_re.shape)
    a_im = jnp.broadcast_to(lb_im, bu_im.shape)

    def combine(e1, e2):
        a1r, a1i, b1r, b1i = e1
        a2r, a2i, b2r, b2i = e2
        return (a2r * a1r - a2i * a1i, a2r * a1i + a2i * a1r,
                a2r * b1r - a2i * b1i + b2r, a2r * b1i + a2i * b1r + b2i)

    _, _, xr, xi = lax.associative_scan(combine, (a_re, a_im, bu_re, bu_im), axis=1)
    y = (jnp.einsum('btgp,gcp->btgc', xr, c_re.astype(jnp.float32))
         - jnp.einsum('btgp,gcp->btgc', xi, c_im.astype(jnp.float32)))
    y = y.reshape(bsz, t_, D_MODEL) + d_skip.astype(jnp.float32) * h.astype(jnp.float32)
    z = jax.nn.gelu(y).astype(h.dtype)
    out = (z @ w_glu_a) * jax.nn.sigmoid(z @ w_glu_b)
    return out, xr[:, -1], xi[:, -1]


def modulate(h, shift, scale):
    return h * (1.0 + scale[:, None, :]) + shift[:, None, :]


def trunk(x, c, pos, s_gla, s_ret, s5_re, s5_im,
          w_ada, b_ada, norm_pre, norm_post, w_in_mix, w_gla_gk, b_gla_gk, gla_head_norm,
          ret_head_norm, w_out_mix, s5_lam_re, s5_lam_im, s5_log_dt, s5_b_re, s5_b_im,
          s5_c_re, s5_c_im, s5_d, w_glu_a, w_glu_b, w_mlp_up, w_mlp_down):
    new_gla, new_ret, new_re, new_im = [], [], [], []
    sc = jax.nn.silu(c)
    for l in range(DEPTH):
        mod = jnp.einsum('bd,sde->sbe', sc, w_ada[l]) + b_ada[l][:, None, :]
        sh0, scl0, gt0 = jnp.split(mod[0], 3, axis=-1)
        sh1, scl1, gt1 = jnp.split(mod[1], 3, axis=-1)
        h = modulate(rmsnorm(x, norm_pre[l, 0]), sh0, scl0)
        i = l // 2
        if l % 2 == 0:
            y, ng, nr = gla_retnet_mixer(h, pos, s_gla[i], s_ret[i], w_in_mix[i], w_gla_gk[i], b_gla_gk[i],
                                         gla_head_norm[i], ret_head_norm[i], w_out_mix[i])
            new_gla.append(ng)
            new_ret.append(nr)
        else:
            y, nre, nim = s5_mixer(h, s5_re[i], s5_im[i], s5_lam_re[i], s5_lam_im[i], s5_log_dt[i],
                                   s5_b_re[i], s5_b_im[i], s5_c_re[i], s5_c_im[i], s5_d[i],
                                   w_glu_a[i], w_glu_b[i])
            new_re.append(nre)
            new_im.append(nim)
        x = x + gt0[:, None, :] * rmsnorm(y, norm_post[l, 0])
        h = modulate(rmsnorm(x, norm_pre[l, 1]), sh1, scl1)
        m = jnp.square(jax.nn.relu(h @ w_mlp_up[l])) @ w_mlp_down[l]
        x = x + gt1[:, None, :] * rmsnorm(m, norm_post[l, 1])
    return x, jnp.stack(new_gla), jnp.stack(new_ret), jnp.stack(new_re), jnp.stack(new_im)


def setup_inputs(seed: int = 0) -> dict:
    key = jax.random.key(seed)
    ks = jax.random.split(key, 32)
    f32 = jnp.float32

    def nrm(k, shape, scale):
        return jax.random.normal(k, shape, f32) * scale

    p_arange = jnp.arange(S5_STATE, dtype=f32)
    return {
        "x_prompt": nrm(ks[0], (BATCH, SEQ, D_MODEL), 1.0),
        "x_sample": nrm(ks[1], (DEC_BATCH, DEC_SEQ, D_MODEL), 1.0),
        "state_gla": nrm(ks[2], (N_MIX_LAYERS, DEC_BATCH, GLA_HEADS, GLA_DK, GLA_DV), 1.0),
        "state_ret": nrm(ks[3], (N_MIX_LAYERS, DEC_BATCH, RET_HEADS, RET_DK, RET_DV), 1.0),
        "state_s5_re": nrm(ks[4], (N_SSM_LAYERS, DEC_BATCH, S5_GROUPS, S5_STATE), 0.1),
        "state_s5_im": nrm(ks[5], (N_SSM_LAYERS, DEC_BATCH, S5_GROUPS, S5_STATE), 0.1),
        "c_prompt": nrm(ks[6], (BATCH, D_MODEL), 1.0),
        "c_sample": nrm(ks[7], (DEC_BATCH, D_MODEL), 1.0),
        "w_ada": nrm(ks[8], (DEPTH, 2, D_MODEL, 3 * D_MODEL), 0.2 * D_MODEL ** -0.5),
        "b_ada": nrm(ks[9], (DEPTH, 2, 3 * D_MODEL), 0.02),
        "norm_pre": 1.0 + nrm(ks[10], (DEPTH, 2, D_MODEL), 0.02),
        "norm_post": 1.0 + nrm(ks[11], (DEPTH, 2, D_MODEL), 0.02),
        "w_in_mix": nrm(ks[12], (N_MIX_LAYERS, D_MODEL, IN_COLS), D_MODEL ** -0.5),
        "w_gla_gk": nrm(ks[13], (N_MIX_LAYERS, GLA_RANK, GLA_KEY), GLA_RANK ** -0.5),
        "b_gla_gk": nrm(ks[14], (N_MIX_LAYERS, GLA_KEY), 0.1),
        "gla_head_norm": 1.0 + nrm(ks[15], (N_MIX_LAYERS, GLA_HEADS, GLA_DV), 0.02),
        "ret_head_norm": 1.0 + nrm(ks[16], (N_MIX_LAYERS, RET_HEADS, RET_DV), 0.02),
        "w_out_mix": nrm(ks[17], (N_MIX_LAYERS, D_MIX, D_MODEL), D_MIX ** -0.5),
        "s5_lam_re": -0.5 + nrm(ks[18], (N_SSM_LAYERS, S5_GROUPS, S5_STATE), 0.01),
        "s5_lam_im": jnp.pi * p_arange + nrm(ks[19], (N_SSM_LAYERS, S5_GROUPS, S5_STATE), 0.01),
        "s5_log_dt": jax.random.uniform(ks[20], (N_SSM_LAYERS, S5_GROUPS), f32,
                                        math.log(0.001), math.log(0.1)),
        "s5_b_re": nrm(ks[21], (N_SSM_LAYERS, S5_GROUPS, S5_STATE, S5_GROUP), (2 * S5_GROUP) ** -0.5),
        "s5_b_im": nrm(ks[22], (N_SSM_LAYERS, S5_GROUPS, S5_STATE, S5_GROUP), (2 * S5_GROUP) ** -0.5),
        "s5_c_re": nrm(ks[23], (N_SSM_LAYERS, S5_GROUPS, S5_GROUP, S5_STATE), S5_STATE ** -0.5),
        "s5_c_im": nrm(ks[24], (N_SSM_LAYERS, S5_GROUPS, S5_GROUP, S5_STATE), S5_STATE ** -0.5),
        "s5_d": nrm(ks[25], (N_SSM_LAYERS, D_MODEL), 1.0),
        "w_glu_a": nrm(ks[26], (N_SSM_LAYERS, D_MODEL, D_MODEL), D_MODEL ** -0.5),
        "w_glu_b": nrm(ks[27], (N_SSM_LAYERS, D_MODEL, D_MODEL), D_MODEL ** -0.5),
        "w_mlp_up": nrm(ks[28], (DEPTH, D_MODEL, D_FF), D_MODEL ** -0.5),
        "w_mlp_down": nrm(ks[29], (DEPTH, D_FF, D_MODEL), D_FF ** -0.5),
    }


def reference(x_prompt, x_sample, state_gla, state_ret, state_s5_re, state_s5_im, c_prompt, c_sample,
              w_ada, b_ada, norm_pre, norm_post, w_in_mix, w_gla_gk, b_gla_gk, gla_head_norm,
              ret_head_norm, w_out_mix, s5_lam_re, s5_lam_im, s5_log_dt, s5_b_re, s5_b_im,
              s5_c_re, s5_c_im, s5_d, w_glu_a, w_glu_b, w_mlp_up, w_mlp_down):
    weights = (w_ada, b_ada, norm_pre, norm_post, w_in_mix, w_gla_gk, b_gla_gk, gla_head_norm,
               ret_head_norm, w_out_mix, s5_lam_re, s5_lam_im, s5_log_dt, s5_b_re, s5_b_im,
               s5_c_re, s5_c_im, s5_d, w_glu_a, w_glu_b, w_mlp_up, w_mlp_down)
    bp, tp = x_prompt.shape[0], x_prompt.shape[1]
    z_gla = jnp.zeros((N_MIX_LAYERS, bp, GLA_HEADS, GLA_DK, GLA_DV), jnp.float32)
    z_ret = jnp.zeros((N_MIX_LAYERS, bp, RET_HEADS, RET_DK, RET_DV), jnp.float32)
    z_s5 = jnp.zeros((N_SSM_LAYERS, bp, S5_GROUPS, S5_STATE), jnp.float32)
    pos_p = jnp.arange(tp, dtype=jnp.float32)
    y_prompt, gla_p, ret_p, s5re_p, s5im_p = trunk(x_prompt, c_prompt, pos_p, z_gla, z_ret, z_s5, z_s5, *weights)
    pos_s = PAST_LEN + jnp.arange(x_sample.shape[1], dtype=jnp.float32)
    y_sample, gla_s, ret_s, s5re_s, s5im_s = trunk(x_sample, c_sample, pos_s, state_gla, state_ret,
                                                   state_s5_re, state_s5_im, *weights)
    return (y_prompt, y_sample, gla_p, gla_s, ret_p, ret_s, s5re_p, s5re_s, s5im_p, s5im_s)
```

```python
import functools
import math

import jax
import jax.numpy as jnp
import numpy as np
from jax import lax
from jax.experimental import pallas as pl
from jax.experimental.pallas import tpu as pltpu

F32 = jnp.float32
BF16 = jnp.bfloat16

EPS = 1e-6
LANES = 128
SUBLANES = 8
MIB = 1024 * 1024

GLA_HEADS = 4
RET_HEADS = 4
HEAD_DK = 128
HEAD_DV = 256
GLA_RANK = 16
GLA_LOGIT_NORM = 16.0
ROPE_BASE = 10000.0
PAST_LEN = 16384
S5_GROUP = 16
S5_STATE = 64
S5_GPB = 8
S5_CW = S5_GPB * S5_STATE
S5_UW = S5_GPB * S5_GROUP
ATT_CHUNK = 128
GLA_SUB = 16
T_PAD = 8
TM_DENSE = 512
TM_S5 = 256


def _cparams(sem, vmem_mib):
    return pltpu.CompilerParams(dimension_semantics=sem, vmem_limit_bytes=vmem_mib * MIB)


def _dot(a, b):
    return jnp.dot(a, b, preferred_element_type=F32)


def _dot_nt(a, b):
    return lax.dot_general(a, b, (((1,), (1,)), ((), ())), preferred_element_type=F32)


def _rms(x, g):
    return x * lax.rsqrt(jnp.mean(x * x, axis=-1, keepdims=True) + EPS) * g


def _rows_affine(y, a, b=None):
    tm, d = y.shape
    r = a.shape[0]
    if r == 1 or r == tm:
        out = y * a
        return out if b is None else out + b
    y3 = y.reshape(tm // r, r, d)
    out = y3 * a[None]
    if b is not None:
        out = out + b[None]
    return out.reshape(tm, d)


def _norm_mod(x, g, mod_ref, d):
    return _rows_affine(_rms(x, g), 1.0 + mod_ref[:, d:2 * d], mod_ref[:, 0:d])


def _gated_residual(x, y, g, mod_ref, d):
    return x + _rows_affine(_rms(y, g), mod_ref[:, 2 * d:3 * d])


def _mod_spec(r, tm, width, ngrid):
    if ngrid == 2:
        if r == 1:
            return pl.BlockSpec((None, 1, width), lambda g, i: (g, 0, 0))
        return pl.BlockSpec((None, r, width), lambda g, i: (g, 0, 0))
    if r == 1:
        return pl.BlockSpec((None, 1, width), lambda g, i, j: (g, 0, 0))
    return pl.BlockSpec((None, r, width), lambda g, i, j: (g, 0, 0))


def _adaln_kernel(c_ref, w_ref, b_ref, o_ref):
    c = c_ref[...]
    sc = (c * jax.nn.sigmoid(c)).astype(BF16)
    o_ref[...] = _dot(sc, w_ref[...].astype(BF16)) + b_ref[...]


def _adaln(c_all, w_ada, b_ada, tn=512):
    ls, d, n = w_ada.shape
    rows = c_all.shape[0]
    return pl.pallas_call(
        _adaln_kernel,
        grid=(ls, n // tn),
        in_specs=[
            pl.BlockSpec((rows, d), lambda l, j: (0, 0)),
            pl.BlockSpec((None, d, tn), lambda l, j: (l, 0, j)),
            pl.BlockSpec((None, 1, tn), lambda l, j: (l, 0, j)),
        ],
        out_specs=pl.BlockSpec((None, rows, tn), lambda l, j: (l, 0, j)),
        out_shape=jax.ShapeDtypeStruct((ls, rows, n), F32),
        compiler_params=_cparams(("parallel", "parallel"), 40),
        name="adaln",
    )(c_all, w_ada, b_ada)


def _log_sigmoid(x):
    return jnp.minimum(x, 0.0) - jnp.log1p(jnp.exp(-jnp.abs(x)))


def _inproj_kernel(x_ref, mod_ref, g_ref, w_ref, wlr_ref, wgk_ref, bgk_ref,
                   proj_ref, glog_ref, h_scr, *, d):
    @pl.when(pl.program_id(2) == 0)
    def _():
        hb = _norm_mod(x_ref[...], g_ref[...], mod_ref, d).astype(BF16)
        h_scr[...] = hb
        glr = _dot(hb, wlr_ref[...])
        logit = _dot(glr.astype(BF16), wgk_ref[...]) + bgk_ref[...]
        glog_ref[...] = _log_sigmoid(logit) * (1.0 / GLA_LOGIT_NORM)

    proj_ref[...] = _dot(h_scr[...], w_ref[...])


def _inproj(x3, mod3, g_pre, w_main, w_lr, w_gk, b_gk, tm, tn=1024):
    gn, t, d = x3.shape
    n = w_main.shape[1]
    r = mod3.shape[1]
    gkey = w_gk.shape[1]
    return pl.pallas_call(
        functools.partial(_inproj_kernel, d=d),
        grid=(gn, t // tm, n // tn),
        in_specs=[
            pl.BlockSpec((None, tm, d), lambda g, i, j: (g, i, 0)),
            _mod_spec(r, tm, 3 * d, 3),
            pl.BlockSpec((1, d), lambda g, i, j: (0, 0)),
            pl.BlockSpec((d, tn), lambda g, i, j: (0, j)),
            pl.BlockSpec((d, LANES), lambda g, i, j: (0, 0)),
            pl.BlockSpec((LANES, gkey), lambda g, i, j: (0, 0)),
            pl.BlockSpec((1, gkey), lambda g, i, j: (0, 0)),
        ],
        out_specs=[
            pl.BlockSpec((None, tm, tn), lambda g, i, j: (g, i, j)),
            pl.BlockSpec((None, tm, gkey), lambda g, i, j: (g, i, 0)),
        ],
        out_shape=[
            jax.ShapeDtypeStruct((gn, t, n), F32),
            jax.ShapeDtypeStruct((gn, t, gkey), F32),
        ],
        scratch_shapes=[pltpu.VMEM((tm, d), BF16)],
        compiler_params=_cparams(("parallel", "parallel", "arbitrary"), 48),
        name="inproj",
    )(x3, mod3, g_pre, w_main, w_lr, w_gk, b_gk)


def _cumsum_rows(g):
    c = g.shape[0]
    row = lax.broadcasted_iota(jnp.int32, g.shape, 0)
    s = 1
    while s < c:
        g = g + jnp.where(row >= s, pltpu.roll(g, s, 0), 0.0)
        s *= 2
    return g


def _pad_rows(a, rows):
    if a.shape[0] == rows:
        return a
    return jnp.concatenate([a, jnp.zeros((rows - a.shape[0], a.shape[1]), a.dtype)], axis=0)


def _col_bcast(row, width):
    sq = jnp.transpose(jnp.broadcast_to(row, (LANES, LANES)))
    return jnp.concatenate([sq] * (width // LANES), axis=1)


def _gla_core(q, k, v, g, s, sub):
    cq = q.shape[0]
    ck = max(cq, LANES)
    b = _cumsum_rows(g)
    be = b - g
    bk = _pad_rows(b, ck)
    kp = _pad_rows(k, ck)
    vp = _pad_rows(v, ck).astype(BF16)
    rowj = lax.broadcasted_iota(jnp.int32, (ck, 1), 0)
    att_rows = []
    for blk in range(cq // sub):
        lo, hi = blk * sub, (blk + 1) * sub
        base = be[lo:lo + 1, :]
        qs = q[lo:hi] * jnp.exp(b[lo:hi] - base)
        ks = jnp.where(rowj < hi, kp * jnp.exp(base - bk), 0.0)
        att_rows.append(_dot_nt(qs.astype(BF16), ks.astype(BF16)))
    att = att_rows[0] if len(att_rows) == 1 else jnp.concatenate(att_rows, axis=0)
    ri = lax.broadcasted_iota(jnp.int32, (cq, ck), 0)
    cj = lax.broadcasted_iota(jnp.int32, (cq, ck), 1)
    att = jnp.where(ri >= cj, att, 0.0)
    o = _dot(att.astype(BF16), vp) + _dot((q * jnp.exp(b)).astype(BF16), s.astype(BF16))
    b_last = b[cq - 1:cq, :]
    k_out = kp * jnp.exp(b_last - bk)
    s_new = s * _col_bcast(jnp.exp(b_last), s.shape[1]) + _dot(jnp.transpose(k_out).astype(BF16), vp)
    return o, s_new


def _ret_core(q, k, v, s, lg, dmat, valid):
    cq = q.shape[0]
    ck = max(cq, LANES)
    kp = _pad_rows(k, ck)
    vp = _pad_rows(v, ck).astype(BF16)
    ti = lax.broadcasted_iota(jnp.int32, (cq, 1), 0).astype(F32)
    tj = lax.broadcasted_iota(jnp.int32, (ck, 1), 0).astype(F32)
    att = _dot_nt(q.astype(BF16), kp.astype(BF16)) * dmat
    q_in = q * jnp.exp((ti + 1.0) * lg)
    o = _dot(att.astype(BF16), vp) + _dot(q_in.astype(BF16), s.astype(BF16))
    k_out = kp * jnp.exp((float(valid - 1) - tj) * lg)
    s_new = s * jnp.exp(float(valid) * lg) + _dot(jnp.transpose(k_out).astype(BF16), vp)
    return o, s_new


def _decay_matrix(cq, ck, lg):
    ri = lax.broadcasted_iota(jnp.int32, (cq, ck), 0)
    cj = lax.broadcasted_iota(jnp.int32, (cq, ck), 1)
    diff = (ri - cj).astype(F32)
    return jnp.where(ri >= cj, jnp.exp(diff * lg), 0.0)


def _rope(x, cosf, sinf):
    return x * cosf + pltpu.roll(x, x.shape[1] // 2, 1) * sinf


def _silu(x):
    return x * jax.nn.sigmoid(x)


def _gla_finish(o, gate, gn):
    o = o * lax.rsqrt(jnp.mean(o * o, axis=-1, keepdims=True) + EPS) * gn
    return (o * _silu(gate)).astype(BF16)


def _ret_finish(o, gate, gn):
    oc = o - jnp.mean(o, axis=-1, keepdims=True)
    oc = oc * lax.rsqrt(jnp.mean(oc * oc, axis=-1, keepdims=True) + EPS) * gn
    return (oc * _silu(gate)).astype(BF16)


def _gla_prompt_kernel(q_ref, k_ref, v_ref, gg_ref, gl_ref, gn_ref, s0_ref, o_ref, s_ref):
    @pl.when(pl.program_id(2) == 0)
    def _():
        s_ref[...] = s0_ref[...]

    q = q_ref[...] * (HEAD_DK ** -0.5)
    o, s_new = _gla_core(q, k_ref[...], v_ref[...], gl_ref[...], s_ref[...], GLA_SUB)
    s_ref[...] = s_new
    o_ref[...] = _gla_finish(o, gg_ref[...], gn_ref[...])


def _ret_prompt_kernel(q_ref, k_ref, v_ref, rg_ref, cos_ref, sin_ref, lg_ref, gn_ref, s0_ref,
                       o_ref, s_ref, d_scr):
    lg = lg_ref[:, 0:1]

    @pl.when(pl.program_id(2) == 0)
    def _():
        s_ref[...] = s0_ref[...]
        d_scr[...] = _decay_matrix(d_scr.shape[0], d_scr.shape[1], lg)

    cosf, sinf = cos_ref[...], sin_ref[...]
    q = _rope(q_ref[...], cosf, sinf)
    k = _rope(k_ref[...], cosf, sinf) * (HEAD_DK ** -0.5)
    o, s_new = _ret_core(q, k, v_ref[...], s_ref[...], lg, d_scr[...], q.shape[0])
    s_ref[...] = s_new
    o_ref[...] = _ret_finish(o, rg_ref[...], gn_ref[...])


def _attn_prompt(proj, glog, gla_norm, ret_norm, cosf, sinf, ret_lg, s0_gla, s0_ret):
    bsz, t, _ = proj.shape
    c = ATT_CHUNK
    grid = (bsz, GLA_HEADS, t // c)
    nk = GLA_HEADS
    kspec = lambda off: pl.BlockSpec((None, c, HEAD_DK), lambda b, h, i, off=off: (b, i, off + h))
    vspec = lambda off: pl.BlockSpec((None, c, HEAD_DV), lambda b, h, i, off=off: (b, i, off + h))
    hspec = pl.BlockSpec((None, 1, HEAD_DV), lambda b, h, i: (h, 0, 0))
    sspec = pl.BlockSpec((None, None, HEAD_DK, HEAD_DV), lambda b, h, i: (b, h, 0, 0))
    ospec = pl.BlockSpec((None, c, HEAD_DV), lambda b, h, i: (b, i, h))
    out_shape = [
        jax.ShapeDtypeStruct((bsz, t, GLA_HEADS * HEAD_DV), BF16),
        jax.ShapeDtypeStruct((bsz, GLA_HEADS, HEAD_DK, HEAD_DV), F32),
    ]
    params = _cparams(("parallel", "parallel", "arbitrary"), 32)
    mg, s_gla = pl.pallas_call(
        _gla_prompt_kernel,
        grid=grid,
        in_specs=[kspec(0), kspec(nk), vspec(nk), vspec(2 * nk),
                  pl.BlockSpec((None, c, HEAD_DK), lambda b, h, i: (b, i, h)),
                  hspec, sspec],
        out_specs=[ospec, sspec],
        out_shape=out_shape,
        compiler_params=params,
        name="gla_prompt",
    )(proj, proj, proj, proj, glog, gla_norm, s0_gla)
    tspec = pl.BlockSpec((c, HEAD_DK), lambda b, h, i: (i, 0))
    mr, s_ret = pl.pallas_call(
        _ret_prompt_kernel,
        grid=grid,
        in_specs=[kspec(6 * nk), kspec(7 * nk), vspec(4 * nk), vspec(5 * nk),
                  tspec, tspec,
                  pl.BlockSpec((None, 1, LANES), lambda b, h, i: (h, 0, 0)),
                  hspec, sspec],
        out_specs=[ospec, sspec],
        out_shape=out_shape,
        scratch_shapes=[pltpu.VMEM((c, c), F32)],
        compiler_params=params,
        name="ret_prompt",
    )(proj, proj, proj, proj, cosf, sinf, ret_lg, ret_norm, s0_ret)
    return mg, mr, s_gla, s_ret


def _gla_sample_kernel(q_ref, k_ref, v_ref, gg_ref, gl_ref, gn_ref, s0_ref, o_ref, s_ref):
    for bi in range(q_ref.shape[0]):
        q = q_ref[bi] * (HEAD_DK ** -0.5)
        o, s_new = _gla_core(q, k_ref[bi], v_ref[bi], gl_ref[bi], s0_ref[bi], q.shape[0])
        s_ref[bi] = s_new
        o_ref[bi] = _gla_finish(o, gg_ref[bi], gn_ref[...])


def _ret_sample_kernel(q_ref, k_ref, v_ref, rg_ref, cos_ref, sin_ref, lg_ref, gn_ref, s0_ref,
                       o_ref, s_ref, *, valid):
    lg = lg_ref[:, 0:1]
    cq = q_ref.shape[1]
    dmat = _decay_matrix(cq, max(cq, LANES), lg)
    cosf, sinf = cos_ref[...], sin_ref[...]
    for bi in range(q_ref.shape[0]):
        q = _rope(q_ref[bi], cosf, sinf)
        k = _rope(k_ref[bi], cosf, sinf) * (HEAD_DK ** -0.5)
        o, s_new = _ret_core(q, k, v_ref[bi], s0_ref[bi], lg, dmat, valid)
        s_ref[bi] = s_new
        o_ref[bi] = _ret_finish(o, rg_ref[bi], gn_ref[...])


def _attn_sample(proj, glog, gla_norm, ret_norm, cosf, sinf, ret_lg, s0_gla, s0_ret, valid, bb=8):
    bsz, tp, _ = proj.shape
    grid = (bsz // bb, GLA_HEADS)
    nk = GLA_HEADS
    kspec = lambda off: pl.BlockSpec((bb, tp, HEAD_DK), lambda i, h, off=off: (i, 0, off + h))
    vspec = lambda off: pl.BlockSpec((bb, tp, HEAD_DV), lambda i, h, off=off: (i, 0, off + h))
    hspec = pl.BlockSpec((None, 1, HEAD_DV), lambda i, h: (h, 0, 0))
    sspec = pl.BlockSpec((bb, None, HEAD_DK, HEAD_DV), lambda i, h: (i, h, 0, 0))
    ospec = pl.BlockSpec((bb, tp, HEAD_DV), lambda i, h: (i, 0, h))
    out_shape = [
        jax.ShapeDtypeStruct((bsz, tp, GLA_HEADS * HEAD_DV), BF16),
        jax.ShapeDtypeStruct((bsz, GLA_HEADS, HEAD_DK, HEAD_DV), F32),
    ]
    params = _cparams(("parallel", "parallel"), 32)
    mg, s_gla = pl.pallas_call(
        _gla_sample_kernel,
        grid=grid,
        in_specs=[kspec(0), kspec(nk), vspec(nk), vspec(2 * nk),
                  pl.BlockSpec((bb, tp, HEAD_DK), lambda i, h: (i, 0, h)),
                  hspec, sspec],
        out_specs=[ospec, sspec],
        out_shape=out_shape,
        compiler_params=params,
        name="gla_sample",
    )(proj, proj, proj, proj, glog, gla_norm, s0_gla)
    tspec = pl.BlockSpec((tp, HEAD_DK), lambda i, h: (0, 0))
    mr, s_ret = pl.pallas_call(
        functools.partial(_ret_sample_kernel, valid=valid),
        grid=grid,
        in_specs=[kspec(6 * nk), kspec(7 * nk), vspec(4 * nk), vspec(5 * nk),
                  tspec, tspec,
                  pl.BlockSpec((None, 1, LANES), lambda i, h: (h, 0, 0)),
                  hspec, sspec],
        out_specs=[ospec, sspec],
        out_shape=out_shape,
        compiler_params=params,
        name="ret_sample",
    )(proj, proj, proj, proj, cosf, sinf, ret_lg, ret_norm, s0_ret)
    return mg, mr, s_gla, s_ret


def _outproj_kernel(x_ref, mod_ref, g_ref, mg_ref, mr_ref, wo_ref, o_ref, *, d):
    half = mg_ref.shape[1]
    y = _dot(mg_ref[...], wo_ref[0:half, :]) + _dot(mr_ref[...], wo_ref[half:2 * half, :])
    o_ref[...] = _gated_residual(x_ref[...], y, g_ref[...], mod_ref, d)


def _outproj(x3, mod3, g_post, mg, mr, w_out, tm):
    gn, t, d = x3.shape
    r = mod3.shape[1]
    half = mg.shape[2]
    return pl.pallas_call(
        functools.partial(_outproj_kernel, d=d),
        grid=(gn, t // tm),
        in_specs=[
            pl.BlockSpec((None, tm, d), lambda g, i: (g, i, 0)),
            _mod_spec(r, tm, 3 * d, 2),
            pl.BlockSpec((1, d), lambda g, i: (0, 0)),
            pl.BlockSpec((None, tm, half), lambda g, i: (g, i, 0)),
            pl.BlockSpec((None, tm, half), lambda g, i: (g, i, 0)),
            pl.BlockSpec((2 * half, d), lambda g, i: (0, 0)),
        ],
        out_specs=pl.BlockSpec((None, tm, d), lambda g, i: (g, i, 0)),
        out_shape=jax.ShapeDtypeStruct((gn, t, d), F32),
        compiler_params=_cparams(("parallel", "parallel"), 52),
        name="outproj",
    )(x3, mod3, g_post, mg, mr, w_out)


def _mlp_kernel(x_ref, mod_ref, gpre_ref, gpost_ref, wup_ref, wdn_ref, o_ref, h_scr, acc_scr, *, d):
    j = pl.program_id(2)

    @pl.when(j == 0)
    def _():
        h_scr[...] = _norm_mod(x_ref[...], gpre_ref[...], mod_ref, d).astype(BF16)
        acc_scr[...] = jnp.zeros_like(acc_scr)

    u = jnp.maximum(_dot(h_scr[...], wup_ref[...]), 0.0)
    acc_scr[...] += _dot((u * u).astype(BF16), wdn_ref[...])

    @pl.when(j == pl.num_programs(2) - 1)
    def _():
        o_ref[...] = _gated_residual(x_ref[...], acc_scr[...], gpost_ref[...], mod_ref, d)


def _mlp(x3, mod3, g_pre, g_post, w_up, w_down, tm, tf=512):
    gn, t, d = x3.shape
    r = mod3.shape[1]
    f = w_up.shape[1]
    return pl.pallas_call(
        functools.partial(_mlp_kernel, d=d),
        grid=(gn, t // tm, f // tf),
        in_specs=[
            pl.BlockSpec((None, tm, d), lambda g, i, j: (g, i, 0)),
            _mod_spec(r, tm, 3 * d, 3),
            pl.BlockSpec((1, d), lambda g, i, j: (0, 0)),
            pl.BlockSpec((1, d), lambda g, i, j: (0, 0)),
            pl.BlockSpec((d, tf), lambda g, i, j: (0, j)),
            pl.BlockSpec((tf, d), lambda g, i, j: (j, 0)),
        ],
        out_specs=pl.BlockSpec((None, tm, d), lambda g, i, j: (g, i, 0)),
        out_shape=jax.ShapeDtypeStruct((gn, t, d), F32),
        scratch_shapes=[pltpu.VMEM((tm, d), BF16), pltpu.VMEM((tm, d), F32)],
        compiler_params=_cparams(("parallel", "parallel", "arbitrary"), 52),
        name="mlp",
    )(x3, mod3, g_pre, g_post, w_up, w_down)


def _s5_disc_kernel(lr_ref, li_ref, ldt_ref, br_ref, bi_ref, pwr_ref, pwi_ref, bbr_ref, bbi_ref):
    lr, li = lr_ref[...], li_ref[...]
    dt = jnp.exp(ldt_ref[...])
    mag = jnp.exp(lr * dt)
    lb_re, lb_im = mag * jnp.cos(li * dt), mag * jnp.sin(li * dt)
    nr, ni = lb_re - 1.0, lb_im
    den = lr * lr + li * li
    f_re = (nr * lr + ni * li) / den
    f_im = (ni * lr - nr * li) / den
    br, bi = br_ref[...], bi_ref[...]
    bbr_ref[...] = f_re * br - f_im * bi
    bbi_ref[...] = f_re * bi + f_im * br
    pr, pi = lb_re, lb_im
    for n in range(pwr_ref.shape[0]):
        pwr_ref[n] = pr
        pwi_ref[n] = pi
        pr, pi = pr * lb_re - pi * lb_im, pr * lb_im + pi * lb_re


def _s5_discretize(lam_re, lam_im, log_dt, bt_re, bt_im):
    g, _, p = lam_re.shape
    c = bt_re.shape[1]
    return pl.pallas_call(
        _s5_disc_kernel,
        out_shape=[
            jax.ShapeDtypeStruct((SUBLANES, g, 1, p), F32),
            jax.ShapeDtypeStruct((SUBLANES, g, 1, p), F32),
            jax.ShapeDtypeStruct((g, c, p), F32),
            jax.ShapeDtypeStruct((g, c, p), F32),
        ],
        name="s5_discretize",
    )(lam_re, lam_im, log_dt, bt_re, bt_im)


def _gelu_tanh(x):
    c0 = math.sqrt(2.0 / math.pi)
    return x * (0.5 * (1.0 + jnp.tanh(c0 * (x + 0.044715 * (x * x * x)))))


def _cmul_add(ar, ai, xr, xi, yr, yi):
    return yr + ar * xr - ai * xi, yi + ar * xi + ai * xr


def _s5_kernel(x_ref, mod_ref, gpre_ref, bblk_ref, cblk_ref, tbl_ref, dskip_ref, s0r_ref, s0i_ref,
               z_ref, sr_ref, si_ref, h_scr, xr_scr, xi_scr, cr_scr, ci_scr, *, d, seg):
    i = pl.program_id(1)
    cb = pl.program_id(2)
    tm = xr_scr.shape[0]
    ncb = h_scr.shape[0]

    @pl.when(cb == 0)
    def _():
        h = _norm_mod(x_ref[...], gpre_ref[...], mod_ref, d)
        for c in range(ncb):
            h_scr[c] = h[:, c * S5_UW:(c + 1) * S5_UW]

    @pl.when(i == 0)
    def _():
        cr_scr[cb] = s0r_ref[...]
        ci_scr[cb] = s0i_ref[...]

    u = h_scr[cb]
    bu = _dot(u.astype(BF16), bblk_ref[...])
    xr_scr[...] = bu[:, 0:S5_CW]
    xi_scr[...] = bu[:, S5_CW:2 * S5_CW]
    car_r, car_i = cr_scr[cb], ci_scr[cb]

    if seg == 1:
        pw_r, pw_i = tbl_ref[0], tbl_ref[1]
        lv = [(tbl_ref[2 + 2 * n], tbl_ref[3 + 2 * n]) for n in range(3)]

        def group(k, carry):
            c_r, c_i = carry
            r0 = pl.multiple_of(k * SUBLANES, SUBLANES)
            a_r = xr_scr[pl.ds(r0, SUBLANES), :]
            a_i = xi_scr[pl.ds(r0, SUBLANES), :]
            for n in range(3):
                sh_r = pltpu.roll(a_r, 1 << n, 0)
                sh_i = pltpu.roll(a_i, 1 << n, 0)
                a_r, a_i = _cmul_add(lv[n][0], lv[n][1], sh_r, sh_i, a_r, a_i)
            a_r, a_i = _cmul_add(pw_r, pw_i, c_r, c_i, a_r, a_i)
            xr_scr[pl.ds(r0, SUBLANES), :] = a_r
            xi_scr[pl.ds(r0, SUBLANES), :] = a_i
            return a_r[SUBLANES - 1:SUBLANES, :], a_i[SUBLANES - 1:SUBLANES, :]

        car_r, car_i = lax.fori_loop(0, tm // SUBLANES, group, (car_r, car_i))
    else:
        l_r, l_i = tbl_ref[0, 0:1, :], tbl_ref[1, 0:1, :]
        for t in range(tm // seg):
            car_r, car_i = _cmul_add(l_r, l_i, car_r, car_i,
                                     xr_scr[t * seg:(t + 1) * seg, :], xi_scr[t * seg:(t + 1) * seg, :])
            xr_scr[t * seg:(t + 1) * seg, :] = car_r
            xi_scr[t * seg:(t + 1) * seg, :] = car_i

    cr_scr[cb] = car_r
    ci_scr[cb] = car_i
    sr_ref[...] = car_r
    si_ref[...] = car_i
    xs = jnp.concatenate([xr_scr[...].astype(BF16), xi_scr[...].astype(BF16)], axis=1)
    y = _dot(xs, cblk_ref[...]) + dskip_ref[...] * u
    z_ref[...] = _gelu_tanh(y).astype(BF16)


def _s5(x3, mod3, g_pre, bblk, cblk, tbl, dskip, s0_re, s0_im, tm):
    gn, t, d = x3.shape
    r = mod3.shape[1]
    seg = s0_re.shape[1]
    ncb = bblk.shape[0]
    nst = s0_re.shape[2]
    nt = t // tm
    sspec = pl.BlockSpec((None, seg, S5_CW), lambda g, i, c: (g, 0, c))
    ospec = pl.BlockSpec((None, None, seg, S5_CW), lambda g, i, c: (g, i, 0, c))
    z3, s_re, s_im = pl.pallas_call(
        functools.partial(_s5_kernel, d=d, seg=seg),
        grid=(gn, t // tm, ncb),
        in_specs=[
            pl.BlockSpec((None, tm, d), lambda g, i, c: (g, i, 0)),
            _mod_spec(r, tm, 3 * d, 3),
            pl.BlockSpec((1, d), lambda g, i, c: (0, 0)),
            pl.BlockSpec((None, S5_UW, 2 * S5_CW), lambda g, i, c: (c, 0, 0)),
            pl.BlockSpec((None, 2 * S5_CW, S5_UW), lambda g, i, c: (c, 0, 0)),
            pl.BlockSpec((SUBLANES, SUBLANES, S5_CW), lambda g, i, c: (0, 0, c)),
            pl.BlockSpec((None, 1, S5_UW), lambda g, i, c: (c, 0, 0)),
            sspec, sspec,
        ],
        out_specs=[
            pl.BlockSpec((None, tm, S5_UW), lambda g, i, c: (g, i, c)),
            ospec, ospec,
        ],
        out_shape=[
            jax.ShapeDtypeStruct((gn, t, d), BF16),
            jax.ShapeDtypeStruct((gn, nt, seg, nst), F32),
            jax.ShapeDtypeStruct((gn, nt, seg, nst), F32),
        ],
        scratch_shapes=[
            pltpu.VMEM((ncb, tm, S5_UW), F32),
            pltpu.VMEM((tm, S5_CW), F32),
            pltpu.VMEM((tm, S5_CW), F32),
            pltpu.VMEM((ncb, seg, S5_CW), F32),
            pltpu.VMEM((ncb, seg, S5_CW), F32),
        ],
        compiler_params=_cparams(("parallel", "arbitrary", "arbitrary"), 40),
        name="s5_mixer",
    )(x3, mod3, g_pre, bblk, cblk, tbl, dskip, s0_re, s0_im)
    return z3, s_re[:, nt - 1], s_im[:, nt - 1]


def _glu_kernel(x_ref, mod_ref, g_ref, z_ref, wa_ref, wb_ref, o_ref, *, d):
    z = z_ref[...]
    y = _dot(z, wa_ref[...]) * jax.nn.sigmoid(_dot(z, wb_ref[...]))
    o_ref[...] = _gated_residual(x_ref[...], y, g_ref[...], mod_ref, d)


def _glu(x3, mod3, g_post, z3, w_a, w_b, tm):
    gn, t, d = x3.shape
    r = mod3.shape[1]
    wspec = pl.BlockSpec((d, d), lambda g, i: (0, 0), pipeline_mode=pl.Buffered(1))
    return pl.pallas_call(
        functools.partial(_glu_kernel, d=d),
        grid=(gn, t // tm),
        in_specs=[
            pl.BlockSpec((None, tm, d), lambda g, i: (g, i, 0)),
            _mod_spec(r, tm, 3 * d, 2),
            pl.BlockSpec((1, d), lambda g, i: (0, 0)),
            pl.BlockSpec((None, tm, d), lambda g, i: (g, i, 0)),
            wspec, wspec,
        ],
        out_specs=pl.BlockSpec((None, tm, d), lambda g, i: (g, i, 0)),
        out_shape=jax.ShapeDtypeStruct((gn, t, d), F32),
        compiler_params=_cparams(("parallel", "parallel"), 52),
        name="glu",
    )(x3, mod3, g_post, z3, w_a, w_b)


def _rope_tables(pos):
    half = HEAD_DK // 2
    inv = ROPE_BASE ** (-jnp.arange(half, dtype=F32) / half)
    ang = pos.astype(F32)[:, None] * inv[None, :]
    cos, sin = jnp.cos(ang), jnp.sin(ang)
    return jnp.concatenate([cos, cos], axis=-1), jnp.concatenate([-sin, sin], axis=-1)


def _block_diag(w):
    ncb, gpb, a, b = w.shape
    eye = jnp.eye(gpb, dtype=w.dtype)
    return jnp.einsum("ngab,gh->ngahb", w, eye).reshape(ncb, gpb * a, gpb * b)


def _s5_tables(pw_re, pw_im):
    n = pw_re.shape[0]
    flat_r = pw_re.reshape(n, -1)
    flat_i = pw_im.reshape(n, -1)
    row = jnp.arange(SUBLANES)[:, None]
    tabs = [flat_r, flat_i]
    for s in (1, 2, 4):
        mask = row >= s
        tabs.append(jnp.where(mask, flat_r[s - 1][None, :], 0.0))
        tabs.append(jnp.where(mask, flat_i[s - 1][None, :], 0.0))
    return jnp.stack(tabs)


def kernel(x_prompt, x_sample, state_gla, state_ret, state_s5_re, state_s5_im, c_prompt, c_sample,
           w_ada, b_ada, norm_pre, norm_post, w_in_mix, w_gla_gk, b_gla_gk, gla_head_norm,
           ret_head_norm, w_out_mix, s5_lam_re, s5_lam_im, s5_log_dt, s5_b_re, s5_b_im,
           s5_c_re, s5_c_im, s5_d, w_glu_a, w_glu_b, w_mlp_up, w_mlp_down):
    bp, tp, d = x_prompt.shape
    bs, ts, _ = x_sample.shape
    depth = w_ada.shape[0]
    gkey = GLA_HEADS * HEAD_DK
    gval = GLA_HEADS * HEAD_DV

    nrow = -(-(bs + bp) // SUBLANES) * SUBLANES
    c_all = jnp.concatenate([c_sample, c_prompt, jnp.zeros((nrow - bs - bp, d), F32)], axis=0)
    mod_all = _adaln(c_all, w_ada.reshape(depth * 2, d, 3 * d), b_ada.reshape(depth * 2, 1, 3 * d))
    mod_s = [mod_all[k, 0:bs][None] for k in range(depth * 2)]
    mod_p = [mod_all[k, bs:bs + bp][:, None, :] for k in range(depth * 2)]

    w_in = w_in_mix[0]
    sec = 2 * gkey + 2 * gval
    w_main = jnp.concatenate([w_in[:, :sec], w_in[:, sec + GLA_RANK:]], axis=1).astype(BF16)
    w_lr = jnp.pad(w_in[:, sec:sec + GLA_RANK], ((0, 0), (0, LANES - GLA_RANK))).astype(BF16)
    w_gk = jnp.pad(w_gla_gk[0], ((0, LANES - GLA_RANK), (0, 0))).astype(BF16)
    b_gk = b_gla_gk[0][None, :]
    w_out = w_out_mix[0].astype(BF16)
    w_up = w_mlp_up.astype(BF16)
    w_dn = w_mlp_down.astype(BF16)
    w_ga = w_glu_a[0].astype(BF16)
    w_gb = w_glu_b[0].astype(BF16)
    gla_norm = gla_head_norm[0][:, None, :]
    ret_norm = ret_head_norm[0][:, None, :]
    gamma_log = jnp.log1p(-jnp.power(2.0, -5.0 - jnp.arange(RET_HEADS, dtype=F32)))
    ret_lg = jnp.broadcast_to(gamma_log[:, None, None], (RET_HEADS, 1, LANES))

    ng = s5_lam_re.shape[1]
    ncb = ng // S5_GPB
    pw_re, pw_im, bbt_re, bbt_im = _s5_discretize(
        s5_lam_re[0][:, None, :], s5_lam_im[0][:, None, :], s5_log_dt[0][:, None, None],
        jnp.swapaxes(s5_b_re[0], 1, 2), jnp.swapaxes(s5_b_im[0], 1, 2))
    bblk = jnp.concatenate([
        _block_diag(bbt_re.reshape(ncb, S5_GPB, S5_GROUP, S5_STATE)),
        _block_diag(bbt_im.reshape(ncb, S5_GPB, S5_GROUP, S5_STATE))], axis=2).astype(BF16)
    ct_re = jnp.swapaxes(s5_c_re[0], 1, 2).reshape(ncb, S5_GPB, S5_STATE, S5_GROUP)
    ct_im = jnp.swapaxes(s5_c_im[0], 1, 2).reshape(ncb, S5_GPB, S5_STATE, S5_GROUP)
    cblk = jnp.concatenate([_block_diag(ct_re), _block_diag(-ct_im)], axis=1).astype(BF16)
    tbl = _s5_tables(pw_re[:, :, 0, :], pw_im[:, :, 0, :])
    dskip = s5_d[0].reshape(ncb, 1, S5_UW)

    def trunk(x3, mods, tm, attn, s5_state):
        proj, glog = _inproj(x3, mods[0], norm_pre[0, 0][None], w_main, w_lr, w_gk, b_gk, tm)
        mg, mr, s_gla, s_ret = attn(proj, glog)
        x3 = _outproj(x3, mods[0], norm_post[0, 0][None], mg, mr, w_out, tm)
        x3 = _mlp(x3, mods[1], norm_pre[0, 1][None], norm_post[0, 1][None], w_up[0], w_dn[0], tm)
        s0_re, s0_im, tm5 = s5_state
        z3, s_re, s_im = _s5(x3, mods[2], norm_pre[1, 0][None], bblk, cblk, tbl, dskip, s0_re, s0_im, tm5)
        x3 = _glu(x3, mods[2], norm_post[1, 0][None], z3, w_ga, w_gb, tm)
        x3 = _mlp(x3, mods[3], norm_pre[1, 1][None], norm_post[1, 1][None], w_up[1], w_dn[1], tm)
        return x3, s_gla, s_ret, s_re, s_im

    cos_p, sin_p = _rope_tables(jnp.arange(tp, dtype=F32))
    zeros_att = jnp.zeros((bp, GLA_HEADS, HEAD_DK, HEAD_DV), F32)
    zeros_s5 = jnp.zeros((bp, 1, ng * S5_STATE), F32)

    def attn_p(proj, glog):
        return _attn_prompt(proj, glog, gla_norm, ret_norm, cos_p, sin_p, ret_lg, zeros_att, zeros_att)

    y_p, gla_p, ret_p, re_p, im_p = trunk(x_prompt, mod_p, min(TM_DENSE, tp), attn_p,
                                          (zeros_s5, zeros_s5, min(TM_S5, tp)))

    pos_s = PAST_LEN + jnp.arange(T_PAD, dtype=F32)
    cos_s, sin_s = _rope_tables(pos_s)

    def attn_s(proj, glog):
        def to_bm(a):
            a = jnp.swapaxes(a.reshape(ts, bs, a.shape[-1]), 0, 1)
            return jnp.pad(a, ((0, 0), (0, T_PAD - ts), (0, 0)))
        mg, mr, s_gla, s_ret = _attn_sample(to_bm(proj), to_bm(glog), gla_norm, ret_norm, cos_s, sin_s,
                                            ret_lg, state_gla[0], state_ret[0], ts)
        to_tm = lambda a: jnp.swapaxes(a[:, :ts], 0, 1).reshape(1, ts * bs, a.shape[-1])
        return to_tm(mg), to_tm(mr), s_gla, s_ret

    xs3 = jnp.swapaxes(x_sample, 0, 1).reshape(1, ts * bs, d)
    s5_s = (state_s5_re[0].reshape(1, bs, -1), state_s5_im[0].reshape(1, bs, -1), ts * bs)
    y_s, gla_s, ret_s, re_s, im_s = trunk(xs3, mod_s, ts * bs, attn_s, s5_s)
    y_s = jnp.swapaxes(y_s.reshape(ts, bs, d), 0, 1)

    st = lambda a, b_: a.reshape(1, b_, ng, S5_STATE)
    return (y_p, y_s, gla_p[None], gla_s[None], ret_p[None], ret_s[None],
            st(re_p, bp), st(re_s, bs), st(im_p, bp), st(im_s, bs))
```

```python
import functools
import math

import jax
import jax.numpy as jnp
import numpy as np
from jax import lax
from jax.experimental import pallas as pl
from jax.experimental.pallas import tpu as pltpu

F32 = jnp.float32
BF16 = jnp.bfloat16

EPS = 1e-6
LANES = 128
SUBLANES = 8
MIB = 1024 * 1024

GLA_HEADS = 4
RET_HEADS = 4
HEAD_DK = 128
HEAD_DV = 256
GLA_RANK = 16
GLA_LOGIT_NORM = 16.0
ROPE_BASE = 10000.0
PAST_LEN = 16384
S5_GROUP = 16
S5_STATE = 64
S5_GPB = 16
S5_UW = S5_GPB * S5_GROUP
ATT_CHUNK = 128
GLA_SUB = 16
T_PAD = 8
TM_DENSE = 512
TM_S5 = 256
TN_INPROJ = 1024
TF_MLP = 1024


def _cparams(sem, vmem_mib):
    return pltpu.CompilerParams(dimension_semantics=sem, vmem_limit_bytes=vmem_mib * MIB)


def _dot(a, b):
    return jnp.dot(a, b, preferred_element_type=F32)


def _dot_nt(a, b):
    return lax.dot_general(a, b, (((1,), (1,)), ((), ())), preferred_element_type=F32)


def _rms(x, g):
    return x * lax.rsqrt(jnp.mean(x * x, axis=-1, keepdims=True) + EPS) * g


def _rows_affine(y, a, b=None):
    tm, d = y.shape
    r = a.shape[0]
    if r == 1 or r == tm:
        out = y * a
        return out if b is None else out + b
    y3 = y.reshape(tm // r, r, d)
    out = y3 * a[None]
    if b is not None:
        out = out + b[None]
    return out.reshape(tm, d)


def _norm_mod(x, g, mod_ref, d):
    return _rows_affine(_rms(x, g), 1.0 + mod_ref[:, d:2 * d], mod_ref[:, 0:d])


def _gated_residual(x, y, g, mod_ref, d):
    return x + _rows_affine(_rms(y, g), mod_ref[:, 2 * d:3 * d])


def _mod_spec(r, tm, width, ngrid):
    if ngrid == 2:
        if r == 1:
            return pl.BlockSpec((None, 1, width), lambda g, i: (g, 0, 0))
        return pl.BlockSpec((None, r, width), lambda g, i: (g, 0, 0))
    if r == 1:
        return pl.BlockSpec((None, 1, width), lambda g, i, j: (g, 0, 0))
    return pl.BlockSpec((None, r, width), lambda g, i, j: (g, 0, 0))


def _adaln_kernel(c_ref, w_ref, b_ref, o_ref):
    c = c_ref[...]
    sc = (c * jax.nn.sigmoid(c)).astype(BF16)
    o_ref[...] = _dot(sc, w_ref[...].astype(BF16)) + b_ref[...]


def _adaln(c_all, w_ada, b_ada, tn=512):
    ls, d, n = w_ada.shape
    rows = c_all.shape[0]
    return pl.pallas_call(
        _adaln_kernel,
        grid=(ls, n // tn),
        in_specs=[
            pl.BlockSpec((rows, d), lambda l, j: (0, 0)),
            pl.BlockSpec((None, d, tn), lambda l, j: (l, 0, j)),
            pl.BlockSpec((None, 1, tn), lambda l, j: (l, 0, j)),
        ],
        out_specs=pl.BlockSpec((None, rows, tn), lambda l, j: (l, 0, j)),
        out_shape=jax.ShapeDtypeStruct((ls, rows, n), F32),
        compiler_params=_cparams(("parallel", "parallel"), 40),
        name="adaln",
    )(c_all, w_ada, b_ada)


def _log_sigmoid(x):
    return jnp.minimum(x, 0.0) - jnp.log1p(jnp.exp(-jnp.abs(x)))


def _inproj_kernel(x_ref, mod_ref, g_ref, w_ref, wlr_ref, wgk_ref, bgk_ref,
                   proj_ref, glog_ref, h_scr, *, d):
    @pl.when(pl.program_id(2) == 0)
    def _():
        hb = _norm_mod(x_ref[...], g_ref[...], mod_ref, d).astype(BF16)
        h_scr[...] = hb
        glr = _dot(hb, wlr_ref[...])
        logit = _dot(glr.astype(BF16), wgk_ref[...]) + bgk_ref[...]
        glog_ref[...] = _log_sigmoid(logit) * (1.0 / GLA_LOGIT_NORM)

    proj_ref[...] = _dot(h_scr[...], w_ref[pl.program_id(2)])


def _inproj(x3, mod3, g_pre, w_main, w_lr, w_gk, b_gk, tm):
    gn, t, d = x3.shape
    nj, _, tn = w_main.shape
    n = nj * tn
    r = mod3.shape[1]
    gkey = w_gk.shape[1]
    return pl.pallas_call(
        functools.partial(_inproj_kernel, d=d),
        grid=(gn, t // tm, n // tn),
        in_specs=[
            pl.BlockSpec((None, tm, d), lambda g, i, j: (g, i, 0)),
            _mod_spec(r, tm, 3 * d, 3),
            pl.BlockSpec((1, d), lambda g, i, j: (0, 0)),
            pl.BlockSpec((nj, d, tn), lambda g, i, j: (0, 0, 0), pipeline_mode=pl.Buffered(1)),
            pl.BlockSpec((d, LANES), lambda g, i, j: (0, 0)),
            pl.BlockSpec((LANES, gkey), lambda g, i, j: (0, 0)),
            pl.BlockSpec((1, gkey), lambda g, i, j: (0, 0)),
        ],
        out_specs=[
            pl.BlockSpec((None, tm, tn), lambda g, i, j: (g, i, j)),
            pl.BlockSpec((None, tm, gkey), lambda g, i, j: (g, i, 0)),
        ],
        out_shape=[
            jax.ShapeDtypeStruct((gn, t, n), F32),
            jax.ShapeDtypeStruct((gn, t, gkey), F32),
        ],
        scratch_shapes=[pltpu.VMEM((tm, d), BF16)],
        compiler_params=_cparams(("parallel", "parallel", "arbitrary"), 56),
        name="inproj",
    )(x3, mod3, g_pre, w_main, w_lr, w_gk, b_gk)


def _cumsum_rows(g):
    c = g.shape[0]
    row = lax.broadcasted_iota(jnp.int32, g.shape, 0)
    s = 1
    while s < c:
        g = g + jnp.where(row >= s, pltpu.roll(g, s, 0), 0.0)
        s *= 2
    return g


def _pad_rows(a, rows):
    if a.shape[0] == rows:
        return a
    return jnp.concatenate([a, jnp.zeros((rows - a.shape[0], a.shape[1]), a.dtype)], axis=0)


def _col_bcast(row, width):
    sq = jnp.transpose(jnp.broadcast_to(row, (LANES, LANES)))
    return jnp.concatenate([sq] * (width // LANES), axis=1)


def _gla_core(q, k, v, g, s, sub):
    cq = q.shape[0]
    ck = max(cq, LANES)
    b = _cumsum_rows(g)
    be = b - g
    bk = _pad_rows(b, ck)
    kp = _pad_rows(k, ck)
    vp = _pad_rows(v, ck).astype(BF16)
    rowj = lax.broadcasted_iota(jnp.int32, (ck, 1), 0)
    att_rows = []
    for blk in range(cq // sub):
        lo, hi = blk * sub, (blk + 1) * sub
        base = be[lo:lo + 1, :]
        qs = q[lo:hi] * jnp.exp(b[lo:hi] - base)
        ks = jnp.where(rowj < hi, kp * jnp.exp(base - bk), 0.0)
        att_rows.append(_dot_nt(qs.astype(BF16), ks.astype(BF16)))
    att = att_rows[0] if len(att_rows) == 1 else jnp.concatenate(att_rows, axis=0)
    ri = lax.broadcasted_iota(jnp.int32, (cq, ck), 0)
    cj = lax.broadcasted_iota(jnp.int32, (cq, ck), 1)
    att = jnp.where(ri >= cj, att, 0.0)
    o = _dot(att.astype(BF16), vp) + _dot((q * jnp.exp(b)).astype(BF16), s.astype(BF16))
    b_last = b[cq - 1:cq, :]
    k_out = kp * jnp.exp(b_last - bk)
    s_new = s * _col_bcast(jnp.exp(b_last), s.shape[1]) + _dot(jnp.transpose(k_out).astype(BF16), vp)
    return o, s_new


def _ret_core(q, k, v, s, lg, dmat, valid):
    cq = q.shape[0]
    ck = max(cq, LANES)
    kp = _pad_rows(k, ck)
    vp = _pad_rows(v, ck).astype(BF16)
    ti = lax.broadcasted_iota(jnp.int32, (cq, 1), 0).astype(F32)
    tj = lax.broadcasted_iota(jnp.int32, (ck, 1), 0).astype(F32)
    att = _dot_nt(q.astype(BF16), kp.astype(BF16)) * dmat
    q_in = q * jnp.exp((ti + 1.0) * lg)
    o = _dot(att.astype(BF16), vp) + _dot(q_in.astype(BF16), s.astype(BF16))
    k_out = kp * jnp.exp((float(valid - 1) - tj) * lg)
    s_new = s * jnp.exp(float(valid) * lg) + _dot(jnp.transpose(k_out).astype(BF16), vp)
    return o, s_new


def _decay_matrix(cq, ck, lg):
    ri = lax.broadcasted_iota(jnp.int32, (cq, ck), 0)
    cj = lax.broadcasted_iota(jnp.int32, (cq, ck), 1)
    diff = (ri - cj).astype(F32)
    return jnp.where(ri >= cj, jnp.exp(diff * lg), 0.0)


def _rope(x, cosf, sinf):
    return x * cosf + pltpu.roll(x, x.shape[1] // 2, 1) * sinf


def _silu(x):
    return x * jax.nn.sigmoid(x)


def _gla_finish(o, gate, gn):
    o = o * lax.rsqrt(jnp.mean(o * o, axis=-1, keepdims=True) + EPS) * gn
    return (o * _silu(gate)).astype(BF16)


def _ret_finish(o, gate, gn):
    oc = o - jnp.mean(o, axis=-1, keepdims=True)
    oc = oc * lax.rsqrt(jnp.mean(oc * oc, axis=-1, keepdims=True) + EPS) * gn
    return (oc * _silu(gate)).astype(BF16)


def _gla_prompt_kernel(q_ref, k_ref, v_ref, gg_ref, gl_ref, gn_ref, s0_ref, o_ref, s_ref):
    @pl.when(pl.program_id(2) == 0)
    def _():
        s_ref[...] = s0_ref[...]

    q = q_ref[...] * (HEAD_DK ** -0.5)
    o, s_new = _gla_core(q, k_ref[...], v_ref[...], gl_ref[...], s_ref[...], GLA_SUB)
    s_ref[...] = s_new
    o_ref[...] = _gla_finish(o, gg_ref[...], gn_ref[...])


def _ret_prompt_kernel(q_ref, k_ref, v_ref, rg_ref, cos_ref, sin_ref, lg_ref, gn_ref, s0_ref,
                       o_ref, s_ref, d_scr):
    lg = lg_ref[:, 0:1]

    @pl.when(pl.program_id(2) == 0)
    def _():
        s_ref[...] = s0_ref[...]
        d_scr[...] = _decay_matrix(d_scr.shape[0], d_scr.shape[1], lg)

    cosf, sinf = cos_ref[...], sin_ref[...]
    q = _rope(q_ref[...], cosf, sinf)
    k = _rope(k_ref[...], cosf, sinf) * (HEAD_DK ** -0.5)
    o, s_new = _ret_core(q, k, v_ref[...], s_ref[...], lg, d_scr[...], q.shape[0])
    s_ref[...] = s_new
    o_ref[...] = _ret_finish(o, rg_ref[...], gn_ref[...])


def _attn_prompt(proj, glog, gla_norm, ret_norm, cosf, sinf, ret_lg, s0_gla, s0_ret):
    bsz, t, _ = proj.shape
    c = ATT_CHUNK
    grid = (bsz, GLA_HEADS, t // c)
    nk = GLA_HEADS
    kspec = lambda off: pl.BlockSpec((None, c, HEAD_DK), lambda b, h, i, off=off: (b, i, off + h))
    vspec = lambda off: pl.BlockSpec((None, c, HEAD_DV), lambda b, h, i, off=off: (b, i, off + h))
    hspec = pl.BlockSpec((None, 1, HEAD_DV), lambda b, h, i: (h, 0, 0))
    sspec = pl.BlockSpec((None, None, HEAD_DK, HEAD_DV), lambda b, h, i: (b, h, 0, 0))
    ospec = pl.BlockSpec((None, c, HEAD_DV), lambda b, h, i: (b, i, h))
    out_shape = [
        jax.ShapeDtypeStruct((bsz, t, GLA_HEADS * HEAD_DV), BF16),
        jax.ShapeDtypeStruct((bsz, GLA_HEADS, HEAD_DK, HEAD_DV), F32),
    ]
    params = _cparams(("parallel", "parallel", "arbitrary"), 32)
    mg, s_gla = pl.pallas_call(
        _gla_prompt_kernel,
        grid=grid,
        in_specs=[kspec(0), kspec(nk), vspec(nk), vspec(2 * nk),
                  pl.BlockSpec((None, c, HEAD_DK), lambda b, h, i: (b, i, h)),
                  hspec, sspec],
        out_specs=[ospec, sspec],
        out_shape=out_shape,
        compiler_params=params,
        name="gla_prompt",
    )(proj, proj, proj, proj, glog, gla_norm, s0_gla)
    tspec = pl.BlockSpec((c, HEAD_DK), lambda b, h, i: (i, 0))
    mr, s_ret = pl.pallas_call(
        _ret_prompt_kernel,
        grid=grid,
        in_specs=[kspec(6 * nk), kspec(7 * nk), vspec(4 * nk), vspec(5 * nk),
                  tspec, tspec,
                  pl.BlockSpec((None, 1, LANES), lambda b, h, i: (h, 0, 0)),
                  hspec, sspec],
        out_specs=[ospec, sspec],
        out_shape=out_shape,
        scratch_shapes=[pltpu.VMEM((c, c), F32)],
        compiler_params=params,
        name="ret_prompt",
    )(proj, proj, proj, proj, cosf, sinf, ret_lg, ret_norm, s0_ret)
    return mg, mr, s_gla, s_ret


def _gla_sample_kernel(q_ref, k_ref, v_ref, gg_ref, gl_ref, gn_ref, s0_ref, o_ref, s_ref):
    for bi in range(q_ref.shape[0]):
        q = q_ref[bi] * (HEAD_DK ** -0.5)
        o, s_new = _gla_core(q, k_ref[bi], v_ref[bi], gl_ref[bi], s0_ref[bi], q.shape[0])
        s_ref[bi] = s_new
        o_ref[bi] = _gla_finish(o, gg_ref[bi], gn_ref[...])


def _ret_sample_kernel(q_ref, k_ref, v_ref, rg_ref, cos_ref, sin_ref, lg_ref, gn_ref, s0_ref,
                       o_ref, s_ref, *, valid):
    lg = lg_ref[:, 0:1]
    cq = q_ref.shape[1]
    dmat = _decay_matrix(cq, max(cq, LANES), lg)
    cosf, sinf = cos_ref[...], sin_ref[...]
    for bi in range(q_ref.shape[0]):
        q = _rope(q_ref[bi], cosf, sinf)
        k = _rope(k_ref[bi], cosf, sinf) * (HEAD_DK ** -0.5)
        o, s_new = _ret_core(q, k, v_ref[bi], s0_ref[bi], lg, dmat, valid)
        s_ref[bi] = s_new
        o_ref[bi] = _ret_finish(o, rg_ref[bi], gn_ref[...])


def _attn_sample(proj, glog, gla_norm, ret_norm, cosf, sinf, ret_lg, s0_gla, s0_ret, valid, bb=8):
    bsz, tp, _ = proj.shape
    grid = (bsz // bb, GLA_HEADS)
    nk = GLA_HEADS
    kspec = lambda off: pl.BlockSpec((bb, tp, HEAD_DK), lambda i, h, off=off: (i, 0, off + h))
    vspec = lambda off: pl.BlockSpec((bb, tp, HEAD_DV), lambda i, h, off=off: (i, 0, off + h))
    hspec = pl.BlockSpec((None, 1, HEAD_DV), lambda i, h: (h, 0, 0))
    sspec = pl.BlockSpec((bb, None, HEAD_DK, HEAD_DV), lambda i, h: (i, h, 0, 0))
    ospec = pl.BlockSpec((bb, tp, HEAD_DV), lambda i, h: (i, 0, h))
    out_shape = [
        jax.ShapeDtypeStruct((bsz, tp, GLA_HEADS * HEAD_DV), BF16),
        jax.ShapeDtypeStruct((bsz, GLA_HEADS, HEAD_DK, HEAD_DV), F32),
    ]
    params = _cparams(("parallel", "parallel"), 32)
    mg, s_gla = pl.pallas_call(
        _gla_sample_kernel,
        grid=grid,
        in_specs=[kspec(0), kspec(nk), vspec(nk), vspec(2 * nk),
                  pl.BlockSpec((bb, tp, HEAD_DK), lambda i, h: (i, 0, h)),
                  hspec, sspec],
        out_specs=[ospec, sspec],
        out_shape=out_shape,
        compiler_params=params,
        name="gla_sample",
    )(proj, proj, proj, proj, glog, gla_norm, s0_gla)
    tspec = pl.BlockSpec((tp, HEAD_DK), lambda i, h: (0, 0))
    mr, s_ret = pl.pallas_call(
        functools.partial(_ret_sample_kernel, valid=valid),
        grid=grid,
        in_specs=[kspec(6 * nk), kspec(7 * nk), vspec(4 * nk), vspec(5 * nk),
                  tspec, tspec,
                  pl.BlockSpec((None, 1, LANES), lambda i, h: (h, 0, 0)),
                  hspec, sspec],
        out_specs=[ospec, sspec],
        out_shape=out_shape,
        compiler_params=params,
        name="ret_sample",
    )(proj, proj, proj, proj, cosf, sinf, ret_lg, ret_norm, s0_ret)
    return mg, mr, s_gla, s_ret


def _outproj_kernel(x_ref, mod_ref, g_ref, mg_ref, mr_ref, wo_ref, o_ref, *, d):
    half = mg_ref.shape[1]
    y = _dot(mg_ref[...], wo_ref[0:half, :]) + _dot(mr_ref[...], wo_ref[half:2 * half, :])
    o_ref[...] = _gated_residual(x_ref[...], y, g_ref[...], mod_ref, d)


def _outproj(x3, mod3, g_post, mg, mr, w_out, tm):
    gn, t, d = x3.shape
    r = mod3.shape[1]
    half = mg.shape[2]
    return pl.pallas_call(
        functools.partial(_outproj_kernel, d=d),
        grid=(gn, t // tm),
        in_specs=[
            pl.BlockSpec((None, tm, d), lambda g, i: (g, i, 0)),
            _mod_spec(r, tm, 3 * d, 2),
            pl.BlockSpec((1, d), lambda g, i: (0, 0)),
            pl.BlockSpec((None, tm, half), lambda g, i: (g, i, 0)),
            pl.BlockSpec((None, tm, half), lambda g, i: (g, i, 0)),
            pl.BlockSpec((2 * half, d), lambda g, i: (0, 0)),
        ],
        out_specs=pl.BlockSpec((None, tm, d), lambda g, i: (g, i, 0)),
        out_shape=jax.ShapeDtypeStruct((gn, t, d), F32),
        compiler_params=_cparams(("parallel", "parallel"), 52),
        name="outproj",
    )(x3, mod3, g_post, mg, mr, w_out)


def _mlp_kernel(x_ref, mod_ref, gpre_ref, gpost_ref, wup_ref, wdn_ref, o_ref, h_scr, acc_scr, *, d):
    j = pl.program_id(2)

    @pl.when(j == 0)
    def _():
        h_scr[...] = _norm_mod(x_ref[...], gpre_ref[...], mod_ref, d).astype(BF16)
        acc_scr[...] = jnp.zeros_like(acc_scr)

    u = jnp.maximum(_dot(h_scr[...], wup_ref[...]), 0.0)
    acc_scr[...] += _dot((u * u).astype(BF16), wdn_ref[...])

    @pl.when(j == pl.num_programs(2) - 1)
    def _():
        o_ref[...] = _gated_residual(x_ref[...], acc_scr[...], gpost_ref[...], mod_ref, d)


def _mlp(x3, mod3, g_pre, g_post, w_up, w_down, tm, tf=TF_MLP):
    gn, t, d = x3.shape
    r = mod3.shape[1]
    f = w_up.shape[1]
    return pl.pallas_call(
        functools.partial(_mlp_kernel, d=d),
        grid=(gn, t // tm, f // tf),
        in_specs=[
            pl.BlockSpec((None, tm, d), lambda g, i, j: (g, i, 0)),
            _mod_spec(r, tm, 3 * d, 3),
            pl.BlockSpec((1, d), lambda g, i, j: (0, 0)),
            pl.BlockSpec((1, d), lambda g, i, j: (0, 0)),
            pl.BlockSpec((d, tf), lambda g, i, j: (0, j)),
            pl.BlockSpec((tf, d), lambda g, i, j: (j, 0)),
        ],
        out_specs=pl.BlockSpec((None, tm, d), lambda g, i, j: (g, i, 0)),
        out_shape=jax.ShapeDtypeStruct((gn, t, d), F32),
        scratch_shapes=[pltpu.VMEM((tm, d), BF16), pltpu.VMEM((tm, d), F32)],
        compiler_params=_cparams(("parallel", "parallel", "arbitrary"), 56),
        name="mlp",
    )(x3, mod3, g_pre, g_post, w_up, w_down)


def _s5_disc_kernel(lr_ref, li_ref, ldt_ref, br_ref, bi_ref, pwr_ref, pwi_ref, bbr_ref, bbi_ref, *,
                    seg_len):
    lr, li = lr_ref[...], li_ref[...]
    dt = jnp.exp(ldt_ref[...])
    mag = jnp.exp(lr * dt)
    lb_re, lb_im = mag * jnp.cos(li * dt), mag * jnp.sin(li * dt)
    nr, ni = lb_re - 1.0, lb_im
    den = lr * lr + li * li
    f_re = (nr * lr + ni * li) / den
    f_im = (ni * lr - nr * li) / den
    br, bi = br_ref[...], bi_ref[...]
    bbr_ref[...] = f_re * br - f_im * bi
    bbi_ref[...] = f_re * bi + f_im * br
    pwr_ref[0] = lb_re
    pwi_ref[0] = lb_im
    qr, qi = None, None
    sr, si = lb_re, lb_im
    e = seg_len
    while e:
        if e & 1:
            qr, qi = (sr, si) if qr is None else (qr * sr - qi * si, qr * si + qi * sr)
        e >>= 1
        if e:
            sr, si = sr * sr - si * si, 2.0 * sr * si
    pr, pi = qr, qi
    for n in range(SUBLANES):
        pwr_ref[1 + n] = pr
        pwi_ref[1 + n] = pi
        pr, pi = pr * qr - pi * qi, pr * qi + pi * qr


def _s5_discretize(lam_re, lam_im, log_dt, bt_re, bt_im, seg_len):
    g, _, p = lam_re.shape
    c = bt_re.shape[1]
    return pl.pallas_call(
        functools.partial(_s5_disc_kernel, seg_len=seg_len),
        out_shape=[
            jax.ShapeDtypeStruct((1 + SUBLANES, g, 1, p), F32),
            jax.ShapeDtypeStruct((1 + SUBLANES, g, 1, p), F32),
            jax.ShapeDtypeStruct((g, c, p), F32),
            jax.ShapeDtypeStruct((g, c, p), F32),
        ],
        name="s5_discretize",
    )(lam_re, lam_im, log_dt, bt_re, bt_im)


def _gelu_tanh(x):
    c0 = math.sqrt(2.0 / math.pi)
    return x * (0.5 * (1.0 + jnp.tanh(c0 * (x + 0.044715 * (x * x * x)))))


def _cmul_add(ar, ai, xr, xi, yr, yi):
    return yr + ar * xr - ai * xi, yi + ar * xi + ai * xr


def _s5_seq_kernel(x_ref, mod_ref, gpre_ref, bblk_ref, cblk_ref, lam_ref, dskip_ref, s0r_ref, s0i_ref,
                   z_ref, sr_ref, si_ref, h_scr, xr_scr, xi_scr, *, d):
    cb = pl.program_id(0)
    ncb, tm, uw = h_scr.shape
    cw = xr_scr.shape[1]
    seg = s0r_ref.shape[0]

    @pl.when(cb == 0)
    def _():
        h = _norm_mod(x_ref[...], gpre_ref[...], mod_ref, d)
        for c in range(ncb):
            h_scr[c] = h[:, c * uw:(c + 1) * uw]

    u = h_scr[cb]
    bu = _dot(u.astype(BF16), bblk_ref[...])
    xr_scr[...] = bu[:, 0:cw]
    xi_scr[...] = bu[:, cw:2 * cw]
    car_r, car_i = s0r_ref[...], s0i_ref[...]
    l_r, l_i = lam_ref[0:1, :], lam_ref[1:2, :]
    for t in range(tm // seg):
        rows = slice(t * seg, (t + 1) * seg)
        car_r, car_i = _cmul_add(l_r, l_i, car_r, car_i, xr_scr[rows, :], xi_scr[rows, :])
        xr_scr[rows, :] = car_r
        xi_scr[rows, :] = car_i
    sr_ref[...] = car_r
    si_ref[...] = car_i
    xs = jnp.concatenate([xr_scr[...].astype(BF16), xi_scr[...].astype(BF16)], axis=1)
    y = _dot(xs, cblk_ref[...]) + dskip_ref[...] * u
    z_ref[...] = _gelu_tanh(y).astype(BF16)


def _s5_seq(x2, mod2, g_pre, bblk, cblk, lam2, dskip, s0_re, s0_im):
    tm, d = x2.shape
    seg, nst = s0_re.shape
    ncb, uw, cw2 = bblk.shape
    cw = cw2 // 2
    sspec = pl.BlockSpec((seg, cw), lambda c: (0, c))
    return pl.pallas_call(
        functools.partial(_s5_seq_kernel, d=d),
        grid=(ncb,),
        in_specs=[
            pl.BlockSpec((tm, d), lambda c: (0, 0)),
            pl.BlockSpec((seg, 3 * d), lambda c: (0, 0)),
            pl.BlockSpec((1, d), lambda c: (0, 0)),
            pl.BlockSpec((None, uw, cw2), lambda c: (c, 0, 0)),
            pl.BlockSpec((None, cw2, uw), lambda c: (c, 0, 0)),
            pl.BlockSpec((2, cw), lambda c: (0, c)),
            pl.BlockSpec((1, uw), lambda c: (0, c)),
            sspec, sspec,
        ],
        out_specs=[pl.BlockSpec((tm, uw), lambda c: (0, c)), sspec, sspec],
        out_shape=[
            jax.ShapeDtypeStruct((tm, d), BF16),
            jax.ShapeDtypeStruct((seg, nst), F32),
            jax.ShapeDtypeStruct((seg, nst), F32),
        ],
        scratch_shapes=[
            pltpu.VMEM((ncb, tm, uw), F32),
            pltpu.VMEM((tm, cw), F32),
            pltpu.VMEM((tm, cw), F32),
        ],
        compiler_params=_cparams(("arbitrary",), 48),
        name="s5_seq",
    )(x2, mod2, g_pre, bblk, cblk, lam2, dskip, s0_re, s0_im)


def _s5_rows_kernel(x_ref, mod_ref, gpre_ref, bblk_ref, cblk_ref, tbl_ref, dskip_ref,
                    z_ref, sr_ref, si_ref, h_scr, xr_scr, xi_scr, cr_scr, ci_scr, *, d):
    ncol, tm, _ = xr_scr.shape
    ncb, uw, _ = bblk_ref.shape
    sl = tm // SUBLANES

    @pl.when(pl.program_id(1) == 0)
    def _():
        cr_scr[...] = jnp.zeros_like(cr_scr)
        ci_scr[...] = jnp.zeros_like(ci_scr)

    h_scr[...] = _norm_mod(x_ref[...], gpre_ref[...], mod_ref, d)
    row0 = lax.broadcasted_iota(jnp.int32, (ncol, SUBLANES, LANES), 1) == 0

    for c in range(ncb):
        us = slice(c * uw, (c + 1) * uw)
        cols = slice(c * ncol, (c + 1) * ncol)
        u = h_scr[:, us]
        bu = _dot(u.astype(BF16), bblk_ref[c])
        for j in range(ncol):
            xr_scr[j] = bu[:, j * LANES:(j + 1) * LANES]
            xi_scr[j] = bu[:, (ncol + j) * LANES:(ncol + j + 1) * LANES]
        l_r, l_i = tbl_ref[0, cols], tbl_ref[1, cols]

        def local(i, s):
            rows = pl.ds(pl.multiple_of(i * SUBLANES, SUBLANES), SUBLANES)
            return _cmul_add(l_r, l_i, s[0], s[1], xr_scr[:, rows, :], xi_scr[:, rows, :])

        zero = jnp.zeros((ncol, SUBLANES, LANES), F32)
        g_r, g_i = lax.fori_loop(0, sl, local, (zero, zero), unroll=True)
        for n in range(3):
            g_r, g_i = _cmul_add(tbl_ref[2 + 2 * n, cols], tbl_ref[3 + 2 * n, cols],
                                 pltpu.roll(g_r, 1 << n, 1), pltpu.roll(g_i, 1 << n, 1), g_r, g_i)
        car_r, car_i = cr_scr[cols], ci_scr[cols]
        g_r, g_i = _cmul_add(tbl_ref[8, cols], tbl_ref[9, cols], car_r, car_i, g_r, g_i)
        in_r = jnp.where(row0, car_r, pltpu.roll(g_r, 1, 1))
        in_i = jnp.where(row0, car_i, pltpu.roll(g_i, 1, 1))

        def full(i, s):
            rows = pl.ds(pl.multiple_of(i * SUBLANES, SUBLANES), SUBLANES)
            s_r, s_i = _cmul_add(l_r, l_i, s[0], s[1], xr_scr[:, rows, :], xi_scr[:, rows, :])
            xr_scr[:, rows, :] = s_r
            xi_scr[:, rows, :] = s_i
            return s_r, s_i

        e_r, e_i = lax.fori_loop(0, sl, full, (in_r, in_i), unroll=True)
        cr_scr[cols] = e_r[:, SUBLANES - 1:SUBLANES, :]
        ci_scr[cols] = e_i[:, SUBLANES - 1:SUBLANES, :]
        xs = jnp.concatenate([xr_scr[j].astype(BF16) for j in range(ncol)]
                             + [xi_scr[j].astype(BF16) for j in range(ncol)], axis=1)
        y = _dot(xs, cblk_ref[c]) + dskip_ref[:, us] * u
        z_ref[:, us] = _gelu_tanh(y).astype(BF16)

    sr_ref[...] = cr_scr[...]
    si_ref[...] = ci_scr[...]


def _s5_rows(x3, mod3, g_pre, bblk, cblk, tbl, dskip, tm):
    gn, t, d = x3.shape
    ncb, uw, cw2 = bblk.shape
    ncol = cw2 // 2 // LANES
    nct = ncb * ncol
    nt = t // tm
    const = lambda shape: pl.BlockSpec(shape, lambda g, i: (0,) * len(shape), pipeline_mode=pl.Buffered(1))
    ospec = pl.BlockSpec((None, None, nct, 1, LANES), lambda g, i: (g, i, 0, 0, 0))
    z3, s_re, s_im = pl.pallas_call(
        functools.partial(_s5_rows_kernel, d=d),
        grid=(gn, nt),
        in_specs=[
            pl.BlockSpec((None, tm, d), lambda g, i: (g, i, 0)),
            _mod_spec(1, tm, 3 * d, 2),
            pl.BlockSpec((1, d), lambda g, i: (0, 0)),
            const(bblk.shape), const(cblk.shape), const(tbl.shape), const(dskip.shape),
        ],
        out_specs=[pl.BlockSpec((None, tm, d), lambda g, i: (g, i, 0)), ospec, ospec],
        out_shape=[
            jax.ShapeDtypeStruct((gn, t, d), BF16),
            jax.ShapeDtypeStruct((gn, nt, nct, 1, LANES), F32),
            jax.ShapeDtypeStruct((gn, nt, nct, 1, LANES), F32),
        ],
        scratch_shapes=[
            pltpu.VMEM((tm, d), F32),
            pltpu.VMEM((ncol, tm, LANES), F32),
            pltpu.VMEM((ncol, tm, LANES), F32),
            pltpu.VMEM((nct, 1, LANES), F32),
            pltpu.VMEM((nct, 1, LANES), F32),
        ],
        compiler_params=_cparams(("parallel", "arbitrary"), 48),
        name="s5_rows",
    )(x3, mod3, g_pre, bblk, cblk, tbl, dskip)
    return z3, s_re[:, nt - 1].reshape(gn, 1, nct * LANES), s_im[:, nt - 1].reshape(gn, 1, nct * LANES)


def _glu_kernel(x_ref, mod_ref, g_ref, z_ref, wa_ref, wb_ref, o_ref, *, d):
    z = z_ref[...]
    y = _dot(z, wa_ref[...]) * jax.nn.sigmoid(_dot(z, wb_ref[...]))
    o_ref[...] = _gated_residual(x_ref[...], y, g_ref[...], mod_ref, d)


def _glu(x3, mod3, g_post, z3, w_a, w_b, tm):
    gn, t, d = x3.shape
    r = mod3.shape[1]
    wspec = pl.BlockSpec((d, d), lambda g, i: (0, 0), pipeline_mode=pl.Buffered(1))
    return pl.pallas_call(
        functools.partial(_glu_kernel, d=d),
        grid=(gn, t // tm),
        in_specs=[
            pl.BlockSpec((None, tm, d), lambda g, i: (g, i, 0)),
            _mod_spec(r, tm, 3 * d, 2),
            pl.BlockSpec((1, d), lambda g, i: (0, 0)),
            pl.BlockSpec((None, tm, d), lambda g, i: (g, i, 0)),
            wspec, wspec,
        ],
        out_specs=pl.BlockSpec((None, tm, d), lambda g, i: (g, i, 0)),
        out_shape=jax.ShapeDtypeStruct((gn, t, d), F32),
        compiler_params=_cparams(("parallel", "parallel"), 52),
        name="glu",
    )(x3, mod3, g_post, z3, w_a, w_b)


def _rope_tables(pos):
    half = HEAD_DK // 2
    inv = ROPE_BASE ** (-jnp.arange(half, dtype=F32) / half)
    ang = pos.astype(F32)[:, None] * inv[None, :]
    cos, sin = jnp.cos(ang), jnp.sin(ang)
    return jnp.concatenate([cos, cos], axis=-1), jnp.concatenate([-sin, sin], axis=-1)


def _block_diag(w):
    ncb, gpb, a, b = w.shape
    eye = jnp.eye(gpb, dtype=w.dtype)
    return jnp.einsum("ngab,gh->ngahb", w, eye).reshape(ncb, gpb * a, gpb * b)


def _s5_tables(pw_re, pw_im):
    n = pw_re.shape[0]
    flat_r = pw_re.reshape(n, -1)
    flat_i = pw_im.reshape(n, -1)
    row = jnp.arange(SUBLANES)[:, None]
    tabs = [jnp.broadcast_to(flat_r[0], (SUBLANES, flat_r.shape[1])),
            jnp.broadcast_to(flat_i[0], (SUBLANES, flat_i.shape[1]))]
    for s in (1, 2, 4):
        mask = row >= s
        tabs.append(jnp.where(mask, flat_r[s][None, :], 0.0))
        tabs.append(jnp.where(mask, flat_i[s][None, :], 0.0))
    tabs += [flat_r[1:], flat_i[1:]]
    tbl = jnp.stack(tabs)
    tbl = tbl.reshape(tbl.shape[0], SUBLANES, -1, LANES).transpose(0, 2, 1, 3)
    return tbl, jnp.stack([flat_r[0], flat_i[0]])


def kernel(x_prompt, x_sample, state_gla, state_ret, state_s5_re, state_s5_im, c_prompt, c_sample,
           w_ada, b_ada, norm_pre, norm_post, w_in_mix, w_gla_gk, b_gla_gk, gla_head_norm,
           ret_head_norm, w_out_mix, s5_lam_re, s5_lam_im, s5_log_dt, s5_b_re, s5_b_im,
           s5_c_re, s5_c_im, s5_d, w_glu_a, w_glu_b, w_mlp_up, w_mlp_down):
    bp, tp, d = x_prompt.shape
    bs, ts, _ = x_sample.shape
    depth = w_ada.shape[0]
    gkey = GLA_HEADS * HEAD_DK
    gval = GLA_HEADS * HEAD_DV

    nrow = -(-(bs + bp) // SUBLANES) * SUBLANES
    c_all = jnp.concatenate([c_sample, c_prompt, jnp.zeros((nrow - bs - bp, d), F32)], axis=0)
    mod_all = _adaln(c_all, w_ada.reshape(depth * 2, d, 3 * d), b_ada.reshape(depth * 2, 1, 3 * d))
    mod_s = [mod_all[k, 0:bs][None] for k in range(depth * 2)]
    mod_p = [mod_all[k, bs:bs + bp][:, None, :] for k in range(depth * 2)]

    w_in = w_in_mix[0]
    sec = 2 * gkey + 2 * gval
    w_main = jnp.concatenate([w_in[:, :sec], w_in[:, sec + GLA_RANK:]], axis=1).astype(BF16)
    w_main = jnp.swapaxes(w_main.reshape(d, -1, TN_INPROJ), 0, 1)
    w_lr = jnp.pad(w_in[:, sec:sec + GLA_RANK], ((0, 0), (0, LANES - GLA_RANK))).astype(BF16)
    w_gk = jnp.pad(w_gla_gk[0], ((0, LANES - GLA_RANK), (0, 0))).astype(BF16)
    b_gk = b_gla_gk[0][None, :]
    w_out = w_out_mix[0].astype(BF16)
    w_up = [w_mlp_up[l].astype(BF16) for l in range(depth)]
    w_dn = [w_mlp_down[l].astype(BF16) for l in range(depth)]
    w_ga = w_glu_a[0].astype(BF16)
    w_gb = w_glu_b[0].astype(BF16)
    gla_norm = gla_head_norm[0][:, None, :]
    ret_norm = ret_head_norm[0][:, None, :]
    gamma_log = jnp.log1p(-jnp.power(2.0, -5.0 - jnp.arange(RET_HEADS, dtype=F32)))
    ret_lg = jnp.broadcast_to(gamma_log[:, None, None], (RET_HEADS, 1, LANES))

    ng = s5_lam_re.shape[1]
    ncb = ng // S5_GPB
    tm5 = min(TM_S5, tp)
    pw_re, pw_im, bbt_re, bbt_im = _s5_discretize(
        s5_lam_re[0][:, None, :], s5_lam_im[0][:, None, :], s5_log_dt[0][:, None, None],
        jnp.swapaxes(s5_b_re[0], 1, 2), jnp.swapaxes(s5_b_im[0], 1, 2), tm5 // SUBLANES)
    bblk = jnp.concatenate([
        _block_diag(bbt_re.reshape(ncb, S5_GPB, S5_GROUP, S5_STATE)),
        _block_diag(bbt_im.reshape(ncb, S5_GPB, S5_GROUP, S5_STATE))], axis=2).astype(BF16)
    ct_re = jnp.swapaxes(s5_c_re[0], 1, 2).reshape(ncb, S5_GPB, S5_STATE, S5_GROUP)
    ct_im = jnp.swapaxes(s5_c_im[0], 1, 2).reshape(ncb, S5_GPB, S5_STATE, S5_GROUP)
    cblk = jnp.concatenate([_block_diag(ct_re), _block_diag(-ct_im)], axis=1).astype(BF16)
    tbl, lam2 = _s5_tables(pw_re[:, :, 0, :], pw_im[:, :, 0, :])
    dskip = s5_d[0][None, :]

    def trunk(x3, mods, tm, attn, s5):
        proj, glog = _inproj(x3, mods[0], norm_pre[0, 0][None], w_main, w_lr, w_gk, b_gk, tm)
        mg, mr, s_gla, s_ret = attn(proj, glog)
        x3 = _outproj(x3, mods[0], norm_post[0, 0][None], mg, mr, w_out, tm)
        x3 = _mlp(x3, mods[1], norm_pre[0, 1][None], norm_post[0, 1][None], w_up[0], w_dn[0], tm)
        z3, s_re, s_im = s5(x3, mods[2])
        x3 = _glu(x3, mods[2], norm_post[1, 0][None], z3, w_ga, w_gb, tm)
        x3 = _mlp(x3, mods[3], norm_pre[1, 1][None], norm_post[1, 1][None], w_up[1], w_dn[1], tm)
        return x3, s_gla, s_ret, s_re, s_im

    cos_p, sin_p = _rope_tables(jnp.arange(tp, dtype=F32))
    zeros_att = jnp.zeros((bp, GLA_HEADS, HEAD_DK, HEAD_DV), F32)

    def attn_p(proj, glog):
        return _attn_prompt(proj, glog, gla_norm, ret_norm, cos_p, sin_p, ret_lg, zeros_att, zeros_att)

    def s5_p(x3, mod3):
        sl = tm5 // SUBLANES
        xp = jnp.swapaxes(x3.reshape(bp, tp // tm5, SUBLANES, sl, d), 2, 3).reshape(bp, tp, d)
        zp, s_re, s_im = _s5_rows(xp, mod3, norm_pre[1, 0][None], bblk, cblk, tbl, dskip, tm5)
        z3 = jnp.swapaxes(zp.reshape(bp, tp // tm5, sl, SUBLANES, d), 2, 3).reshape(bp, tp, d)
        return z3, s_re, s_im

    y_p, gla_p, ret_p, re_p, im_p = trunk(x_prompt, mod_p, min(TM_DENSE, tp), attn_p, s5_p)

    pos_s = PAST_LEN + jnp.arange(T_PAD, dtype=F32)
    cos_s, sin_s = _rope_tables(pos_s)

    def attn_s(proj, glog):
        def to_bm(a):
            a = jnp.swapaxes(a.reshape(ts, bs, a.shape[-1]), 0, 1)
            return jnp.pad(a, ((0, 0), (0, T_PAD - ts), (0, 0)))
        mg, mr, s_gla, s_ret = _attn_sample(to_bm(proj), to_bm(glog), gla_norm, ret_norm, cos_s, sin_s,
                                            ret_lg, state_gla[0], state_ret[0], ts)
        to_tm = lambda a: jnp.swapaxes(a[:, :ts], 0, 1).reshape(1, ts * bs, a.shape[-1])
        return to_tm(mg), to_tm(mr), s_gla, s_ret

    xs3 = jnp.swapaxes(x_sample, 0, 1).reshape(1, ts * bs, d)

    def s5_s(x3, mod3):
        z2, s_re, s_im = _s5_seq(x3[0], mod3[0], norm_pre[1, 0][None], bblk, cblk, lam2, dskip,
                                 state_s5_re[0].reshape(bs, -1), state_s5_im[0].reshape(bs, -1))
        return z2[None], s_re, s_im

    y_s, gla_s, ret_s, re_s, im_s = trunk(xs3, mod_s, ts * bs, attn_s, s5_s)
    y_s = jnp.swapaxes(y_s.reshape(ts, bs, d), 0, 1)

    st = lambda a, b_: a.reshape(1, b_, ng, S5_STATE)
    return (y_p, y_s, gla_p[None], gla_s[None], ret_p[None], ret_s[None],
            st(re_p, bp), st(re_s, bs), st(im_p, bp), st(im_s, bs))
```

```python
import functools
import math

import jax
import jax.numpy as jnp
import numpy as np
from jax import lax
from jax.experimental import pallas as pl
from jax.experimental.pallas import tpu as pltpu

F32 = jnp.float32
BF16 = jnp.bfloat16

EPS = 1e-6
LANES = 128
SUBLANES = 8
MIB = 1024 * 1024

GLA_HEADS = 4
RET_HEADS = 4
HEAD_DK = 128
HEAD_DV = 256
GLA_RANK = 16
GLA_LOGIT_NORM = 16.0
ROPE_BASE = 10000.0
PAST_LEN = 16384
S5_GROUP = 16
S5_STATE = 64
S5_GPB = 16
S5_UW = S5_GPB * S5_GROUP
ATT_CHUNK = 128
GLA_SUB = 16
T_PAD = 8
TM_DENSE = 512
TM_S5 = 256
TN_INPROJ = 1024
TF_MLP = 1024


def _cparams(sem, vmem_mib):
    return pltpu.CompilerParams(dimension_semantics=sem, vmem_limit_bytes=vmem_mib * MIB)


def _dot(a, b):
    return jnp.dot(a, b, preferred_element_type=F32)


def _dot_nt(a, b):
    return lax.dot_general(a, b, (((1,), (1,)), ((), ())), preferred_element_type=F32)


def _rms(x, g):
    return x * lax.rsqrt(jnp.mean(x * x, axis=-1, keepdims=True) + EPS) * g


def _rows_affine(y, a, b=None):
    tm, d = y.shape
    r = a.shape[0]
    if r == 1 or r == tm:
        out = y * a
        return out if b is None else out + b
    y3 = y.reshape(tm // r, r, d)
    out = y3 * a[None]
    if b is not None:
        out = out + b[None]
    return out.reshape(tm, d)


def _norm_mod(x, g, mod_ref, d):
    return _rows_affine(_rms(x, g), 1.0 + mod_ref[:, d:2 * d], mod_ref[:, 0:d])


def _gated_residual(x, y, g, mod_ref, d):
    return x + _rows_affine(_rms(y, g), mod_ref[:, 2 * d:3 * d])


def _mod_spec(r, tm, width, ngrid):
    if ngrid == 2:
        if r == 1:
            return pl.BlockSpec((None, 1, width), lambda g, i: (g, 0, 0))
        return pl.BlockSpec((None, r, width), lambda g, i: (g, 0, 0))
    if r == 1:
        return pl.BlockSpec((None, 1, width), lambda g, i, j: (g, 0, 0))
    return pl.BlockSpec((None, r, width), lambda g, i, j: (g, 0, 0))


def _adaln_kernel(c_ref, w_ref, b_ref, o_ref):
    c = c_ref[...]
    sc = (c * jax.nn.sigmoid(c)).astype(BF16)
    o_ref[...] = _dot(sc, w_ref[...].astype(BF16)) + b_ref[...]


def _adaln(c_all, w_ada, b_ada, tn=512):
    ls, d, n = w_ada.shape
    rows = c_all.shape[0]
    return pl.pallas_call(
        _adaln_kernel,
        grid=(ls, n // tn),
        in_specs=[
            pl.BlockSpec((rows, d), lambda l, j: (0, 0)),
            pl.BlockSpec((None, d, tn), lambda l, j: (l, 0, j)),
            pl.BlockSpec((None, 1, tn), lambda l, j: (l, 0, j)),
        ],
        out_specs=pl.BlockSpec((None, rows, tn), lambda l, j: (l, 0, j)),
        out_shape=jax.ShapeDtypeStruct((ls, rows, n), F32),
        compiler_params=_cparams(("parallel", "parallel"), 40),
        name="adaln",
    )(c_all, w_ada, b_ada)


def _log_sigmoid(x):
    return jnp.minimum(x, 0.0) - jnp.log1p(jnp.exp(-jnp.abs(x)))


def _inproj_kernel(x_ref, mod_ref, g_ref, w_ref, wlr_ref, wgk_ref, bgk_ref,
                   proj_ref, glog_ref, h_scr, *, d):
    @pl.when(pl.program_id(2) == 0)
    def _():
        hb = _norm_mod(x_ref[...], g_ref[...], mod_ref, d).astype(BF16)
        h_scr[...] = hb
        glr = _dot(hb, wlr_ref[...])
        logit = _dot(glr.astype(BF16), wgk_ref[...]) + bgk_ref[...]
        glog_ref[...] = _log_sigmoid(logit) * (1.0 / GLA_LOGIT_NORM)

    proj_ref[...] = _dot(h_scr[...], w_ref[pl.program_id(2)])


def _inproj(x3, mod3, g_pre, w_main, w_lr, w_gk, b_gk, tm):
    gn, t, d = x3.shape
    nj, _, tn = w_main.shape
    n = nj * tn
    r = mod3.shape[1]
    gkey = w_gk.shape[1]
    return pl.pallas_call(
        functools.partial(_inproj_kernel, d=d),
        grid=(gn, t // tm, n // tn),
        in_specs=[
            pl.BlockSpec((None, tm, d), lambda g, i, j: (g, i, 0)),
            _mod_spec(r, tm, 3 * d, 3),
            pl.BlockSpec((1, d), lambda g, i, j: (0, 0)),
            pl.BlockSpec((nj, d, tn), lambda g, i, j: (0, 0, 0), pipeline_mode=pl.Buffered(1)),
            pl.BlockSpec((d, LANES), lambda g, i, j: (0, 0)),
            pl.BlockSpec((LANES, gkey), lambda g, i, j: (0, 0)),
            pl.BlockSpec((1, gkey), lambda g, i, j: (0, 0)),
        ],
        out_specs=[
            pl.BlockSpec((None, tm, tn), lambda g, i, j: (g, i, j)),
            pl.BlockSpec((None, tm, gkey), lambda g, i, j: (g, i, 0)),
        ],
        out_shape=[
            jax.ShapeDtypeStruct((gn, t, n), F32),
            jax.ShapeDtypeStruct((gn, t, gkey), F32),
        ],
        scratch_shapes=[pltpu.VMEM((tm, d), BF16)],
        compiler_params=_cparams(("parallel", "parallel", "arbitrary"), 56),
        name="inproj",
    )(x3, mod3, g_pre, w_main, w_lr, w_gk, b_gk)


def _cumsum_rows(g):
    c = g.shape[0]
    row = lax.broadcasted_iota(jnp.int32, g.shape, 0)
    s = 1
    while s < c:
        g = g + jnp.where(row >= s, pltpu.roll(g, s, 0), 0.0)
        s *= 2
    return g


def _pad_rows(a, rows):
    if a.shape[0] == rows:
        return a
    return jnp.concatenate([a, jnp.zeros((rows - a.shape[0], a.shape[1]), a.dtype)], axis=0)


def _col_bcast(row, width):
    sq = jnp.transpose(jnp.broadcast_to(row, (LANES, LANES)))
    return jnp.concatenate([sq] * (width // LANES), axis=1)


def _gla_core(q, k, v, g, s, sub):
    cq = q.shape[0]
    ck = max(cq, LANES)
    b = _cumsum_rows(g)
    be = b - g
    bk = _pad_rows(b, ck)
    kp = _pad_rows(k, ck)
    vp = _pad_rows(v, ck).astype(BF16)
    rowj = lax.broadcasted_iota(jnp.int32, (ck, 1), 0)
    att_rows = []
    for blk in range(cq // sub):
        lo, hi = blk * sub, (blk + 1) * sub
        base = be[lo:lo + 1, :]
        qs = q[lo:hi] * jnp.exp(b[lo:hi] - base)
        ks = jnp.where(rowj < hi, kp * jnp.exp(base - bk), 0.0)
        att_rows.append(_dot_nt(qs.astype(BF16), ks.astype(BF16)))
    att = att_rows[0] if len(att_rows) == 1 else jnp.concatenate(att_rows, axis=0)
    ri = lax.broadcasted_iota(jnp.int32, (cq, ck), 0)
    cj = lax.broadcasted_iota(jnp.int32, (cq, ck), 1)
    att = jnp.where(ri >= cj, att, 0.0)
    o = _dot(att.astype(BF16), vp) + _dot((q * jnp.exp(b)).astype(BF16), s.astype(BF16))
    b_last = b[cq - 1:cq, :]
    k_out = kp * jnp.exp(b_last - bk)
    s_new = s * _col_bcast(jnp.exp(b_last), s.shape[1]) + _dot(jnp.transpose(k_out).astype(BF16), vp)
    return o, s_new


def _ret_core(q, k, v, s, lg, dmat, valid):
    cq = q.shape[0]
    ck = max(cq, LANES)
    kp = _pad_rows(k, ck)
    vp = _pad_rows(v, ck).astype(BF16)
    ti = lax.broadcasted_iota(jnp.int32, (cq, 1), 0).astype(F32)
    tj = lax.broadcasted_iota(jnp.int32, (ck, 1), 0).astype(F32)
    att = _dot_nt(q.astype(BF16), kp.astype(BF16)) * dmat
    q_in = q * jnp.exp((ti + 1.0) * lg)
    o = _dot(att.astype(BF16), vp) + _dot(q_in.astype(BF16), s.astype(BF16))
    k_out = kp * jnp.exp((float(valid - 1) - tj) * lg)
    s_new = s * jnp.exp(float(valid) * lg) + _dot(jnp.transpose(k_out).astype(BF16), vp)
    return o, s_new


def _decay_matrix(cq, ck, lg):
    ri = lax.broadcasted_iota(jnp.int32, (cq, ck), 0)
    cj = lax.broadcasted_iota(jnp.int32, (cq, ck), 1)
    diff = (ri - cj).astype(F32)
    return jnp.where(ri >= cj, jnp.exp(diff * lg), 0.0)


def _rope(x, cosf, sinf):
    return x * cosf + pltpu.roll(x, x.shape[1] // 2, 1) * sinf


def _silu(x):
    return x * jax.nn.sigmoid(x)


def _gla_finish(o, gate, gn):
    o = o * lax.rsqrt(jnp.mean(o * o, axis=-1, keepdims=True) + EPS) * gn
    return (o * _silu(gate)).astype(BF16)


def _ret_finish(o, gate, gn):
    oc = o - jnp.mean(o, axis=-1, keepdims=True)
    oc = oc * lax.rsqrt(jnp.mean(oc * oc, axis=-1, keepdims=True) + EPS) * gn
    return (oc * _silu(gate)).astype(BF16)


def _head(ref, h, width):
    return ref[:, h * width:(h + 1) * width]


def _gla_prompt_kernel(q_ref, k_ref, v_ref, gg_ref, gl_ref, gn_ref, s0_ref, o_ref, s_ref):
    @pl.when(pl.program_id(1) == 0)
    def _():
        s_ref[...] = s0_ref[...]

    for h in range(s_ref.shape[0]):
        q = _head(q_ref, h, HEAD_DK) * (HEAD_DK ** -0.5)
        o, s_new = _gla_core(q, _head(k_ref, h, HEAD_DK), _head(v_ref, h, HEAD_DV),
                             _head(gl_ref, h, HEAD_DK), s_ref[h], GLA_SUB)
        s_ref[h] = s_new
        o_ref[:, h * HEAD_DV:(h + 1) * HEAD_DV] = _gla_finish(o, _head(gg_ref, h, HEAD_DV), gn_ref[h])


def _ret_prompt_kernel(q_ref, k_ref, v_ref, rg_ref, cos_ref, sin_ref, lg_ref, gn_ref, s0_ref,
                       o_ref, s_ref, d_scr):
    @pl.when(pl.program_id(1) == 0)
    def _():
        s_ref[...] = s0_ref[...]
        for h in range(d_scr.shape[0]):
            d_scr[h] = _decay_matrix(d_scr.shape[1], d_scr.shape[2], lg_ref[h][:, 0:1])

    cosf, sinf = cos_ref[...], sin_ref[...]
    for h in range(s_ref.shape[0]):
        q = _rope(_head(q_ref, h, HEAD_DK), cosf, sinf)
        k = _rope(_head(k_ref, h, HEAD_DK), cosf, sinf) * (HEAD_DK ** -0.5)
        o, s_new = _ret_core(q, k, _head(v_ref, h, HEAD_DV), s_ref[h], lg_ref[h][:, 0:1], d_scr[h],
                             q.shape[0])
        s_ref[h] = s_new
        o_ref[:, h * HEAD_DV:(h + 1) * HEAD_DV] = _ret_finish(o, _head(rg_ref, h, HEAD_DV), gn_ref[h])


def _attn_prompt(proj, glog, gla_norm, ret_norm, cosf, sinf, ret_lg, s0_gla, s0_ret):
    bsz, t, _ = proj.shape
    c = ATT_CHUNK
    nh = GLA_HEADS
    grid = (bsz, t // c)
    kw, vw = nh * HEAD_DK, nh * HEAD_DV
    kspec = lambda blk: pl.BlockSpec((None, c, kw), lambda b, i, blk=blk: (b, i, blk))
    vspec = lambda blk: pl.BlockSpec((None, c, vw), lambda b, i, blk=blk: (b, i, blk))
    hspec = pl.BlockSpec((nh, 1, HEAD_DV), lambda b, i: (0, 0, 0))
    sspec = pl.BlockSpec((None, nh, HEAD_DK, HEAD_DV), lambda b, i: (b, 0, 0, 0))
    ospec = pl.BlockSpec((None, c, vw), lambda b, i: (b, i, 0))
    out_shape = [
        jax.ShapeDtypeStruct((bsz, t, vw), BF16),
        jax.ShapeDtypeStruct((bsz, nh, HEAD_DK, HEAD_DV), F32),
    ]
    params = _cparams(("parallel", "arbitrary"), 32)
    mg, s_gla = pl.pallas_call(
        _gla_prompt_kernel,
        grid=grid,
        in_specs=[kspec(0), kspec(1), vspec(1), vspec(2), kspec(0), hspec, sspec],
        out_specs=[ospec, sspec],
        out_shape=out_shape,
        compiler_params=params,
        name="gla_prompt",
    )(proj, proj, proj, proj, glog, gla_norm, s0_gla)
    tspec = pl.BlockSpec((c, HEAD_DK), lambda b, i: (i, 0))
    mr, s_ret = pl.pallas_call(
        _ret_prompt_kernel,
        grid=grid,
        in_specs=[kspec(6), kspec(7), vspec(4), vspec(5), tspec, tspec,
                  pl.BlockSpec((nh, 1, LANES), lambda b, i: (0, 0, 0)), hspec, sspec],
        out_specs=[ospec, sspec],
        out_shape=out_shape,
        scratch_shapes=[pltpu.VMEM((nh, c, c), F32)],
        compiler_params=params,
        name="ret_prompt",
    )(proj, proj, proj, proj, cosf, sinf, ret_lg, ret_norm, s0_ret)
    return mg, mr, s_gla, s_ret


def _gla_sample_kernel(q_ref, k_ref, v_ref, gg_ref, gl_ref, gn_ref, s0_ref, o_ref, s_ref):
    for bi in range(q_ref.shape[0]):
        q = q_ref[bi] * (HEAD_DK ** -0.5)
        o, s_new = _gla_core(q, k_ref[bi], v_ref[bi], gl_ref[bi], s0_ref[bi], q.shape[0])
        s_ref[bi] = s_new
        o_ref[bi] = _gla_finish(o, gg_ref[bi], gn_ref[...])


def _ret_sample_kernel(q_ref, k_ref, v_ref, rg_ref, cos_ref, sin_ref, lg_ref, gn_ref, s0_ref,
                       o_ref, s_ref, *, valid):
    lg = lg_ref[:, 0:1]
    cq = q_ref.shape[1]
    dmat = _decay_matrix(cq, max(cq, LANES), lg)
    cosf, sinf = cos_ref[...], sin_ref[...]
    for bi in range(q_ref.shape[0]):
        q = _rope(q_ref[bi], cosf, sinf)
        k = _rope(k_ref[bi], cosf, sinf) * (HEAD_DK ** -0.5)
        o, s_new = _ret_core(q, k, v_ref[bi], s0_ref[bi], lg, dmat, valid)
        s_ref[bi] = s_new
        o_ref[bi] = _ret_finish(o, rg_ref[bi], gn_ref[...])


def _attn_sample(proj, glog, gla_norm, ret_norm, cosf, sinf, ret_lg, s0_gla, s0_ret, valid, bb=8):
    bsz, tp, _ = proj.shape
    grid = (bsz // bb, GLA_HEADS)
    nk = GLA_HEADS
    kspec = lambda off: pl.BlockSpec((bb, tp, HEAD_DK), lambda i, h, off=off: (i, 0, off + h))
    vspec = lambda off: pl.BlockSpec((bb, tp, HEAD_DV), lambda i, h, off=off: (i, 0, off + h))
    hspec = pl.BlockSpec((None, 1, HEAD_DV), lambda i, h: (h, 0, 0))
    sspec = pl.BlockSpec((bb, None, HEAD_DK, HEAD_DV), lambda i, h: (i, h, 0, 0))
    ospec = pl.BlockSpec((bb, tp, HEAD_DV), lambda i, h: (i, 0, h))
    out_shape = [
        jax.ShapeDtypeStruct((bsz, tp, GLA_HEADS * HEAD_DV), BF16),
        jax.ShapeDtypeStruct((bsz, GLA_HEADS, HEAD_DK, HEAD_DV), F32),
    ]
    params = _cparams(("parallel", "parallel"), 32)
    mg, s_gla = pl.pallas_call(
        _gla_sample_kernel,
        grid=grid,
        in_specs=[kspec(0), kspec(nk), vspec(nk), vspec(2 * nk),
                  pl.BlockSpec((bb, tp, HEAD_DK), lambda i, h: (i, 0, h)),
                  hspec, sspec],
        out_specs=[ospec, sspec],
        out_shape=out_shape,
        compiler_params=params,
        name="gla_sample",
    )(proj, proj, proj, proj, glog, gla_norm, s0_gla)
    tspec = pl.BlockSpec((tp, HEAD_DK), lambda i, h: (0, 0))
    mr, s_ret = pl.pallas_call(
        functools.partial(_ret_sample_kernel, valid=valid),
        grid=grid,
        in_specs=[kspec(6 * nk), kspec(7 * nk), vspec(4 * nk), vspec(5 * nk),
                  tspec, tspec,
                  pl.BlockSpec((None, 1, LANES), lambda i, h: (h, 0, 0)),
                  hspec, sspec],
        out_specs=[ospec, sspec],
        out_shape=out_shape,
        compiler_params=params,
        name="ret_sample",
    )(proj, proj, proj, proj, cosf, sinf, ret_lg, ret_norm, s0_ret)
    return mg, mr, s_gla, s_ret


def _outproj_kernel(x_ref, mod_ref, g_ref, mg_ref, mr_ref, wo_ref, o_ref, *, d):
    half = mg_ref.shape[1]
    y = _dot(mg_ref[...], wo_ref[0:half, :]) + _dot(mr_ref[...], wo_ref[half:2 * half, :])
    o_ref[...] = _gated_residual(x_ref[...], y, g_ref[...], mod_ref, d)


def _outproj(x3, mod3, g_post, mg, mr, w_out, tm):
    gn, t, d = x3.shape
    r = mod3.shape[1]
    half = mg.shape[2]
    return pl.pallas_call(
        functools.partial(_outproj_kernel, d=d),
        grid=(gn, t // tm),
        in_specs=[
            pl.BlockSpec((None, tm, d), lambda g, i: (g, i, 0)),
            _mod_spec(r, tm, 3 * d, 2),
            pl.BlockSpec((1, d), lambda g, i: (0, 0)),
            pl.BlockSpec((None, tm, half), lambda g, i: (g, i, 0)),
            pl.BlockSpec((None, tm, half), lambda g, i: (g, i, 0)),
            pl.BlockSpec((2 * half, d), lambda g, i: (0, 0)),
        ],
        out_specs=pl.BlockSpec((None, tm, d), lambda g, i: (g, i, 0)),
        out_shape=jax.ShapeDtypeStruct((gn, t, d), F32),
        compiler_params=_cparams(("parallel", "parallel"), 52),
        name="outproj",
    )(x3, mod3, g_post, mg, mr, w_out)


def _mlp_step(x_ref, mod_ref, gpre_ref, gpost_ref, wup, wdn, o_ref, h_scr, acc_scr, d):
    j = pl.program_id(2)

    @pl.when(j == 0)
    def _():
        h_scr[...] = _norm_mod(x_ref[...], gpre_ref[...], mod_ref, d).astype(BF16)
        acc_scr[...] = jnp.zeros_like(acc_scr)

    u = jnp.maximum(_dot(h_scr[...], wup), 0.0)
    acc_scr[...] += _dot((u * u).astype(BF16), wdn)

    @pl.when(j == pl.num_programs(2) - 1)
    def _():
        o_ref[...] = _gated_residual(x_ref[...], acc_scr[...], gpost_ref[...], mod_ref, d)


def _mlp_kernel(x_ref, mod_ref, gpre_ref, gpost_ref, wup_ref, wdn_ref, o_ref, h_scr, acc_scr, *, d):
    _mlp_step(x_ref, mod_ref, gpre_ref, gpost_ref, wup_ref[...], wdn_ref[...], o_ref, h_scr, acc_scr, d)


def _mlp_cast_kernel(x_ref, mod_ref, gpre_ref, gpost_ref, wup_ref, wdn_ref,
                     o_ref, wupb_ref, wdnb_ref, h_scr, acc_scr, *, d):
    wup = wup_ref[...].astype(BF16)
    wdn = wdn_ref[...].astype(BF16)
    wupb_ref[...] = wup
    wdnb_ref[...] = wdn
    _mlp_step(x_ref, mod_ref, gpre_ref, gpost_ref, wup, wdn, o_ref, h_scr, acc_scr, d)


def _mlp_cast(x3, mod3, g_pre, g_post, w_up_all, w_down_all, layer, tf=512):
    gn, t, d = x3.shape
    assert gn == 1
    r = mod3.shape[1]
    f = w_up_all.shape[2]
    return pl.pallas_call(
        functools.partial(_mlp_cast_kernel, d=d),
        grid=(1, 1, f // tf),
        in_specs=[
            pl.BlockSpec((None, t, d), lambda g, i, j: (0, 0, 0)),
            _mod_spec(r, t, 3 * d, 3),
            pl.BlockSpec((1, d), lambda g, i, j: (0, 0)),
            pl.BlockSpec((1, d), lambda g, i, j: (0, 0)),
            pl.BlockSpec((None, d, tf), lambda g, i, j: (layer, 0, j)),
            pl.BlockSpec((None, tf, d), lambda g, i, j: (layer, j, 0)),
        ],
        out_specs=[
            pl.BlockSpec((None, t, d), lambda g, i, j: (0, 0, 0)),
            pl.BlockSpec((d, tf), lambda g, i, j: (0, j)),
            pl.BlockSpec((tf, d), lambda g, i, j: (j, 0)),
        ],
        out_shape=[
            jax.ShapeDtypeStruct((1, t, d), F32),
            jax.ShapeDtypeStruct((d, f), BF16),
            jax.ShapeDtypeStruct((f, d), BF16),
        ],
        scratch_shapes=[pltpu.VMEM((t, d), BF16), pltpu.VMEM((t, d), F32)],
        compiler_params=_cparams(("arbitrary", "arbitrary", "arbitrary"), 56),
        name="mlp_cast",
    )(x3, mod3, g_pre, g_post, w_up_all, w_down_all)


def _mlp(x3, mod3, g_pre, g_post, w_up, w_down, tm, tf=TF_MLP):
    gn, t, d = x3.shape
    r = mod3.shape[1]
    f = w_up.shape[1]
    return pl.pallas_call(
        functools.partial(_mlp_kernel, d=d),
        grid=(gn, t // tm, f // tf),
        in_specs=[
            pl.BlockSpec((None, tm, d), lambda g, i, j: (g, i, 0)),
            _mod_spec(r, tm, 3 * d, 3),
            pl.BlockSpec((1, d), lambda g, i, j: (0, 0)),
            pl.BlockSpec((1, d), lambda g, i, j: (0, 0)),
            pl.BlockSpec((d, tf), lambda g, i, j: (0, j)),
            pl.BlockSpec((tf, d), lambda g, i, j: (j, 0)),
        ],
        out_specs=pl.BlockSpec((None, tm, d), lambda g, i, j: (g, i, 0)),
        out_shape=jax.ShapeDtypeStruct((gn, t, d), F32),
        scratch_shapes=[pltpu.VMEM((tm, d), BF16), pltpu.VMEM((tm, d), F32)],
        compiler_params=_cparams(("parallel", "parallel", "arbitrary"), 56),
        name="mlp",
    )(x3, mod3, g_pre, g_post, w_up, w_down)


def _s5_disc_kernel(lr_ref, li_ref, ldt_ref, br_ref, bi_ref, pwr_ref, pwi_ref, bbr_ref, bbi_ref, *,
                    seg_len):
    lr, li = lr_ref[...], li_ref[...]
    dt = jnp.exp(ldt_ref[...])
    mag = jnp.exp(lr * dt)
    lb_re, lb_im = mag * jnp.cos(li * dt), mag * jnp.sin(li * dt)
    nr, ni = lb_re - 1.0, lb_im
    den = lr * lr + li * li
    f_re = (nr * lr + ni * li) / den
    f_im = (ni * lr - nr * li) / den
    br, bi = br_ref[...], bi_ref[...]
    bbr_ref[...] = f_re * br - f_im * bi
    bbi_ref[...] = f_re * bi + f_im * br
    pwr_ref[0] = lb_re
    pwi_ref[0] = lb_im
    qr, qi = None, None
    sr, si = lb_re, lb_im
    e = seg_len
    while e:
        if e & 1:
            qr, qi = (sr, si) if qr is None else (qr * sr - qi * si, qr * si + qi * sr)
        e >>= 1
        if e:
            sr, si = sr * sr - si * si, 2.0 * sr * si
    pr, pi = qr, qi
    for n in range(SUBLANES):
        pwr_ref[1 + n] = pr
        pwi_ref[1 + n] = pi
        pr, pi = pr * qr - pi * qi, pr * qi + pi * qr


def _s5_discretize(lam_re, lam_im, log_dt, bt_re, bt_im, seg_len):
    g, _, p = lam_re.shape
    c = bt_re.shape[1]
    return pl.pallas_call(
        functools.partial(_s5_disc_kernel, seg_len=seg_len),
        out_shape=[
            jax.ShapeDtypeStruct((1 + SUBLANES, g, 1, p), F32),
            jax.ShapeDtypeStruct((1 + SUBLANES, g, 1, p), F32),
            jax.ShapeDtypeStruct((g, c, p), F32),
            jax.ShapeDtypeStruct((g, c, p), F32),
        ],
        name="s5_discretize",
    )(lam_re, lam_im, log_dt, bt_re, bt_im)


def _gelu_tanh(x):
    c0 = math.sqrt(2.0 / math.pi)
    return x * (0.5 * (1.0 + jnp.tanh(c0 * (x + 0.044715 * (x * x * x)))))


def _cmul_add(ar, ai, xr, xi, yr, yi):
    return yr + ar * xr - ai * xi, yi + ar * xi + ai * xr


def _s5_seq_kernel(x_ref, mod_ref, gpre_ref, bblk_ref, cblk_ref, lam_ref, dskip_ref, s0r_ref, s0i_ref,
                   z_ref, sr_ref, si_ref, h_scr, xr_scr, xi_scr, *, d):
    cb = pl.program_id(0)
    ncb, tm, uw = h_scr.shape
    cw = xr_scr.shape[1]
    seg = s0r_ref.shape[0]

    @pl.when(cb == 0)
    def _():
        h = _norm_mod(x_ref[...], gpre_ref[...], mod_ref, d)
        for c in range(ncb):
            h_scr[c] = h[:, c * uw:(c + 1) * uw]

    u = h_scr[cb]
    bu = _dot(u.astype(BF16), bblk_ref[...])
    xr_scr[...] = bu[:, 0:cw]
    xi_scr[...] = bu[:, cw:2 * cw]
    car_r, car_i = s0r_ref[...], s0i_ref[...]
    l_r, l_i = lam_ref[0:1, :], lam_ref[1:2, :]
    for t in range(tm // seg):
        rows = slice(t * seg, (t + 1) * seg)
        car_r, car_i = _cmul_add(l_r, l_i, car_r, car_i, xr_scr[rows, :], xi_scr[rows, :])
        xr_scr[rows, :] = car_r
        xi_scr[rows, :] = car_i
    sr_ref[...] = car_r
    si_ref[...] = car_i
    xs = jnp.concatenate([xr_scr[...].astype(BF16), xi_scr[...].astype(BF16)], axis=1)
    y = _dot(xs, cblk_ref[...]) + dskip_ref[...] * u
    z_ref[...] = _gelu_tanh(y).astype(BF16)


def _s5_seq(x2, mod2, g_pre, bblk, cblk, lam2, dskip, s0_re, s0_im):
    tm, d = x2.shape
    seg, nst = s0_re.shape
    ncb, uw, cw2 = bblk.shape
    cw = cw2 // 2
    sspec = pl.BlockSpec((seg, cw), lambda c: (0, c))
    return pl.pallas_call(
        functools.partial(_s5_seq_kernel, d=d),
        grid=(ncb,),
        in_specs=[
            pl.BlockSpec((tm, d), lambda c: (0, 0)),
            pl.BlockSpec((seg, 3 * d), lambda c: (0, 0)),
            pl.BlockSpec((1, d), lambda c: (0, 0)),
            pl.BlockSpec((None, uw, cw2), lambda c: (c, 0, 0)),
            pl.BlockSpec((None, cw2, uw), lambda c: (c, 0, 0)),
            pl.BlockSpec((2, cw), lambda c: (0, c)),
            pl.BlockSpec((1, uw), lambda c: (0, c)),
            sspec, sspec,
        ],
        out_specs=[pl.BlockSpec((tm, uw), lambda c: (0, c)), sspec, sspec],
        out_shape=[
            jax.ShapeDtypeStruct((tm, d), BF16),
            jax.ShapeDtypeStruct((seg, nst), F32),
            jax.ShapeDtypeStruct((seg, nst), F32),
        ],
        scratch_shapes=[
            pltpu.VMEM((ncb, tm, uw), F32),
            pltpu.VMEM((tm, cw), F32),
            pltpu.VMEM((tm, cw), F32),
        ],
        compiler_params=_cparams(("arbitrary",), 48),
        name="s5_seq",
    )(x2, mod2, g_pre, bblk, cblk, lam2, dskip, s0_re, s0_im)


def _s5_rows_kernel(x_ref, mod_ref, gpre_ref, bblk_ref, cblk_ref, tbl_ref, dskip_ref,
                    z_ref, sr_ref, si_ref, h_scr, xr_scr, xi_scr, cr_scr, ci_scr, *, d):
    ncol, tm, _ = xr_scr.shape
    ncb, uw, _ = bblk_ref.shape
    sl = tm // SUBLANES

    @pl.when(pl.program_id(1) == 0)
    def _():
        cr_scr[...] = jnp.zeros_like(cr_scr)
        ci_scr[...] = jnp.zeros_like(ci_scr)

    h_scr[...] = _norm_mod(x_ref[...], gpre_ref[...], mod_ref, d)
    row0 = lax.broadcasted_iota(jnp.int32, (ncol, SUBLANES, LANES), 1) == 0

    for c in range(ncb):
        us = slice(c * uw, (c + 1) * uw)
        cols = slice(c * ncol, (c + 1) * ncol)
        u = h_scr[:, us]
        bu = _dot(u.astype(BF16), bblk_ref[c])
        for j in range(ncol):
            xr_scr[j] = bu[:, j * LANES:(j + 1) * LANES]
            xi_scr[j] = bu[:, (ncol + j) * LANES:(ncol + j + 1) * LANES]
        l_r, l_i = tbl_ref[0, cols], tbl_ref[1, cols]

        def local(i, s):
            rows = pl.ds(pl.multiple_of(i * SUBLANES, SUBLANES), SUBLANES)
            return _cmul_add(l_r, l_i, s[0], s[1], xr_scr[:, rows, :], xi_scr[:, rows, :])

        zero = jnp.zeros((ncol, SUBLANES, LANES), F32)
        g_r, g_i = lax.fori_loop(0, sl, local, (zero, zero), unroll=True)
        for n in range(3):
            g_r, g_i = _cmul_add(tbl_ref[2 + 2 * n, cols], tbl_ref[3 + 2 * n, cols],
                                 pltpu.roll(g_r, 1 << n, 1), pltpu.roll(g_i, 1 << n, 1), g_r, g_i)
        car_r, car_i = cr_scr[cols], ci_scr[cols]
        g_r, g_i = _cmul_add(tbl_ref[8, cols], tbl_ref[9, cols], car_r, car_i, g_r, g_i)
        in_r = jnp.where(row0, car_r, pltpu.roll(g_r, 1, 1))
        in_i = jnp.where(row0, car_i, pltpu.roll(g_i, 1, 1))

        def full(i, s):
            rows = pl.ds(pl.multiple_of(i * SUBLANES, SUBLANES), SUBLANES)
            s_r, s_i = _cmul_add(l_r, l_i, s[0], s[1], xr_scr[:, rows, :], xi_scr[:, rows, :])
            xr_scr[:, rows, :] = s_r
            xi_scr[:, rows, :] = s_i
            return s_r, s_i

        e_r, e_i = lax.fori_loop(0, sl, full, (in_r, in_i), unroll=True)
        cr_scr[cols] = e_r[:, SUBLANES - 1:SUBLANES, :]
        ci_scr[cols] = e_i[:, SUBLANES - 1:SUBLANES, :]
        xs = jnp.concatenate([xr_scr[j].astype(BF16) for j in range(ncol)]
                             + [xi_scr[j].astype(BF16) for j in range(ncol)], axis=1)
        y = _dot(xs, cblk_ref[c]) + dskip_ref[:, us] * u
        z_ref[:, us] = _gelu_tanh(y).astype(BF16)

    sr_ref[...] = cr_scr[...]
    si_ref[...] = ci_scr[...]


def _s5_rows(x3, mod3, g_pre, bblk, cblk, tbl, dskip, tm):
    gn, t, d = x3.shape
    ncb, uw, cw2 = bblk.shape
    ncol = cw2 // 2 // LANES
    nct = ncb * ncol
    nt = t // tm
    const = lambda shape: pl.BlockSpec(shape, lambda g, i: (0,) * len(shape), pipeline_mode=pl.Buffered(1))
    ospec = pl.BlockSpec((None, None, nct, 1, LANES), lambda g, i: (g, i, 0, 0, 0))
    z3, s_re, s_im = pl.pallas_call(
        functools.partial(_s5_rows_kernel, d=d),
        grid=(gn, nt),
        in_specs=[
            pl.BlockSpec((None, tm, d), lambda g, i: (g, i, 0)),
            _mod_spec(1, tm, 3 * d, 2),
            pl.BlockSpec((1, d), lambda g, i: (0, 0)),
            const(bblk.shape), const(cblk.shape), const(tbl.shape), const(dskip.shape),
        ],
        out_specs=[pl.BlockSpec((None, tm, d), lambda g, i: (g, i, 0)), ospec, ospec],
        out_shape=[
            jax.ShapeDtypeStruct((gn, t, d), BF16),
            jax.ShapeDtypeStruct((gn, nt, nct, 1, LANES), F32),
            jax.ShapeDtypeStruct((gn, nt, nct, 1, LANES), F32),
        ],
        scratch_shapes=[
            pltpu.VMEM((tm, d), F32),
            pltpu.VMEM((ncol, tm, LANES), F32),
            pltpu.VMEM((ncol, tm, LANES), F32),
            pltpu.VMEM((nct, 1, LANES), F32),
            pltpu.VMEM((nct, 1, LANES), F32),
        ],
        compiler_params=_cparams(("parallel", "arbitrary"), 48),
        name="s5_rows",
    )(x3, mod3, g_pre, bblk, cblk, tbl, dskip)
    return z3, s_re[:, nt - 1].reshape(gn, 1, nct * LANES), s_im[:, nt - 1].reshape(gn, 1, nct * LANES)


def _glu_kernel(x_ref, mod_ref, g_ref, z_ref, wa_ref, wb_ref, o_ref, *, d):
    z = z_ref[...]
    y = _dot(z, wa_ref[...]) * jax.nn.sigmoid(_dot(z, wb_ref[...]))
    o_ref[...] = _gated_residual(x_ref[...], y, g_ref[...], mod_ref, d)


def _glu(x3, mod3, g_post, z3, w_a, w_b, tm):
    gn, t, d = x3.shape
    r = mod3.shape[1]
    wspec = pl.BlockSpec((d, d), lambda g, i: (0, 0), pipeline_mode=pl.Buffered(1))
    return pl.pallas_call(
        functools.partial(_glu_kernel, d=d),
        grid=(gn, t // tm),
        in_specs=[
            pl.BlockSpec((None, tm, d), lambda g, i: (g, i, 0)),
            _mod_spec(r, tm, 3 * d, 2),
            pl.BlockSpec((1, d), lambda g, i: (0, 0)),
            pl.BlockSpec((None, tm, d), lambda g, i: (g, i, 0)),
            wspec, wspec,
        ],
        out_specs=pl.BlockSpec((None, tm, d), lambda g, i: (g, i, 0)),
        out_shape=jax.ShapeDtypeStruct((gn, t, d), F32),
        compiler_params=_cparams(("parallel", "parallel"), 52),
        name="glu",
    )(x3, mod3, g_post, z3, w_a, w_b)


def _rope_tables(pos):
    half = HEAD_DK // 2
    inv = ROPE_BASE ** (-jnp.arange(half, dtype=F32) / half)
    ang = pos.astype(F32)[:, None] * inv[None, :]
    cos, sin = jnp.cos(ang), jnp.sin(ang)
    return jnp.concatenate([cos, cos], axis=-1), jnp.concatenate([-sin, sin], axis=-1)


def _block_diag(w):
    ncb, gpb, a, b = w.shape
    eye = jnp.eye(gpb, dtype=w.dtype)
    return jnp.einsum("ngab,gh->ngahb", w, eye).reshape(ncb, gpb * a, gpb * b)


def _s5_tables(pw_re, pw_im):
    n = pw_re.shape[0]
    flat_r = pw_re.reshape(n, -1)
    flat_i = pw_im.reshape(n, -1)
    row = jnp.arange(SUBLANES)[:, None]
    tabs = [jnp.broadcast_to(flat_r[0], (SUBLANES, flat_r.shape[1])),
            jnp.broadcast_to(flat_i[0], (SUBLANES, flat_i.shape[1]))]
    for s in (1, 2, 4):
        mask = row >= s
        tabs.append(jnp.where(mask, flat_r[s][None, :], 0.0))
        tabs.append(jnp.where(mask, flat_i[s][None, :], 0.0))
    tabs += [flat_r[1:], flat_i[1:]]
    tbl = jnp.stack(tabs)
    tbl = tbl.reshape(tbl.shape[0], SUBLANES, -1, LANES).transpose(0, 2, 1, 3)
    return tbl, jnp.stack([flat_r[0], flat_i[0]])


def kernel(x_prompt, x_sample, state_gla, state_ret, state_s5_re, state_s5_im, c_prompt, c_sample,
           w_ada, b_ada, norm_pre, norm_post, w_in_mix, w_gla_gk, b_gla_gk, gla_head_norm,
           ret_head_norm, w_out_mix, s5_lam_re, s5_lam_im, s5_log_dt, s5_b_re, s5_b_im,
           s5_c_re, s5_c_im, s5_d, w_glu_a, w_glu_b, w_mlp_up, w_mlp_down):
    bp, tp, d = x_prompt.shape
    bs, ts, _ = x_sample.shape
    depth = w_ada.shape[0]
    gkey = GLA_HEADS * HEAD_DK
    gval = GLA_HEADS * HEAD_DV

    nrow = -(-(bs + bp) // SUBLANES) * SUBLANES
    c_all = jnp.concatenate([c_sample, c_prompt, jnp.zeros((nrow - bs - bp, d), F32)], axis=0)
    mod_all = _adaln(c_all, w_ada.reshape(depth * 2, d, 3 * d), b_ada.reshape(depth * 2, 1, 3 * d))
    mod_s = [mod_all[k, 0:bs][None] for k in range(depth * 2)]
    mod_p = [mod_all[k, bs:bs + bp][:, None, :] for k in range(depth * 2)]

    w_in = w_in_mix[0]
    sec = 2 * gkey + 2 * gval
    w_main = jnp.concatenate([w_in[:, :sec], w_in[:, sec + GLA_RANK:]], axis=1).astype(BF16)
    w_main = jnp.swapaxes(w_main.reshape(d, -1, TN_INPROJ), 0, 1)
    w_lr = jnp.pad(w_in[:, sec:sec + GLA_RANK], ((0, 0), (0, LANES - GLA_RANK))).astype(BF16)
    w_gk = jnp.pad(w_gla_gk[0], ((0, LANES - GLA_RANK), (0, 0))).astype(BF16)
    b_gk = b_gla_gk[0][None, :]
    w_out = w_out_mix[0].astype(BF16)
    w_ga = w_glu_a[0].astype(BF16)
    w_gb = w_glu_b[0].astype(BF16)
    gla_norm = gla_head_norm[0][:, None, :]
    ret_norm = ret_head_norm[0][:, None, :]
    gamma_log = jnp.log1p(-jnp.power(2.0, -5.0 - jnp.arange(RET_HEADS, dtype=F32)))
    ret_lg = jnp.broadcast_to(gamma_log[:, None, None], (RET_HEADS, 1, LANES))

    ng = s5_lam_re.shape[1]
    ncb = ng // S5_GPB
    tm5 = min(TM_S5, tp)
    pw_re, pw_im, bbt_re, bbt_im = _s5_discretize(
        s5_lam_re[0][:, None, :], s5_lam_im[0][:, None, :], s5_log_dt[0][:, None, None],
        jnp.swapaxes(s5_b_re[0], 1, 2), jnp.swapaxes(s5_b_im[0], 1, 2), tm5 // SUBLANES)
    bblk = jnp.concatenate([
        _block_diag(bbt_re.reshape(ncb, S5_GPB, S5_GROUP, S5_STATE)),
        _block_diag(bbt_im.reshape(ncb, S5_GPB, S5_GROUP, S5_STATE))], axis=2).astype(BF16)
    ct_re = jnp.swapaxes(s5_c_re[0], 1, 2).reshape(ncb, S5_GPB, S5_STATE, S5_GROUP)
    ct_im = jnp.swapaxes(s5_c_im[0], 1, 2).reshape(ncb, S5_GPB, S5_STATE, S5_GROUP)
    cblk = jnp.concatenate([_block_diag(ct_re), _block_diag(-ct_im)], axis=1).astype(BF16)
    tbl, lam2 = _s5_tables(pw_re[:, :, 0, :], pw_im[:, :, 0, :])
    dskip = s5_d[0][None, :]

    def trunk(x3, mods, tm, attn, s5, mlp):
        proj, glog = _inproj(x3, mods[0], norm_pre[0, 0][None], w_main, w_lr, w_gk, b_gk, tm)
        mg, mr, s_gla, s_ret = attn(proj, glog)
        x3 = _outproj(x3, mods[0], norm_post[0, 0][None], mg, mr, w_out, tm)
        x3 = mlp(0, x3, mods[1])
        z3, s_re, s_im = s5(x3, mods[2])
        x3 = _glu(x3, mods[2], norm_post[1, 0][None], z3, w_ga, w_gb, tm)
        x3 = mlp(1, x3, mods[3])
        return x3, s_gla, s_ret, s_re, s_im

    pos_s = PAST_LEN + jnp.arange(T_PAD, dtype=F32)
    cos_s, sin_s = _rope_tables(pos_s)

    def attn_s(proj, glog):
        def to_bm(a):
            a = jnp.swapaxes(a.reshape(ts, bs, a.shape[-1]), 0, 1)
            return jnp.pad(a, ((0, 0), (0, T_PAD - ts), (0, 0)))
        mg, mr, s_gla, s_ret = _attn_sample(to_bm(proj), to_bm(glog), gla_norm, ret_norm, cos_s, sin_s,
                                            ret_lg, state_gla[0], state_ret[0], ts)
        to_tm = lambda a: jnp.swapaxes(a[:, :ts], 0, 1).reshape(1, ts * bs, a.shape[-1])
        return to_tm(mg), to_tm(mr), s_gla, s_ret

    xs3 = jnp.swapaxes(x_sample, 0, 1).reshape(1, ts * bs, d)

    def s5_s(x3, mod3):
        z2, s_re, s_im = _s5_seq(x3[0], mod3[0], norm_pre[1, 0][None], bblk, cblk, lam2, dskip,
                                 state_s5_re[0].reshape(bs, -1), state_s5_im[0].reshape(bs, -1))
        return z2[None], s_re, s_im

    w_up, w_dn = {}, {}

    def mlp_s(l, x3, mod3):
        x3, w_up[l], w_dn[l] = _mlp_cast(x3, mod3, norm_pre[l, 1][None], norm_post[l, 1][None],
                                         w_mlp_up, w_mlp_down, l)
        return x3

    y_s, gla_s, ret_s, re_s, im_s = trunk(xs3, mod_s, ts * bs, attn_s, s5_s, mlp_s)
    y_s = jnp.swapaxes(y_s.reshape(ts, bs, d), 0, 1)

    cos_p, sin_p = _rope_tables(jnp.arange(tp, dtype=F32))
    zeros_att = jnp.zeros((bp, GLA_HEADS, HEAD_DK, HEAD_DV), F32)
    tm_p = min(TM_DENSE, tp)

    def attn_p(proj, glog):
        return _attn_prompt(proj, glog, gla_norm, ret_norm, cos_p, sin_p, ret_lg, zeros_att, zeros_att)

    def s5_p(x3, mod3):
        sl = tm5 // SUBLANES
        xp = jnp.swapaxes(x3.reshape(bp, tp // tm5, SUBLANES, sl, d), 2, 3).reshape(bp, tp, d)
        zp, s_re, s_im = _s5_rows(xp, mod3, norm_pre[1, 0][None], bblk, cblk, tbl, dskip, tm5)
        z3 = jnp.swapaxes(zp.reshape(bp, tp // tm5, sl, SUBLANES, d), 2, 3).reshape(bp, tp, d)
        return z3, s_re, s_im

    def mlp_p(l, x3, mod3):
        return _mlp(x3, mod3, norm_pre[l, 1][None], norm_post[l, 1][None], w_up[l], w_dn[l], tm_p)

    y_p, gla_p, ret_p, re_p, im_p = trunk(x_prompt, mod_p, tm_p, attn_p, s5_p, mlp_p)

    st = lambda a, b_: a.reshape(1, b_, ng, S5_STATE)
    return (y_p, y_s, gla_p[None], gla_s[None], ret_p[None], ret_s[None],
            st(re_p, bp), st(re_s, bs), st(im_p, bp), st(im_s, bs))
```

```python
import functools
import math

import jax
import jax.numpy as jnp
import numpy as np
from jax import lax
from jax.experimental import pallas as pl
from jax.experimental.pallas import tpu as pltpu

F32 = jnp.float32
BF16 = jnp.bfloat16

EPS = 1e-6
LANES = 128
SUBLANES = 8
MIB = 1024 * 1024

GLA_HEADS = 4
RET_HEADS = 4
HEAD_DK = 128
HEAD_DV = 256
GLA_RANK = 16
GLA_LOGIT_NORM = 16.0
ROPE_BASE = 10000.0
PAST_LEN = 16384
S5_GROUP = 16
S5_STATE = 64
S5_GPB = 16
S5_UW = S5_GPB * S5_GROUP
ATT_CHUNK = 128
GLA_SUB = 16
T_PAD = 8
TM_DENSE = 512
TM_S5 = 256
TN_INPROJ = 1024
TF_MLP = 1024


def _cparams(sem, vmem_mib):
    return pltpu.CompilerParams(dimension_semantics=sem, vmem_limit_bytes=vmem_mib * MIB)


def _dot(a, b):
    return jnp.dot(a, b, preferred_element_type=F32)


def _dot_nt(a, b):
    return lax.dot_general(a, b, (((1,), (1,)), ((), ())), preferred_element_type=F32)


def _rms(x, g):
    return x * lax.rsqrt(jnp.mean(x * x, axis=-1, keepdims=True) + EPS) * g


def _rows_affine(y, a, b=None):
    tm, d = y.shape
    r = a.shape[0]
    if r == 1 or r == tm:
        out = y * a
        return out if b is None else out + b
    y3 = y.reshape(tm // r, r, d)
    out = y3 * a[None]
    if b is not None:
        out = out + b[None]
    return out.reshape(tm, d)


def _norm_mod(x, g, mod_ref, d):
    return _rows_affine(_rms(x, g), 1.0 + mod_ref[:, d:2 * d], mod_ref[:, 0:d])


def _gated_residual(x, y, g, mod_ref, d):
    return x + _rows_affine(_rms(y, g), mod_ref[:, 2 * d:3 * d])


def _mod_spec(r, tm, width, ngrid):
    if ngrid == 2:
        if r == 1:
            return pl.BlockSpec((None, 1, width), lambda g, i: (g, 0, 0))
        return pl.BlockSpec((None, r, width), lambda g, i: (g, 0, 0))
    if r == 1:
        return pl.BlockSpec((None, 1, width), lambda g, i, j: (g, 0, 0))
    return pl.BlockSpec((None, r, width), lambda g, i, j: (g, 0, 0))


def _adaln_kernel(c_ref, w_ref, b_ref, o_ref):
    c = c_ref[...]
    sc = (c * jax.nn.sigmoid(c)).astype(BF16)
    o_ref[...] = _dot(sc, w_ref[...].astype(BF16)) + b_ref[...]


def _adaln(c_all, w_ada, b_ada, tn=512):
    ls, d, n = w_ada.shape
    rows = c_all.shape[0]
    return pl.pallas_call(
        _adaln_kernel,
        grid=(ls, n // tn),
        in_specs=[
            pl.BlockSpec((rows, d), lambda l, j: (0, 0)),
            pl.BlockSpec((None, d, tn), lambda l, j: (l, 0, j)),
            pl.BlockSpec((None, 1, tn), lambda l, j: (l, 0, j)),
        ],
        out_specs=pl.BlockSpec((None, rows, tn), lambda l, j: (l, 0, j)),
        out_shape=jax.ShapeDtypeStruct((ls, rows, n), F32),
        compiler_params=_cparams(("parallel", "parallel"), 40),
        name="adaln",
    )(c_all, w_ada, b_ada)


def _log_sigmoid(x):
    return jnp.minimum(x, 0.0) - jnp.log1p(jnp.exp(-jnp.abs(x)))


def _inproj_prologue(x_ref, mod_ref, g_ref, wlr_ref, wgk_ref, bgk_ref, glog_ref, h_scr, d):
    @pl.when(pl.program_id(2) == 0)
    def _():
        hb = _norm_mod(x_ref[...], g_ref[...], mod_ref, d).astype(BF16)
        h_scr[...] = hb
        glr = _dot(hb, wlr_ref[...])
        logit = _dot(glr.astype(BF16), wgk_ref[...]) + bgk_ref[...]
        glog_ref[...] = _log_sigmoid(logit) * (1.0 / GLA_LOGIT_NORM)


def _inproj_kernel(x_ref, mod_ref, g_ref, w_ref, wlr_ref, wgk_ref, bgk_ref,
                   proj_ref, glog_ref, h_scr, *, d):
    _inproj_prologue(x_ref, mod_ref, g_ref, wlr_ref, wgk_ref, bgk_ref, glog_ref, h_scr, d)
    proj_ref[...] = _dot(h_scr[...], w_ref[pl.program_id(2)])


def _inproj_cast_kernel(x_ref, mod_ref, g_ref, wa_ref, wb_ref, wlr_ref, wgk_ref, bgk_ref,
                        proj_ref, glog_ref, wout_ref, h_scr, *, d, n_lo):
    j = pl.program_id(2)

    @pl.when(j < n_lo)
    def _():
        wout_ref[...] = wa_ref[...].astype(BF16)

    @pl.when(j >= n_lo)
    def _():
        w = jnp.concatenate([wa_ref[:, GLA_RANK:], wb_ref[:, :GLA_RANK]], axis=1)
        wout_ref[...] = w.astype(BF16)

    _inproj_prologue(x_ref, mod_ref, g_ref, wlr_ref, wgk_ref, bgk_ref, glog_ref, h_scr, d)
    proj_ref[...] = _dot(h_scr[...], wout_ref[...])


def _inproj_cast(x3, mod3, g_pre, w_raw, layer, w_lr, w_gk, b_gk, tn=TN_INPROJ):
    gn, t, d = x3.shape
    assert gn == 1
    sec = (w_raw.shape[2] - GLA_RANK) // 2
    assert sec % tn == 0 and tn % LANES == 0
    nj = 2 * sec // tn
    n = nj * tn
    r = mod3.shape[1]
    gkey = w_gk.shape[1]
    return pl.pallas_call(
        functools.partial(_inproj_cast_kernel, d=d, n_lo=sec // tn),
        grid=(1, 1, nj),
        in_specs=[
            pl.BlockSpec((None, t, d), lambda g, i, j: (0, 0, 0)),
            _mod_spec(r, t, 3 * d, 3),
            pl.BlockSpec((1, d), lambda g, i, j: (0, 0)),
            pl.BlockSpec((None, d, tn), lambda g, i, j: (layer, 0, j)),
            pl.BlockSpec((None, d, LANES), lambda g, i, j: (layer, 0, (j + 1) * (tn // LANES))),
            pl.BlockSpec((d, LANES), lambda g, i, j: (0, 0)),
            pl.BlockSpec((LANES, gkey), lambda g, i, j: (0, 0)),
            pl.BlockSpec((1, gkey), lambda g, i, j: (0, 0)),
        ],
        out_specs=[
            pl.BlockSpec((None, t, tn), lambda g, i, j: (0, 0, j)),
            pl.BlockSpec((None, t, gkey), lambda g, i, j: (0, 0, 0)),
            pl.BlockSpec((None, d, tn), lambda g, i, j: (j, 0, 0)),
        ],
        out_shape=[
            jax.ShapeDtypeStruct((1, t, n), F32),
            jax.ShapeDtypeStruct((1, t, gkey), F32),
            jax.ShapeDtypeStruct((nj, d, tn), BF16),
        ],
        scratch_shapes=[pltpu.VMEM((t, d), BF16)],
        compiler_params=_cparams(("arbitrary", "arbitrary", "arbitrary"), 56),
        name="inproj_cast",
    )(x3, mod3, g_pre, w_raw, w_raw, w_lr, w_gk, b_gk)


def _inproj(x3, mod3, g_pre, w_main, w_lr, w_gk, b_gk, tm):
    gn, t, d = x3.shape
    nj, _, tn = w_main.shape
    n = nj * tn
    r = mod3.shape[1]
    gkey = w_gk.shape[1]
    return pl.pallas_call(
        functools.partial(_inproj_kernel, d=d),
        grid=(gn, t // tm, n // tn),
        in_specs=[
            pl.BlockSpec((None, tm, d), lambda g, i, j: (g, i, 0)),
            _mod_spec(r, tm, 3 * d, 3),
            pl.BlockSpec((1, d), lambda g, i, j: (0, 0)),
            pl.BlockSpec((nj, d, tn), lambda g, i, j: (0, 0, 0), pipeline_mode=pl.Buffered(1)),
            pl.BlockSpec((d, LANES), lambda g, i, j: (0, 0)),
            pl.BlockSpec((LANES, gkey), lambda g, i, j: (0, 0)),
            pl.BlockSpec((1, gkey), lambda g, i, j: (0, 0)),
        ],
        out_specs=[
            pl.BlockSpec((None, tm, tn), lambda g, i, j: (g, i, j)),
            pl.BlockSpec((None, tm, gkey), lambda g, i, j: (g, i, 0)),
        ],
        out_shape=[
            jax.ShapeDtypeStruct((gn, t, n), F32),
            jax.ShapeDtypeStruct((gn, t, gkey), F32),
        ],
        scratch_shapes=[pltpu.VMEM((tm, d), BF16)],
        compiler_params=_cparams(("parallel", "parallel", "arbitrary"), 56),
        name="inproj",
    )(x3, mod3, g_pre, w_main, w_lr, w_gk, b_gk)


def _cumsum_rows(g):
    c = g.shape[0]
    row = lax.broadcasted_iota(jnp.int32, g.shape, 0)
    s = 1
    while s < c:
        g = g + jnp.where(row >= s, pltpu.roll(g, s, 0), 0.0)
        s *= 2
    return g


def _pad_rows(a, rows):
    if a.shape[0] == rows:
        return a
    return jnp.concatenate([a, jnp.zeros((rows - a.shape[0], a.shape[1]), a.dtype)], axis=0)


def _col_bcast(row, width):
    sq = jnp.transpose(jnp.broadcast_to(row, (LANES, LANES)))
    return jnp.concatenate([sq] * (width // LANES), axis=1)


def _gla_core(q, k, v, g, s, sub):
    cq = q.shape[0]
    ck = max(cq, LANES)
    b = _cumsum_rows(g)
    be = b - g
    bk = _pad_rows(b, ck)
    kp = _pad_rows(k, ck)
    vp = _pad_rows(v, ck).astype(BF16)
    rowj = lax.broadcasted_iota(jnp.int32, (ck, 1), 0)
    att_rows = []
    for blk in range(cq // sub):
        lo, hi = blk * sub, (blk + 1) * sub
        base = be[lo:lo + 1, :]
        qs = q[lo:hi] * jnp.exp(b[lo:hi] - base)
        ks = jnp.where(rowj < hi, kp * jnp.exp(base - bk), 0.0)
        att_rows.append(_dot_nt(qs.astype(BF16), ks.astype(BF16)))
    att = att_rows[0] if len(att_rows) == 1 else jnp.concatenate(att_rows, axis=0)
    ri = lax.broadcasted_iota(jnp.int32, (cq, ck), 0)
    cj = lax.broadcasted_iota(jnp.int32, (cq, ck), 1)
    att = jnp.where(ri >= cj, att, 0.0)
    o = _dot(att.astype(BF16), vp) + _dot((q * jnp.exp(b)).astype(BF16), s.astype(BF16))
    b_last = b[cq - 1:cq, :]
    k_out = kp * jnp.exp(b_last - bk)
    s_new = s * _col_bcast(jnp.exp(b_last), s.shape[1]) + _dot(jnp.transpose(k_out).astype(BF16), vp)
    return o, s_new


def _ret_core(q, k, v, s, lg, dmat, valid):
    cq = q.shape[0]
    ck = max(cq, LANES)
    kp = _pad_rows(k, ck)
    vp = _pad_rows(v, ck).astype(BF16)
    ti = lax.broadcasted_iota(jnp.int32, (cq, 1), 0).astype(F32)
    tj = lax.broadcasted_iota(jnp.int32, (ck, 1), 0).astype(F32)
    att = _dot_nt(q.astype(BF16), kp.astype(BF16)) * dmat
    q_in = q * jnp.exp((ti + 1.0) * lg)
    o = _dot(att.astype(BF16), vp) + _dot(q_in.astype(BF16), s.astype(BF16))
    k_out = kp * jnp.exp((float(valid - 1) - tj) * lg)
    s_new = s * jnp.exp(float(valid) * lg) + _dot(jnp.transpose(k_out).astype(BF16), vp)
    return o, s_new


def _decay_matrix(cq, ck, lg):
    ri = lax.broadcasted_iota(jnp.int32, (cq, ck), 0)
    cj = lax.broadcasted_iota(jnp.int32, (cq, ck), 1)
    diff = (ri - cj).astype(F32)
    return jnp.where(ri >= cj, jnp.exp(diff * lg), 0.0)


def _rope(x, cosf, sinf):
    return x * cosf + pltpu.roll(x, x.shape[1] // 2, 1) * sinf


def _silu(x):
    return x * jax.nn.sigmoid(x)


def _gla_finish(o, gate, gn):
    o = o * lax.rsqrt(jnp.mean(o * o, axis=-1, keepdims=True) + EPS) * gn
    return (o * _silu(gate)).astype(BF16)


def _ret_finish(o, gate, gn):
    oc = o - jnp.mean(o, axis=-1, keepdims=True)
    oc = oc * lax.rsqrt(jnp.mean(oc * oc, axis=-1, keepdims=True) + EPS) * gn
    return (oc * _silu(gate)).astype(BF16)


def _head(ref, h, width):
    return ref[:, h * width:(h + 1) * width]


def _gla_prompt_kernel(q_ref, k_ref, v_ref, gg_ref, gl_ref, gn_ref, s0_ref, o_ref, s_ref):
    @pl.when(pl.program_id(1) == 0)
    def _():
        s_ref[...] = s0_ref[...]

    for h in range(s_ref.shape[0]):
        q = _head(q_ref, h, HEAD_DK) * (HEAD_DK ** -0.5)
        o, s_new = _gla_core(q, _head(k_ref, h, HEAD_DK), _head(v_ref, h, HEAD_DV),
                             _head(gl_ref, h, HEAD_DK), s_ref[h], GLA_SUB)
        s_ref[h] = s_new
        o_ref[:, h * HEAD_DV:(h + 1) * HEAD_DV] = _gla_finish(o, _head(gg_ref, h, HEAD_DV), gn_ref[h])


def _ret_prompt_kernel(q_ref, k_ref, v_ref, rg_ref, cos_ref, sin_ref, lg_ref, gn_ref, s0_ref,
                       o_ref, s_ref, d_scr):
    @pl.when(pl.program_id(1) == 0)
    def _():
        s_ref[...] = s0_ref[...]
        for h in range(d_scr.shape[0]):
            d_scr[h] = _decay_matrix(d_scr.shape[1], d_scr.shape[2], lg_ref[h][:, 0:1])

    cosf, sinf = cos_ref[...], sin_ref[...]
    for h in range(s_ref.shape[0]):
        q = _rope(_head(q_ref, h, HEAD_DK), cosf, sinf)
        k = _rope(_head(k_ref, h, HEAD_DK), cosf, sinf) * (HEAD_DK ** -0.5)
        o, s_new = _ret_core(q, k, _head(v_ref, h, HEAD_DV), s_ref[h], lg_ref[h][:, 0:1], d_scr[h],
                             q.shape[0])
        s_ref[h] = s_new
        o_ref[:, h * HEAD_DV:(h + 1) * HEAD_DV] = _ret_finish(o, _head(rg_ref, h, HEAD_DV), gn_ref[h])


def _attn_prompt(proj, glog, gla_norm, ret_norm, cosf, sinf, ret_lg, s0_gla, s0_ret):
    bsz, t, _ = proj.shape
    c = ATT_CHUNK
    nh = GLA_HEADS
    grid = (bsz, t // c)
    kw, vw = nh * HEAD_DK, nh * HEAD_DV
    kspec = lambda blk: pl.BlockSpec((None, c, kw), lambda b, i, blk=blk: (b, i, blk))
    vspec = lambda blk: pl.BlockSpec((None, c, vw), lambda b, i, blk=blk: (b, i, blk))
    hspec = pl.BlockSpec((nh, 1, HEAD_DV), lambda b, i: (0, 0, 0))
    sspec = pl.BlockSpec((None, nh, HEAD_DK, HEAD_DV), lambda b, i: (b, 0, 0, 0))
    ospec = pl.BlockSpec((None, c, vw), lambda b, i: (b, i, 0))
    out_shape = [
        jax.ShapeDtypeStruct((bsz, t, vw), BF16),
        jax.ShapeDtypeStruct((bsz, nh, HEAD_DK, HEAD_DV), F32),
    ]
    params = _cparams(("parallel", "arbitrary"), 32)
    mg, s_gla = pl.pallas_call(
        _gla_prompt_kernel,
        grid=grid,
        in_specs=[kspec(0), kspec(1), vspec(1), vspec(2), kspec(0), hspec, sspec],
        out_specs=[ospec, sspec],
        out_shape=out_shape,
        compiler_params=params,
        name="gla_prompt",
    )(proj, proj, proj, proj, glog, gla_norm, s0_gla)
    tspec = pl.BlockSpec((c, HEAD_DK), lambda b, i: (i, 0))
    mr, s_ret = pl.pallas_call(
        _ret_prompt_kernel,
        grid=grid,
        in_specs=[kspec(6), kspec(7), vspec(4), vspec(5), tspec, tspec,
                  pl.BlockSpec((nh, 1, LANES), lambda b, i: (0, 0, 0)), hspec, sspec],
        out_specs=[ospec, sspec],
        out_shape=out_shape,
        scratch_shapes=[pltpu.VMEM((nh, c, c), F32)],
        compiler_params=params,
        name="ret_prompt",
    )(proj, proj, proj, proj, cosf, sinf, ret_lg, ret_norm, s0_ret)
    return mg, mr, s_gla, s_ret


def _gla_sample_kernel(q_ref, k_ref, v_ref, gg_ref, gl_ref, gn_ref, s0_ref, o_ref, s_ref):
    for bi in range(q_ref.shape[0]):
        q = q_ref[bi] * (HEAD_DK ** -0.5)
        o, s_new = _gla_core(q, k_ref[bi], v_ref[bi], gl_ref[bi], s0_ref[bi], q.shape[0])
        s_ref[bi] = s_new
        o_ref[bi] = _gla_finish(o, gg_ref[bi], gn_ref[...])


def _ret_sample_kernel(q_ref, k_ref, v_ref, rg_ref, cos_ref, sin_ref, lg_ref, gn_ref, s0_ref,
                       o_ref, s_ref, *, valid):
    lg = lg_ref[:, 0:1]
    cq = q_ref.shape[1]
    dmat = _decay_matrix(cq, max(cq, LANES), lg)
    cosf, sinf = cos_ref[...], sin_ref[...]
    for bi in range(q_ref.shape[0]):
        q = _rope(q_ref[bi], cosf, sinf)
        k = _rope(k_ref[bi], cosf, sinf) * (HEAD_DK ** -0.5)
        o, s_new = _ret_core(q, k, v_ref[bi], s0_ref[bi], lg, dmat, valid)
        s_ref[bi] = s_new
        o_ref[bi] = _ret_finish(o, rg_ref[bi], gn_ref[...])


def _attn_sample(proj, glog, gla_norm, ret_norm, cosf, sinf, ret_lg, s0_gla, s0_ret, valid, bb=8):
    bsz, tp, _ = proj.shape
    grid = (bsz // bb, GLA_HEADS)
    nk = GLA_HEADS
    kspec = lambda off: pl.BlockSpec((bb, tp, HEAD_DK), lambda i, h, off=off: (i, 0, off + h))
    vspec = lambda off: pl.BlockSpec((bb, tp, HEAD_DV), lambda i, h, off=off: (i, 0, off + h))
    hspec = pl.BlockSpec((None, 1, HEAD_DV), lambda i, h: (h, 0, 0))
    sspec = pl.BlockSpec((bb, None, HEAD_DK, HEAD_DV), lambda i, h: (i, h, 0, 0))
    ospec = pl.BlockSpec((bb, tp, HEAD_DV), lambda i, h: (i, 0, h))
    out_shape = [
        jax.ShapeDtypeStruct((bsz, tp, GLA_HEADS * HEAD_DV), BF16),
        jax.ShapeDtypeStruct((bsz, GLA_HEADS, HEAD_DK, HEAD_DV), F32),
    ]
    params = _cparams(("parallel", "parallel"), 32)
    mg, s_gla = pl.pallas_call(
        _gla_sample_kernel,
        grid=grid,
        in_specs=[kspec(0), kspec(nk), vspec(nk), vspec(2 * nk),
                  pl.BlockSpec((bb, tp, HEAD_DK), lambda i, h: (i, 0, h)),
                  hspec, sspec],
        out_specs=[ospec, sspec],
        out_shape=out_shape,
        compiler_params=params,
        name="gla_sample",
    )(proj, proj, proj, proj, glog, gla_norm, s0_gla)
    tspec = pl.BlockSpec((tp, HEAD_DK), lambda i, h: (0, 0))
    mr, s_ret = pl.pallas_call(
        functools.partial(_ret_sample_kernel, valid=valid),
        grid=grid,
        in_specs=[kspec(6 * nk), kspec(7 * nk), vspec(4 * nk), vspec(5 * nk),
                  tspec, tspec,
                  pl.BlockSpec((None, 1, LANES), lambda i, h: (h, 0, 0)),
                  hspec, sspec],
        out_specs=[ospec, sspec],
        out_shape=out_shape,
        compiler_params=params,
        name="ret_sample",
    )(proj, proj, proj, proj, cosf, sinf, ret_lg, ret_norm, s0_ret)
    return mg, mr, s_gla, s_ret


def _outproj_kernel(x_ref, mod_ref, g_ref, mg_ref, mr_ref, wo_ref, o_ref, *, d):
    half = mg_ref.shape[1]
    y = _dot(mg_ref[...], wo_ref[0:half, :]) + _dot(mr_ref[...], wo_ref[half:2 * half, :])
    o_ref[...] = _gated_residual(x_ref[...], y, g_ref[...], mod_ref, d)


def _outproj(x3, mod3, g_post, mg, mr, w_out, tm):
    gn, t, d = x3.shape
    r = mod3.shape[1]
    half = mg.shape[2]
    return pl.pallas_call(
        functools.partial(_outproj_kernel, d=d),
        grid=(gn, t // tm),
        in_specs=[
            pl.BlockSpec((None, tm, d), lambda g, i: (g, i, 0)),
            _mod_spec(r, tm, 3 * d, 2),
            pl.BlockSpec((1, d), lambda g, i: (0, 0)),
            pl.BlockSpec((None, tm, half), lambda g, i: (g, i, 0)),
            pl.BlockSpec((None, tm, half), lambda g, i: (g, i, 0)),
            pl.BlockSpec((2 * half, d), lambda g, i: (0, 0)),
        ],
        out_specs=pl.BlockSpec((None, tm, d), lambda g, i: (g, i, 0)),
        out_shape=jax.ShapeDtypeStruct((gn, t, d), F32),
        compiler_params=_cparams(("parallel", "parallel"), 52),
        name="outproj",
    )(x3, mod3, g_post, mg, mr, w_out)


def _mlp_kernel(x_ref, mod_ref, gpre_ref, gpost_ref, wup_ref, wdn_ref, o_ref, h_scr, acc_scr, *, d):
    j = pl.program_id(2)

    @pl.when(j == 0)
    def _():
        h_scr[...] = _norm_mod(x_ref[...], gpre_ref[...], mod_ref, d).astype(BF16)
        acc_scr[...] = jnp.zeros_like(acc_scr)

    u = jnp.maximum(_dot(h_scr[...], wup_ref[...]), 0.0)
    acc_scr[...] += _dot((u * u).astype(BF16), wdn_ref[...])

    @pl.when(j == pl.num_programs(2) - 1)
    def _():
        o_ref[...] = _gated_residual(x_ref[...], acc_scr[...], gpost_ref[...], mod_ref, d)


def _mlp_cast_kernel(x_ref, mod_ref, gpre_ref, gpost_ref, wup_ref, wdn_ref,
                     o_ref, wupb_ref, wdnb_ref, h_scr, acc_scr, *, d):
    wupb_ref[...] = wup_ref[...].astype(BF16)
    wdnb_ref[...] = wdn_ref[...].astype(BF16)
    _mlp_kernel(x_ref, mod_ref, gpre_ref, gpost_ref, wupb_ref, wdnb_ref, o_ref, h_scr, acc_scr, d=d)


def _mlp_cast(x3, mod3, g_pre, g_post, w_up_all, w_down_all, layer, tf=512):
    gn, t, d = x3.shape
    assert gn == 1
    r = mod3.shape[1]
    f = w_up_all.shape[2]
    return pl.pallas_call(
        functools.partial(_mlp_cast_kernel, d=d),
        grid=(1, 1, f // tf),
        in_specs=[
            pl.BlockSpec((None, t, d), lambda g, i, j: (0, 0, 0)),
            _mod_spec(r, t, 3 * d, 3),
            pl.BlockSpec((1, d), lambda g, i, j: (0, 0)),
            pl.BlockSpec((1, d), lambda g, i, j: (0, 0)),
            pl.BlockSpec((None, d, tf), lambda g, i, j: (layer, 0, j)),
            pl.BlockSpec((None, tf, d), lambda g, i, j: (layer, j, 0)),
        ],
        out_specs=[
            pl.BlockSpec((None, t, d), lambda g, i, j: (0, 0, 0)),
            pl.BlockSpec((d, tf), lambda g, i, j: (0, j)),
            pl.BlockSpec((tf, d), lambda g, i, j: (j, 0)),
        ],
        out_shape=[
            jax.ShapeDtypeStruct((1, t, d), F32),
            jax.ShapeDtypeStruct((d, f), BF16),
            jax.ShapeDtypeStruct((f, d), BF16),
        ],
        scratch_shapes=[pltpu.VMEM((t, d), BF16), pltpu.VMEM((t, d), F32)],
        compiler_params=_cparams(("arbitrary", "arbitrary", "arbitrary"), 56),
        name="mlp_cast",
    )(x3, mod3, g_pre, g_post, w_up_all, w_down_all)


def _mlp(x3, mod3, g_pre, g_post, w_up, w_down, tm, tf=TF_MLP):
    gn, t, d = x3.shape
    r = mod3.shape[1]
    f = w_up.shape[1]
    return pl.pallas_call(
        functools.partial(_mlp_kernel, d=d),
        grid=(gn, t // tm, f // tf),
        in_specs=[
            pl.BlockSpec((None, tm, d), lambda g, i, j: (g, i, 0)),
            _mod_spec(r, tm, 3 * d, 3),
            pl.BlockSpec((1, d), lambda g, i, j: (0, 0)),
            pl.BlockSpec((1, d), lambda g, i, j: (0, 0)),
            pl.BlockSpec((d, tf), lambda g, i, j: (0, j)),
            pl.BlockSpec((tf, d), lambda g, i, j: (j, 0)),
        ],
        out_specs=pl.BlockSpec((None, tm, d), lambda g, i, j: (g, i, 0)),
        out_shape=jax.ShapeDtypeStruct((gn, t, d), F32),
        scratch_shapes=[pltpu.VMEM((tm, d), BF16), pltpu.VMEM((tm, d), F32)],
        compiler_params=_cparams(("parallel", "parallel", "arbitrary"), 56),
        name="mlp",
    )(x3, mod3, g_pre, g_post, w_up, w_down)


def _s5_disc_kernel(lr_ref, li_ref, ldt_ref, br_ref, bi_ref, pwr_ref, pwi_ref, bbr_ref, bbi_ref, *,
                    seg_len):
    lr, li = lr_ref[...], li_ref[...]
    dt = jnp.exp(ldt_ref[...])
    mag = jnp.exp(lr * dt)
    lb_re, lb_im = mag * jnp.cos(li * dt), mag * jnp.sin(li * dt)
    nr, ni = lb_re - 1.0, lb_im
    den = lr * lr + li * li
    f_re = (nr * lr + ni * li) / den
    f_im = (ni * lr - nr * li) / den
    br, bi = br_ref[...], bi_ref[...]
    bbr_ref[...] = f_re * br - f_im * bi
    bbi_ref[...] = f_re * bi + f_im * br
    pwr_ref[0] = lb_re
    pwi_ref[0] = lb_im
    qr, qi = None, None
    sr, si = lb_re, lb_im
    e = seg_len
    while e:
        if e & 1:
            qr, qi = (sr, si) if qr is None else (qr * sr - qi * si, qr * si + qi * sr)
        e >>= 1
        if e:
            sr, si = sr * sr - si * si, 2.0 * sr * si
    pr, pi = qr, qi
    for n in range(SUBLANES):
        pwr_ref[1 + n] = pr
        pwi_ref[1 + n] = pi
        pr, pi = pr * qr - pi * qi, pr * qi + pi * qr


def _s5_discretize(lam_re, lam_im, log_dt, bt_re, bt_im, seg_len):
    g, _, p = lam_re.shape
    c = bt_re.shape[1]
    return pl.pallas_call(
        functools.partial(_s5_disc_kernel, seg_len=seg_len),
        out_shape=[
            jax.ShapeDtypeStruct((1 + SUBLANES, g, 1, p), F32),
            jax.ShapeDtypeStruct((1 + SUBLANES, g, 1, p), F32),
            jax.ShapeDtypeStruct((g, c, p), F32),
            jax.ShapeDtypeStruct((g, c, p), F32),
        ],
        name="s5_discretize",
    )(lam_re, lam_im, log_dt, bt_re, bt_im)


def _gelu_tanh(x):
    c0 = math.sqrt(2.0 / math.pi)
    return x * (0.5 * (1.0 + jnp.tanh(c0 * (x + 0.044715 * (x * x * x)))))


def _cmul_add(ar, ai, xr, xi, yr, yi):
    return yr + ar * xr - ai * xi, yi + ar * xi + ai * xr


def _s5_seq_kernel(x_ref, mod_ref, gpre_ref, bblk_ref, cblk_ref, lam_ref, dskip_ref, s0r_ref, s0i_ref,
                   z_ref, sr_ref, si_ref, h_scr, xr_scr, xi_scr, *, d):
    cb = pl.program_id(0)
    ncb, tm, uw = h_scr.shape
    cw = xr_scr.shape[1]
    seg = s0r_ref.shape[0]

    @pl.when(cb == 0)
    def _():
        h = _norm_mod(x_ref[...], gpre_ref[...], mod_ref, d)
        for c in range(ncb):
            h_scr[c] = h[:, c * uw:(c + 1) * uw]

    u = h_scr[cb]
    bu = _dot(u.astype(BF16), bblk_ref[...])
    xr_scr[...] = bu[:, 0:cw]
    xi_scr[...] = bu[:, cw:2 * cw]
    car_r, car_i = s0r_ref[...], s0i_ref[...]
    l_r, l_i = lam_ref[0:1, :], lam_ref[1:2, :]
    for t in range(tm // seg):
        rows = slice(t * seg, (t + 1) * seg)
        car_r, car_i = _cmul_add(l_r, l_i, car_r, car_i, xr_scr[rows, :], xi_scr[rows, :])
        xr_scr[rows, :] = car_r
        xi_scr[rows, :] = car_i
    sr_ref[...] = car_r
    si_ref[...] = car_i
    xs = jnp.concatenate([xr_scr[...].astype(BF16), xi_scr[...].astype(BF16)], axis=1)
    y = _dot(xs, cblk_ref[...]) + dskip_ref[...] * u
    z_ref[...] = _gelu_tanh(y).astype(BF16)


def _s5_seq(x2, mod2, g_pre, bblk, cblk, lam2, dskip, s0_re, s0_im):
    tm, d = x2.shape
    seg, nst = s0_re.shape
    ncb, uw, cw2 = bblk.shape
    cw = cw2 // 2
    sspec = pl.BlockSpec((seg, cw), lambda c: (0, c))
    return pl.pallas_call(
        functools.partial(_s5_seq_kernel, d=d),
        grid=(ncb,),
        in_specs=[
            pl.BlockSpec((tm, d), lambda c: (0, 0)),
            pl.BlockSpec((seg, 3 * d), lambda c: (0, 0)),
            pl.BlockSpec((1, d), lambda c: (0, 0)),
            pl.BlockSpec((None, uw, cw2), lambda c: (c, 0, 0)),
            pl.BlockSpec((None, cw2, uw), lambda c: (c, 0, 0)),
            pl.BlockSpec((2, cw), lambda c: (0, c)),
            pl.BlockSpec((1, uw), lambda c: (0, c)),
            sspec, sspec,
        ],
        out_specs=[pl.BlockSpec((tm, uw), lambda c: (0, c)), sspec, sspec],
        out_shape=[
            jax.ShapeDtypeStruct((tm, d), BF16),
            jax.ShapeDtypeStruct((seg, nst), F32),
            jax.ShapeDtypeStruct((seg, nst), F32),
        ],
        scratch_shapes=[
            pltpu.VMEM((ncb, tm, uw), F32),
            pltpu.VMEM((tm, cw), F32),
            pltpu.VMEM((tm, cw), F32),
        ],
        compiler_params=_cparams(("arbitrary",), 48),
        name="s5_seq",
    )(x2, mod2, g_pre, bblk, cblk, lam2, dskip, s0_re, s0_im)


def _s5_rows_kernel(x_ref, mod_ref, gpre_ref, bblk_ref, cblk_ref, tbl_ref, dskip_ref,
                    z_ref, sr_ref, si_ref, h_scr, xr_scr, xi_scr, cr_scr, ci_scr, *, d):
    ncol, tm, _ = xr_scr.shape
    ncb, uw, _ = bblk_ref.shape
    sl = tm // SUBLANES

    @pl.when(pl.program_id(1) == 0)
    def _():
        cr_scr[...] = jnp.zeros_like(cr_scr)
        ci_scr[...] = jnp.zeros_like(ci_scr)

    h_scr[...] = _norm_mod(x_ref[...], gpre_ref[...], mod_ref, d)
    row0 = lax.broadcasted_iota(jnp.int32, (ncol, SUBLANES, LANES), 1) == 0

    for c in range(ncb):
        us = slice(c * uw, (c + 1) * uw)
        cols = slice(c * ncol, (c + 1) * ncol)
        u = h_scr[:, us]
        bu = _dot(u.astype(BF16), bblk_ref[c])
        for j in range(ncol):
            xr_scr[j] = bu[:, j * LANES:(j + 1) * LANES]
            xi_scr[j] = bu[:, (ncol + j) * LANES:(ncol + j + 1) * LANES]
        l_r, l_i = tbl_ref[0, cols], tbl_ref[1, cols]

        def local(i, s):
            rows = pl.ds(pl.multiple_of(i * SUBLANES, SUBLANES), SUBLANES)
            return _cmul_add(l_r, l_i, s[0], s[1], xr_scr[:, rows, :], xi_scr[:, rows, :])

        zero = jnp.zeros((ncol, SUBLANES, LANES), F32)
        g_r, g_i = lax.fori_loop(0, sl, local, (zero, zero), unroll=True)
        for n in range(3):
            g_r, g_i = _cmul_add(tbl_ref[2 + 2 * n, cols], tbl_ref[3 + 2 * n, cols],
                                 pltpu.roll(g_r, 1 << n, 1), pltpu.roll(g_i, 1 << n, 1), g_r, g_i)
        car_r, car_i = cr_scr[cols], ci_scr[cols]
        g_r, g_i = _cmul_add(tbl_ref[8, cols], tbl_ref[9, cols], car_r, car_i, g_r, g_i)
        in_r = jnp.where(row0, car_r, pltpu.roll(g_r, 1, 1))
        in_i = jnp.where(row0, car_i, pltpu.roll(g_i, 1, 1))

        def full(i, s):
            rows = pl.ds(pl.multiple_of(i * SUBLANES, SUBLANES), SUBLANES)
            s_r, s_i = _cmul_add(l_r, l_i, s[0], s[1], xr_scr[:, rows, :], xi_scr[:, rows, :])
            xr_scr[:, rows, :] = s_r
            xi_scr[:, rows, :] = s_i
            return s_r, s_i

        e_r, e_i = lax.fori_loop(0, sl, full, (in_r, in_i), unroll=True)
        cr_scr[cols] = e_r[:, SUBLANES - 1:SUBLANES, :]
        ci_scr[cols] = e_i[:, SUBLANES - 1:SUBLANES, :]
        xs = jnp.concatenate([xr_scr[j].astype(BF16) for j in range(ncol)]
                             + [xi_scr[j].astype(BF16) for j in range(ncol)], axis=1)
        y = _dot(xs, cblk_ref[c]) + dskip_ref[:, us] * u
        z_ref[:, us] = _gelu_tanh(y).astype(BF16)

    sr_ref[...] = cr_scr[...]
    si_ref[...] = ci_scr[...]


def _s5_rows(x3, mod3, g_pre, bblk, cblk, tbl, dskip, tm):
    gn, t, d = x3.shape
    ncb, uw, cw2 = bblk.shape
    ncol = cw2 // 2 // LANES
    nct = ncb * ncol
    nt = t // tm
    const = lambda shape: pl.BlockSpec(shape, lambda g, i: (0,) * len(shape), pipeline_mode=pl.Buffered(1))
    ospec = pl.BlockSpec((None, None, nct, 1, LANES), lambda g, i: (g, i, 0, 0, 0))
    z3, s_re, s_im = pl.pallas_call(
        functools.partial(_s5_rows_kernel, d=d),
        grid=(gn, nt),
        in_specs=[
            pl.BlockSpec((None, tm, d), lambda g, i: (g, i, 0)),
            _mod_spec(1, tm, 3 * d, 2),
            pl.BlockSpec((1, d), lambda g, i: (0, 0)),
            const(bblk.shape), const(cblk.shape), const(tbl.shape), const(dskip.shape),
        ],
        out_specs=[pl.BlockSpec((None, tm, d), lambda g, i: (g, i, 0)), ospec, ospec],
        out_shape=[
            jax.ShapeDtypeStruct((gn, t, d), BF16),
            jax.ShapeDtypeStruct((gn, nt, nct, 1, LANES), F32),
            jax.ShapeDtypeStruct((gn, nt, nct, 1, LANES), F32),
        ],
        scratch_shapes=[
            pltpu.VMEM((tm, d), F32),
            pltpu.VMEM((ncol, tm, LANES), F32),
            pltpu.VMEM((ncol, tm, LANES), F32),
            pltpu.VMEM((nct, 1, LANES), F32),
            pltpu.VMEM((nct, 1, LANES), F32),
        ],
        compiler_params=_cparams(("parallel", "arbitrary"), 48),
        name="s5_rows",
    )(x3, mod3, g_pre, bblk, cblk, tbl, dskip)
    return z3, s_re[:, nt - 1].reshape(gn, 1, nct * LANES), s_im[:, nt - 1].reshape(gn, 1, nct * LANES)


def _glu_kernel(x_ref, mod_ref, g_ref, z_ref, wa_ref, wb_ref, o_ref, *, d):
    z = z_ref[...]
    y = _dot(z, wa_ref[...]) * jax.nn.sigmoid(_dot(z, wb_ref[...]))
    o_ref[...] = _gated_residual(x_ref[...], y, g_ref[...], mod_ref, d)


def _glu(x3, mod3, g_post, z3, w_a, w_b, tm):
    gn, t, d = x3.shape
    r = mod3.shape[1]
    wspec = pl.BlockSpec((d, d), lambda g, i: (0, 0), pipeline_mode=pl.Buffered(1))
    return pl.pallas_call(
        functools.partial(_glu_kernel, d=d),
        grid=(gn, t // tm),
        in_specs=[
            pl.BlockSpec((None, tm, d), lambda g, i: (g, i, 0)),
            _mod_spec(r, tm, 3 * d, 2),
            pl.BlockSpec((1, d), lambda g, i: (0, 0)),
            pl.BlockSpec((None, tm, d), lambda g, i: (g, i, 0)),
            wspec, wspec,
        ],
        out_specs=pl.BlockSpec((None, tm, d), lambda g, i: (g, i, 0)),
        out_shape=jax.ShapeDtypeStruct((gn, t, d), F32),
        compiler_params=_cparams(("parallel", "parallel"), 52),
        name="glu",
    )(x3, mod3, g_post, z3, w_a, w_b)


def _rope_tables(pos):
    half = HEAD_DK // 2
    inv = ROPE_BASE ** (-jnp.arange(half, dtype=F32) / half)
    ang = pos.astype(F32)[:, None] * inv[None, :]
    cos, sin = jnp.cos(ang), jnp.sin(ang)
    return jnp.concatenate([cos, cos], axis=-1), jnp.concatenate([-sin, sin], axis=-1)


def _block_diag(w):
    ncb, gpb, a, b = w.shape
    tiled = jnp.broadcast_to(w.reshape(ncb, gpb * a, 1, b), (ncb, gpb * a, gpb, b))
    row_g = jnp.arange(gpb * a)[:, None, None] // a
    col_g = jnp.arange(gpb)[None, :, None]
    return jnp.where(row_g == col_g, tiled, 0.0).reshape(ncb, gpb * a, gpb * b)


def _s5_tables(pw_re, pw_im):
    n = pw_re.shape[0]
    flat_r = pw_re.reshape(n, -1)
    flat_i = pw_im.reshape(n, -1)
    row = jnp.arange(SUBLANES)[:, None]
    tabs = [jnp.broadcast_to(flat_r[0], (SUBLANES, flat_r.shape[1])),
            jnp.broadcast_to(flat_i[0], (SUBLANES, flat_i.shape[1]))]
    for s in (1, 2, 4):
        mask = row >= s
        tabs.append(jnp.where(mask, flat_r[s][None, :], 0.0))
        tabs.append(jnp.where(mask, flat_i[s][None, :], 0.0))
    tabs += [flat_r[1:], flat_i[1:]]
    tbl = jnp.stack(tabs)
    tbl = tbl.reshape(tbl.shape[0], SUBLANES, -1, LANES).transpose(0, 2, 1, 3)
    return tbl, jnp.stack([flat_r[0], flat_i[0]])


def kernel(x_prompt, x_sample, state_gla, state_ret, state_s5_re, state_s5_im, c_prompt, c_sample,
           w_ada, b_ada, norm_pre, norm_post, w_in_mix, w_gla_gk, b_gla_gk, gla_head_norm,
           ret_head_norm, w_out_mix, s5_lam_re, s5_lam_im, s5_log_dt, s5_b_re, s5_b_im,
           s5_c_re, s5_c_im, s5_d, w_glu_a, w_glu_b, w_mlp_up, w_mlp_down):
    bp, tp, d = x_prompt.shape
    bs, ts, _ = x_sample.shape
    depth = w_ada.shape[0]
    gkey = GLA_HEADS * HEAD_DK
    gval = GLA_HEADS * HEAD_DV

    nrow = -(-(bs + bp) // SUBLANES) * SUBLANES
    c_all = jnp.concatenate([c_sample, c_prompt, jnp.zeros((nrow - bs - bp, d), F32)], axis=0)
    mod_all = _adaln(c_all, w_ada.reshape(depth * 2, d, 3 * d), b_ada.reshape(depth * 2, 1, 3 * d))
    mod_s = [mod_all[k, 0:bs][None] for k in range(depth * 2)]
    mod_p = [mod_all[k, bs:bs + bp][:, None, :] for k in range(depth * 2)]

    w_in = w_in_mix[0]
    sec = 2 * gkey + 2 * gval
    w_lr = jnp.pad(w_in[:, sec:sec + GLA_RANK], ((0, 0), (0, LANES - GLA_RANK))).astype(BF16)
    w_gk = jnp.pad(w_gla_gk[0], ((0, LANES - GLA_RANK), (0, 0))).astype(BF16)
    b_gk = b_gla_gk[0][None, :]
    w_out = w_out_mix[0].astype(BF16)
    w_ga = w_glu_a[0].astype(BF16)
    w_gb = w_glu_b[0].astype(BF16)
    gla_norm = gla_head_norm[0][:, None, :]
    ret_norm = ret_head_norm[0][:, None, :]
    gamma_log = jnp.log1p(-jnp.power(2.0, -5.0 - jnp.arange(RET_HEADS, dtype=F32)))
    ret_lg = jnp.broadcast_to(gamma_log[:, None, None], (RET_HEADS, 1, LANES))

    ng = s5_lam_re.shape[1]
    ncb = ng // S5_GPB
    tm5 = min(TM_S5, tp)
    pw_re, pw_im, bbt_re, bbt_im = _s5_discretize(
        s5_lam_re[0][:, None, :], s5_lam_im[0][:, None, :], s5_log_dt[0][:, None, None],
        jnp.swapaxes(s5_b_re[0], 1, 2), jnp.swapaxes(s5_b_im[0], 1, 2), tm5 // SUBLANES)
    bblk = jnp.concatenate([
        _block_diag(bbt_re.reshape(ncb, S5_GPB, S5_GROUP, S5_STATE)),
        _block_diag(bbt_im.reshape(ncb, S5_GPB, S5_GROUP, S5_STATE))], axis=2).astype(BF16)
    ct_re = jnp.swapaxes(s5_c_re[0], 1, 2).reshape(ncb, S5_GPB, S5_STATE, S5_GROUP)
    ct_im = jnp.swapaxes(s5_c_im[0], 1, 2).reshape(ncb, S5_GPB, S5_STATE, S5_GROUP)
    cblk = jnp.concatenate([_block_diag(ct_re), _block_diag(-ct_im)], axis=1).astype(BF16)
    tbl, lam2 = _s5_tables(pw_re[:, :, 0, :], pw_im[:, :, 0, :])
    dskip = s5_d[0][None, :]

    def trunk(x3, mods, tm, inproj, attn, s5, mlp):
        proj, glog = inproj(x3, mods[0])
        mg, mr, s_gla, s_ret = attn(proj, glog)
        x3 = _outproj(x3, mods[0], norm_post[0, 0][None], mg, mr, w_out, tm)
        x3 = mlp(0, x3, mods[1])
        z3, s_re, s_im = s5(x3, mods[2])
        x3 = _glu(x3, mods[2], norm_post[1, 0][None], z3, w_ga, w_gb, tm)
        x3 = mlp(1, x3, mods[3])
        return x3, s_gla, s_ret, s_re, s_im

    pos_s = PAST_LEN + jnp.arange(T_PAD, dtype=F32)
    cos_s, sin_s = _rope_tables(pos_s)

    def attn_s(proj, glog):
        def to_bm(a):
            a = jnp.swapaxes(a.reshape(ts, bs, a.shape[-1]), 0, 1)
            return jnp.pad(a, ((0, 0), (0, T_PAD - ts), (0, 0)))
        mg, mr, s_gla, s_ret = _attn_sample(to_bm(proj), to_bm(glog), gla_norm, ret_norm, cos_s, sin_s,
                                            ret_lg, state_gla[0], state_ret[0], ts)
        to_tm = lambda a: jnp.swapaxes(a[:, :ts], 0, 1).reshape(1, ts * bs, a.shape[-1])
        return to_tm(mg), to_tm(mr), s_gla, s_ret

    xs3 = jnp.swapaxes(x_sample, 0, 1).reshape(1, ts * bs, d)

    def s5_s(x3, mod3):
        z2, s_re, s_im = _s5_seq(x3[0], mod3[0], norm_pre[1, 0][None], bblk, cblk, lam2, dskip,
                                 state_s5_re[0].reshape(bs, -1), state_s5_im[0].reshape(bs, -1))
        return z2[None], s_re, s_im

    w_up, w_dn, w_main = {}, {}, {}

    def inproj_s(x3, mod3):
        proj, glog, w_main[0] = _inproj_cast(x3, mod3, norm_pre[0, 0][None], w_in_mix, 0, w_lr, w_gk, b_gk)
        return proj, glog

    def mlp_s(l, x3, mod3):
        x3, w_up[l], w_dn[l] = _mlp_cast(x3, mod3, norm_pre[l, 1][None], norm_post[l, 1][None],
                                         w_mlp_up, w_mlp_down, l)
        return x3

    y_s, gla_s, ret_s, re_s, im_s = trunk(xs3, mod_s, ts * bs, inproj_s, attn_s, s5_s, mlp_s)
    y_s = jnp.swapaxes(y_s.reshape(ts, bs, d), 0, 1)

    cos_p, sin_p = _rope_tables(jnp.arange(tp, dtype=F32))
    zeros_att = jnp.zeros((bp, GLA_HEADS, HEAD_DK, HEAD_DV), F32)
    tm_p = min(TM_DENSE, tp)

    def attn_p(proj, glog):
        return _attn_prompt(proj, glog, gla_norm, ret_norm, cos_p, sin_p, ret_lg, zeros_att, zeros_att)

    def s5_p(x3, mod3):
        sl = tm5 // SUBLANES
        xp = jnp.swapaxes(x3.reshape(bp, tp // tm5, SUBLANES, sl, d), 2, 3).reshape(bp, tp, d)
        zp, s_re, s_im = _s5_rows(xp, mod3, norm_pre[1, 0][None], bblk, cblk, tbl, dskip, tm5)
        z3 = jnp.swapaxes(zp.reshape(bp, tp // tm5, sl, SUBLANES, d), 2, 3).reshape(bp, tp, d)
        return z3, s_re, s_im

    def mlp_p(l, x3, mod3):
        return _mlp(x3, mod3, norm_pre[l, 1][None], norm_post[l, 1][None], w_up[l], w_dn[l], tm_p)

    def inproj_p(x3, mod3):
        return _inproj(x3, mod3, norm_pre[0, 0][None], w_main[0], w_lr, w_gk, b_gk, tm_p)

    y_p, gla_p, ret_p, re_p, im_p = trunk(x_prompt, mod_p, tm_p, inproj_p, attn_p, s5_p, mlp_p)

    st = lambda a, b_: a.reshape(1, b_, ng, S5_STATE)
    return (y_p, y_s, gla_p[None], gla_s[None], ret_p[None], ret_s[None],
            st(re_p, bp), st(re_s, bs), st(im_p, bp), st(im_s, bs))
```

```python
import functools
import math

import jax
import jax.numpy as jnp
import numpy as np
from jax import lax
from jax.experimental import pallas as pl
from jax.experimental.pallas import tpu as pltpu

F32 = jnp.float32
BF16 = jnp.bfloat16

EPS = 1e-6
LANES = 128
SUBLANES = 8
MIB = 1024 * 1024

GLA_HEADS = 4
RET_HEADS = 4
HEAD_DK = 128
HEAD_DV = 256
GLA_RANK = 16
GLA_LOGIT_NORM = 16.0
ROPE_BASE = 10000.0
PAST_LEN = 16384
S5_GROUP = 16
S5_STATE = 64
S5_GPB = 16
S5_UW = S5_GPB * S5_GROUP
ATT_CHUNK = 128
GLA_SUB = 16
T_PAD = 8
TM_DENSE = 512
TM_S5 = 256
TN_INPROJ = 1024
TF_MLP = 1024


def _cparams(sem, vmem_mib):
    return pltpu.CompilerParams(dimension_semantics=sem, vmem_limit_bytes=vmem_mib * MIB)


def _dot(a, b):
    return jnp.dot(a, b, preferred_element_type=F32)


def _dot_nt(a, b):
    return lax.dot_general(a, b, (((1,), (1,)), ((), ())), preferred_element_type=F32)


def _rms(x, g):
    return x * lax.rsqrt(jnp.mean(x * x, axis=-1, keepdims=True) + EPS) * g


def _rows_affine(y, a, b=None):
    tm, d = y.shape
    r = a.shape[0]
    if r == 1 or r == tm:
        out = y * a
        return out if b is None else out + b
    y3 = y.reshape(tm // r, r, d)
    out = y3 * a[None]
    if b is not None:
        out = out + b[None]
    return out.reshape(tm, d)


def _norm_mod(x, g, mod_ref, d):
    return _rows_affine(_rms(x, g), 1.0 + mod_ref[:, d:2 * d], mod_ref[:, 0:d])


def _gated_residual(x, y, g, mod_ref, d):
    return x + _rows_affine(_rms(y, g), mod_ref[:, 2 * d:3 * d])


def _mod_spec(r, tm, width, ngrid):
    if ngrid == 2:
        if r == 1:
            return pl.BlockSpec((None, 1, width), lambda g, i: (g, 0, 0))
        return pl.BlockSpec((None, r, width), lambda g, i: (g, 0, 0))
    if r == 1:
        return pl.BlockSpec((None, 1, width), lambda g, i, j: (g, 0, 0))
    return pl.BlockSpec((None, r, width), lambda g, i, j: (g, 0, 0))


def _adaln_kernel(c_ref, w_ref, b_ref, o_ref):
    c = c_ref[...]
    sc = (c * jax.nn.sigmoid(c)).astype(BF16)
    o_ref[...] = _dot(sc, w_ref[...].astype(BF16)) + b_ref[...]


def _adaln(c_all, w_ada, b_ada, tn=1024):
    ls, d, n = w_ada.shape
    rows = c_all.shape[0]
    return pl.pallas_call(
        _adaln_kernel,
        grid=(ls, n // tn),
        in_specs=[
            pl.BlockSpec((rows, d), lambda l, j: (0, 0)),
            pl.BlockSpec((None, d, tn), lambda l, j: (l, 0, j)),
            pl.BlockSpec((None, 1, tn), lambda l, j: (l, 0, j)),
        ],
        out_specs=pl.BlockSpec((None, rows, tn), lambda l, j: (l, 0, j)),
        out_shape=jax.ShapeDtypeStruct((ls, rows, n), F32),
        compiler_params=_cparams(("parallel", "parallel"), 40),
        name="adaln",
    )(c_all, w_ada, b_ada)


def _log_sigmoid(x):
    return jnp.minimum(x, 0.0) - jnp.log1p(jnp.exp(-jnp.abs(x)))


def _gate_logits(hb, wlr_t, wgk_ref, bgk_ref, glog_ref):
    glr = _dot_nt(hb, wlr_t)
    logit = _dot(glr.astype(BF16), wgk_ref[...]) + bgk_ref[...]
    glog_ref[...] = _log_sigmoid(logit) * (1.0 / GLA_LOGIT_NORM)


def _inproj_kernel(x_ref, mod_ref, g_ref, w_ref, wlr_ref, wgk_ref, bgk_ref,
                   proj_ref, glog_ref, h_scr, *, d, tps):
    j = pl.program_id(2)

    @pl.when(j == 0)
    def _():
        hb = _norm_mod(x_ref[...], g_ref[...], mod_ref, d).astype(BF16)
        h_scr[...] = hb
        _gate_logits(hb, wlr_ref[...], wgk_ref, bgk_ref, glog_ref)

    tn = w_ref.shape[2]
    for k in range(tps):
        proj_ref[:, k * tn:(k + 1) * tn] = _dot(h_scr[...], w_ref[j * tps + k])


def _inproj_cast_kernel(x_ref, mod_ref, g_ref, wa_ref, wb_ref, wgk_ref, bgk_ref,
                        proj_ref, glog_ref, wout_ref, wlr_ref, h_scr, *, d, n_lo):
    j = pl.program_id(2)

    @pl.when(j == 0)
    def _():
        h_scr[...] = _norm_mod(x_ref[...], g_ref[...], mod_ref, d).astype(BF16)

    @pl.when(j < n_lo)
    def _():
        wout_ref[...] = jnp.transpose(wa_ref[...]).astype(BF16)

    @pl.when(j >= n_lo)
    def _():
        w = jnp.concatenate([wa_ref[GLA_RANK:, :], wb_ref[:GLA_RANK, :]], axis=0)
        wout_ref[...] = jnp.transpose(w).astype(BF16)

    @pl.when(j == n_lo)
    def _():
        wlr_t = _pad_rows(wa_ref[:GLA_RANK, :], LANES).astype(BF16)
        wlr_ref[...] = wlr_t
        _gate_logits(h_scr[...], wlr_t, wgk_ref, bgk_ref, glog_ref)

    proj_ref[...] = _dot(h_scr[...], wout_ref[...])


def _inproj_cast(x3, mod3, g_pre, w_raw_t, layer, w_gk, b_gk, tn=TN_INPROJ):
    gn, t, d = x3.shape
    assert gn == 1
    sec = (w_raw_t.shape[1] - GLA_RANK) // 2
    assert sec % tn == 0 and tn % LANES == 0
    nj = 2 * sec // tn
    n = nj * tn
    r = mod3.shape[1]
    gkey = w_gk.shape[1]
    return pl.pallas_call(
        functools.partial(_inproj_cast_kernel, d=d, n_lo=sec // tn),
        grid=(1, 1, nj),
        in_specs=[
            pl.BlockSpec((None, t, d), lambda g, i, j: (0, 0, 0)),
            _mod_spec(r, t, 3 * d, 3),
            pl.BlockSpec((1, d), lambda g, i, j: (0, 0)),
            pl.BlockSpec((None, tn, d), lambda g, i, j: (layer, j, 0)),
            pl.BlockSpec((None, LANES, d), lambda g, i, j: (layer, (j + 1) * (tn // LANES), 0)),
            pl.BlockSpec((LANES, gkey), lambda g, i, j: (0, 0)),
            pl.BlockSpec((1, gkey), lambda g, i, j: (0, 0)),
        ],
        out_specs=[
            pl.BlockSpec((None, t, tn), lambda g, i, j: (0, 0, j)),
            pl.BlockSpec((None, t, gkey), lambda g, i, j: (0, 0, 0)),
            pl.BlockSpec((None, d, tn), lambda g, i, j: (j, 0, 0)),
            pl.BlockSpec((LANES, d), lambda g, i, j: (0, 0)),
        ],
        out_shape=[
            jax.ShapeDtypeStruct((1, t, n), F32),
            jax.ShapeDtypeStruct((1, t, gkey), F32),
            jax.ShapeDtypeStruct((nj, d, tn), BF16),
            jax.ShapeDtypeStruct((LANES, d), BF16),
        ],
        scratch_shapes=[pltpu.VMEM((t, d), BF16)],
        compiler_params=_cparams(("arbitrary", "arbitrary", "arbitrary"), 56),
        name="inproj_cast",
    )(x3, mod3, g_pre, w_raw_t, w_raw_t, w_gk, b_gk)


def _inproj(x3, mod3, g_pre, w_main, w_lr_t, w_gk, b_gk, tm, tps=2):
    gn, t, d = x3.shape
    nj, _, tn = w_main.shape
    n = nj * tn
    r = mod3.shape[1]
    gkey = w_gk.shape[1]
    return pl.pallas_call(
        functools.partial(_inproj_kernel, d=d, tps=tps),
        grid=(gn, t // tm, nj // tps),
        in_specs=[
            pl.BlockSpec((None, tm, d), lambda g, i, j: (g, i, 0)),
            _mod_spec(r, tm, 3 * d, 3),
            pl.BlockSpec((1, d), lambda g, i, j: (0, 0)),
            pl.BlockSpec((nj, d, tn), lambda g, i, j: (0, 0, 0), pipeline_mode=pl.Buffered(1)),
            pl.BlockSpec((LANES, d), lambda g, i, j: (0, 0)),
            pl.BlockSpec((LANES, gkey), lambda g, i, j: (0, 0)),
            pl.BlockSpec((1, gkey), lambda g, i, j: (0, 0)),
        ],
        out_specs=[
            pl.BlockSpec((None, tm, tps * tn), lambda g, i, j: (g, i, j)),
            pl.BlockSpec((None, tm, gkey), lambda g, i, j: (g, i, 0)),
        ],
        out_shape=[
            jax.ShapeDtypeStruct((gn, t, n), F32),
            jax.ShapeDtypeStruct((gn, t, gkey), F32),
        ],
        scratch_shapes=[pltpu.VMEM((tm, d), BF16)],
        compiler_params=_cparams(("parallel", "parallel", "arbitrary"), 56),
        name="inproj",
    )(x3, mod3, g_pre, w_main, w_lr_t, w_gk, b_gk)


def _cumsum_rows(g):
    c = g.shape[0]
    row = lax.broadcasted_iota(jnp.int32, g.shape, 0)
    s = 1
    while s < c:
        g = g + jnp.where(row >= s, pltpu.roll(g, s, 0), 0.0)
        s *= 2
    return g


def _pad_rows(a, rows):
    if a.shape[0] == rows:
        return a
    return jnp.concatenate([a, jnp.zeros((rows - a.shape[0], a.shape[1]), a.dtype)], axis=0)


def _col_bcast(row, width):
    sq = jnp.transpose(jnp.broadcast_to(row, (LANES, LANES)))
    return jnp.concatenate([sq] * (width // LANES), axis=1)


def _gla_core(q, k, v, g, s, sub):
    cq = q.shape[0]
    ck = max(cq, LANES)
    b = _cumsum_rows(g)
    be = b - g
    bk = _pad_rows(b, ck)
    kp = _pad_rows(k, ck)
    vp = _pad_rows(v, ck).astype(BF16)
    rowj = lax.broadcasted_iota(jnp.int32, (ck, 1), 0)
    att_rows = []
    for blk in range(cq // sub):
        lo, hi = blk * sub, (blk + 1) * sub
        base = be[lo:lo + 1, :]
        qs = q[lo:hi] * jnp.exp(b[lo:hi] - base)
        ks = jnp.where(rowj < hi, kp * jnp.exp(base - bk), 0.0)
        att_rows.append(_dot_nt(qs.astype(BF16), ks.astype(BF16)))
    att = att_rows[0] if len(att_rows) == 1 else jnp.concatenate(att_rows, axis=0)
    ri = lax.broadcasted_iota(jnp.int32, (cq, ck), 0)
    cj = lax.broadcasted_iota(jnp.int32, (cq, ck), 1)
    att = jnp.where(ri >= cj, att, 0.0)
    o = _dot(att.astype(BF16), vp) + _dot((q * jnp.exp(b)).astype(BF16), s.astype(BF16))
    b_last = b[cq - 1:cq, :]
    k_out = kp * jnp.exp(b_last - bk)
    s_new = s * _col_bcast(jnp.exp(b_last), s.shape[1]) + _dot(jnp.transpose(k_out).astype(BF16), vp)
    return o, s_new


def _ret_core(q, k, v, s, lg, dmat, valid):
    cq = q.shape[0]
    ck = max(cq, LANES)
    kp = _pad_rows(k, ck)
    vp = _pad_rows(v, ck).astype(BF16)
    ti = lax.broadcasted_iota(jnp.int32, (cq, 1), 0).astype(F32)
    tj = lax.broadcasted_iota(jnp.int32, (ck, 1), 0).astype(F32)
    att = _dot_nt(q.astype(BF16), kp.astype(BF16)) * dmat
    q_in = q * jnp.exp((ti + 1.0) * lg)
    o = _dot(att.astype(BF16), vp) + _dot(q_in.astype(BF16), s.astype(BF16))
    k_out = kp * jnp.exp((float(valid - 1) - tj) * lg)
    s_new = s * jnp.exp(float(valid) * lg) + _dot(jnp.transpose(k_out).astype(BF16), vp)
    return o, s_new


def _decay_matrix(cq, ck, lg):
    ri = lax.broadcasted_iota(jnp.int32, (cq, ck), 0)
    cj = lax.broadcasted_iota(jnp.int32, (cq, ck), 1)
    diff = (ri - cj).astype(F32)
    return jnp.where(ri >= cj, jnp.exp(diff * lg), 0.0)


def _rope(x, cosf, sinf):
    return x * cosf + pltpu.roll(x, x.shape[1] // 2, 1) * sinf


def _silu(x):
    return x * jax.nn.sigmoid(x)


def _gla_finish(o, gate, gn):
    o = o * lax.rsqrt(jnp.mean(o * o, axis=-1, keepdims=True) + EPS) * gn
    return (o * _silu(gate)).astype(BF16)


def _ret_finish(o, gate, gn):
    oc = o - jnp.mean(o, axis=-1, keepdims=True)
    oc = oc * lax.rsqrt(jnp.mean(oc * oc, axis=-1, keepdims=True) + EPS) * gn
    return (oc * _silu(gate)).astype(BF16)


def _head(ref, h, width):
    return ref[:, h * width:(h + 1) * width]


def _gla_prompt_kernel(q_ref, k_ref, v_ref, gg_ref, gl_ref, gn_ref, s0_ref, o_ref, s_ref):
    @pl.when(pl.program_id(1) == 0)
    def _():
        s_ref[...] = s0_ref[...]

    for h in range(s_ref.shape[0]):
        q = _head(q_ref, h, HEAD_DK) * (HEAD_DK ** -0.5)
        o, s_new = _gla_core(q, _head(k_ref, h, HEAD_DK), _head(v_ref, h, HEAD_DV),
                             _head(gl_ref, h, HEAD_DK), s_ref[h], GLA_SUB)
        s_ref[h] = s_new
        o_ref[:, h * HEAD_DV:(h + 1) * HEAD_DV] = _gla_finish(o, _head(gg_ref, h, HEAD_DV), gn_ref[h])


def _ret_prompt_kernel(q_ref, k_ref, v_ref, rg_ref, cos_ref, sin_ref, lg_ref, gn_ref, s0_ref,
                       o_ref, s_ref, d_scr):
    @pl.when(pl.program_id(1) == 0)
    def _():
        s_ref[...] = s0_ref[...]
        for h in range(d_scr.shape[0]):
            d_scr[h] = _decay_matrix(d_scr.shape[1], d_scr.shape[2], lg_ref[h][:, 0:1])

    cosf, sinf = cos_ref[...], sin_ref[...]
    for h in range(s_ref.shape[0]):
        q = _rope(_head(q_ref, h, HEAD_DK), cosf, sinf)
        k = _rope(_head(k_ref, h, HEAD_DK), cosf, sinf) * (HEAD_DK ** -0.5)
        o, s_new = _ret_core(q, k, _head(v_ref, h, HEAD_DV), s_ref[h], lg_ref[h][:, 0:1], d_scr[h],
                             q.shape[0])
        s_ref[h] = s_new
        o_ref[:, h * HEAD_DV:(h + 1) * HEAD_DV] = _ret_finish(o, _head(rg_ref, h, HEAD_DV), gn_ref[h])


def _attn_prompt(proj, glog, gla_norm, ret_norm, cosf, sinf, ret_lg, s0_gla, s0_ret):
    bsz, t, _ = proj.shape
    c = ATT_CHUNK
    nh = GLA_HEADS
    grid = (bsz, t // c)
    kw, vw = nh * HEAD_DK, nh * HEAD_DV
    kspec = lambda blk: pl.BlockSpec((None, c, kw), lambda b, i, blk=blk: (b, i, blk))
    vspec = lambda blk: pl.BlockSpec((None, c, vw), lambda b, i, blk=blk: (b, i, blk))
    hspec = pl.BlockSpec((nh, 1, HEAD_DV), lambda b, i: (0, 0, 0))
    sspec = pl.BlockSpec((None, nh, HEAD_DK, HEAD_DV), lambda b, i: (b, 0, 0, 0))
    ospec = pl.BlockSpec((None, c, vw), lambda b, i: (b, i, 0))
    out_shape = [
        jax.ShapeDtypeStruct((bsz, t, vw), BF16),
        jax.ShapeDtypeStruct((bsz, nh, HEAD_DK, HEAD_DV), F32),
    ]
    params = _cparams(("parallel", "arbitrary"), 32)
    mg, s_gla = pl.pallas_call(
        _gla_prompt_kernel,
        grid=grid,
        in_specs=[kspec(0), kspec(1), vspec(1), vspec(2), kspec(0), hspec, sspec],
        out_specs=[ospec, sspec],
        out_shape=out_shape,
        compiler_params=params,
        name="gla_prompt",
    )(proj, proj, proj, proj, glog, gla_norm, s0_gla)
    tspec = pl.BlockSpec((c, HEAD_DK), lambda b, i: (i, 0))
    mr, s_ret = pl.pallas_call(
        _ret_prompt_kernel,
        grid=grid,
        in_specs=[kspec(6), kspec(7), vspec(4), vspec(5), tspec, tspec,
                  pl.BlockSpec((nh, 1, LANES), lambda b, i: (0, 0, 0)), hspec, sspec],
        out_specs=[ospec, sspec],
        out_shape=out_shape,
        scratch_shapes=[pltpu.VMEM((nh, c, c), F32)],
        compiler_params=params,
        name="ret_prompt",
    )(proj, proj, proj, proj, cosf, sinf, ret_lg, ret_norm, s0_ret)
    return mg, mr, s_gla, s_ret


def _gla_sample_kernel(q_ref, k_ref, v_ref, gg_ref, gl_ref, gn_ref, s0_ref, o_ref, s_ref):
    for bi in range(q_ref.shape[0]):
        q = q_ref[bi] * (HEAD_DK ** -0.5)
        o, s_new = _gla_core(q, k_ref[bi], v_ref[bi], gl_ref[bi], s0_ref[bi], q.shape[0])
        s_ref[bi] = s_new
        o_ref[bi] = _gla_finish(o, gg_ref[bi], gn_ref[...])


def _ret_sample_kernel(q_ref, k_ref, v_ref, rg_ref, cos_ref, sin_ref, lg_ref, gn_ref, s0_ref,
                       o_ref, s_ref, *, valid):
    lg = lg_ref[:, 0:1]
    cq = q_ref.shape[1]
    dmat = _decay_matrix(cq, max(cq, LANES), lg)
    cosf, sinf = cos_ref[...], sin_ref[...]
    for bi in range(q_ref.shape[0]):
        q = _rope(q_ref[bi], cosf, sinf)
        k = _rope(k_ref[bi], cosf, sinf) * (HEAD_DK ** -0.5)
        o, s_new = _ret_core(q, k, v_ref[bi], s0_ref[bi], lg, dmat, valid)
        s_ref[bi] = s_new
        o_ref[bi] = _ret_finish(o, rg_ref[bi], gn_ref[...])


def _attn_sample(proj, glog, gla_norm, ret_norm, cosf, sinf, ret_lg, s0_gla, s0_ret, valid, bb=8):
    bsz, tp, _ = proj.shape
    grid = (bsz // bb, GLA_HEADS)
    nk = GLA_HEADS
    kspec = lambda off: pl.BlockSpec((bb, tp, HEAD_DK), lambda i, h, off=off: (i, 0, off + h))
    vspec = lambda off: pl.BlockSpec((bb, tp, HEAD_DV), lambda i, h, off=off: (i, 0, off + h))
    hspec = pl.BlockSpec((None, 1, HEAD_DV), lambda i, h: (h, 0, 0))
    sspec = pl.BlockSpec((bb, None, HEAD_DK, HEAD_DV), lambda i, h: (i, h, 0, 0))
    ospec = pl.BlockSpec((bb, tp, HEAD_DV), lambda i, h: (i, 0, h))
    out_shape = [
        jax.ShapeDtypeStruct((bsz, tp, GLA_HEADS * HEAD_DV), BF16),
        jax.ShapeDtypeStruct((bsz, GLA_HEADS, HEAD_DK, HEAD_DV), F32),
    ]
    params = _cparams(("parallel", "parallel"), 32)
    mg, s_gla = pl.pallas_call(
        _gla_sample_kernel,
        grid=grid,
        in_specs=[kspec(0), kspec(nk), vspec(nk), vspec(2 * nk),
                  pl.BlockSpec((bb, tp, HEAD_DK), lambda i, h: (i, 0, h)),
                  hspec, sspec],
        out_specs=[ospec, sspec],
        out_shape=out_shape,
        compiler_params=params,
        name="gla_sample",
    )(proj, proj, proj, proj, glog, gla_norm, s0_gla)
    tspec = pl.BlockSpec((tp, HEAD_DK), lambda i, h: (0, 0))
    mr, s_ret = pl.pallas_call(
        functools.partial(_ret_sample_kernel, valid=valid),
        grid=grid,
        in_specs=[kspec(6 * nk), kspec(7 * nk), vspec(4 * nk), vspec(5 * nk),
                  tspec, tspec,
                  pl.BlockSpec((None, 1, LANES), lambda i, h: (h, 0, 0)),
                  hspec, sspec],
        out_specs=[ospec, sspec],
        out_shape=out_shape,
        compiler_params=params,
        name="ret_sample",
    )(proj, proj, proj, proj, cosf, sinf, ret_lg, ret_norm, s0_ret)
    return mg, mr, s_gla, s_ret


def _residual_out(x_ref, y, g_ref, mod_ref, outs, d):
    x_new = _gated_residual(x_ref[...], y, g_ref[...], mod_ref, d)
    if len(outs) == 1:
        outs[0][...] = x_new
    else:
        modn_ref, gn_ref, o_ref, h_ref = outs
        o_ref[...] = x_new
        h_ref[...] = _norm_mod(x_new, gn_ref[...], modn_ref, d).astype(BF16)


def _next_specs(nxt, tm, d):
    if nxt is None:
        return [], [], [], []
    mod_next, g_next, shape = nxt
    return ([_mod_spec(mod_next.shape[1], tm, 3 * d, 2), pl.BlockSpec((1, d), lambda g, i: (0, 0))],
            [pl.BlockSpec((None, tm, d), lambda g, i: (g, i, 0))],
            [jax.ShapeDtypeStruct(shape, BF16)], [mod_next, g_next])


def _outproj_kernel(x_ref, mod_ref, g_ref, mg_ref, mr_ref, wo_ref, *outs, d):
    half = mg_ref.shape[1]
    y = _dot(mg_ref[...], wo_ref[0:half, :]) + _dot(mr_ref[...], wo_ref[half:2 * half, :])
    _residual_out(x_ref, y, g_ref, mod_ref, outs, d)


def _outproj(x3, mod3, g_post, mg, mr, w_out, tm, nxt=None):
    gn, t, d = x3.shape
    r = mod3.shape[1]
    half = mg.shape[2]
    n_in, n_out, n_shape, n_ops = _next_specs(None if nxt is None else (*nxt, x3.shape), tm, d)
    res = pl.pallas_call(
        functools.partial(_outproj_kernel, d=d),
        grid=(gn, t // tm),
        in_specs=[
            pl.BlockSpec((None, tm, d), lambda g, i: (g, i, 0)),
            _mod_spec(r, tm, 3 * d, 2),
            pl.BlockSpec((1, d), lambda g, i: (0, 0)),
            pl.BlockSpec((None, tm, half), lambda g, i: (g, i, 0)),
            pl.BlockSpec((None, tm, half), lambda g, i: (g, i, 0)),
            pl.BlockSpec((2 * half, d), lambda g, i: (0, 0)),
        ] + n_in,
        out_specs=[pl.BlockSpec((None, tm, d), lambda g, i: (g, i, 0))] + n_out,
        out_shape=[jax.ShapeDtypeStruct((gn, t, d), F32)] + n_shape,
        compiler_params=_cparams(("parallel", "parallel"), 56),
        name="outproj",
    )(x3, mod3, g_post, mg, mr, w_out, *n_ops)
    return res if nxt is not None else (res[0], None)


def _mlp_kernel(x_ref, mod_ref, gpre_ref, gpost_ref, wup_ref, wdn_ref, o_ref, h_scr, acc_scr, *, d):
    j = pl.program_id(2)

    @pl.when(j == 0)
    def _():
        h_scr[...] = _norm_mod(x_ref[...], gpre_ref[...], mod_ref, d).astype(BF16)
        acc_scr[...] = jnp.zeros_like(acc_scr)

    u = jnp.maximum(_dot(h_scr[...], wup_ref[...]), 0.0)
    acc_scr[...] += _dot((u * u).astype(BF16), wdn_ref[...])

    @pl.when(j == pl.num_programs(2) - 1)
    def _():
        o_ref[...] = _gated_residual(x_ref[...], acc_scr[...], gpost_ref[...], mod_ref, d)


def _mlp_cast_kernel(x_ref, mod_ref, gpre_ref, gpost_ref, wup_ref, wdn_ref,
                     o_ref, wupb_ref, wdnb_ref, h_scr, acc_scr, *, d):
    wupb_ref[...] = wup_ref[...].astype(BF16)
    wdnb_ref[...] = wdn_ref[...].astype(BF16)
    _mlp_kernel(x_ref, mod_ref, gpre_ref, gpost_ref, wupb_ref, wdnb_ref, o_ref, h_scr, acc_scr, d=d)


def _mlp_cast(x3, mod3, g_pre, g_post, w_up_all, w_down_all, layer, tf=512):
    gn, t, d = x3.shape
    assert gn == 1
    r = mod3.shape[1]
    f = w_up_all.shape[2]
    return pl.pallas_call(
        functools.partial(_mlp_cast_kernel, d=d),
        grid=(1, 1, f // tf),
        in_specs=[
            pl.BlockSpec((None, t, d), lambda g, i, j: (0, 0, 0)),
            _mod_spec(r, t, 3 * d, 3),
            pl.BlockSpec((1, d), lambda g, i, j: (0, 0)),
            pl.BlockSpec((1, d), lambda g, i, j: (0, 0)),
            pl.BlockSpec((None, d, tf), lambda g, i, j: (layer, 0, j)),
            pl.BlockSpec((None, tf, d), lambda g, i, j: (layer, j, 0)),
        ],
        out_specs=[
            pl.BlockSpec((None, t, d), lambda g, i, j: (0, 0, 0)),
            pl.BlockSpec((d, tf), lambda g, i, j: (0, j)),
            pl.BlockSpec((tf, d), lambda g, i, j: (j, 0)),
        ],
        out_shape=[
            jax.ShapeDtypeStruct((1, t, d), F32),
            jax.ShapeDtypeStruct((d, f), BF16),
            jax.ShapeDtypeStruct((f, d), BF16),
        ],
        scratch_shapes=[pltpu.VMEM((t, d), BF16), pltpu.VMEM((t, d), F32)],
        compiler_params=_cparams(("arbitrary", "arbitrary", "arbitrary"), 56),
        name="mlp_cast",
    )(x3, mod3, g_pre, g_post, w_up_all, w_down_all)


def _mlp_h_kernel(x_ref, h_ref, mod_ref, gpost_ref, wup_ref, wdn_ref, o_ref, acc_scr, *, d):
    j = pl.program_id(2)

    @pl.when(j == 0)
    def _():
        acc_scr[...] = jnp.zeros_like(acc_scr)

    u = jnp.maximum(_dot(h_ref[...], wup_ref[...]), 0.0)
    acc_scr[...] += _dot((u * u).astype(BF16), wdn_ref[...])

    @pl.when(j == pl.num_programs(2) - 1)
    def _():
        o_ref[...] = _gated_residual(x_ref[...], acc_scr[...], gpost_ref[...], mod_ref, d)


def _mlp_h(x3, h3, mod3, g_post, w_up, w_down, tm, tf=TF_MLP):
    gn, t, d = x3.shape
    r = mod3.shape[1]
    f = w_up.shape[1]
    return pl.pallas_call(
        functools.partial(_mlp_h_kernel, d=d),
        grid=(gn, t // tm, f // tf),
        in_specs=[
            pl.BlockSpec((None, tm, d), lambda g, i, j: (g, i, 0)),
            pl.BlockSpec((None, tm, d), lambda g, i, j: (g, i, 0)),
            _mod_spec(r, tm, 3 * d, 3),
            pl.BlockSpec((1, d), lambda g, i, j: (0, 0)),
            pl.BlockSpec((d, tf), lambda g, i, j: (0, j)),
            pl.BlockSpec((tf, d), lambda g, i, j: (j, 0)),
        ],
        out_specs=pl.BlockSpec((None, tm, d), lambda g, i, j: (g, i, 0)),
        out_shape=jax.ShapeDtypeStruct((gn, t, d), F32),
        scratch_shapes=[pltpu.VMEM((tm, d), F32)],
        compiler_params=_cparams(("parallel", "parallel", "arbitrary"), 56),
        name="mlp",
    )(x3, h3, mod3, g_post, w_up, w_down)


def _s5_disc_kernel(lr_ref, li_ref, ldt_ref, br_ref, bi_ref, pwr_ref, pwi_ref, bbr_ref, bbi_ref, *,
                    seg_len):
    lr, li = lr_ref[...], li_ref[...]
    dt = jnp.exp(ldt_ref[...])
    mag = jnp.exp(lr * dt)
    lb_re, lb_im = mag * jnp.cos(li * dt), mag * jnp.sin(li * dt)
    nr, ni = lb_re - 1.0, lb_im
    den = lr * lr + li * li
    f_re = (nr * lr + ni * li) / den
    f_im = (ni * lr - nr * li) / den
    br, bi = br_ref[...], bi_ref[...]
    bbr_ref[...] = f_re * br - f_im * bi
    bbi_ref[...] = f_re * bi + f_im * br
    pwr_ref[0] = lb_re
    pwi_ref[0] = lb_im
    qr, qi = None, None
    sr, si = lb_re, lb_im
    e = seg_len
    while e:
        if e & 1:
            qr, qi = (sr, si) if qr is None else (qr * sr - qi * si, qr * si + qi * sr)
        e >>= 1
        if e:
            sr, si = sr * sr - si * si, 2.0 * sr * si
    pr, pi = qr, qi
    for n in range(SUBLANES):
        pwr_ref[1 + n] = pr
        pwi_ref[1 + n] = pi
        pr, pi = pr * qr - pi * qi, pr * qi + pi * qr


def _s5_discretize(lam_re, lam_im, log_dt, bt_re, bt_im, seg_len):
    g, _, p = lam_re.shape
    c = bt_re.shape[1]
    return pl.pallas_call(
        functools.partial(_s5_disc_kernel, seg_len=seg_len),
        out_shape=[
            jax.ShapeDtypeStruct((1 + SUBLANES, g, 1, p), F32),
            jax.ShapeDtypeStruct((1 + SUBLANES, g, 1, p), F32),
            jax.ShapeDtypeStruct((g, c, p), F32),
            jax.ShapeDtypeStruct((g, c, p), F32),
        ],
        name="s5_discretize",
    )(lam_re, lam_im, log_dt, bt_re, bt_im)


def _gelu_tanh(x):
    c0 = math.sqrt(2.0 / math.pi)
    return x * (0.5 * (1.0 + jnp.tanh(c0 * (x + 0.044715 * (x * x * x)))))


def _cmul_add(ar, ai, xr, xi, yr, yi):
    return yr + ar * xr - ai * xi, yi + ar * xi + ai * xr


def _s5_seq_kernel(x_ref, mod_ref, gpre_ref, bblk_ref, cblk_ref, lam_ref, dskip_ref, s0r_ref, s0i_ref,
                   z_ref, sr_ref, si_ref, h_scr, xr_scr, xi_scr, *, d):
    cb = pl.program_id(0)
    ncb, tm, uw = h_scr.shape
    cw = xr_scr.shape[1]
    seg = s0r_ref.shape[0]

    @pl.when(cb == 0)
    def _():
        h = _norm_mod(x_ref[...], gpre_ref[...], mod_ref, d)
        for c in range(ncb):
            h_scr[c] = h[:, c * uw:(c + 1) * uw]

    u = h_scr[cb]
    bu = _dot(u.astype(BF16), bblk_ref[...])
    xr_scr[...] = bu[:, 0:cw]
    xi_scr[...] = bu[:, cw:2 * cw]
    car_r, car_i = s0r_ref[...], s0i_ref[...]
    l_r, l_i = lam_ref[0:1, :], lam_ref[1:2, :]
    for t in range(tm // seg):
        rows = slice(t * seg, (t + 1) * seg)
        car_r, car_i = _cmul_add(l_r, l_i, car_r, car_i, xr_scr[rows, :], xi_scr[rows, :])
        xr_scr[rows, :] = car_r
        xi_scr[rows, :] = car_i
    sr_ref[...] = car_r
    si_ref[...] = car_i
    xs = jnp.concatenate([xr_scr[...].astype(BF16), xi_scr[...].astype(BF16)], axis=1)
    y = _dot(xs, cblk_ref[...]) + dskip_ref[...] * u
    z_ref[...] = _gelu_tanh(y).astype(BF16)


def _s5_seq(x2, mod2, g_pre, bblk, cblk, lam2, dskip, s0_re, s0_im):
    tm, d = x2.shape
    seg, nst = s0_re.shape
    ncb, uw, cw2 = bblk.shape
    cw = cw2 // 2
    sspec = pl.BlockSpec((seg, cw), lambda c: (0, c))
    return pl.pallas_call(
        functools.partial(_s5_seq_kernel, d=d),
        grid=(ncb,),
        in_specs=[
            pl.BlockSpec((tm, d), lambda c: (0, 0)),
            pl.BlockSpec((seg, 3 * d), lambda c: (0, 0)),
            pl.BlockSpec((1, d), lambda c: (0, 0)),
            pl.BlockSpec((None, uw, cw2), lambda c: (c, 0, 0)),
            pl.BlockSpec((None, cw2, uw), lambda c: (c, 0, 0)),
            pl.BlockSpec((2, cw), lambda c: (0, c)),
            pl.BlockSpec((1, uw), lambda c: (0, c)),
            sspec, sspec,
        ],
        out_specs=[pl.BlockSpec((tm, uw), lambda c: (0, c)), sspec, sspec],
        out_shape=[
            jax.ShapeDtypeStruct((tm, d), BF16),
            jax.ShapeDtypeStruct((seg, nst), F32),
            jax.ShapeDtypeStruct((seg, nst), F32),
        ],
        scratch_shapes=[
            pltpu.VMEM((ncb, tm, uw), F32),
            pltpu.VMEM((tm, cw), F32),
            pltpu.VMEM((tm, cw), F32),
        ],
        compiler_params=_cparams(("arbitrary",), 48),
        name="s5_seq",
    )(x2, mod2, g_pre, bblk, cblk, lam2, dskip, s0_re, s0_im)


def _s5_rows_kernel(x_ref, mod_ref, gpre_ref, bblk_ref, cblk_ref, tbl_ref, dskip_ref,
                    z_ref, sr_ref, si_ref, h_scr, xr_scr, xi_scr, cr_scr, ci_scr, *, d):
    ncol, tm, _ = xr_scr.shape
    ncb, uw, _ = bblk_ref.shape
    sl = tm // SUBLANES

    @pl.when(pl.program_id(1) == 0)
    def _():
        cr_scr[...] = jnp.zeros_like(cr_scr)
        ci_scr[...] = jnp.zeros_like(ci_scr)

    h_scr[...] = _norm_mod(x_ref[...], gpre_ref[...], mod_ref, d)
    row0 = lax.broadcasted_iota(jnp.int32, (ncol, SUBLANES, LANES), 1) == 0

    for c in range(ncb):
        us = slice(c * uw, (c + 1) * uw)
        cols = slice(c * ncol, (c + 1) * ncol)
        u = h_scr[:, us]
        bu = _dot(u.astype(BF16), bblk_ref[c])
        for j in range(ncol):
            xr_scr[j] = bu[:, j * LANES:(j + 1) * LANES]
            xi_scr[j] = bu[:, (ncol + j) * LANES:(ncol + j + 1) * LANES]
        l_r, l_i = tbl_ref[0, cols], tbl_ref[1, cols]

        def local(i, s):
            rows = pl.ds(pl.multiple_of(i * SUBLANES, SUBLANES), SUBLANES)
            return _cmul_add(l_r, l_i, s[0], s[1], xr_scr[:, rows, :], xi_scr[:, rows, :])

        zero = jnp.zeros((ncol, SUBLANES, LANES), F32)
        g_r, g_i = lax.fori_loop(0, sl, local, (zero, zero), unroll=True)
        for n in range(3):
            g_r, g_i = _cmul_add(tbl_ref[2 + 2 * n, cols], tbl_ref[3 + 2 * n, cols],
                                 pltpu.roll(g_r, 1 << n, 1), pltpu.roll(g_i, 1 << n, 1), g_r, g_i)
        car_r, car_i = cr_scr[cols], ci_scr[cols]
        g_r, g_i = _cmul_add(tbl_ref[8, cols], tbl_ref[9, cols], car_r, car_i, g_r, g_i)
        in_r = jnp.where(row0, car_r, pltpu.roll(g_r, 1, 1))
        in_i = jnp.where(row0, car_i, pltpu.roll(g_i, 1, 1))

        def full(i, s):
            rows = pl.ds(pl.multiple_of(i * SUBLANES, SUBLANES), SUBLANES)
            s_r, s_i = _cmul_add(l_r, l_i, s[0], s[1], xr_scr[:, rows, :], xi_scr[:, rows, :])
            xr_scr[:, rows, :] = s_r
            xi_scr[:, rows, :] = s_i
            return s_r, s_i

        e_r, e_i = lax.fori_loop(0, sl, full, (in_r, in_i), unroll=True)
        cr_scr[cols] = e_r[:, SUBLANES - 1:SUBLANES, :]
        ci_scr[cols] = e_i[:, SUBLANES - 1:SUBLANES, :]
        xs = jnp.concatenate([xr_scr[j].astype(BF16) for j in range(ncol)]
                             + [xi_scr[j].astype(BF16) for j in range(ncol)], axis=1)
        y = _dot(xs, cblk_ref[c]) + dskip_ref[:, us] * u
        z_ref[:, us] = _gelu_tanh(y).astype(BF16)

    sr_ref[...] = cr_scr[...]
    si_ref[...] = ci_scr[...]


def _s5_rows(x3, mod3, g_pre, bblk, cblk, tbl, dskip, tm):
    gn, t, d = x3.shape
    ncb, uw, cw2 = bblk.shape
    ncol = cw2 // 2 // LANES
    nct = ncb * ncol
    nt = t // tm
    const = lambda shape: pl.BlockSpec(shape, lambda g, i: (0,) * len(shape), pipeline_mode=pl.Buffered(1))
    ospec = pl.BlockSpec((None, None, nct, 1, LANES), lambda g, i: (g, i, 0, 0, 0))
    z3, s_re, s_im = pl.pallas_call(
        functools.partial(_s5_rows_kernel, d=d),
        grid=(gn, nt),
        in_specs=[
            pl.BlockSpec((None, tm, d), lambda g, i: (g, i, 0)),
            _mod_spec(1, tm, 3 * d, 2),
            pl.BlockSpec((1, d), lambda g, i: (0, 0)),
            const(bblk.shape), const(cblk.shape), const(tbl.shape), const(dskip.shape),
        ],
        out_specs=[pl.BlockSpec((None, tm, d), lambda g, i: (g, i, 0)), ospec, ospec],
        out_shape=[
            jax.ShapeDtypeStruct((gn, t, d), BF16),
            jax.ShapeDtypeStruct((gn, nt, nct, 1, LANES), F32),
            jax.ShapeDtypeStruct((gn, nt, nct, 1, LANES), F32),
        ],
        scratch_shapes=[
            pltpu.VMEM((tm, d), F32),
            pltpu.VMEM((ncol, tm, LANES), F32),
            pltpu.VMEM((ncol, tm, LANES), F32),
            pltpu.VMEM((nct, 1, LANES), F32),
            pltpu.VMEM((nct, 1, LANES), F32),
        ],
        compiler_params=_cparams(("parallel", "arbitrary"), 48),
        name="s5_rows",
    )(x3, mod3, g_pre, bblk, cblk, tbl, dskip)
    return z3, s_re[:, nt - 1].reshape(gn, 1, nct * LANES), s_im[:, nt - 1].reshape(gn, 1, nct * LANES)


def _glu_kernel(x_ref, mod_ref, g_ref, z_ref, wa_ref, wb_ref, *outs, d):
    z = z_ref[...]
    y = _dot(z, wa_ref[...]) * jax.nn.sigmoid(_dot(z, wb_ref[...]))
    _residual_out(x_ref, y, g_ref, mod_ref, outs, d)


def _glu(x3, mod3, g_post, z3, w_a, w_b, tm, nxt=None):
    gn, t, d = x3.shape
    r = mod3.shape[1]
    wspec = pl.BlockSpec((d, d), lambda g, i: (0, 0), pipeline_mode=pl.Buffered(1))
    n_in, n_out, n_shape, n_ops = _next_specs(None if nxt is None else (*nxt, x3.shape), tm, d)
    res = pl.pallas_call(
        functools.partial(_glu_kernel, d=d),
        grid=(gn, t // tm),
        in_specs=[
            pl.BlockSpec((None, tm, d), lambda g, i: (g, i, 0)),
            _mod_spec(r, tm, 3 * d, 2),
            pl.BlockSpec((1, d), lambda g, i: (0, 0)),
            pl.BlockSpec((None, tm, d), lambda g, i: (g, i, 0)),
            wspec, wspec,
        ] + n_in,
        out_specs=[pl.BlockSpec((None, tm, d), lambda g, i: (g, i, 0))] + n_out,
        out_shape=[jax.ShapeDtypeStruct((gn, t, d), F32)] + n_shape,
        compiler_params=_cparams(("parallel", "parallel"), 56),
        name="glu",
    )(x3, mod3, g_post, z3, w_a, w_b, *n_ops)
    return res if nxt is not None else (res[0], None)


def _rope_tables(pos):
    half = HEAD_DK // 2
    inv = ROPE_BASE ** (-jnp.arange(half, dtype=F32) / half)
    ang = pos.astype(F32)[:, None] * inv[None, :]
    cos, sin = jnp.cos(ang), jnp.sin(ang)
    return jnp.concatenate([cos, cos], axis=-1), jnp.concatenate([-sin, sin], axis=-1)


def _block_diag(w):
    ncb, gpb, a, b = w.shape
    tiled = jnp.broadcast_to(w.reshape(ncb, gpb * a, 1, b), (ncb, gpb * a, gpb, b))
    row_g = jnp.arange(gpb * a)[:, None, None] // a
    col_g = jnp.arange(gpb)[None, :, None]
    return jnp.where(row_g == col_g, tiled, 0.0).reshape(ncb, gpb * a, gpb * b)


def _s5_tables(pw_re, pw_im):
    n = pw_re.shape[0]
    flat_r = pw_re.reshape(n, -1)
    flat_i = pw_im.reshape(n, -1)
    row = jnp.arange(SUBLANES)[:, None]
    tabs = [jnp.broadcast_to(flat_r[0], (SUBLANES, flat_r.shape[1])),
            jnp.broadcast_to(flat_i[0], (SUBLANES, flat_i.shape[1]))]
    for s in (1, 2, 4):
        mask = row >= s
        tabs.append(jnp.where(mask, flat_r[s][None, :], 0.0))
        tabs.append(jnp.where(mask, flat_i[s][None, :], 0.0))
    tabs += [flat_r[1:], flat_i[1:]]
    tbl = jnp.stack(tabs)
    tbl = tbl.reshape(tbl.shape[0], SUBLANES, -1, LANES).transpose(0, 2, 1, 3)
    return tbl, jnp.stack([flat_r[0], flat_i[0]])


def kernel(x_prompt, x_sample, state_gla, state_ret, state_s5_re, state_s5_im, c_prompt, c_sample,
           w_ada, b_ada, norm_pre, norm_post, w_in_mix, w_gla_gk, b_gla_gk, gla_head_norm,
           ret_head_norm, w_out_mix, s5_lam_re, s5_lam_im, s5_log_dt, s5_b_re, s5_b_im,
           s5_c_re, s5_c_im, s5_d, w_glu_a, w_glu_b, w_mlp_up, w_mlp_down):
    bp, tp, d = x_prompt.shape
    bs, ts, _ = x_sample.shape
    depth = w_ada.shape[0]

    nrow = -(-(bs + bp) // SUBLANES) * SUBLANES
    c_all = jnp.concatenate([c_sample, c_prompt, jnp.zeros((nrow - bs - bp, d), F32)], axis=0)
    mod_all = _adaln(c_all, w_ada.reshape(depth * 2, d, 3 * d), b_ada.reshape(depth * 2, 1, 3 * d))
    mod_s = [mod_all[k, 0:bs][None] for k in range(depth * 2)]
    mod_p = [mod_all[k, bs:bs + bp][:, None, :] for k in range(depth * 2)]

    w_in_t = jnp.swapaxes(w_in_mix, 1, 2)
    w_gk = jnp.pad(w_gla_gk[0], ((0, LANES - GLA_RANK), (0, 0))).astype(BF16)
    b_gk = b_gla_gk[0][None, :]
    w_out = w_out_mix[0].astype(BF16)
    w_ga = w_glu_a[0].astype(BF16)
    w_gb = w_glu_b[0].astype(BF16)
    gla_norm = gla_head_norm[0][:, None, :]
    ret_norm = ret_head_norm[0][:, None, :]
    gamma_log = jnp.log1p(-jnp.power(2.0, -5.0 - jnp.arange(RET_HEADS, dtype=F32)))
    ret_lg = jnp.broadcast_to(gamma_log[:, None, None], (RET_HEADS, 1, LANES))

    ng = s5_lam_re.shape[1]
    ncb = ng // S5_GPB
    tm5 = min(TM_S5, tp)
    pw_re, pw_im, bbt_re, bbt_im = _s5_discretize(
        s5_lam_re[0][:, None, :], s5_lam_im[0][:, None, :], s5_log_dt[0][:, None, None],
        jnp.swapaxes(s5_b_re[0], 1, 2), jnp.swapaxes(s5_b_im[0], 1, 2), tm5 // SUBLANES)
    bblk = jnp.concatenate([
        _block_diag(bbt_re.reshape(ncb, S5_GPB, S5_GROUP, S5_STATE)),
        _block_diag(bbt_im.reshape(ncb, S5_GPB, S5_GROUP, S5_STATE))], axis=2).astype(BF16)
    ct_re = jnp.swapaxes(s5_c_re[0], 1, 2).reshape(ncb, S5_GPB, S5_STATE, S5_GROUP)
    ct_im = jnp.swapaxes(s5_c_im[0], 1, 2).reshape(ncb, S5_GPB, S5_STATE, S5_GROUP)
    cblk = jnp.concatenate([_block_diag(ct_re), _block_diag(-ct_im)], axis=1).astype(BF16)
    tbl, lam2 = _s5_tables(pw_re[:, :, 0, :], pw_im[:, :, 0, :])
    dskip = s5_d[0][None, :]

    def trunk(x3, mods, tm, inproj, attn, s5, mlp, emit_h):
        nxt = lambda l: (mods[2 * l + 1], norm_pre[l, 1][None]) if emit_h else None
        proj, glog = inproj(x3, mods[0])
        mg, mr, s_gla, s_ret = attn(proj, glog)
        x3, h3 = _outproj(x3, mods[0], norm_post[0, 0][None], mg, mr, w_out, tm, nxt(0))
        x3 = mlp(0, x3, h3, mods[1])
        z3, s_re, s_im = s5(x3, mods[2])
        x3, h3 = _glu(x3, mods[2], norm_post[1, 0][None], z3, w_ga, w_gb, tm, nxt(1))
        x3 = mlp(1, x3, h3, mods[3])
        return x3, s_gla, s_ret, s_re, s_im

    pos_s = PAST_LEN + jnp.arange(T_PAD, dtype=F32)
    cos_s, sin_s = _rope_tables(pos_s)

    def attn_s(proj, glog):
        def to_bm(a):
            a = jnp.swapaxes(a.reshape(ts, bs, a.shape[-1]), 0, 1)
            return jnp.pad(a, ((0, 0), (0, T_PAD - ts), (0, 0)))
        mg, mr, s_gla, s_ret = _attn_sample(to_bm(proj), to_bm(glog), gla_norm, ret_norm, cos_s, sin_s,
                                            ret_lg, state_gla[0], state_ret[0], ts)
        to_tm = lambda a: jnp.swapaxes(a[:, :ts], 0, 1).reshape(1, ts * bs, a.shape[-1])
        return to_tm(mg), to_tm(mr), s_gla, s_ret

    xs3 = jnp.swapaxes(x_sample, 0, 1).reshape(1, ts * bs, d)

    def s5_s(x3, mod3):
        z2, s_re, s_im = _s5_seq(x3[0], mod3[0], norm_pre[1, 0][None], bblk, cblk, lam2, dskip,
                                 state_s5_re[0].reshape(bs, -1), state_s5_im[0].reshape(bs, -1))
        return z2[None], s_re, s_im

    w_up, w_dn, w_main = {}, {}, {}

    def inproj_s(x3, mod3):
        proj, glog, w_main[0], w_main["lr"] = _inproj_cast(x3, mod3, norm_pre[0, 0][None], w_in_t, 0,
                                                            w_gk, b_gk)
        return proj, glog

    def mlp_s(l, x3, h3, mod3):
        x3, w_up[l], w_dn[l] = _mlp_cast(x3, mod3, norm_pre[l, 1][None], norm_post[l, 1][None],
                                         w_mlp_up, w_mlp_down, l)
        return x3

    y_s, gla_s, ret_s, re_s, im_s = trunk(xs3, mod_s, ts * bs, inproj_s, attn_s, s5_s, mlp_s, False)
    y_s = jnp.swapaxes(y_s.reshape(ts, bs, d), 0, 1)

    cos_p, sin_p = _rope_tables(jnp.arange(tp, dtype=F32))
    zeros_att = jnp.zeros((bp, GLA_HEADS, HEAD_DK, HEAD_DV), F32)
    tm_p = min(TM_DENSE, tp)

    def attn_p(proj, glog):
        return _attn_prompt(proj, glog, gla_norm, ret_norm, cos_p, sin_p, ret_lg, zeros_att, zeros_att)

    def s5_p(x3, mod3):
        sl = tm5 // SUBLANES
        xp = jnp.swapaxes(x3.reshape(bp, tp // tm5, SUBLANES, sl, d), 2, 3).reshape(bp, tp, d)
        zp, s_re, s_im = _s5_rows(xp, mod3, norm_pre[1, 0][None], bblk, cblk, tbl, dskip, tm5)
        z3 = jnp.swapaxes(zp.reshape(bp, tp // tm5, sl, SUBLANES, d), 2, 3).reshape(bp, tp, d)
        return z3, s_re, s_im

    def mlp_p(l, x3, h3, mod3):
        return _mlp_h(x3, h3, mod3, norm_post[l, 1][None], w_up[l], w_dn[l], tm_p)

    def inproj_p(x3, mod3):
        return _inproj(x3, mod3, norm_pre[0, 0][None], w_main[0], w_main["lr"], w_gk, b_gk, tm_p)

    y_p, gla_p, ret_p, re_p, im_p = trunk(x_prompt, mod_p, tm_p, inproj_p, attn_p, s5_p, mlp_p, True)

    st = lambda a, b_: a.reshape(1, b_, ng, S5_STATE)
    return (y_p, y_s, gla_p[None], gla_s[None], ret_p[None], ret_s[None],
            st(re_p, bp), st(re_s, bs), st(im_p, bp), st(im_s, bs))
```

```python
import functools
import math

import jax
import jax.numpy as jnp
import numpy as np
from jax import lax
from jax.experimental import pallas as pl
from jax.experimental.pallas import tpu as pltpu

F32 = jnp.float32
BF16 = jnp.bfloat16

EPS = 1e-6
LANES = 128
SUBLANES = 8
MIB = 1024 * 1024

GLA_HEADS = 4
RET_HEADS = 4
HEAD_DK = 128
HEAD_DV = 256
GLA_RANK = 16
GLA_LOGIT_NORM = 16.0
ROPE_BASE = 10000.0
PAST_LEN = 16384
S5_GROUP = 16
S5_STATE = 64
S5_GPB = 16
S5_UW = S5_GPB * S5_GROUP
ATT_CHUNK = 128
GLA_SUB = 16
TM_DENSE = 512
TM_S5 = 256
TN_INPROJ = 1024
TF_MLP = 1024


def _cparams(sem, vmem_mib):
    return pltpu.CompilerParams(dimension_semantics=sem, vmem_limit_bytes=vmem_mib * MIB)


def _dot(a, b):
    return jnp.dot(a, b, preferred_element_type=F32)


def _dot_nt(a, b):
    return lax.dot_general(a, b, (((1,), (1,)), ((), ())), preferred_element_type=F32)


def _rms(x, g):
    return x * lax.rsqrt(jnp.mean(x * x, axis=-1, keepdims=True) + EPS) * g


def _rows_affine(y, a, b=None):
    tm, d = y.shape
    r = a.shape[0]
    if r == 1 or r == tm:
        out = y * a
        return out if b is None else out + b
    y3 = y.reshape(tm // r, r, d)
    out = y3 * a[None]
    if b is not None:
        out = out + b[None]
    return out.reshape(tm, d)


def _norm_mod(x, g, mod_ref, d):
    return _rows_affine(_rms(x, g), 1.0 + mod_ref[:, d:2 * d], mod_ref[:, 0:d])


def _gated_residual(x, y, g, mod_ref, d):
    return x + _rows_affine(_rms(y, g), mod_ref[:, 2 * d:3 * d])


def _mod_spec(r, tm, width, ngrid):
    if ngrid == 2:
        if r == 1:
            return pl.BlockSpec((None, 1, width), lambda g, i: (g, 0, 0))
        return pl.BlockSpec((None, r, width), lambda g, i: (g, 0, 0))
    if r == 1:
        return pl.BlockSpec((None, 1, width), lambda g, i, j: (g, 0, 0))
    return pl.BlockSpec((None, r, width), lambda g, i, j: (g, 0, 0))


def _adaln_kernel(c_ref, w_ref, b_ref, o_ref):
    c = c_ref[...]
    sc = (c * jax.nn.sigmoid(c)).astype(BF16)
    o_ref[...] = _dot(sc, w_ref[...].astype(BF16)) + b_ref[...]


def _adaln(c_all, w_ada, b_ada, tn=1024):
    ls, d, n = w_ada.shape
    rows = c_all.shape[0]
    return pl.pallas_call(
        _adaln_kernel,
        grid=(ls, n // tn),
        in_specs=[
            pl.BlockSpec((rows, d), lambda l, j: (0, 0)),
            pl.BlockSpec((None, d, tn), lambda l, j: (l, 0, j)),
            pl.BlockSpec((None, 1, tn), lambda l, j: (l, 0, j)),
        ],
        out_specs=pl.BlockSpec((None, rows, tn), lambda l, j: (l, 0, j)),
        out_shape=jax.ShapeDtypeStruct((ls, rows, n), F32),
        compiler_params=_cparams(("parallel", "parallel"), 40),
        name="adaln",
    )(c_all, w_ada, b_ada)


def _log_sigmoid(x):
    return jnp.minimum(x, 0.0) - jnp.log1p(jnp.exp(-jnp.abs(x)))


def _gate_logits(hb, wlr_t, wgk_ref, bgk_ref, glog_ref):
    glr = _dot_nt(hb, wlr_t)
    logit = _dot(glr.astype(BF16), wgk_ref[...]) + bgk_ref[...]
    glog_ref[...] = _log_sigmoid(logit) * (1.0 / GLA_LOGIT_NORM)


def _inproj_kernel(x_ref, mod_ref, g_ref, w_ref, wlr_ref, wgk_ref, bgk_ref,
                   proj_ref, glog_ref, h_scr, *, d, tps):
    j = pl.program_id(2)

    @pl.when(j == 0)
    def _():
        hb = _norm_mod(x_ref[...], g_ref[...], mod_ref, d).astype(BF16)
        h_scr[...] = hb
        _gate_logits(hb, wlr_ref[...], wgk_ref, bgk_ref, glog_ref)

    tn = w_ref.shape[2]
    for k in range(tps):
        proj_ref[:, k * tn:(k + 1) * tn] = _dot(h_scr[...], w_ref[j * tps + k])


def _inproj_cast_kernel(x_ref, mod_ref, g_ref, wa_ref, wb_ref, wgk_ref, bgk_ref,
                        proj_ref, glog_ref, wout_ref, wlr_ref, h_scr, *, d, n_lo):
    j = pl.program_id(2)

    @pl.when(j == 0)
    def _():
        h_scr[...] = _norm_mod(x_ref[...], g_ref[...], mod_ref, d).astype(BF16)

    @pl.when(j < n_lo)
    def _():
        wout_ref[...] = jnp.transpose(wa_ref[...]).astype(BF16)

    @pl.when(j >= n_lo)
    def _():
        w = jnp.concatenate([wa_ref[GLA_RANK:, :], wb_ref[:GLA_RANK, :]], axis=0)
        wout_ref[...] = jnp.transpose(w).astype(BF16)

    @pl.when(j == n_lo)
    def _():
        wlr_t = _pad_rows(wa_ref[:GLA_RANK, :], LANES).astype(BF16)
        wlr_ref[...] = wlr_t
        _gate_logits(h_scr[...], wlr_t, wgk_ref, bgk_ref, glog_ref)

    proj_ref[...] = _dot(h_scr[...], wout_ref[...])


def _inproj_cast(x3, mod3, g_pre, w_raw_t, layer, w_gk, b_gk, tn=TN_INPROJ):
    gn, t, d = x3.shape
    assert gn == 1
    sec = (w_raw_t.shape[1] - GLA_RANK) // 2
    assert sec % tn == 0 and tn % LANES == 0
    nj = 2 * sec // tn
    n = nj * tn
    r = mod3.shape[1]
    gkey = w_gk.shape[1]
    return pl.pallas_call(
        functools.partial(_inproj_cast_kernel, d=d, n_lo=sec // tn),
        grid=(1, 1, nj),
        in_specs=[
            pl.BlockSpec((None, t, d), lambda g, i, j: (0, 0, 0)),
            _mod_spec(r, t, 3 * d, 3),
            pl.BlockSpec((1, d), lambda g, i, j: (0, 0)),
            pl.BlockSpec((None, tn, d), lambda g, i, j: (layer, j, 0)),
            pl.BlockSpec((None, LANES, d), lambda g, i, j: (layer, (j + 1) * (tn // LANES), 0)),
            pl.BlockSpec((LANES, gkey), lambda g, i, j: (0, 0)),
            pl.BlockSpec((1, gkey), lambda g, i, j: (0, 0)),
        ],
        out_specs=[
            pl.BlockSpec((None, t, tn), lambda g, i, j: (0, 0, j)),
            pl.BlockSpec((None, t, gkey), lambda g, i, j: (0, 0, 0)),
            pl.BlockSpec((None, d, tn), lambda g, i, j: (j, 0, 0)),
            pl.BlockSpec((LANES, d), lambda g, i, j: (0, 0)),
        ],
        out_shape=[
            jax.ShapeDtypeStruct((1, t, n), F32),
            jax.ShapeDtypeStruct((1, t, gkey), F32),
            jax.ShapeDtypeStruct((nj, d, tn), BF16),
            jax.ShapeDtypeStruct((LANES, d), BF16),
        ],
        scratch_shapes=[pltpu.VMEM((t, d), BF16)],
        compiler_params=_cparams(("arbitrary", "arbitrary", "arbitrary"), 56),
        name="inproj_cast",
    )(x3, mod3, g_pre, w_raw_t, w_raw_t, w_gk, b_gk)


def _inproj(x3, mod3, g_pre, w_main, w_lr_t, w_gk, b_gk, tm, tps=2):
    gn, t, d = x3.shape
    nj, _, tn = w_main.shape
    n = nj * tn
    r = mod3.shape[1]
    gkey = w_gk.shape[1]
    return pl.pallas_call(
        functools.partial(_inproj_kernel, d=d, tps=tps),
        grid=(gn, t // tm, nj // tps),
        in_specs=[
            pl.BlockSpec((None, tm, d), lambda g, i, j: (g, i, 0)),
            _mod_spec(r, tm, 3 * d, 3),
            pl.BlockSpec((1, d), lambda g, i, j: (0, 0)),
            pl.BlockSpec((nj, d, tn), lambda g, i, j: (0, 0, 0), pipeline_mode=pl.Buffered(1)),
            pl.BlockSpec((LANES, d), lambda g, i, j: (0, 0)),
            pl.BlockSpec((LANES, gkey), lambda g, i, j: (0, 0)),
            pl.BlockSpec((1, gkey), lambda g, i, j: (0, 0)),
        ],
        out_specs=[
            pl.BlockSpec((None, tm, tps * tn), lambda g, i, j: (g, i, j)),
            pl.BlockSpec((None, tm, gkey), lambda g, i, j: (g, i, 0)),
        ],
        out_shape=[
            jax.ShapeDtypeStruct((gn, t, n), F32),
            jax.ShapeDtypeStruct((gn, t, gkey), F32),
        ],
        scratch_shapes=[pltpu.VMEM((tm, d), BF16)],
        compiler_params=_cparams(("parallel", "parallel", "arbitrary"), 56),
        name="inproj",
    )(x3, mod3, g_pre, w_main, w_lr_t, w_gk, b_gk)


def _cumsum_rows(g):
    c = g.shape[0]
    row = lax.broadcasted_iota(jnp.int32, g.shape, 0)
    s = 1
    while s < c:
        g = g + jnp.where(row >= s, pltpu.roll(g, s, 0), 0.0)
        s *= 2
    return g


def _pad_rows(a, rows):
    if a.shape[0] == rows:
        return a
    return jnp.concatenate([a, jnp.zeros((rows - a.shape[0], a.shape[1]), a.dtype)], axis=0)


def _col_bcast(row, width):
    sq = jnp.transpose(jnp.broadcast_to(row, (LANES, LANES)))
    return jnp.concatenate([sq] * (width // LANES), axis=1)


def _gla_core(q, k, v, g, s, sub):
    cq = q.shape[0]
    ck = max(cq, LANES)
    b = _cumsum_rows(g)
    be = b - g
    bk = _pad_rows(b, ck)
    kp = _pad_rows(k, ck)
    vp = _pad_rows(v, ck).astype(BF16)
    rowj = lax.broadcasted_iota(jnp.int32, (ck, 1), 0)
    att_rows = []
    for blk in range(cq // sub):
        lo, hi = blk * sub, (blk + 1) * sub
        base = be[lo:lo + 1, :]
        qs = q[lo:hi] * jnp.exp(b[lo:hi] - base)
        ks = jnp.where(rowj < hi, kp * jnp.exp(base - bk), 0.0)
        att_rows.append(_dot_nt(qs.astype(BF16), ks.astype(BF16)))
    att = att_rows[0] if len(att_rows) == 1 else jnp.concatenate(att_rows, axis=0)
    ri = lax.broadcasted_iota(jnp.int32, (cq, ck), 0)
    cj = lax.broadcasted_iota(jnp.int32, (cq, ck), 1)
    att = jnp.where(ri >= cj, att, 0.0)
    o = _dot(att.astype(BF16), vp) + _dot((q * jnp.exp(b)).astype(BF16), s.astype(BF16))
    b_last = b[cq - 1:cq, :]
    k_out = kp * jnp.exp(b_last - bk)
    s_new = s * _col_bcast(jnp.exp(b_last), s.shape[1]) + _dot(jnp.transpose(k_out).astype(BF16), vp)
    return o, s_new


def _ret_core(q, k, v, s, lg, dmat, valid):
    cq = q.shape[0]
    ck = max(cq, LANES)
    kp = _pad_rows(k, ck)
    vp = _pad_rows(v, ck).astype(BF16)
    ti = lax.broadcasted_iota(jnp.int32, (cq, 1), 0).astype(F32)
    tj = lax.broadcasted_iota(jnp.int32, (ck, 1), 0).astype(F32)
    att = _dot_nt(q.astype(BF16), kp.astype(BF16)) * dmat
    q_in = q * jnp.exp((ti + 1.0) * lg)
    o = _dot(att.astype(BF16), vp) + _dot(q_in.astype(BF16), s.astype(BF16))
    k_out = kp * jnp.exp((float(valid - 1) - tj) * lg)
    s_new = s * jnp.exp(float(valid) * lg) + _dot(jnp.transpose(k_out).astype(BF16), vp)
    return o, s_new


def _decay_matrix(cq, ck, lg):
    ri = lax.broadcasted_iota(jnp.int32, (cq, ck), 0)
    cj = lax.broadcasted_iota(jnp.int32, (cq, ck), 1)
    diff = (ri - cj).astype(F32)
    return jnp.where(ri >= cj, jnp.exp(diff * lg), 0.0)


def _rope(x, cosf, sinf):
    return x * cosf + pltpu.roll(x, x.shape[1] // 2, 1) * sinf


def _silu(x):
    return x * jax.nn.sigmoid(x)


def _gla_finish(o, gate, gn):
    o = o * lax.rsqrt(jnp.mean(o * o, axis=-1, keepdims=True) + EPS) * gn
    return (o * _silu(gate)).astype(BF16)


def _ret_finish(o, gate, gn):
    oc = o - jnp.mean(o, axis=-1, keepdims=True)
    oc = oc * lax.rsqrt(jnp.mean(oc * oc, axis=-1, keepdims=True) + EPS) * gn
    return (oc * _silu(gate)).astype(BF16)


def _head(ref, h, width):
    return ref[:, h * width:(h + 1) * width]


def _gla_prompt_kernel(q_ref, k_ref, v_ref, gg_ref, gl_ref, gn_ref, s0_ref, o_ref, s_ref):
    @pl.when(pl.program_id(1) == 0)
    def _():
        s_ref[...] = s0_ref[...]

    for h in range(s_ref.shape[0]):
        q = _head(q_ref, h, HEAD_DK) * (HEAD_DK ** -0.5)
        o, s_new = _gla_core(q, _head(k_ref, h, HEAD_DK), _head(v_ref, h, HEAD_DV),
                             _head(gl_ref, h, HEAD_DK), s_ref[h], GLA_SUB)
        s_ref[h] = s_new
        o_ref[:, h * HEAD_DV:(h + 1) * HEAD_DV] = _gla_finish(o, _head(gg_ref, h, HEAD_DV), gn_ref[h])


def _ret_prompt_kernel(q_ref, k_ref, v_ref, rg_ref, cos_ref, sin_ref, lg_ref, gn_ref, s0_ref,
                       o_ref, s_ref, d_scr):
    @pl.when(pl.program_id(1) == 0)
    def _():
        s_ref[...] = s0_ref[...]
        for h in range(d_scr.shape[0]):
            d_scr[h] = _decay_matrix(d_scr.shape[1], d_scr.shape[2], lg_ref[h][:, 0:1])

    cosf, sinf = cos_ref[...], sin_ref[...]
    for h in range(s_ref.shape[0]):
        q = _rope(_head(q_ref, h, HEAD_DK), cosf, sinf)
        k = _rope(_head(k_ref, h, HEAD_DK), cosf, sinf) * (HEAD_DK ** -0.5)
        o, s_new = _ret_core(q, k, _head(v_ref, h, HEAD_DV), s_ref[h], lg_ref[h][:, 0:1], d_scr[h],
                             q.shape[0])
        s_ref[h] = s_new
        o_ref[:, h * HEAD_DV:(h + 1) * HEAD_DV] = _ret_finish(o, _head(rg_ref, h, HEAD_DV), gn_ref[h])


def _attn_prompt(proj, glog, gla_norm, ret_norm, cosf, sinf, ret_lg, s0_gla, s0_ret):
    bsz, t, _ = proj.shape
    c = ATT_CHUNK
    nh = GLA_HEADS
    grid = (bsz, t // c)
    kw, vw = nh * HEAD_DK, nh * HEAD_DV
    kspec = lambda blk: pl.BlockSpec((None, c, kw), lambda b, i, blk=blk: (b, i, blk))
    vspec = lambda blk: pl.BlockSpec((None, c, vw), lambda b, i, blk=blk: (b, i, blk))
    hspec = pl.BlockSpec((nh, 1, HEAD_DV), lambda b, i: (0, 0, 0))
    sspec = pl.BlockSpec((None, nh, HEAD_DK, HEAD_DV), lambda b, i: (b, 0, 0, 0))
    ospec = pl.BlockSpec((None, c, vw), lambda b, i: (b, i, 0))
    out_shape = [
        jax.ShapeDtypeStruct((bsz, t, vw), BF16),
        jax.ShapeDtypeStruct((bsz, nh, HEAD_DK, HEAD_DV), F32),
    ]
    params = _cparams(("parallel", "arbitrary"), 32)
    mg, s_gla = pl.pallas_call(
        _gla_prompt_kernel,
        grid=grid,
        in_specs=[kspec(0), kspec(1), vspec(1), vspec(2), kspec(0), hspec, sspec],
        out_specs=[ospec, sspec],
        out_shape=out_shape,
        compiler_params=params,
        name="gla_prompt",
    )(proj, proj, proj, proj, glog, gla_norm, s0_gla)
    tspec = pl.BlockSpec((c, HEAD_DK), lambda b, i: (i, 0))
    mr, s_ret = pl.pallas_call(
        _ret_prompt_kernel,
        grid=grid,
        in_specs=[kspec(6), kspec(7), vspec(4), vspec(5), tspec, tspec,
                  pl.BlockSpec((nh, 1, LANES), lambda b, i: (0, 0, 0)), hspec, sspec],
        out_specs=[ospec, sspec],
        out_shape=out_shape,
        scratch_shapes=[pltpu.VMEM((nh, c, c), F32)],
        compiler_params=params,
        name="ret_prompt",
    )(proj, proj, proj, proj, cosf, sinf, ret_lg, ret_norm, s0_ret)
    return mg, mr, s_gla, s_ret


def _seq_rows(ref):
    ts, bb, w = ref.shape
    return ref[...].reshape(ts * bb, w)


def _seq_masks(n, bb):
    r = lax.broadcasted_iota(jnp.int32, (n, n), 0)
    c = lax.broadcasted_iota(jnp.int32, (n, n), 1)
    return (r % bb == c % bb) & (r >= c), (r - c).astype(F32) * (1.0 / bb)


def _seq_state_terms(q_in, k_out, v, s0, bb):
    n, dk = q_in.shape
    rown = lax.broadcasted_iota(jnp.int32, (n, 1), 0) % bb
    q_bd = jnp.concatenate([jnp.where(rown == j, q_in, 0.0) for j in range(bb)], axis=1)
    o_inter = _dot(q_bd.astype(BF16), s0.astype(BF16))
    k_t = jnp.transpose(_pad_rows(k_out, LANES))
    coln = lax.broadcasted_iota(jnp.int32, (1, LANES), 1) % bb
    k_bd = jnp.concatenate([jnp.where(coln == j, k_t, 0.0) for j in range(bb)], axis=0)
    ds = _dot(k_bd.astype(BF16), _pad_rows(v, LANES).astype(BF16))
    return o_inter, ds


def _gla_sample_kernel(q_ref, k_ref, v_ref, gg_ref, gl_ref, gn_ref, s0_ref, o_ref, s_ref):
    ts, bb, dk = q_ref.shape
    dv = v_ref.shape[2]
    n = ts * bb
    q = _seq_rows(q_ref) * (dk ** -0.5)
    k, v, g = _seq_rows(k_ref), _seq_rows(v_ref), _seq_rows(gl_ref)
    steps = [g[0:bb]]
    for t in range(1, ts):
        steps.append(steps[-1] + g[t * bb:(t + 1) * bb])
    b = jnp.concatenate(steps, axis=0)
    b_last = steps[-1]
    q_in = q * jnp.exp(b)
    mask, _ = _seq_masks(n, bb)
    att = jnp.where(mask, _dot_nt(q_in.astype(BF16), (k * jnp.exp(-b)).astype(BF16)), 0.0)
    k_out = k * jnp.exp(jnp.concatenate([b_last] * ts, axis=0) - b)
    s0 = s0_ref[...].reshape(bb * dk, dv)
    o_inter, ds = _seq_state_terms(q_in, k_out, v, s0, bb)
    o = _dot(att.astype(BF16), v.astype(BF16)) + o_inter
    e_last = jnp.exp(b_last)
    dec = jnp.concatenate([_col_bcast(e_last[j:j + 1, :], dv) for j in range(bb)], axis=0)
    s_ref[...] = (s0 * dec + ds).reshape(bb, dk, dv)
    o_ref[...] = _gla_finish(o, _seq_rows(gg_ref), gn_ref[...]).reshape(ts, bb, dv)


def _ret_sample_kernel(q_ref, k_ref, v_ref, rg_ref, cos_ref, sin_ref, lg_ref, gn_ref, s0_ref,
                       o_ref, s_ref):
    ts, bb, dk = q_ref.shape
    dv = v_ref.shape[2]
    n = ts * bb
    lg = lg_ref[:, 0:1]
    rows = lambda tab: jnp.concatenate(
        [jnp.broadcast_to(tab[t:t + 1, :], (bb, dk)) for t in range(ts)], axis=0)
    cosf, sinf = rows(cos_ref[...]), rows(sin_ref[...])
    q = _rope(_seq_rows(q_ref), cosf, sinf)
    k = _rope(_seq_rows(k_ref), cosf, sinf) * (dk ** -0.5)
    v = _seq_rows(v_ref)
    mask, dt = _seq_masks(n, bb)
    att = _dot_nt(q.astype(BF16), k.astype(BF16)) * jnp.where(mask, jnp.exp(dt * lg), 0.0)
    tt = (lax.broadcasted_iota(jnp.int32, (n, 1), 0) // bb).astype(F32)
    q_in = q * jnp.exp((tt + 1.0) * lg)
    k_out = k * jnp.exp((float(ts - 1) - tt) * lg)
    s0 = s0_ref[...].reshape(bb * dk, dv)
    o_inter, ds = _seq_state_terms(q_in, k_out, v, s0, bb)
    o = _dot(att.astype(BF16), v.astype(BF16)) + o_inter
    s_ref[...] = (s0 * jnp.exp(float(ts) * lg) + ds).reshape(bb, dk, dv)
    o_ref[...] = _ret_finish(o, _seq_rows(rg_ref), gn_ref[...]).reshape(ts, bb, dv)


def _attn_sample(proj, glog, gla_norm, ret_norm, cosf, sinf, ret_lg, s0_gla, s0_ret, bb=16):
    ts, bsz, _ = proj.shape
    grid = (bsz // bb, GLA_HEADS)
    nk = GLA_HEADS
    kspec = lambda off: pl.BlockSpec((ts, bb, HEAD_DK), lambda i, h, off=off: (0, i, off + h))
    vspec = lambda off: pl.BlockSpec((ts, bb, HEAD_DV), lambda i, h, off=off: (0, i, off + h))
    hspec = pl.BlockSpec((None, 1, HEAD_DV), lambda i, h: (h, 0, 0))
    sspec = pl.BlockSpec((bb, None, HEAD_DK, HEAD_DV), lambda i, h: (i, h, 0, 0))
    ospec = pl.BlockSpec((ts, bb, HEAD_DV), lambda i, h: (0, i, h))
    out_shape = [
        jax.ShapeDtypeStruct((ts, bsz, GLA_HEADS * HEAD_DV), BF16),
        jax.ShapeDtypeStruct((bsz, GLA_HEADS, HEAD_DK, HEAD_DV), F32),
    ]
    params = _cparams(("parallel", "parallel"), 40)
    mg, s_gla = pl.pallas_call(
        _gla_sample_kernel,
        grid=grid,
        in_specs=[kspec(0), kspec(nk), vspec(nk), vspec(2 * nk),
                  pl.BlockSpec((ts, bb, HEAD_DK), lambda i, h: (0, i, h)),
                  hspec, sspec],
        out_specs=[ospec, sspec],
        out_shape=out_shape,
        compiler_params=params,
        name="gla_sample",
    )(proj, proj, proj, proj, glog, gla_norm, s0_gla)
    tspec = pl.BlockSpec((ts, HEAD_DK), lambda i, h: (0, 0))
    mr, s_ret = pl.pallas_call(
        _ret_sample_kernel,
        grid=grid,
        in_specs=[kspec(6 * nk), kspec(7 * nk), vspec(4 * nk), vspec(5 * nk),
                  tspec, tspec,
                  pl.BlockSpec((None, 1, LANES), lambda i, h: (h, 0, 0)),
                  hspec, sspec],
        out_specs=[ospec, sspec],
        out_shape=out_shape,
        compiler_params=params,
        name="ret_sample",
    )(proj, proj, proj, proj, cosf, sinf, ret_lg, ret_norm, s0_ret)
    return mg, mr, s_gla, s_ret


def _residual_out(x_ref, y, g_ref, mod_ref, outs, d):
    x_new = _gated_residual(x_ref[...], y, g_ref[...], mod_ref, d)
    if len(outs) == 1:
        outs[0][...] = x_new
    else:
        modn_ref, gn_ref, o_ref, h_ref = outs
        o_ref[...] = x_new
        h_ref[...] = _norm_mod(x_new, gn_ref[...], modn_ref, d).astype(BF16)


def _next_specs(nxt, tm, d):
    if nxt is None:
        return [], [], [], []
    mod_next, g_next, shape = nxt
    return ([_mod_spec(mod_next.shape[1], tm, 3 * d, 2), pl.BlockSpec((1, d), lambda g, i: (0, 0))],
            [pl.BlockSpec((None, tm, d), lambda g, i: (g, i, 0))],
            [jax.ShapeDtypeStruct(shape, BF16)], [mod_next, g_next])


def _outproj_kernel(x_ref, mod_ref, g_ref, mg_ref, mr_ref, wo_ref, *outs, d):
    half = mg_ref.shape[1]
    y = _dot(mg_ref[...], wo_ref[0:half, :]) + _dot(mr_ref[...], wo_ref[half:2 * half, :])
    _residual_out(x_ref, y, g_ref, mod_ref, outs, d)


def _outproj(x3, mod3, g_post, mg, mr, w_out, tm, nxt=None):
    gn, t, d = x3.shape
    r = mod3.shape[1]
    half = mg.shape[2]
    n_in, n_out, n_shape, n_ops = _next_specs(None if nxt is None else (*nxt, x3.shape), tm, d)
    res = pl.pallas_call(
        functools.partial(_outproj_kernel, d=d),
        grid=(gn, t // tm),
        in_specs=[
            pl.BlockSpec((None, tm, d), lambda g, i: (g, i, 0)),
            _mod_spec(r, tm, 3 * d, 2),
            pl.BlockSpec((1, d), lambda g, i: (0, 0)),
            pl.BlockSpec((None, tm, half), lambda g, i: (g, i, 0)),
            pl.BlockSpec((None, tm, half), lambda g, i: (g, i, 0)),
            pl.BlockSpec((2 * half, d), lambda g, i: (0, 0)),
        ] + n_in,
        out_specs=[pl.BlockSpec((None, tm, d), lambda g, i: (g, i, 0))] + n_out,
        out_shape=[jax.ShapeDtypeStruct((gn, t, d), F32)] + n_shape,
        compiler_params=_cparams(("parallel", "parallel"), 56),
        name="outproj",
    )(x3, mod3, g_post, mg, mr, w_out, *n_ops)
    return res if nxt is not None else (res[0], None)


def _mlp_kernel(x_ref, mod_ref, gpre_ref, gpost_ref, wup_ref, wdn_ref, o_ref, h_scr, acc_scr, *, d):
    j = pl.program_id(2)

    @pl.when(j == 0)
    def _():
        h_scr[...] = _norm_mod(x_ref[...], gpre_ref[...], mod_ref, d).astype(BF16)
        acc_scr[...] = jnp.zeros_like(acc_scr)

    u = jnp.maximum(_dot(h_scr[...], wup_ref[...]), 0.0)
    acc_scr[...] += _dot((u * u).astype(BF16), wdn_ref[...])

    @pl.when(j == pl.num_programs(2) - 1)
    def _():
        o_ref[...] = _gated_residual(x_ref[...], acc_scr[...], gpost_ref[...], mod_ref, d)


def _mlp_cast_kernel(x_ref, mod_ref, gpre_ref, gpost_ref, wup_ref, wdn_ref,
                     o_ref, wupb_ref, wdnb_ref, h_scr, acc_scr, *, d):
    wupb_ref[...] = wup_ref[...].astype(BF16)
    wdnb_ref[...] = wdn_ref[...].astype(BF16)
    _mlp_kernel(x_ref, mod_ref, gpre_ref, gpost_ref, wupb_ref, wdnb_ref, o_ref, h_scr, acc_scr, d=d)


def _mlp_cast(x3, mod3, g_pre, g_post, w_up_all, w_down_all, layer, tf=512):
    gn, t, d = x3.shape
    assert gn == 1
    r = mod3.shape[1]
    f = w_up_all.shape[2]
    return pl.pallas_call(
        functools.partial(_mlp_cast_kernel, d=d),
        grid=(1, 1, f // tf),
        in_specs=[
            pl.BlockSpec((None, t, d), lambda g, i, j: (0, 0, 0)),
            _mod_spec(r, t, 3 * d, 3),
            pl.BlockSpec((1, d), lambda g, i, j: (0, 0)),
            pl.BlockSpec((1, d), lambda g, i, j: (0, 0)),
            pl.BlockSpec((None, d, tf), lambda g, i, j: (layer, 0, j)),
            pl.BlockSpec((None, tf, d), lambda g, i, j: (layer, j, 0)),
        ],
        out_specs=[
            pl.BlockSpec((None, t, d), lambda g, i, j: (0, 0, 0)),
            pl.BlockSpec((d, tf), lambda g, i, j: (0, j)),
            pl.BlockSpec((tf, d), lambda g, i, j: (j, 0)),
        ],
        out_shape=[
            jax.ShapeDtypeStruct((1, t, d), F32),
            jax.ShapeDtypeStruct((d, f), BF16),
            jax.ShapeDtypeStruct((f, d), BF16),
        ],
        scratch_shapes=[pltpu.VMEM((t, d), BF16), pltpu.VMEM((t, d), F32)],
        compiler_params=_cparams(("arbitrary", "arbitrary", "arbitrary"), 56),
        name="mlp_cast",
    )(x3, mod3, g_pre, g_post, w_up_all, w_down_all)


def _mlp_h_kernel(x_ref, h_ref, mod_ref, gpost_ref, wup_ref, wdn_ref, o_ref, acc_scr, *, d):
    j = pl.program_id(2)

    @pl.when(j == 0)
    def _():
        acc_scr[...] = jnp.zeros_like(acc_scr)

    u = jnp.maximum(_dot(h_ref[...], wup_ref[...]), 0.0)
    acc_scr[...] += _dot((u * u).astype(BF16), wdn_ref[...])

    @pl.when(j == pl.num_programs(2) - 1)
    def _():
        o_ref[...] = _gated_residual(x_ref[...], acc_scr[...], gpost_ref[...], mod_ref, d)


def _mlp_h(x3, h3, mod3, g_post, w_up, w_down, tm, tf=TF_MLP):
    gn, t, d = x3.shape
    r = mod3.shape[1]
    f = w_up.shape[1]
    return pl.pallas_call(
        functools.partial(_mlp_h_kernel, d=d),
        grid=(gn, t // tm, f // tf),
        in_specs=[
            pl.BlockSpec((None, tm, d), lambda g, i, j: (g, i, 0)),
            pl.BlockSpec((None, tm, d), lambda g, i, j: (g, i, 0)),
            _mod_spec(r, tm, 3 * d, 3),
            pl.BlockSpec((1, d), lambda g, i, j: (0, 0)),
            pl.BlockSpec((d, tf), lambda g, i, j: (0, j)),
            pl.BlockSpec((tf, d), lambda g, i, j: (j, 0)),
        ],
        out_specs=pl.BlockSpec((None, tm, d), lambda g, i, j: (g, i, 0)),
        out_shape=jax.ShapeDtypeStruct((gn, t, d), F32),
        scratch_shapes=[pltpu.VMEM((tm, d), F32)],
        compiler_params=_cparams(("parallel", "parallel", "arbitrary"), 56),
        name="mlp",
    )(x3, h3, mod3, g_post, w_up, w_down)


def _s5_disc_kernel(lr_ref, li_ref, ldt_ref, br_ref, bi_ref, pwr_ref, pwi_ref, bbr_ref, bbi_ref, *,
                    seg_len):
    lr, li = lr_ref[...], li_ref[...]
    dt = jnp.exp(ldt_ref[...])
    mag = jnp.exp(lr * dt)
    lb_re, lb_im = mag * jnp.cos(li * dt), mag * jnp.sin(li * dt)
    nr, ni = lb_re - 1.0, lb_im
    den = lr * lr + li * li
    f_re = (nr * lr + ni * li) / den
    f_im = (ni * lr - nr * li) / den
    br, bi = br_ref[...], bi_ref[...]
    bbr_ref[...] = f_re * br - f_im * bi
    bbi_ref[...] = f_re * bi + f_im * br
    pwr_ref[0] = lb_re
    pwi_ref[0] = lb_im
    qr, qi = None, None
    sr, si = lb_re, lb_im
    e = seg_len
    while e:
        if e & 1:
            qr, qi = (sr, si) if qr is None else (qr * sr - qi * si, qr * si + qi * sr)
        e >>= 1
        if e:
            sr, si = sr * sr - si * si, 2.0 * sr * si
    pr, pi = qr, qi
    for n in range(SUBLANES):
        pwr_ref[1 + n] = pr
        pwi_ref[1 + n] = pi
        pr, pi = pr * qr - pi * qi, pr * qi + pi * qr


def _s5_discretize(lam_re, lam_im, log_dt, bt_re, bt_im, seg_len):
    g, _, p = lam_re.shape
    c = bt_re.shape[1]
    return pl.pallas_call(
        functools.partial(_s5_disc_kernel, seg_len=seg_len),
        out_shape=[
            jax.ShapeDtypeStruct((1 + SUBLANES, g, 1, p), F32),
            jax.ShapeDtypeStruct((1 + SUBLANES, g, 1, p), F32),
            jax.ShapeDtypeStruct((g, c, p), F32),
            jax.ShapeDtypeStruct((g, c, p), F32),
        ],
        name="s5_discretize",
    )(lam_re, lam_im, log_dt, bt_re, bt_im)


def _gelu_tanh(x):
    c0 = math.sqrt(2.0 / math.pi)
    return x * (0.5 * (1.0 + jnp.tanh(c0 * (x + 0.044715 * (x * x * x)))))


def _cmul_add(ar, ai, xr, xi, yr, yi):
    return yr + ar * xr - ai * xi, yi + ar * xi + ai * xr


def _s5_seq_kernel(x_ref, mod_ref, gpre_ref, bblk_ref, cblk_ref, lam_ref, dskip_ref, s0r_ref, s0i_ref,
                   z_ref, sr_ref, si_ref, h_scr, xr_scr, xi_scr, *, d):
    cb = pl.program_id(0)
    ncb, tm, uw = h_scr.shape
    cw = xr_scr.shape[1]
    seg = s0r_ref.shape[0]

    @pl.when(cb == 0)
    def _():
        h = _norm_mod(x_ref[...], gpre_ref[...], mod_ref, d)
        for c in range(ncb):
            h_scr[c] = h[:, c * uw:(c + 1) * uw]

    u = h_scr[cb]
    bu = _dot(u.astype(BF16), bblk_ref[...])
    xr_scr[...] = bu[:, 0:cw]
    xi_scr[...] = bu[:, cw:2 * cw]
    car_r, car_i = s0r_ref[...], s0i_ref[...]
    l_r, l_i = lam_ref[0:1, :], lam_ref[1:2, :]
    for t in range(tm // seg):
        rows = slice(t * seg, (t + 1) * seg)
        car_r, car_i = _cmul_add(l_r, l_i, car_r, car_i, xr_scr[rows, :], xi_scr[rows, :])
        xr_scr[rows, :] = car_r
        xi_scr[rows, :] = car_i
    sr_ref[...] = car_r
    si_ref[...] = car_i
    xs = jnp.concatenate([xr_scr[...].astype(BF16), xi_scr[...].astype(BF16)], axis=1)
    y = _dot(xs, cblk_ref[...]) + dskip_ref[...] * u
    z_ref[...] = _gelu_tanh(y).astype(BF16)


def _s5_seq(x2, mod2, g_pre, bblk, cblk, lam2, dskip, s0_re, s0_im):
    tm, d = x2.shape
    seg, nst = s0_re.shape
    ncb, uw, cw2 = bblk.shape
    cw = cw2 // 2
    sspec = pl.BlockSpec((seg, cw), lambda c: (0, c))
    return pl.pallas_call(
        functools.partial(_s5_seq_kernel, d=d),
        grid=(ncb,),
        in_specs=[
            pl.BlockSpec((tm, d), lambda c: (0, 0)),
            pl.BlockSpec((seg, 3 * d), lambda c: (0, 0)),
            pl.BlockSpec((1, d), lambda c: (0, 0)),
            pl.BlockSpec((None, uw, cw2), lambda c: (c, 0, 0)),
            pl.BlockSpec((None, cw2, uw), lambda c: (c, 0, 0)),
            pl.BlockSpec((2, cw), lambda c: (0, c)),
            pl.BlockSpec((1, uw), lambda c: (0, c)),
            sspec, sspec,
        ],
        out_specs=[pl.BlockSpec((tm, uw), lambda c: (0, c)), sspec, sspec],
        out_shape=[
            jax.ShapeDtypeStruct((tm, d), BF16),
            jax.ShapeDtypeStruct((seg, nst), F32),
            jax.ShapeDtypeStruct((seg, nst), F32),
        ],
        scratch_shapes=[
            pltpu.VMEM((ncb, tm, uw), F32),
            pltpu.VMEM((tm, cw), F32),
            pltpu.VMEM((tm, cw), F32),
        ],
        compiler_params=_cparams(("arbitrary",), 48),
        name="s5_seq",
    )(x2, mod2, g_pre, bblk, cblk, lam2, dskip, s0_re, s0_im)


def _s5_rows_kernel(x_ref, mod_ref, gpre_ref, bblk_ref, cblk_ref, tbl_ref, dskip_ref,
                    z_ref, sr_ref, si_ref, h_scr, xr_scr, xi_scr, cr_scr, ci_scr, *, d):
    ncol, tm, _ = xr_scr.shape
    ncb, uw, _ = bblk_ref.shape
    sl = tm // SUBLANES

    @pl.when(pl.program_id(1) == 0)
    def _():
        cr_scr[...] = jnp.zeros_like(cr_scr)
        ci_scr[...] = jnp.zeros_like(ci_scr)

    h_scr[...] = _norm_mod(x_ref[...], gpre_ref[...], mod_ref, d)
    row0 = lax.broadcasted_iota(jnp.int32, (ncol, SUBLANES, LANES), 1) == 0

    for c in range(ncb):
        us = slice(c * uw, (c + 1) * uw)
        cols = slice(c * ncol, (c + 1) * ncol)
        u = h_scr[:, us]
        bu = _dot(u.astype(BF16), bblk_ref[c])
        for j in range(ncol):
            xr_scr[j] = bu[:, j * LANES:(j + 1) * LANES]
            xi_scr[j] = bu[:, (ncol + j) * LANES:(ncol + j + 1) * LANES]
        l_r, l_i = tbl_ref[0, cols], tbl_ref[1, cols]

        def local(i, s):
            rows = pl.ds(pl.multiple_of(i * SUBLANES, SUBLANES), SUBLANES)
            return _cmul_add(l_r, l_i, s[0], s[1], xr_scr[:, rows, :], xi_scr[:, rows, :])

        zero = jnp.zeros((ncol, SUBLANES, LANES), F32)
        g_r, g_i = lax.fori_loop(0, sl, local, (zero, zero), unroll=True)
        for n in range(3):
            g_r, g_i = _cmul_add(tbl_ref[2 + 2 * n, cols], tbl_ref[3 + 2 * n, cols],
                                 pltpu.roll(g_r, 1 << n, 1), pltpu.roll(g_i, 1 << n, 1), g_r, g_i)
        car_r, car_i = cr_scr[cols], ci_scr[cols]
        g_r, g_i = _cmul_add(tbl_ref[8, cols], tbl_ref[9, cols], car_r, car_i, g_r, g_i)
        in_r = jnp.where(row0, car_r, pltpu.roll(g_r, 1, 1))
        in_i = jnp.where(row0, car_i, pltpu.roll(g_i, 1, 1))

        def full(i, s):
            rows = pl.ds(pl.multiple_of(i * SUBLANES, SUBLANES), SUBLANES)
            s_r, s_i = _cmul_add(l_r, l_i, s[0], s[1], xr_scr[:, rows, :], xi_scr[:, rows, :])
            xr_scr[:, rows, :] = s_r
            xi_scr[:, rows, :] = s_i
            return s_r, s_i

        e_r, e_i = lax.fori_loop(0, sl, full, (in_r, in_i), unroll=True)
        cr_scr[cols] = e_r[:, SUBLANES - 1:SUBLANES, :]
        ci_scr[cols] = e_i[:, SUBLANES - 1:SUBLANES, :]
        xs = jnp.concatenate([xr_scr[j].astype(BF16) for j in range(ncol)]
                             + [xi_scr[j].astype(BF16) for j in range(ncol)], axis=1)
        y = _dot(xs, cblk_ref[c]) + dskip_ref[:, us] * u
        z_ref[:, us] = _gelu_tanh(y).astype(BF16)

    sr_ref[...] = cr_scr[...]
    si_ref[...] = ci_scr[...]


def _s5_rows(x3, mod3, g_pre, bblk, cblk, tbl, dskip, tm):
    gn, t, d = x3.shape
    ncb, uw, cw2 = bblk.shape
    ncol = cw2 // 2 // LANES
    nct = ncb * ncol
    nt = t // tm
    const = lambda shape: pl.BlockSpec(shape, lambda g, i: (0,) * len(shape), pipeline_mode=pl.Buffered(1))
    ospec = pl.BlockSpec((None, None, nct, 1, LANES), lambda g, i: (g, i, 0, 0, 0))
    z3, s_re, s_im = pl.pallas_call(
        functools.partial(_s5_rows_kernel, d=d),
        grid=(gn, nt),
        in_specs=[
            pl.BlockSpec((None, tm, d), lambda g, i: (g, i, 0)),
            _mod_spec(1, tm, 3 * d, 2),
            pl.BlockSpec((1, d), lambda g, i: (0, 0)),
            const(bblk.shape), const(cblk.shape), const(tbl.shape), const(dskip.shape),
        ],
        out_specs=[pl.BlockSpec((None, tm, d), lambda g, i: (g, i, 0)), ospec, ospec],
        out_shape=[
            jax.ShapeDtypeStruct((gn, t, d), BF16),
            jax.ShapeDtypeStruct((gn, nt, nct, 1, LANES), F32),
            jax.ShapeDtypeStruct((gn, nt, nct, 1, LANES), F32),
        ],
        scratch_shapes=[
            pltpu.VMEM((tm, d), F32),
            pltpu.VMEM((ncol, tm, LANES), F32),
            pltpu.VMEM((ncol, tm, LANES), F32),
            pltpu.VMEM((nct, 1, LANES), F32),
            pltpu.VMEM((nct, 1, LANES), F32),
        ],
        compiler_params=_cparams(("parallel", "arbitrary"), 48),
        name="s5_rows",
    )(x3, mod3, g_pre, bblk, cblk, tbl, dskip)
    return z3, s_re[:, nt - 1].reshape(gn, 1, nct * LANES), s_im[:, nt - 1].reshape(gn, 1, nct * LANES)


def _glu_kernel(x_ref, mod_ref, g_ref, z_ref, wa_ref, wb_ref, *outs, d):
    z = z_ref[...]
    y = _dot(z, wa_ref[...]) * jax.nn.sigmoid(_dot(z, wb_ref[...]))
    _residual_out(x_ref, y, g_ref, mod_ref, outs, d)


def _glu(x3, mod3, g_post, z3, w_a, w_b, tm, nxt=None):
    gn, t, d = x3.shape
    r = mod3.shape[1]
    wspec = pl.BlockSpec((d, d), lambda g, i: (0, 0), pipeline_mode=pl.Buffered(1))
    n_in, n_out, n_shape, n_ops = _next_specs(None if nxt is None else (*nxt, x3.shape), tm, d)
    res = pl.pallas_call(
        functools.partial(_glu_kernel, d=d),
        grid=(gn, t // tm),
        in_specs=[
            pl.BlockSpec((None, tm, d), lambda g, i: (g, i, 0)),
            _mod_spec(r, tm, 3 * d, 2),
            pl.BlockSpec((1, d), lambda g, i: (0, 0)),
            pl.BlockSpec((None, tm, d), lambda g, i: (g, i, 0)),
            wspec, wspec,
        ] + n_in,
        out_specs=[pl.BlockSpec((None, tm, d), lambda g, i: (g, i, 0))] + n_out,
        out_shape=[jax.ShapeDtypeStruct((gn, t, d), F32)] + n_shape,
        compiler_params=_cparams(("parallel", "parallel"), 56),
        name="glu",
    )(x3, mod3, g_post, z3, w_a, w_b, *n_ops)
    return res if nxt is not None else (res[0], None)


def _rope_tables(pos):
    half = HEAD_DK // 2
    inv = ROPE_BASE ** (-jnp.arange(half, dtype=F32) / half)
    ang = pos.astype(F32)[:, None] * inv[None, :]
    cos, sin = jnp.cos(ang), jnp.sin(ang)
    return jnp.concatenate([cos, cos], axis=-1), jnp.concatenate([-sin, sin], axis=-1)


def _block_diag(w):
    ncb, gpb, a, b = w.shape
    tiled = jnp.broadcast_to(w.reshape(ncb, gpb * a, 1, b), (ncb, gpb * a, gpb, b))
    row_g = jnp.arange(gpb * a)[:, None, None] // a
    col_g = jnp.arange(gpb)[None, :, None]
    return jnp.where(row_g == col_g, tiled, 0.0).reshape(ncb, gpb * a, gpb * b)


def _s5_tables(pw_re, pw_im):
    n = pw_re.shape[0]
    flat_r = pw_re.reshape(n, -1)
    flat_i = pw_im.reshape(n, -1)
    row = jnp.arange(SUBLANES)[:, None]
    tabs = [jnp.broadcast_to(flat_r[0], (SUBLANES, flat_r.shape[1])),
            jnp.broadcast_to(flat_i[0], (SUBLANES, flat_i.shape[1]))]
    for s in (1, 2, 4):
        mask = row >= s
        tabs.append(jnp.where(mask, flat_r[s][None, :], 0.0))
        tabs.append(jnp.where(mask, flat_i[s][None, :], 0.0))
    tabs += [flat_r[1:], flat_i[1:]]
    tbl = jnp.stack(tabs)
    tbl = tbl.reshape(tbl.shape[0], SUBLANES, -1, LANES).transpose(0, 2, 1, 3)
    return tbl, jnp.stack([flat_r[0], flat_i[0]])


def kernel(x_prompt, x_sample, state_gla, state_ret, state_s5_re, state_s5_im, c_prompt, c_sample,
           w_ada, b_ada, norm_pre, norm_post, w_in_mix, w_gla_gk, b_gla_gk, gla_head_norm,
           ret_head_norm, w_out_mix, s5_lam_re, s5_lam_im, s5_log_dt, s5_b_re, s5_b_im,
           s5_c_re, s5_c_im, s5_d, w_glu_a, w_glu_b, w_mlp_up, w_mlp_down):
    bp, tp, d = x_prompt.shape
    bs, ts, _ = x_sample.shape
    depth = w_ada.shape[0]

    nrow = -(-(bs + bp) // SUBLANES) * SUBLANES
    c_all = jnp.concatenate([c_sample, c_prompt, jnp.zeros((nrow - bs - bp, d), F32)], axis=0)
    mod_all = _adaln(c_all, w_ada.reshape(depth * 2, d, 3 * d), b_ada.reshape(depth * 2, 1, 3 * d))
    mod_s = [mod_all[k, 0:bs][None] for k in range(depth * 2)]
    mod_p = [mod_all[k, bs:bs + bp][:, None, :] for k in range(depth * 2)]

    w_in_t = jnp.swapaxes(w_in_mix, 1, 2)
    w_gk = jnp.pad(w_gla_gk[0], ((0, LANES - GLA_RANK), (0, 0))).astype(BF16)
    b_gk = b_gla_gk[0][None, :]
    w_out = w_out_mix[0].astype(BF16)
    w_ga = w_glu_a[0].astype(BF16)
    w_gb = w_glu_b[0].astype(BF16)
    gla_norm = gla_head_norm[0][:, None, :]
    ret_norm = ret_head_norm[0][:, None, :]
    gamma_log = jnp.log1p(-jnp.power(2.0, -5.0 - jnp.arange(RET_HEADS, dtype=F32)))
    ret_lg = jnp.broadcast_to(gamma_log[:, None, None], (RET_HEADS, 1, LANES))

    ng = s5_lam_re.shape[1]
    ncb = ng // S5_GPB
    tm5 = min(TM_S5, tp)
    pw_re, pw_im, bbt_re, bbt_im = _s5_discretize(
        s5_lam_re[0][:, None, :], s5_lam_im[0][:, None, :], s5_log_dt[0][:, None, None],
        jnp.swapaxes(s5_b_re[0], 1, 2), jnp.swapaxes(s5_b_im[0], 1, 2), tm5 // SUBLANES)
    bblk = jnp.concatenate([
        _block_diag(bbt_re.reshape(ncb, S5_GPB, S5_GROUP, S5_STATE)),
        _block_diag(bbt_im.reshape(ncb, S5_GPB, S5_GROUP, S5_STATE))], axis=2).astype(BF16)
    ct_re = jnp.swapaxes(s5_c_re[0], 1, 2).reshape(ncb, S5_GPB, S5_STATE, S5_GROUP)
    ct_im = jnp.swapaxes(s5_c_im[0], 1, 2).reshape(ncb, S5_GPB, S5_STATE, S5_GROUP)
    cblk = jnp.concatenate([_block_diag(ct_re), _block_diag(-ct_im)], axis=1).astype(BF16)
    tbl, lam2 = _s5_tables(pw_re[:, :, 0, :], pw_im[:, :, 0, :])
    dskip = s5_d[0][None, :]

    def trunk(x3, mods, tm, inproj, attn, s5, mlp, emit_h):
        nxt = lambda l: (mods[2 * l + 1], norm_pre[l, 1][None]) if emit_h else None
        proj, glog = inproj(x3, mods[0])
        mg, mr, s_gla, s_ret = attn(proj, glog)
        x3, h3 = _outproj(x3, mods[0], norm_post[0, 0][None], mg, mr, w_out, tm, nxt(0))
        x3 = mlp(0, x3, h3, mods[1])
        z3, s_re, s_im = s5(x3, mods[2])
        x3, h3 = _glu(x3, mods[2], norm_post[1, 0][None], z3, w_ga, w_gb, tm, nxt(1))
        x3 = mlp(1, x3, h3, mods[3])
        return x3, s_gla, s_ret, s_re, s_im

    cos_s, sin_s = _rope_tables(PAST_LEN + jnp.arange(ts, dtype=F32))

    def attn_s(proj, glog):
        tm_rows = lambda a: a.reshape(ts, bs, a.shape[-1])
        mg, mr, s_gla, s_ret = _attn_sample(tm_rows(proj), tm_rows(glog), gla_norm, ret_norm, cos_s, sin_s,
                                            ret_lg, state_gla[0], state_ret[0])
        flat = lambda a: a.reshape(1, ts * bs, a.shape[-1])
        return flat(mg), flat(mr), s_gla, s_ret

    xs3 = jnp.swapaxes(x_sample, 0, 1).reshape(1, ts * bs, d)

    def s5_s(x3, mod3):
        z2, s_re, s_im = _s5_seq(x3[0], mod3[0], norm_pre[1, 0][None], bblk, cblk, lam2, dskip,
                                 state_s5_re[0].reshape(bs, -1), state_s5_im[0].reshape(bs, -1))
        return z2[None], s_re, s_im

    w_up, w_dn, w_main = {}, {}, {}

    def inproj_s(x3, mod3):
        proj, glog, w_main[0], w_main["lr"] = _inproj_cast(x3, mod3, norm_pre[0, 0][None], w_in_t, 0,
                                                            w_gk, b_gk)
        return proj, glog

    def mlp_s(l, x3, h3, mod3):
        x3, w_up[l], w_dn[l] = _mlp_cast(x3, mod3, norm_pre[l, 1][None], norm_post[l, 1][None],
                                         w_mlp_up, w_mlp_down, l)
        return x3

    y_s, gla_s, ret_s, re_s, im_s = trunk(xs3, mod_s, ts * bs, inproj_s, attn_s, s5_s, mlp_s, False)
    y_s = jnp.swapaxes(y_s.reshape(ts, bs, d), 0, 1)

    cos_p, sin_p = _rope_tables(jnp.arange(tp, dtype=F32))
    zeros_att = jnp.zeros((bp, GLA_HEADS, HEAD_DK, HEAD_DV), F32)
    tm_p = min(TM_DENSE, tp)

    def attn_p(proj, glog):
        return _attn_prompt(proj, glog, gla_norm, ret_norm, cos_p, sin_p, ret_lg, zeros_att, zeros_att)

    def s5_p(x3, mod3):
        sl = tm5 // SUBLANES
        xp = jnp.swapaxes(x3.reshape(bp, tp // tm5, SUBLANES, sl, d), 2, 3).reshape(bp, tp, d)
        zp, s_re, s_im = _s5_rows(xp, mod3, norm_pre[1, 0][None], bblk, cblk, tbl, dskip, tm5)
        z3 = jnp.swapaxes(zp.reshape(bp, tp // tm5, sl, SUBLANES, d), 2, 3).reshape(bp, tp, d)
        return z3, s_re, s_im

    def mlp_p(l, x3, h3, mod3):
        return _mlp_h(x3, h3, mod3, norm_post[l, 1][None], w_up[l], w_dn[l], tm_p)

    def inproj_p(x3, mod3):
        return _inproj(x3, mod3, norm_pre[0, 0][None], w_main[0], w_main["lr"], w_gk, b_gk, tm_p)

    y_p, gla_p, ret_p, re_p, im_p = trunk(x_prompt, mod_p, tm_p, inproj_p, attn_p, s5_p, mlp_p, True)

    st = lambda a, b_: a.reshape(1, b_, ng, S5_STATE)
    return (y_p, y_s, gla_p[None], gla_s[None], ret_p[None], ret_s[None],
            st(re_p, bp), st(re_s, bs), st(im_p, bp), st(im_s, bs))
```

```python
import functools
import math

import jax
import jax.numpy as jnp
import numpy as np
from jax import lax
from jax.experimental import pallas as pl
from jax.experimental.pallas import tpu as pltpu

F32 = jnp.float32
BF16 = jnp.bfloat16

EPS = 1e-6
LANES = 128
SUBLANES = 8
MIB = 1024 * 1024

GLA_HEADS = 4
RET_HEADS = 4
HEAD_DK = 128
HEAD_DV = 256
GLA_RANK = 16
GLA_LOGIT_NORM = 16.0
ROPE_BASE = 10000.0
PAST_LEN = 16384
S5_GROUP = 16
S5_STATE = 64
S5_GPB = 16
S5_UW = S5_GPB * S5_GROUP
ATT_CHUNK = 128
GLA_SUB = 16
TM_DENSE = 512
TM_S5 = 256
TN_INPROJ = 1024
TF_MLP = 1024


def _cparams(sem, vmem_mib):
    return pltpu.CompilerParams(dimension_semantics=sem, vmem_limit_bytes=vmem_mib * MIB)


def _dot(a, b):
    return jnp.dot(a, b, preferred_element_type=F32)


def _dot_nt(a, b):
    return lax.dot_general(a, b, (((1,), (1,)), ((), ())), preferred_element_type=F32)


def _rms(x, g):
    return x * lax.rsqrt(jnp.mean(x * x, axis=-1, keepdims=True) + EPS) * g


def _rows_affine(y, a, b=None):
    tm, d = y.shape
    r = a.shape[0]
    if r == 1 or r == tm:
        out = y * a
        return out if b is None else out + b
    y3 = y.reshape(tm // r, r, d)
    out = y3 * a[None]
    if b is not None:
        out = out + b[None]
    return out.reshape(tm, d)


def _norm_mod(x, g, mod_ref, d):
    return _rows_affine(_rms(x, g), 1.0 + mod_ref[:, d:2 * d], mod_ref[:, 0:d])


def _gated_residual(x, y, g, mod_ref, d):
    return x + _rows_affine(_rms(y, g), mod_ref[:, 2 * d:3 * d])


def _mod_spec(r, tm, width, ngrid):
    if ngrid == 2:
        if r == 1:
            return pl.BlockSpec((None, 1, width), lambda g, i: (g, 0, 0))
        return pl.BlockSpec((None, r, width), lambda g, i: (g, 0, 0))
    if r == 1:
        return pl.BlockSpec((None, 1, width), lambda g, i, j: (g, 0, 0))
    return pl.BlockSpec((None, r, width), lambda g, i, j: (g, 0, 0))


def _adaln_kernel(c_ref, w_ref, b_ref, o_ref):
    c = c_ref[...]
    sc = (c * jax.nn.sigmoid(c)).astype(BF16)
    o_ref[...] = _dot(sc, w_ref[...].astype(BF16)) + b_ref[...]


def _adaln(c_all, w_ada, b_ada, tn=1024):
    ls, d, n = w_ada.shape
    rows = c_all.shape[0]
    return pl.pallas_call(
        _adaln_kernel,
        grid=(ls, n // tn),
        in_specs=[
            pl.BlockSpec((rows, d), lambda l, j: (0, 0)),
            pl.BlockSpec((None, d, tn), lambda l, j: (l, 0, j)),
            pl.BlockSpec((None, 1, tn), lambda l, j: (l, 0, j)),
        ],
        out_specs=pl.BlockSpec((None, rows, tn), lambda l, j: (l, 0, j)),
        out_shape=jax.ShapeDtypeStruct((ls, rows, n), F32),
        compiler_params=_cparams(("parallel", "parallel"), 40),
        name="adaln",
    )(c_all, w_ada, b_ada)


def _log_sigmoid(x):
    return jnp.minimum(x, 0.0) - jnp.log1p(jnp.exp(-jnp.abs(x)))


def _gate_logits(hb, wlr_t, wgk_ref, bgk_ref, glog_ref):
    glr = _dot_nt(hb, wlr_t)
    logit = _dot(glr.astype(BF16), wgk_ref[...]) + bgk_ref[...]
    glog_ref[...] = _log_sigmoid(logit) * (1.0 / GLA_LOGIT_NORM)


def _inproj_kernel(x_ref, mod_ref, g_ref, w_ref, wlr_ref, wgk_ref, bgk_ref,
                   proj_ref, glog_ref, h_scr, *, d, tps):
    j = pl.program_id(2)

    @pl.when(j == 0)
    def _():
        hb = _norm_mod(x_ref[...], g_ref[...], mod_ref, d).astype(BF16)
        h_scr[...] = hb
        _gate_logits(hb, wlr_ref[...], wgk_ref, bgk_ref, glog_ref)

    tn = w_ref.shape[2]
    for k in range(tps):
        proj_ref[:, k * tn:(k + 1) * tn] = _dot(h_scr[...], w_ref[j * tps + k])


def _inproj_cast_kernel(x_ref, mod_ref, g_ref, wa_ref, wb_ref, wgk_ref, bgk_ref,
                        proj_ref, glog_ref, wout_ref, wlr_ref, h_scr, *, d, n_lo):
    j = pl.program_id(2)

    @pl.when(j == 0)
    def _():
        h_scr[...] = _norm_mod(x_ref[...], g_ref[...], mod_ref, d).astype(BF16)

    @pl.when(j < n_lo)
    def _():
        wout_ref[...] = jnp.transpose(wa_ref[...]).astype(BF16)

    @pl.when(j >= n_lo)
    def _():
        w = jnp.concatenate([wa_ref[GLA_RANK:, :], wb_ref[:GLA_RANK, :]], axis=0)
        wout_ref[...] = jnp.transpose(w).astype(BF16)

    @pl.when(j == n_lo)
    def _():
        wlr_t = _pad_rows(wa_ref[:GLA_RANK, :], LANES).astype(BF16)
        wlr_ref[...] = wlr_t
        _gate_logits(h_scr[...], wlr_t, wgk_ref, bgk_ref, glog_ref)

    proj_ref[...] = _dot(h_scr[...], wout_ref[...])


def _inproj_cast(x3, mod3, g_pre, w_raw_t, layer, w_gk, b_gk, tn=TN_INPROJ):
    gn, t, d = x3.shape
    assert gn == 1
    sec = (w_raw_t.shape[1] - GLA_RANK) // 2
    assert sec % tn == 0 and tn % LANES == 0
    nj = 2 * sec // tn
    n = nj * tn
    r = mod3.shape[1]
    gkey = w_gk.shape[1]
    return pl.pallas_call(
        functools.partial(_inproj_cast_kernel, d=d, n_lo=sec // tn),
        grid=(1, 1, nj),
        in_specs=[
            pl.BlockSpec((None, t, d), lambda g, i, j: (0, 0, 0)),
            _mod_spec(r, t, 3 * d, 3),
            pl.BlockSpec((1, d), lambda g, i, j: (0, 0)),
            pl.BlockSpec((None, tn, d), lambda g, i, j: (layer, j, 0)),
            pl.BlockSpec((None, LANES, d), lambda g, i, j: (layer, (j + 1) * (tn // LANES), 0)),
            pl.BlockSpec((LANES, gkey), lambda g, i, j: (0, 0)),
            pl.BlockSpec((1, gkey), lambda g, i, j: (0, 0)),
        ],
        out_specs=[
            pl.BlockSpec((None, t, tn), lambda g, i, j: (0, 0, j)),
            pl.BlockSpec((None, t, gkey), lambda g, i, j: (0, 0, 0)),
            pl.BlockSpec((None, d, tn), lambda g, i, j: (j, 0, 0)),
            pl.BlockSpec((LANES, d), lambda g, i, j: (0, 0)),
        ],
        out_shape=[
            jax.ShapeDtypeStruct((1, t, n), F32),
            jax.ShapeDtypeStruct((1, t, gkey), F32),
            jax.ShapeDtypeStruct((nj, d, tn), BF16),
            jax.ShapeDtypeStruct((LANES, d), BF16),
        ],
        scratch_shapes=[pltpu.VMEM((t, d), BF16)],
        compiler_params=_cparams(("arbitrary", "arbitrary", "arbitrary"), 56),
        name="inproj_cast",
    )(x3, mod3, g_pre, w_raw_t, w_raw_t, w_gk, b_gk)


def _inproj(x3, mod3, g_pre, w_main, w_lr_t, w_gk, b_gk, tm, tps=2):
    gn, t, d = x3.shape
    nj, _, tn = w_main.shape
    n = nj * tn
    r = mod3.shape[1]
    gkey = w_gk.shape[1]
    return pl.pallas_call(
        functools.partial(_inproj_kernel, d=d, tps=tps),
        grid=(gn, t // tm, nj // tps),
        in_specs=[
            pl.BlockSpec((None, tm, d), lambda g, i, j: (g, i, 0)),
            _mod_spec(r, tm, 3 * d, 3),
            pl.BlockSpec((1, d), lambda g, i, j: (0, 0)),
            pl.BlockSpec((nj, d, tn), lambda g, i, j: (0, 0, 0), pipeline_mode=pl.Buffered(1)),
            pl.BlockSpec((LANES, d), lambda g, i, j: (0, 0)),
            pl.BlockSpec((LANES, gkey), lambda g, i, j: (0, 0)),
            pl.BlockSpec((1, gkey), lambda g, i, j: (0, 0)),
        ],
        out_specs=[
            pl.BlockSpec((None, tm, tps * tn), lambda g, i, j: (g, i, j)),
            pl.BlockSpec((None, tm, gkey), lambda g, i, j: (g, i, 0)),
        ],
        out_shape=[
            jax.ShapeDtypeStruct((gn, t, n), F32),
            jax.ShapeDtypeStruct((gn, t, gkey), F32),
        ],
        scratch_shapes=[pltpu.VMEM((tm, d), BF16)],
        compiler_params=_cparams(("parallel", "parallel", "arbitrary"), 56),
        name="inproj",
    )(x3, mod3, g_pre, w_main, w_lr_t, w_gk, b_gk)


def _cumsum_rows(g):
    c = g.shape[0]
    row = lax.broadcasted_iota(jnp.int32, g.shape, 0)
    s = 1
    while s < c:
        g = g + jnp.where(row >= s, pltpu.roll(g, s, 0), 0.0)
        s *= 2
    return g


def _pad_rows(a, rows):
    if a.shape[0] == rows:
        return a
    return jnp.concatenate([a, jnp.zeros((rows - a.shape[0], a.shape[1]), a.dtype)], axis=0)


def _col_bcast(row, width):
    sq = jnp.transpose(jnp.broadcast_to(row, (LANES, LANES)))
    return jnp.concatenate([sq] * (width // LANES), axis=1)


def _gla_core(q, k, v, g, s, sub):
    cq = q.shape[0]
    ck = max(cq, LANES)
    b = _cumsum_rows(g)
    be = b - g
    bk = _pad_rows(b, ck)
    kp = _pad_rows(k, ck)
    vp = _pad_rows(v, ck).astype(BF16)
    rowj = lax.broadcasted_iota(jnp.int32, (ck, 1), 0)
    att_rows = []
    for blk in range(cq // sub):
        lo, hi = blk * sub, (blk + 1) * sub
        base = be[lo:lo + 1, :]
        qs = q[lo:hi] * jnp.exp(b[lo:hi] - base)
        ks = jnp.where(rowj < hi, kp * jnp.exp(base - bk), 0.0)
        att_rows.append(_dot_nt(qs.astype(BF16), ks.astype(BF16)))
    att = att_rows[0] if len(att_rows) == 1 else jnp.concatenate(att_rows, axis=0)
    ri = lax.broadcasted_iota(jnp.int32, (cq, ck), 0)
    cj = lax.broadcasted_iota(jnp.int32, (cq, ck), 1)
    att = jnp.where(ri >= cj, att, 0.0)
    o = _dot(att.astype(BF16), vp) + _dot((q * jnp.exp(b)).astype(BF16), s.astype(BF16))
    b_last = b[cq - 1:cq, :]
    k_out = kp * jnp.exp(b_last - bk)
    s_new = s * _col_bcast(jnp.exp(b_last), s.shape[1]) + _dot(jnp.transpose(k_out).astype(BF16), vp)
    return o, s_new


def _ret_core(q, k, v, s, lg, dmat, valid):
    cq = q.shape[0]
    ck = max(cq, LANES)
    kp = _pad_rows(k, ck)
    vp = _pad_rows(v, ck).astype(BF16)
    ti = lax.broadcasted_iota(jnp.int32, (cq, 1), 0).astype(F32)
    tj = lax.broadcasted_iota(jnp.int32, (ck, 1), 0).astype(F32)
    att = _dot_nt(q.astype(BF16), kp.astype(BF16)) * dmat
    q_in = q * jnp.exp((ti + 1.0) * lg)
    o = _dot(att.astype(BF16), vp) + _dot(q_in.astype(BF16), s.astype(BF16))
    k_out = kp * jnp.exp((float(valid - 1) - tj) * lg)
    s_new = s * jnp.exp(float(valid) * lg) + _dot(jnp.transpose(k_out).astype(BF16), vp)
    return o, s_new


def _decay_matrix(cq, ck, lg):
    ri = lax.broadcasted_iota(jnp.int32, (cq, ck), 0)
    cj = lax.broadcasted_iota(jnp.int32, (cq, ck), 1)
    diff = (ri - cj).astype(F32)
    return jnp.where(ri >= cj, jnp.exp(diff * lg), 0.0)


def _rope(x, cosf, sinf):
    return x * cosf + pltpu.roll(x, x.shape[1] // 2, 1) * sinf


def _silu(x):
    return x * jax.nn.sigmoid(x)


def _gla_finish(o, gate, gn):
    o = o * lax.rsqrt(jnp.mean(o * o, axis=-1, keepdims=True) + EPS) * gn
    return (o * _silu(gate)).astype(BF16)


def _ret_finish(o, gate, gn):
    oc = o - jnp.mean(o, axis=-1, keepdims=True)
    oc = oc * lax.rsqrt(jnp.mean(oc * oc, axis=-1, keepdims=True) + EPS) * gn
    return (oc * _silu(gate)).astype(BF16)


def _head(ref, h, width):
    return ref[:, h * width:(h + 1) * width]


def _attn_prompt_kernel(gq_ref, gk_ref, gv_ref, gg_ref, gl_ref, ggn_ref, gs0_ref,
                        rq_ref, rk_ref, rv_ref, rg_ref, cos_ref, sin_ref, lg_ref, rgn_ref, rs0_ref,
                        go_ref, gs_ref, ro_ref, rs_ref, d_scr):
    @pl.when(pl.program_id(1) == 0)
    def _():
        gs_ref[...] = gs0_ref[...]
        rs_ref[...] = rs0_ref[...]
        for h in range(d_scr.shape[0]):
            d_scr[h] = _decay_matrix(d_scr.shape[1], d_scr.shape[2], lg_ref[h][:, 0:1])

    cosf, sinf = cos_ref[...], sin_ref[...]
    for h in range(gs_ref.shape[0]):
        vcols = slice(h * HEAD_DV, (h + 1) * HEAD_DV)
        q = _head(gq_ref, h, HEAD_DK) * (HEAD_DK ** -0.5)
        o, s_new = _gla_core(q, _head(gk_ref, h, HEAD_DK), _head(gv_ref, h, HEAD_DV),
                             _head(gl_ref, h, HEAD_DK), gs_ref[h], GLA_SUB)
        gs_ref[h] = s_new
        go_ref[:, vcols] = _gla_finish(o, _head(gg_ref, h, HEAD_DV), ggn_ref[h])
        q = _rope(_head(rq_ref, h, HEAD_DK), cosf, sinf)
        k = _rope(_head(rk_ref, h, HEAD_DK), cosf, sinf) * (HEAD_DK ** -0.5)
        o, s_new = _ret_core(q, k, _head(rv_ref, h, HEAD_DV), rs_ref[h], lg_ref[h][:, 0:1], d_scr[h],
                             q.shape[0])
        rs_ref[h] = s_new
        ro_ref[:, vcols] = _ret_finish(o, _head(rg_ref, h, HEAD_DV), rgn_ref[h])


def _attn_prompt(proj, glog, gla_norm, ret_norm, cosf, sinf, ret_lg, s0_gla, s0_ret):
    bsz, t, _ = proj.shape
    c = ATT_CHUNK
    nh = GLA_HEADS
    kw, vw = nh * HEAD_DK, nh * HEAD_DV
    kspec = lambda blk: pl.BlockSpec((None, c, kw), lambda b, i, blk=blk: (b, i, blk))
    vspec = lambda blk: pl.BlockSpec((None, c, vw), lambda b, i, blk=blk: (b, i, blk))
    hspec = pl.BlockSpec((nh, 1, HEAD_DV), lambda b, i: (0, 0, 0))
    sspec = pl.BlockSpec((None, nh, HEAD_DK, HEAD_DV), lambda b, i: (b, 0, 0, 0))
    ospec = pl.BlockSpec((None, c, vw), lambda b, i: (b, i, 0))
    tspec = pl.BlockSpec((c, HEAD_DK), lambda b, i: (i, 0))
    o_shape = jax.ShapeDtypeStruct((bsz, t, vw), BF16)
    s_shape = jax.ShapeDtypeStruct((bsz, nh, HEAD_DK, HEAD_DV), F32)
    mg, s_gla, mr, s_ret = pl.pallas_call(
        _attn_prompt_kernel,
        grid=(bsz, t // c),
        in_specs=[kspec(0), kspec(1), vspec(1), vspec(2), kspec(0), hspec, sspec,
                  kspec(6), kspec(7), vspec(4), vspec(5), tspec, tspec,
                  pl.BlockSpec((nh, 1, LANES), lambda b, i: (0, 0, 0)), hspec, sspec],
        out_specs=[ospec, sspec, ospec, sspec],
        out_shape=[o_shape, s_shape, o_shape, s_shape],
        scratch_shapes=[pltpu.VMEM((nh, c, c), F32)],
        compiler_params=_cparams(("parallel", "arbitrary"), 32),
        name="attn_prompt",
    )(proj, proj, proj, proj, glog, gla_norm, s0_gla,
      proj, proj, proj, proj, cosf, sinf, ret_lg, ret_norm, s0_ret)
    return mg, mr, s_gla, s_ret


def _seq_rows(ref):
    ts, bb, w = ref.shape
    return ref[...].reshape(ts * bb, w)


def _seq_masks(n, bb):
    r = lax.broadcasted_iota(jnp.int32, (n, n), 0)
    c = lax.broadcasted_iota(jnp.int32, (n, n), 1)
    return (r % bb == c % bb) & (r >= c), (r - c).astype(F32) * (1.0 / bb)


def _seq_state_terms(q_in, k_out, v, s0, bb):
    n, dk = q_in.shape
    rown = lax.broadcasted_iota(jnp.int32, (n, 1), 0) % bb
    q_bd = jnp.concatenate([jnp.where(rown == j, q_in, 0.0) for j in range(bb)], axis=1)
    o_inter = _dot(q_bd.astype(BF16), s0.astype(BF16))
    k_t = jnp.transpose(_pad_rows(k_out, LANES))
    coln = lax.broadcasted_iota(jnp.int32, (1, LANES), 1) % bb
    k_bd = jnp.concatenate([jnp.where(coln == j, k_t, 0.0) for j in range(bb)], axis=0)
    ds = _dot(k_bd.astype(BF16), _pad_rows(v, LANES).astype(BF16))
    return o_inter, ds


def _gla_sample_kernel(q_ref, k_ref, v_ref, gg_ref, gl_ref, gn_ref, s0_ref, o_ref, s_ref):
    ts, bb, dk = q_ref.shape
    dv = v_ref.shape[2]
    n = ts * bb
    q = _seq_rows(q_ref) * (dk ** -0.5)
    k, v, g = _seq_rows(k_ref), _seq_rows(v_ref), _seq_rows(gl_ref)
    steps = [g[0:bb]]
    for t in range(1, ts):
        steps.append(steps[-1] + g[t * bb:(t + 1) * bb])
    b = jnp.concatenate(steps, axis=0)
    b_last = steps[-1]
    q_in = q * jnp.exp(b)
    mask, _ = _seq_masks(n, bb)
    att = jnp.where(mask, _dot_nt(q_in.astype(BF16), (k * jnp.exp(-b)).astype(BF16)), 0.0)
    k_out = k * jnp.exp(jnp.concatenate([b_last] * ts, axis=0) - b)
    s0 = s0_ref[...].reshape(bb * dk, dv)
    o_inter, ds = _seq_state_terms(q_in, k_out, v, s0, bb)
    o = _dot(att.astype(BF16), v.astype(BF16)) + o_inter
    e_last = jnp.exp(b_last)
    dec = jnp.concatenate([_col_bcast(e_last[j:j + 1, :], dv) for j in range(bb)], axis=0)
    s_ref[...] = (s0 * dec + ds).reshape(bb, dk, dv)
    o_ref[...] = _gla_finish(o, _seq_rows(gg_ref), gn_ref[...]).reshape(ts, bb, dv)


def _ret_sample_kernel(q_ref, k_ref, v_ref, rg_ref, cos_ref, sin_ref, lg_ref, gn_ref, s0_ref,
                       o_ref, s_ref):
    ts, bb, dk = q_ref.shape
    dv = v_ref.shape[2]
    n = ts * bb
    lg = lg_ref[:, 0:1]
    rows = lambda tab: jnp.concatenate(
        [jnp.broadcast_to(tab[t:t + 1, :], (bb, dk)) for t in range(ts)], axis=0)
    cosf, sinf = rows(cos_ref[...]), rows(sin_ref[...])
    q = _rope(_seq_rows(q_ref), cosf, sinf)
    k = _rope(_seq_rows(k_ref), cosf, sinf) * (dk ** -0.5)
    v = _seq_rows(v_ref)
    mask, dt = _seq_masks(n, bb)
    att = _dot_nt(q.astype(BF16), k.astype(BF16)) * jnp.where(mask, jnp.exp(dt * lg), 0.0)
    tt = (lax.broadcasted_iota(jnp.int32, (n, 1), 0) // bb).astype(F32)
    q_in = q * jnp.exp((tt + 1.0) * lg)
    k_out = k * jnp.exp((float(ts - 1) - tt) * lg)
    s0 = s0_ref[...].reshape(bb * dk, dv)
    o_inter, ds = _seq_state_terms(q_in, k_out, v, s0, bb)
    o = _dot(att.astype(BF16), v.astype(BF16)) + o_inter
    s_ref[...] = (s0 * jnp.exp(float(ts) * lg) + ds).reshape(bb, dk, dv)
    o_ref[...] = _ret_finish(o, _seq_rows(rg_ref), gn_ref[...]).reshape(ts, bb, dv)


def _attn_sample(proj, glog, gla_norm, ret_norm, cosf, sinf, ret_lg, s0_gla, s0_ret, bb=16):
    ts, bsz, _ = proj.shape
    grid = (bsz // bb, GLA_HEADS)
    nk = GLA_HEADS
    kspec = lambda off: pl.BlockSpec((ts, bb, HEAD_DK), lambda i, h, off=off: (0, i, off + h))
    vspec = lambda off: pl.BlockSpec((ts, bb, HEAD_DV), lambda i, h, off=off: (0, i, off + h))
    hspec = pl.BlockSpec((None, 1, HEAD_DV), lambda i, h: (h, 0, 0))
    sspec = pl.BlockSpec((bb, None, HEAD_DK, HEAD_DV), lambda i, h: (i, h, 0, 0))
    ospec = pl.BlockSpec((ts, bb, HEAD_DV), lambda i, h: (0, i, h))
    out_shape = [
        jax.ShapeDtypeStruct((ts, bsz, GLA_HEADS * HEAD_DV), BF16),
        jax.ShapeDtypeStruct((bsz, GLA_HEADS, HEAD_DK, HEAD_DV), F32),
    ]
    params = _cparams(("parallel", "parallel"), 40)
    mg, s_gla = pl.pallas_call(
        _gla_sample_kernel,
        grid=grid,
        in_specs=[kspec(0), kspec(nk), vspec(nk), vspec(2 * nk),
                  pl.BlockSpec((ts, bb, HEAD_DK), lambda i, h: (0, i, h)),
                  hspec, sspec],
        out_specs=[ospec, sspec],
        out_shape=out_shape,
        compiler_params=params,
        name="gla_sample",
    )(proj, proj, proj, proj, glog, gla_norm, s0_gla)
    tspec = pl.BlockSpec((ts, HEAD_DK), lambda i, h: (0, 0))
    mr, s_ret = pl.pallas_call(
        _ret_sample_kernel,
        grid=grid,
        in_specs=[kspec(6 * nk), kspec(7 * nk), vspec(4 * nk), vspec(5 * nk),
                  tspec, tspec,
                  pl.BlockSpec((None, 1, LANES), lambda i, h: (h, 0, 0)),
                  hspec, sspec],
        out_specs=[ospec, sspec],
        out_shape=out_shape,
        compiler_params=params,
        name="ret_sample",
    )(proj, proj, proj, proj, cosf, sinf, ret_lg, ret_norm, s0_ret)
    return mg, mr, s_gla, s_ret


def _residual_out(x_ref, y, g_ref, mod_ref, outs, d):
    x_new = _gated_residual(x_ref[...], y, g_ref[...], mod_ref, d)
    if len(outs) == 1:
        outs[0][...] = x_new
    else:
        modn_ref, gn_ref, o_ref, h_ref = outs
        o_ref[...] = x_new
        h_ref[...] = _norm_mod(x_new, gn_ref[...], modn_ref, d).astype(BF16)


def _next_specs(nxt, tm, d):
    if nxt is None:
        return [], [], [], []
    mod_next, g_next, shape = nxt
    return ([_mod_spec(mod_next.shape[1], tm, 3 * d, 2), pl.BlockSpec((1, d), lambda g, i: (0, 0))],
            [pl.BlockSpec((None, tm, d), lambda g, i: (g, i, 0))],
            [jax.ShapeDtypeStruct(shape, BF16)], [mod_next, g_next])


def _outproj_kernel(x_ref, mod_ref, g_ref, mg_ref, mr_ref, wo_ref, *outs, d):
    half = mg_ref.shape[1]
    y = _dot(mg_ref[...], wo_ref[0:half, :]) + _dot(mr_ref[...], wo_ref[half:2 * half, :])
    _residual_out(x_ref, y, g_ref, mod_ref, outs, d)


def _outproj(x3, mod3, g_post, mg, mr, w_out, tm, nxt=None):
    gn, t, d = x3.shape
    r = mod3.shape[1]
    half = mg.shape[2]
    n_in, n_out, n_shape, n_ops = _next_specs(None if nxt is None else (*nxt, x3.shape), tm, d)
    res = pl.pallas_call(
        functools.partial(_outproj_kernel, d=d),
        grid=(gn, t // tm),
        in_specs=[
            pl.BlockSpec((None, tm, d), lambda g, i: (g, i, 0)),
            _mod_spec(r, tm, 3 * d, 2),
            pl.BlockSpec((1, d), lambda g, i: (0, 0)),
            pl.BlockSpec((None, tm, half), lambda g, i: (g, i, 0)),
            pl.BlockSpec((None, tm, half), lambda g, i: (g, i, 0)),
            pl.BlockSpec((2 * half, d), lambda g, i: (0, 0)),
        ] + n_in,
        out_specs=[pl.BlockSpec((None, tm, d), lambda g, i: (g, i, 0))] + n_out,
        out_shape=[jax.ShapeDtypeStruct((gn, t, d), F32)] + n_shape,
        compiler_params=_cparams(("parallel", "parallel"), 56),
        name="outproj",
    )(x3, mod3, g_post, mg, mr, w_out, *n_ops)
    return res if nxt is not None else (res[0], None)


def _mlp_kernel(x_ref, mod_ref, gpre_ref, gpost_ref, wup_ref, wdn_ref, o_ref, h_scr, acc_scr, *, d):
    j = pl.program_id(2)

    @pl.when(j == 0)
    def _():
        h_scr[...] = _norm_mod(x_ref[...], gpre_ref[...], mod_ref, d).astype(BF16)
        acc_scr[...] = jnp.zeros_like(acc_scr)

    u = jnp.maximum(_dot(h_scr[...], wup_ref[...]), 0.0)
    acc_scr[...] += _dot((u * u).astype(BF16), wdn_ref[...])

    @pl.when(j == pl.num_programs(2) - 1)
    def _():
        o_ref[...] = _gated_residual(x_ref[...], acc_scr[...], gpost_ref[...], mod_ref, d)


def _mlp_cast_kernel(x_ref, mod_ref, gpre_ref, gpost_ref, wup_ref, wdn_ref,
                     o_ref, wupb_ref, wdnb_ref, h_scr, acc_scr, *, d):
    wupb_ref[...] = wup_ref[...].astype(BF16)
    wdnb_ref[...] = wdn_ref[...].astype(BF16)
    _mlp_kernel(x_ref, mod_ref, gpre_ref, gpost_ref, wupb_ref, wdnb_ref, o_ref, h_scr, acc_scr, d=d)


def _mlp_cast(x3, mod3, g_pre, g_post, w_up_all, w_down_all, layer, tf=512):
    gn, t, d = x3.shape
    assert gn == 1
    r = mod3.shape[1]
    f = w_up_all.shape[2]
    return pl.pallas_call(
        functools.partial(_mlp_cast_kernel, d=d),
        grid=(1, 1, f // tf),
        in_specs=[
            pl.BlockSpec((None, t, d), lambda g, i, j: (0, 0, 0)),
            _mod_spec(r, t, 3 * d, 3),
            pl.BlockSpec((1, d), lambda g, i, j: (0, 0)),
            pl.BlockSpec((1, d), lambda g, i, j: (0, 0)),
            pl.BlockSpec((None, d, tf), lambda g, i, j: (layer, 0, j)),
            pl.BlockSpec((None, tf, d), lambda g, i, j: (layer, j, 0)),
        ],
        out_specs=[
            pl.BlockSpec((None, t, d), lambda g, i, j: (0, 0, 0)),
            pl.BlockSpec((d, tf), lambda g, i, j: (0, j)),
            pl.BlockSpec((tf, d), lambda g, i, j: (j, 0)),
        ],
        out_shape=[
            jax.ShapeDtypeStruct((1, t, d), F32),
            jax.ShapeDtypeStruct((d, f), BF16),
            jax.ShapeDtypeStruct((f, d), BF16),
        ],
        scratch_shapes=[pltpu.VMEM((t, d), BF16), pltpu.VMEM((t, d), F32)],
        compiler_params=_cparams(("arbitrary", "arbitrary", "arbitrary"), 56),
        name="mlp_cast",
    )(x3, mod3, g_pre, g_post, w_up_all, w_down_all)


def _mlp(x3, mod3, g_pre, g_post, w_up, w_down, tf=TF_MLP):
    gn, t, d = x3.shape
    assert gn == 1
    r = mod3.shape[1]
    f = w_up.shape[1]
    return pl.pallas_call(
        functools.partial(_mlp_kernel, d=d),
        grid=(1, 1, f // tf),
        in_specs=[
            pl.BlockSpec((None, t, d), lambda g, i, j: (0, 0, 0)),
            _mod_spec(r, t, 3 * d, 3),
            pl.BlockSpec((1, d), lambda g, i, j: (0, 0)),
            pl.BlockSpec((1, d), lambda g, i, j: (0, 0)),
            pl.BlockSpec((d, tf), lambda g, i, j: (0, j)),
            pl.BlockSpec((tf, d), lambda g, i, j: (j, 0)),
        ],
        out_specs=pl.BlockSpec((None, t, d), lambda g, i, j: (0, 0, 0)),
        out_shape=jax.ShapeDtypeStruct((1, t, d), F32),
        scratch_shapes=[pltpu.VMEM((t, d), BF16), pltpu.VMEM((t, d), F32)],
        compiler_params=_cparams(("arbitrary", "arbitrary", "arbitrary"), 56),
        name="mlp_rows",
    )(x3, mod3, g_pre, g_post, w_up, w_down)


def _mlp_h_kernel(x_ref, h_ref, mod_ref, gpost_ref, wup_ref, wdn_ref, o_ref, acc_scr, *, d):
    j = pl.program_id(2)

    @pl.when(j == 0)
    def _():
        acc_scr[...] = jnp.zeros_like(acc_scr)

    u = jnp.maximum(_dot(h_ref[...], wup_ref[...]), 0.0)
    acc_scr[...] += _dot((u * u).astype(BF16), wdn_ref[...])

    @pl.when(j == pl.num_programs(2) - 1)
    def _():
        o_ref[...] = _gated_residual(x_ref[...], acc_scr[...], gpost_ref[...], mod_ref, d)


def _mlp_h(x3, h3, mod3, g_post, w_up, w_down, tm, tf=TF_MLP):
    gn, t, d = x3.shape
    r = mod3.shape[1]
    f = w_up.shape[1]
    return pl.pallas_call(
        functools.partial(_mlp_h_kernel, d=d),
        grid=(gn, t // tm, f // tf),
        in_specs=[
            pl.BlockSpec((None, tm, d), lambda g, i, j: (g, i, 0)),
            pl.BlockSpec((None, tm, d), lambda g, i, j: (g, i, 0)),
            _mod_spec(r, tm, 3 * d, 3),
            pl.BlockSpec((1, d), lambda g, i, j: (0, 0)),
            pl.BlockSpec((d, tf), lambda g, i, j: (0, j)),
            pl.BlockSpec((tf, d), lambda g, i, j: (j, 0)),
        ],
        out_specs=pl.BlockSpec((None, tm, d), lambda g, i, j: (g, i, 0)),
        out_shape=jax.ShapeDtypeStruct((gn, t, d), F32),
        scratch_shapes=[pltpu.VMEM((tm, d), F32)],
        compiler_params=_cparams(("parallel", "parallel", "arbitrary"), 56),
        name="mlp",
    )(x3, h3, mod3, g_post, w_up, w_down)


def _s5_disc_kernel(lr_ref, li_ref, ldt_ref, br_ref, bi_ref, pwr_ref, pwi_ref, bbr_ref, bbi_ref, *,
                    seg_len):
    lr, li = lr_ref[...], li_ref[...]
    dt = jnp.exp(ldt_ref[...])
    mag = jnp.exp(lr * dt)
    lb_re, lb_im = mag * jnp.cos(li * dt), mag * jnp.sin(li * dt)
    nr, ni = lb_re - 1.0, lb_im
    den = lr * lr + li * li
    f_re = (nr * lr + ni * li) / den
    f_im = (ni * lr - nr * li) / den
    br, bi = br_ref[...], bi_ref[...]
    bbr_ref[...] = f_re * br - f_im * bi
    bbi_ref[...] = f_re * bi + f_im * br
    pwr_ref[0] = lb_re
    pwi_ref[0] = lb_im
    qr, qi = None, None
    sr, si = lb_re, lb_im
    e = seg_len
    while e:
        if e & 1:
            qr, qi = (sr, si) if qr is None else (qr * sr - qi * si, qr * si + qi * sr)
        e >>= 1
        if e:
            sr, si = sr * sr - si * si, 2.0 * sr * si
    pr, pi = qr, qi
    for n in range(SUBLANES):
        pwr_ref[1 + n] = pr
        pwi_ref[1 + n] = pi
        pr, pi = pr * qr - pi * qi, pr * qi + pi * qr


def _s5_discretize(lam_re, lam_im, log_dt, bt_re, bt_im, seg_len):
    g, _, p = lam_re.shape
    c = bt_re.shape[1]
    return pl.pallas_call(
        functools.partial(_s5_disc_kernel, seg_len=seg_len),
        out_shape=[
            jax.ShapeDtypeStruct((1 + SUBLANES, g, 1, p), F32),
            jax.ShapeDtypeStruct((1 + SUBLANES, g, 1, p), F32),
            jax.ShapeDtypeStruct((g, c, p), F32),
            jax.ShapeDtypeStruct((g, c, p), F32),
        ],
        name="s5_discretize",
    )(lam_re, lam_im, log_dt, bt_re, bt_im)


def _gelu_tanh(x):
    c0 = math.sqrt(2.0 / math.pi)
    return x * (0.5 * (1.0 + jnp.tanh(c0 * (x + 0.044715 * (x * x * x)))))


def _cmul_add(ar, ai, xr, xi, yr, yi):
    return yr + ar * xr - ai * xi, yi + ar * xi + ai * xr


def _s5_seq_kernel(x_ref, mod_ref, gpre_ref, bblk_ref, cblk_ref, lam_ref, dskip_ref, s0r_ref, s0i_ref,
                   z_ref, sr_ref, si_ref, h_scr, xr_scr, xi_scr, *, d):
    cb = pl.program_id(0)
    ncb, tm, uw = h_scr.shape
    cw = xr_scr.shape[1]
    seg = s0r_ref.shape[0]

    @pl.when(cb == 0)
    def _():
        h = _norm_mod(x_ref[...], gpre_ref[...], mod_ref, d)
        for c in range(ncb):
            h_scr[c] = h[:, c * uw:(c + 1) * uw]

    u = h_scr[cb]
    bu = _dot(u.astype(BF16), bblk_ref[...])
    xr_scr[...] = bu[:, 0:cw]
    xi_scr[...] = bu[:, cw:2 * cw]
    car_r, car_i = s0r_ref[...], s0i_ref[...]
    l_r, l_i = lam_ref[0:1, :], lam_ref[1:2, :]
    for t in range(tm // seg):
        rows = slice(t * seg, (t + 1) * seg)
        car_r, car_i = _cmul_add(l_r, l_i, car_r, car_i, xr_scr[rows, :], xi_scr[rows, :])
        xr_scr[rows, :] = car_r
        xi_scr[rows, :] = car_i
    sr_ref[...] = car_r
    si_ref[...] = car_i
    xs = jnp.concatenate([xr_scr[...].astype(BF16), xi_scr[...].astype(BF16)], axis=1)
    y = _dot(xs, cblk_ref[...]) + dskip_ref[...] * u
    z_ref[...] = _gelu_tanh(y).astype(BF16)


def _s5_seq(x2, mod2, g_pre, bblk, cblk, lam2, dskip, s0_re, s0_im):
    tm, d = x2.shape
    seg, nst = s0_re.shape
    ncb, uw, cw2 = bblk.shape
    cw = cw2 // 2
    sspec = pl.BlockSpec((seg, cw), lambda c: (0, c))
    return pl.pallas_call(
        functools.partial(_s5_seq_kernel, d=d),
        grid=(ncb,),
        in_specs=[
            pl.BlockSpec((tm, d), lambda c: (0, 0)),
            pl.BlockSpec((seg, 3 * d), lambda c: (0, 0)),
            pl.BlockSpec((1, d), lambda c: (0, 0)),
            pl.BlockSpec((None, uw, cw2), lambda c: (c, 0, 0)),
            pl.BlockSpec((None, cw2, uw), lambda c: (c, 0, 0)),
            pl.BlockSpec((2, cw), lambda c: (0, c)),
            pl.BlockSpec((1, uw), lambda c: (0, c)),
            sspec, sspec,
        ],
        out_specs=[pl.BlockSpec((tm, uw), lambda c: (0, c)), sspec, sspec],
        out_shape=[
            jax.ShapeDtypeStruct((tm, d), BF16),
            jax.ShapeDtypeStruct((seg, nst), F32),
            jax.ShapeDtypeStruct((seg, nst), F32),
        ],
        scratch_shapes=[
            pltpu.VMEM((ncb, tm, uw), F32),
            pltpu.VMEM((tm, cw), F32),
            pltpu.VMEM((tm, cw), F32),
        ],
        compiler_params=_cparams(("arbitrary",), 48),
        name="s5_seq",
    )(x2, mod2, g_pre, bblk, cblk, lam2, dskip, s0_re, s0_im)


def _s5_rows_kernel(*refs, d, ncast):
    x_ref, mod_ref, gpre_ref, bblk_ref, cblk_ref, tbl_ref, dskip_ref = refs[:7]
    cast_in = refs[7:7 + ncast]
    z_ref, sr_ref, si_ref = refs[7 + ncast:10 + ncast]
    cast_out = refs[10 + ncast:10 + 2 * ncast]
    h_scr, xr_scr, xi_scr, cr_scr, ci_scr = refs[10 + 2 * ncast:]
    for src, dst in zip(cast_in, cast_out):
        dst[...] = src[...].astype(BF16)

    ncol, tm, _ = xr_scr.shape
    ncb, uw, _ = bblk_ref.shape
    sl = tm // SUBLANES

    @pl.when(pl.program_id(1) == 0)
    def _():
        cr_scr[...] = jnp.zeros_like(cr_scr)
        ci_scr[...] = jnp.zeros_like(ci_scr)

    h_scr[...] = _norm_mod(x_ref[...], gpre_ref[...], mod_ref, d)
    row0 = lax.broadcasted_iota(jnp.int32, (ncol, SUBLANES, LANES), 1) == 0

    for c in range(ncb):
        us = slice(c * uw, (c + 1) * uw)
        cols = slice(c * ncol, (c + 1) * ncol)
        u = h_scr[:, us]
        bu = _dot(u.astype(BF16), bblk_ref[c])
        for j in range(ncol):
            xr_scr[j] = bu[:, j * LANES:(j + 1) * LANES]
            xi_scr[j] = bu[:, (ncol + j) * LANES:(ncol + j + 1) * LANES]
        l_r, l_i = tbl_ref[0, cols], tbl_ref[1, cols]

        def local(i, s):
            rows = pl.ds(pl.multiple_of(i * SUBLANES, SUBLANES), SUBLANES)
            return _cmul_add(l_r, l_i, s[0], s[1], xr_scr[:, rows, :], xi_scr[:, rows, :])

        zero = jnp.zeros((ncol, SUBLANES, LANES), F32)
        g_r, g_i = lax.fori_loop(0, sl, local, (zero, zero), unroll=True)
        for n in range(3):
            g_r, g_i = _cmul_add(tbl_ref[2 + 2 * n, cols], tbl_ref[3 + 2 * n, cols],
                                 pltpu.roll(g_r, 1 << n, 1), pltpu.roll(g_i, 1 << n, 1), g_r, g_i)
        car_r, car_i = cr_scr[cols], ci_scr[cols]
        g_r, g_i = _cmul_add(tbl_ref[8, cols], tbl_ref[9, cols], car_r, car_i, g_r, g_i)
        in_r = jnp.where(row0, car_r, pltpu.roll(g_r, 1, 1))
        in_i = jnp.where(row0, car_i, pltpu.roll(g_i, 1, 1))

        def full(i, s):
            rows = pl.ds(pl.multiple_of(i * SUBLANES, SUBLANES), SUBLANES)
            s_r, s_i = _cmul_add(l_r, l_i, s[0], s[1], xr_scr[:, rows, :], xi_scr[:, rows, :])
            xr_scr[:, rows, :] = s_r
            xi_scr[:, rows, :] = s_i
            return s_r, s_i

        e_r, e_i = lax.fori_loop(0, sl, full, (in_r, in_i), unroll=True)
        cr_scr[cols] = e_r[:, SUBLANES - 1:SUBLANES, :]
        ci_scr[cols] = e_i[:, SUBLANES - 1:SUBLANES, :]
        xs = jnp.concatenate([xr_scr[j].astype(BF16) for j in range(ncol)]
                             + [xi_scr[j].astype(BF16) for j in range(ncol)], axis=1)
        y = _dot(xs, cblk_ref[c]) + dskip_ref[:, us] * u
        z_ref[:, us] = _gelu_tanh(y).astype(BF16)

    sr_ref[...] = cr_scr[...]
    si_ref[...] = ci_scr[...]


def _s5_rows(x3, mod3, g_pre, bblk, cblk, tbl, dskip, tm, casts=()):
    gn, t, d = x3.shape
    ncb, uw, cw2 = bblk.shape
    ncol = cw2 // 2 // LANES
    nct = ncb * ncol
    nt = t // tm
    nstep = gn * nt
    const = lambda shape: pl.BlockSpec(shape, lambda g, i: (0,) * len(shape), pipeline_mode=pl.Buffered(1))
    ospec = pl.BlockSpec((None, None, nct, 1, LANES), lambda g, i: (g, i, 0, 0, 0))
    c_in, c_out, c_shape = [], [], []
    for w, layer in casts:
        _, rows, cols = w.shape
        slab = rows // nstep
        assert slab * nstep == rows and slab % (2 * SUBLANES) == 0
        c_in.append(pl.BlockSpec((None, slab, cols), lambda g, i, layer=layer: (layer, g * nt + i, 0)))
        c_out.append(pl.BlockSpec((slab, cols), lambda g, i: (g * nt + i, 0)))
        c_shape.append(jax.ShapeDtypeStruct((rows, cols), BF16))
    res = pl.pallas_call(
        functools.partial(_s5_rows_kernel, d=d, ncast=len(casts)),
        grid=(gn, nt),
        in_specs=[
            pl.BlockSpec((None, tm, d), lambda g, i: (g, i, 0)),
            _mod_spec(1, tm, 3 * d, 2),
            pl.BlockSpec((1, d), lambda g, i: (0, 0)),
            const(bblk.shape), const(cblk.shape), const(tbl.shape), const(dskip.shape),
        ] + c_in,
        out_specs=[pl.BlockSpec((None, tm, d), lambda g, i: (g, i, 0)), ospec, ospec] + c_out,
        out_shape=[
            jax.ShapeDtypeStruct((gn, t, d), BF16),
            jax.ShapeDtypeStruct((gn, nt, nct, 1, LANES), F32),
            jax.ShapeDtypeStruct((gn, nt, nct, 1, LANES), F32),
        ] + c_shape,
        scratch_shapes=[
            pltpu.VMEM((tm, d), F32),
            pltpu.VMEM((ncol, tm, LANES), F32),
            pltpu.VMEM((ncol, tm, LANES), F32),
            pltpu.VMEM((nct, 1, LANES), F32),
            pltpu.VMEM((nct, 1, LANES), F32),
        ],
        compiler_params=_cparams(("arbitrary", "arbitrary"), 56),
        name="s5_rows",
    )(x3, mod3, g_pre, bblk, cblk, tbl, dskip, *[w for w, _ in casts])
    z3, s_re, s_im = res[:3]
    last = lambda s: s[:, nt - 1].reshape(gn, 1, nct * LANES)
    return z3, last(s_re), last(s_im), list(res[3:])


def _glu_kernel(x_ref, mod_ref, g_ref, z_ref, wa_ref, wb_ref, *outs, d):
    z = z_ref[...]
    y = _dot(z, wa_ref[...]) * jax.nn.sigmoid(_dot(z, wb_ref[...]))
    _residual_out(x_ref, y, g_ref, mod_ref, outs, d)


def _glu(x3, mod3, g_post, z3, w_a, w_b, tm, nxt=None):
    gn, t, d = x3.shape
    r = mod3.shape[1]
    wspec = pl.BlockSpec((d, d), lambda g, i: (0, 0), pipeline_mode=pl.Buffered(1))
    n_in, n_out, n_shape, n_ops = _next_specs(None if nxt is None else (*nxt, x3.shape), tm, d)
    res = pl.pallas_call(
        functools.partial(_glu_kernel, d=d),
        grid=(gn, t // tm),
        in_specs=[
            pl.BlockSpec((None, tm, d), lambda g, i: (g, i, 0)),
            _mod_spec(r, tm, 3 * d, 2),
            pl.BlockSpec((1, d), lambda g, i: (0, 0)),
            pl.BlockSpec((None, tm, d), lambda g, i: (g, i, 0)),
            wspec, wspec,
        ] + n_in,
        out_specs=[pl.BlockSpec((None, tm, d), lambda g, i: (g, i, 0))] + n_out,
        out_shape=[jax.ShapeDtypeStruct((gn, t, d), F32)] + n_shape,
        compiler_params=_cparams(("parallel", "parallel"), 56),
        name="glu",
    )(x3, mod3, g_post, z3, w_a, w_b, *n_ops)
    return res if nxt is not None else (res[0], None)


def _rope_tables(pos):
    half = HEAD_DK // 2
    inv = ROPE_BASE ** (-jnp.arange(half, dtype=F32) / half)
    ang = pos.astype(F32)[:, None] * inv[None, :]
    cos, sin = jnp.cos(ang), jnp.sin(ang)
    return jnp.concatenate([cos, cos], axis=-1), jnp.concatenate([-sin, sin], axis=-1)


def _block_diag(w):
    ncb, gpb, a, b = w.shape
    tiled = jnp.broadcast_to(w.reshape(ncb, gpb * a, 1, b), (ncb, gpb * a, gpb, b))
    row_g = jnp.arange(gpb * a)[:, None, None] // a
    col_g = jnp.arange(gpb)[None, :, None]
    return jnp.where(row_g == col_g, tiled, 0.0).reshape(ncb, gpb * a, gpb * b)


def _s5_tables(pw_re, pw_im):
    n = pw_re.shape[0]
    flat_r = pw_re.reshape(n, -1)
    flat_i = pw_im.reshape(n, -1)
    row = jnp.arange(SUBLANES)[:, None]
    tabs = [jnp.broadcast_to(flat_r[0], (SUBLANES, flat_r.shape[1])),
            jnp.broadcast_to(flat_i[0], (SUBLANES, flat_i.shape[1]))]
    for s in (1, 2, 4):
        mask = row >= s
        tabs.append(jnp.where(mask, flat_r[s][None, :], 0.0))
        tabs.append(jnp.where(mask, flat_i[s][None, :], 0.0))
    tabs += [flat_r[1:], flat_i[1:]]
    tbl = jnp.stack(tabs)
    tbl = tbl.reshape(tbl.shape[0], SUBLANES, -1, LANES).transpose(0, 2, 1, 3)
    return tbl, jnp.stack([flat_r[0], flat_i[0]])


def kernel(x_prompt, x_sample, state_gla, state_ret, state_s5_re, state_s5_im, c_prompt, c_sample,
           w_ada, b_ada, norm_pre, norm_post, w_in_mix, w_gla_gk, b_gla_gk, gla_head_norm,
           ret_head_norm, w_out_mix, s5_lam_re, s5_lam_im, s5_log_dt, s5_b_re, s5_b_im,
           s5_c_re, s5_c_im, s5_d, w_glu_a, w_glu_b, w_mlp_up, w_mlp_down):
    bp, tp, d = x_prompt.shape
    bs, ts, _ = x_sample.shape
    depth = w_ada.shape[0]

    nrow = -(-(bs + bp) // SUBLANES) * SUBLANES
    c_all = jnp.concatenate([c_sample, c_prompt, jnp.zeros((nrow - bs - bp, d), F32)], axis=0)
    mod_all = _adaln(c_all, w_ada.reshape(depth * 2, d, 3 * d), b_ada.reshape(depth * 2, 1, 3 * d))
    mod_s = [mod_all[k, 0:bs][None] for k in range(depth * 2)]
    mod_p = [mod_all[k, bs:bs + bp][:, None, :] for k in range(depth * 2)]

    w_in_t = jnp.swapaxes(w_in_mix, 1, 2)
    w_gk = jnp.pad(w_gla_gk[0], ((0, LANES - GLA_RANK), (0, 0))).astype(BF16)
    b_gk = b_gla_gk[0][None, :]
    w_out = w_out_mix[0].astype(BF16)
    gla_norm = gla_head_norm[0][:, None, :]
    ret_norm = ret_head_norm[0][:, None, :]
    gamma_log = jnp.log1p(-jnp.power(2.0, -5.0 - jnp.arange(RET_HEADS, dtype=F32)))
    ret_lg = jnp.broadcast_to(gamma_log[:, None, None], (RET_HEADS, 1, LANES))

    ng = s5_lam_re.shape[1]
    ncb = ng // S5_GPB
    tm5 = min(TM_S5, tp)
    pw_re, pw_im, bbt_re, bbt_im = _s5_discretize(
        s5_lam_re[0][:, None, :], s5_lam_im[0][:, None, :], s5_log_dt[0][:, None, None],
        jnp.swapaxes(s5_b_re[0], 1, 2), jnp.swapaxes(s5_b_im[0], 1, 2), tm5 // SUBLANES)
    bblk = jnp.concatenate([
        _block_diag(bbt_re.reshape(ncb, S5_GPB, S5_GROUP, S5_STATE)),
        _block_diag(bbt_im.reshape(ncb, S5_GPB, S5_GROUP, S5_STATE))], axis=2).astype(BF16)
    ct_re = jnp.swapaxes(s5_c_re[0], 1, 2).reshape(ncb, S5_GPB, S5_STATE, S5_GROUP)
    ct_im = jnp.swapaxes(s5_c_im[0], 1, 2).reshape(ncb, S5_GPB, S5_STATE, S5_GROUP)
    cblk = jnp.concatenate([_block_diag(ct_re), _block_diag(-ct_im)], axis=1).astype(BF16)
    tbl, lam2 = _s5_tables(pw_re[:, :, 0, :], pw_im[:, :, 0, :])
    dskip = s5_d[0][None, :]

    nxt = lambda mods, l, emit_h: (mods[2 * l + 1], norm_pre[l, 1][None]) if emit_h else None

    def layer0(x3, mods, tm, inproj, attn, mlp, emit_h):
        proj, glog = inproj(x3, mods[0])
        mg, mr, s_gla, s_ret = attn(proj, glog)
        x3, h3 = _outproj(x3, mods[0], norm_post[0, 0][None], mg, mr, w_out, tm, nxt(mods, 0, emit_h))
        return mlp(0, x3, h3, mods[1]), s_gla, s_ret

    def layer1_tail(x3, z3, mods, tm, glu_w, mlp, emit_h):
        x3, h3 = _glu(x3, mods[2], norm_post[1, 0][None], z3, glu_w[0], glu_w[1], tm, nxt(mods, 1, emit_h))
        return mlp(1, x3, h3, mods[3])

    cos_s, sin_s = _rope_tables(PAST_LEN + jnp.arange(ts, dtype=F32))

    def attn_s(proj, glog):
        tm_rows = lambda a: a.reshape(ts, bs, a.shape[-1])
        mg, mr, s_gla, s_ret = _attn_sample(tm_rows(proj), tm_rows(glog), gla_norm, ret_norm, cos_s, sin_s,
                                            ret_lg, state_gla[0], state_ret[0])
        flat = lambda a: a.reshape(1, ts * bs, a.shape[-1])
        return flat(mg), flat(mr), s_gla, s_ret

    xs3 = jnp.swapaxes(x_sample, 0, 1).reshape(1, ts * bs, d)

    def s5_s(x3, mod3):
        z2, s_re, s_im = _s5_seq(x3[0], mod3[0], norm_pre[1, 0][None], bblk, cblk, lam2, dskip,
                                 state_s5_re[0].reshape(bs, -1), state_s5_im[0].reshape(bs, -1))
        return z2[None], s_re, s_im

    w_up, w_dn, w_main = {}, {}, {}

    def inproj_s(x3, mod3):
        proj, glog, w_main[0], w_main["lr"] = _inproj_cast(x3, mod3, norm_pre[0, 0][None], w_in_t, 0,
                                                            w_gk, b_gk)
        return proj, glog

    def mlp_s(l, x3, h3, mod3):
        g_pre, g_post = norm_pre[l, 1][None], norm_post[l, 1][None]
        if l in w_up:
            return _mlp(x3, mod3, g_pre, g_post, w_up[l], w_dn[l])
        x3, w_up[l], w_dn[l] = _mlp_cast(x3, mod3, g_pre, g_post, w_mlp_up, w_mlp_down, l)
        return x3

    tm_s = ts * bs
    xs1, gla_s, ret_s = layer0(xs3, mod_s, tm_s, inproj_s, attn_s, mlp_s, False)

    cos_p, sin_p = _rope_tables(jnp.arange(tp, dtype=F32))
    zeros_att = jnp.zeros((bp, GLA_HEADS, HEAD_DK, HEAD_DV), F32)
    tm_p = min(TM_DENSE, tp)

    def attn_p(proj, glog):
        return _attn_prompt(proj, glog, gla_norm, ret_norm, cos_p, sin_p, ret_lg, zeros_att, zeros_att)

    def mlp_p(l, x3, h3, mod3):
        return _mlp_h(x3, h3, mod3, norm_post[l, 1][None], w_up[l], w_dn[l], tm_p)

    def inproj_p(x3, mod3):
        return _inproj(x3, mod3, norm_pre[0, 0][None], w_main[0], w_main["lr"], w_gk, b_gk, tm_p)

    xp1, gla_p, ret_p = layer0(x_prompt, mod_p, tm_p, inproj_p, attn_p, mlp_p, True)

    sl = tm5 // SUBLANES
    xpp = jnp.swapaxes(xp1.reshape(bp, tp // tm5, SUBLANES, sl, d), 2, 3).reshape(bp, tp, d)
    zpp, re_p, im_p, (w_up[1], w_dn[1], w_ga, w_gb) = _s5_rows(
        xpp, mod_p[2], norm_pre[1, 0][None], bblk, cblk, tbl, dskip, tm5,
        casts=((w_mlp_up, 1), (w_mlp_down, 1), (w_glu_a, 0), (w_glu_b, 0)))
    zp3 = jnp.swapaxes(zpp.reshape(bp, tp // tm5, sl, SUBLANES, d), 2, 3).reshape(bp, tp, d)

    zs3, re_s, im_s = s5_s(xs1, mod_s[2])
    y_s = layer1_tail(xs1, zs3, mod_s, tm_s, (w_ga, w_gb), mlp_s, False)
    y_s = jnp.swapaxes(y_s.reshape(ts, bs, d), 0, 1)
    y_p = layer1_tail(xp1, zp3, mod_p, tm_p, (w_ga, w_gb), mlp_p, True)

    st = lambda a, b_: a.reshape(1, b_, ng, S5_STATE)
    return (y_p, y_s, gla_p[None], gla_s[None], ret_p[None], ret_s[None],
            st(re_p, bp), st(re_s, bs), st(im_p, bp), st(im_s, bs))
```

```python
import functools
import math

import jax
import jax.numpy as jnp
import numpy as np
from jax import lax
from jax.experimental import pallas as pl
from jax.experimental.pallas import tpu as pltpu

F32 = jnp.float32
BF16 = jnp.bfloat16

EPS = 1e-6
LANES = 128
SUBLANES = 8
MIB = 1024 * 1024

GLA_HEADS = 4
RET_HEADS = 4
HEAD_DK = 128
HEAD_DV = 256
GLA_RANK = 16
GLA_LOGIT_NORM = 16.0
ROPE_BASE = 10000.0
PAST_LEN = 16384
S5_GROUP = 16
S5_STATE = 64
S5_GPB = 16
S5_UW = S5_GPB * S5_GROUP
ATT_CHUNK = 128
GLA_SUB = 16
TM_DENSE = 512
TM_S5 = 256
TN_INPROJ = 1024
TF_MLP = 1024


def _cparams(sem, vmem_mib):
    return pltpu.CompilerParams(dimension_semantics=sem, vmem_limit_bytes=vmem_mib * MIB)


def _dot(a, b):
    return jnp.dot(a, b, preferred_element_type=F32)


def _dot_nt(a, b):
    return lax.dot_general(a, b, (((1,), (1,)), ((), ())), preferred_element_type=F32)


def _rms(x, g):
    return x * lax.rsqrt(jnp.mean(x * x, axis=-1, keepdims=True) + EPS) * g


def _rows_affine(y, a, b=None):
    tm, d = y.shape
    r = a.shape[0]
    if r == 1 or r == tm:
        out = y * a
        return out if b is None else out + b
    y3 = y.reshape(tm // r, r, d)
    out = y3 * a[None]
    if b is not None:
        out = out + b[None]
    return out.reshape(tm, d)


def _norm_mod(x, g, mod_ref, d):
    return _rows_affine(_rms(x, g), 1.0 + mod_ref[:, d:2 * d], mod_ref[:, 0:d])


def _gated_residual(x, y, g, mod_ref, d):
    return x + _rows_affine(_rms(y, g), mod_ref[:, 2 * d:3 * d])


def _mod_spec(r, tm, width, ngrid):
    if ngrid == 2:
        if r == 1:
            return pl.BlockSpec((None, 1, width), lambda g, i: (g, 0, 0))
        return pl.BlockSpec((None, r, width), lambda g, i: (g, 0, 0))
    if r == 1:
        return pl.BlockSpec((None, 1, width), lambda g, i, j: (g, 0, 0))
    return pl.BlockSpec((None, r, width), lambda g, i, j: (g, 0, 0))


def _adaln_kernel(c_ref, w_ref, b_ref, o_ref):
    c = c_ref[...]
    sc = (c * jax.nn.sigmoid(c)).astype(BF16)
    o_ref[...] = _dot(sc, w_ref[...].astype(BF16)) + b_ref[...]


def _adaln(c_all, w_ada, b_ada, tn=1024):
    ls, d, n = w_ada.shape
    rows = c_all.shape[0]
    return pl.pallas_call(
        _adaln_kernel,
        grid=(ls, n // tn),
        in_specs=[
            pl.BlockSpec((rows, d), lambda l, j: (0, 0)),
            pl.BlockSpec((None, d, tn), lambda l, j: (l, 0, j)),
            pl.BlockSpec((None, 1, tn), lambda l, j: (l, 0, j)),
        ],
        out_specs=pl.BlockSpec((None, rows, tn), lambda l, j: (l, 0, j)),
        out_shape=jax.ShapeDtypeStruct((ls, rows, n), F32),
        compiler_params=_cparams(("parallel", "parallel"), 40),
        name="adaln",
    )(c_all, w_ada, b_ada)


def _log_sigmoid(x):
    return jnp.minimum(x, 0.0) - jnp.log1p(jnp.exp(-jnp.abs(x)))


def _gate_logits(hb, wlr_t, wgk_ref, bgk_ref, glog_ref):
    glr = _dot_nt(hb, wlr_t)
    logit = _dot(glr.astype(BF16), wgk_ref[...]) + bgk_ref[...]
    glog_ref[...] = _log_sigmoid(logit) * (1.0 / GLA_LOGIT_NORM)


def _inproj_kernel(x_ref, mod_ref, g_ref, w_ref, wlr_ref, wgk_ref, bgk_ref,
                   proj_ref, glog_ref, h_scr, *, d, tps):
    j = pl.program_id(2)

    @pl.when(j == 0)
    def _():
        hb = _norm_mod(x_ref[...], g_ref[...], mod_ref, d).astype(BF16)
        h_scr[...] = hb
        _gate_logits(hb, wlr_ref[...], wgk_ref, bgk_ref, glog_ref)

    tn = w_ref.shape[2]
    for k in range(tps):
        proj_ref[:, k * tn:(k + 1) * tn] = _dot(h_scr[...], w_ref[j * tps + k])


def _inproj_cast_kernel(x_ref, mod_ref, g_ref, wa_ref, wb_ref, wgk_ref, bgk_ref,
                        proj_ref, glog_ref, wout_ref, wlr_ref, h_scr, *, d, n_lo):
    j = pl.program_id(2)

    @pl.when(j == 0)
    def _():
        h_scr[...] = _norm_mod(x_ref[...], g_ref[...], mod_ref, d).astype(BF16)

    @pl.when(j < n_lo)
    def _():
        wout_ref[...] = jnp.transpose(wa_ref[...]).astype(BF16)

    @pl.when(j >= n_lo)
    def _():
        w = jnp.concatenate([wa_ref[GLA_RANK:, :], wb_ref[:GLA_RANK, :]], axis=0)
        wout_ref[...] = jnp.transpose(w).astype(BF16)

    @pl.when(j == n_lo)
    def _():
        wlr_t = _pad_rows(wa_ref[:GLA_RANK, :], LANES).astype(BF16)
        wlr_ref[...] = wlr_t
        _gate_logits(h_scr[...], wlr_t, wgk_ref, bgk_ref, glog_ref)

    proj_ref[...] = _dot(h_scr[...], wout_ref[...])


def _inproj_cast(x3, mod3, g_pre, w_raw_t, layer, w_gk, b_gk, tn=TN_INPROJ):
    gn, t, d = x3.shape
    assert gn == 1
    sec = (w_raw_t.shape[1] - GLA_RANK) // 2
    assert sec % tn == 0 and tn % LANES == 0
    nj = 2 * sec // tn
    n = nj * tn
    r = mod3.shape[1]
    gkey = w_gk.shape[1]
    return pl.pallas_call(
        functools.partial(_inproj_cast_kernel, d=d, n_lo=sec // tn),
        grid=(1, 1, nj),
        in_specs=[
            pl.BlockSpec((None, t, d), lambda g, i, j: (0, 0, 0)),
            _mod_spec(r, t, 3 * d, 3),
            pl.BlockSpec((1, d), lambda g, i, j: (0, 0)),
            pl.BlockSpec((None, tn, d), lambda g, i, j: (layer, j, 0)),
            pl.BlockSpec((None, LANES, d), lambda g, i, j: (layer, (j + 1) * (tn // LANES), 0)),
            pl.BlockSpec((LANES, gkey), lambda g, i, j: (0, 0)),
            pl.BlockSpec((1, gkey), lambda g, i, j: (0, 0)),
        ],
        out_specs=[
            pl.BlockSpec((None, t, tn), lambda g, i, j: (0, 0, j)),
            pl.BlockSpec((None, t, gkey), lambda g, i, j: (0, 0, 0)),
            pl.BlockSpec((None, d, tn), lambda g, i, j: (j, 0, 0)),
            pl.BlockSpec((LANES, d), lambda g, i, j: (0, 0)),
        ],
        out_shape=[
            jax.ShapeDtypeStruct((1, t, n), F32),
            jax.ShapeDtypeStruct((1, t, gkey), F32),
            jax.ShapeDtypeStruct((nj, d, tn), BF16),
            jax.ShapeDtypeStruct((LANES, d), BF16),
        ],
        scratch_shapes=[pltpu.VMEM((t, d), BF16)],
        compiler_params=_cparams(("arbitrary", "arbitrary", "arbitrary"), 56),
        name="inproj_cast",
    )(x3, mod3, g_pre, w_raw_t, w_raw_t, w_gk, b_gk)


def _inproj(x3, mod3, g_pre, w_main, w_lr_t, w_gk, b_gk, tm, tps=3):
    gn, t, d = x3.shape
    nj, _, tn = w_main.shape
    n = nj * tn
    r = mod3.shape[1]
    gkey = w_gk.shape[1]
    return pl.pallas_call(
        functools.partial(_inproj_kernel, d=d, tps=tps),
        grid=(gn, t // tm, nj // tps),
        in_specs=[
            pl.BlockSpec((None, tm, d), lambda g, i, j: (g, i, 0)),
            _mod_spec(r, tm, 3 * d, 3),
            pl.BlockSpec((1, d), lambda g, i, j: (0, 0)),
            pl.BlockSpec((nj, d, tn), lambda g, i, j: (0, 0, 0), pipeline_mode=pl.Buffered(1)),
            pl.BlockSpec((LANES, d), lambda g, i, j: (0, 0)),
            pl.BlockSpec((LANES, gkey), lambda g, i, j: (0, 0)),
            pl.BlockSpec((1, gkey), lambda g, i, j: (0, 0)),
        ],
        out_specs=[
            pl.BlockSpec((None, tm, tps * tn), lambda g, i, j: (g, i, j)),
            pl.BlockSpec((None, tm, gkey), lambda g, i, j: (g, i, 0)),
        ],
        out_shape=[
            jax.ShapeDtypeStruct((gn, t, n), F32),
            jax.ShapeDtypeStruct((gn, t, gkey), F32),
        ],
        scratch_shapes=[pltpu.VMEM((tm, d), BF16)],
        compiler_params=_cparams(("parallel", "parallel", "arbitrary"), 56),
        name="inproj",
    )(x3, mod3, g_pre, w_main, w_lr_t, w_gk, b_gk)


def _cumsum_rows(g):
    c = g.shape[0]
    row = lax.broadcasted_iota(jnp.int32, g.shape, 0)
    s = 1
    while s < c:
        g = g + jnp.where(row >= s, pltpu.roll(g, s, 0), 0.0)
        s *= 2
    return g


def _pad_rows(a, rows):
    if a.shape[0] == rows:
        return a
    return jnp.concatenate([a, jnp.zeros((rows - a.shape[0], a.shape[1]), a.dtype)], axis=0)


def _col_bcast(row, width):
    sq = jnp.transpose(jnp.broadcast_to(row, (LANES, LANES)))
    return jnp.concatenate([sq] * (width // LANES), axis=1)


def _gla_core(q, k, v, g, s, sub):
    cq = q.shape[0]
    ck = max(cq, LANES)
    b = _cumsum_rows(g)
    be = b - g
    bk = _pad_rows(b, ck)
    kp = _pad_rows(k, ck)
    vp = _pad_rows(v, ck).astype(BF16)
    rowj = lax.broadcasted_iota(jnp.int32, (ck, 1), 0)
    att_rows = []
    for blk in range(cq // sub):
        lo, hi = blk * sub, (blk + 1) * sub
        base = be[lo:lo + 1, :]
        qs = q[lo:hi] * jnp.exp(b[lo:hi] - base)
        ks = jnp.where(rowj < hi, kp * jnp.exp(base - bk), 0.0)
        att_rows.append(_dot_nt(qs.astype(BF16), ks.astype(BF16)))
    att = att_rows[0] if len(att_rows) == 1 else jnp.concatenate(att_rows, axis=0)
    ri = lax.broadcasted_iota(jnp.int32, (cq, ck), 0)
    cj = lax.broadcasted_iota(jnp.int32, (cq, ck), 1)
    att = jnp.where(ri >= cj, att, 0.0)
    o = _dot(att.astype(BF16), vp) + _dot((q * jnp.exp(b)).astype(BF16), s.astype(BF16))
    b_last = b[cq - 1:cq, :]
    k_out = kp * jnp.exp(b_last - bk)
    s_new = s * _col_bcast(jnp.exp(b_last), s.shape[1]) + _dot(jnp.transpose(k_out).astype(BF16), vp)
    return o, s_new


def _ret_core(q, k, v, s, lg, dmat, valid):
    cq = q.shape[0]
    ck = max(cq, LANES)
    kp = _pad_rows(k, ck)
    vp = _pad_rows(v, ck).astype(BF16)
    ti = lax.broadcasted_iota(jnp.int32, (cq, 1), 0).astype(F32)
    tj = lax.broadcasted_iota(jnp.int32, (ck, 1), 0).astype(F32)
    att = _dot_nt(q.astype(BF16), kp.astype(BF16)) * dmat
    q_in = q * jnp.exp((ti + 1.0) * lg)
    o = _dot(att.astype(BF16), vp) + _dot(q_in.astype(BF16), s.astype(BF16))
    k_out = kp * jnp.exp((float(valid - 1) - tj) * lg)
    s_new = s * jnp.exp(float(valid) * lg) + _dot(jnp.transpose(k_out).astype(BF16), vp)
    return o, s_new


def _decay_matrix(cq, ck, lg):
    ri = lax.broadcasted_iota(jnp.int32, (cq, ck), 0)
    cj = lax.broadcasted_iota(jnp.int32, (cq, ck), 1)
    diff = (ri - cj).astype(F32)
    return jnp.where(ri >= cj, jnp.exp(diff * lg), 0.0)


def _rope(x, cosf, sinf):
    return x * cosf + pltpu.roll(x, x.shape[1] // 2, 1) * sinf


def _silu(x):
    return x * jax.nn.sigmoid(x)


def _gla_finish(o, gate, gn):
    o = o * lax.rsqrt(jnp.mean(o * o, axis=-1, keepdims=True) + EPS) * gn
    return (o * _silu(gate)).astype(BF16)


def _ret_finish(o, gate, gn):
    oc = o - jnp.mean(o, axis=-1, keepdims=True)
    oc = oc * lax.rsqrt(jnp.mean(oc * oc, axis=-1, keepdims=True) + EPS) * gn
    return (oc * _silu(gate)).astype(BF16)


def _head(ref, h, width):
    return ref[:, h * width:(h + 1) * width]


def _attn_prompt_kernel(gq_ref, gk_ref, gv_ref, gg_ref, gl_ref, ggn_ref, gs0_ref,
                        rq_ref, rk_ref, rv_ref, rg_ref, cos_ref, sin_ref, lg_ref, rgn_ref, rs0_ref,
                        go_ref, gs_ref, ro_ref, rs_ref, d_scr):
    @pl.when(pl.program_id(1) == 0)
    def _():
        gs_ref[...] = gs0_ref[...]
        rs_ref[...] = rs0_ref[...]
        for h in range(d_scr.shape[0]):
            d_scr[h] = _decay_matrix(d_scr.shape[1], d_scr.shape[2], lg_ref[h][:, 0:1])

    cosf, sinf = cos_ref[...], sin_ref[...]
    for h in range(gs_ref.shape[0]):
        vcols = slice(h * HEAD_DV, (h + 1) * HEAD_DV)
        q = _head(gq_ref, h, HEAD_DK) * (HEAD_DK ** -0.5)
        o, s_new = _gla_core(q, _head(gk_ref, h, HEAD_DK), _head(gv_ref, h, HEAD_DV),
                             _head(gl_ref, h, HEAD_DK), gs_ref[h], GLA_SUB)
        gs_ref[h] = s_new
        go_ref[:, vcols] = _gla_finish(o, _head(gg_ref, h, HEAD_DV), ggn_ref[h])
        q = _rope(_head(rq_ref, h, HEAD_DK), cosf, sinf)
        k = _rope(_head(rk_ref, h, HEAD_DK), cosf, sinf) * (HEAD_DK ** -0.5)
        o, s_new = _ret_core(q, k, _head(rv_ref, h, HEAD_DV), rs_ref[h], lg_ref[h][:, 0:1], d_scr[h],
                             q.shape[0])
        rs_ref[h] = s_new
        ro_ref[:, vcols] = _ret_finish(o, _head(rg_ref, h, HEAD_DV), rgn_ref[h])


def _attn_prompt(proj, glog, gla_norm, ret_norm, cosf, sinf, ret_lg, s0_gla, s0_ret):
    bsz, t, _ = proj.shape
    c = ATT_CHUNK
    nh = GLA_HEADS
    kw, vw = nh * HEAD_DK, nh * HEAD_DV
    kspec = lambda blk: pl.BlockSpec((None, c, kw), lambda b, i, blk=blk: (b, i, blk))
    vspec = lambda blk: pl.BlockSpec((None, c, vw), lambda b, i, blk=blk: (b, i, blk))
    hspec = pl.BlockSpec((nh, 1, HEAD_DV), lambda b, i: (0, 0, 0))
    sspec = pl.BlockSpec((None, nh, HEAD_DK, HEAD_DV), lambda b, i: (b, 0, 0, 0))
    ospec = pl.BlockSpec((None, c, vw), lambda b, i: (b, i, 0))
    tspec = pl.BlockSpec((c, HEAD_DK), lambda b, i: (i, 0))
    o_shape = jax.ShapeDtypeStruct((bsz, t, vw), BF16)
    s_shape = jax.ShapeDtypeStruct((bsz, nh, HEAD_DK, HEAD_DV), F32)
    mg, s_gla, mr, s_ret = pl.pallas_call(
        _attn_prompt_kernel,
        grid=(bsz, t // c),
        in_specs=[kspec(0), kspec(1), vspec(1), vspec(2), kspec(0), hspec, sspec,
                  kspec(6), kspec(7), vspec(4), vspec(5), tspec, tspec,
                  pl.BlockSpec((nh, 1, LANES), lambda b, i: (0, 0, 0)), hspec, sspec],
        out_specs=[ospec, sspec, ospec, sspec],
        out_shape=[o_shape, s_shape, o_shape, s_shape],
        scratch_shapes=[pltpu.VMEM((nh, c, c), F32)],
        compiler_params=_cparams(("parallel", "arbitrary"), 32),
        name="attn_prompt",
    )(proj, proj, proj, proj, glog, gla_norm, s0_gla,
      proj, proj, proj, proj, cosf, sinf, ret_lg, ret_norm, s0_ret)
    return mg, mr, s_gla, s_ret


def _seq_rows(ref):
    ts, bb, w = ref.shape
    return ref[...].reshape(ts * bb, w)


def _seq_masks(n, bb):
    r = lax.broadcasted_iota(jnp.int32, (n, n), 0)
    c = lax.broadcasted_iota(jnp.int32, (n, n), 1)
    return (r % bb == c % bb) & (r >= c), (r - c).astype(F32) * (1.0 / bb)


def _seq_state_terms(q_in, k_out, v, s0, bb):
    n, dk = q_in.shape
    rown = lax.broadcasted_iota(jnp.int32, (n, 1), 0) % bb
    q_bd = jnp.concatenate([jnp.where(rown == j, q_in, 0.0) for j in range(bb)], axis=1)
    o_inter = _dot(q_bd.astype(BF16), s0.astype(BF16))
    k_t = jnp.transpose(_pad_rows(k_out, LANES))
    coln = lax.broadcasted_iota(jnp.int32, (1, LANES), 1) % bb
    k_bd = jnp.concatenate([jnp.where(coln == j, k_t, 0.0) for j in range(bb)], axis=0)
    ds = _dot(k_bd.astype(BF16), _pad_rows(v, LANES).astype(BF16))
    return o_inter, ds


def _gla_sample_kernel(q_ref, k_ref, v_ref, gg_ref, gl_ref, gn_ref, s0_ref, o_ref, s_ref):
    ts, bb, dk = q_ref.shape
    dv = v_ref.shape[2]
    n = ts * bb
    q = _seq_rows(q_ref) * (dk ** -0.5)
    k, v, g = _seq_rows(k_ref), _seq_rows(v_ref), _seq_rows(gl_ref)
    steps = [g[0:bb]]
    for t in range(1, ts):
        steps.append(steps[-1] + g[t * bb:(t + 1) * bb])
    b = jnp.concatenate(steps, axis=0)
    b_last = steps[-1]
    q_in = q * jnp.exp(b)
    mask, _ = _seq_masks(n, bb)
    att = jnp.where(mask, _dot_nt(q_in.astype(BF16), (k * jnp.exp(-b)).astype(BF16)), 0.0)
    k_out = k * jnp.exp(jnp.concatenate([b_last] * ts, axis=0) - b)
    s0 = s0_ref[...].reshape(bb * dk, dv)
    o_inter, ds = _seq_state_terms(q_in, k_out, v, s0, bb)
    o = _dot(att.astype(BF16), v.astype(BF16)) + o_inter
    e_last = jnp.exp(b_last)
    dec = jnp.concatenate([_col_bcast(e_last[j:j + 1, :], dv) for j in range(bb)], axis=0)
    s_ref[...] = (s0 * dec + ds).reshape(bb, dk, dv)
    o_ref[...] = _gla_finish(o, _seq_rows(gg_ref), gn_ref[...]).reshape(ts, bb, dv)


def _ret_sample_kernel(q_ref, k_ref, v_ref, rg_ref, cos_ref, sin_ref, lg_ref, gn_ref, s0_ref,
                       o_ref, s_ref):
    ts, bb, dk = q_ref.shape
    dv = v_ref.shape[2]
    n = ts * bb
    lg = lg_ref[:, 0:1]
    rows = lambda tab: jnp.concatenate(
        [jnp.broadcast_to(tab[t:t + 1, :], (bb, dk)) for t in range(ts)], axis=0)
    cosf, sinf = rows(cos_ref[...]), rows(sin_ref[...])
    q = _rope(_seq_rows(q_ref), cosf, sinf)
    k = _rope(_seq_rows(k_ref), cosf, sinf) * (dk ** -0.5)
    v = _seq_rows(v_ref)
    mask, dt = _seq_masks(n, bb)
    att = _dot_nt(q.astype(BF16), k.astype(BF16)) * jnp.where(mask, jnp.exp(dt * lg), 0.0)
    tt = (lax.broadcasted_iota(jnp.int32, (n, 1), 0) // bb).astype(F32)
    q_in = q * jnp.exp((tt + 1.0) * lg)
    k_out = k * jnp.exp((float(ts - 1) - tt) * lg)
    s0 = s0_ref[...].reshape(bb * dk, dv)
    o_inter, ds = _seq_state_terms(q_in, k_out, v, s0, bb)
    o = _dot(att.astype(BF16), v.astype(BF16)) + o_inter
    s_ref[...] = (s0 * jnp.exp(float(ts) * lg) + ds).reshape(bb, dk, dv)
    o_ref[...] = _ret_finish(o, _seq_rows(rg_ref), gn_ref[...]).reshape(ts, bb, dv)


def _attn_sample(proj, glog, gla_norm, ret_norm, cosf, sinf, ret_lg, s0_gla, s0_ret, bb=16):
    ts, bsz, _ = proj.shape
    grid = (bsz // bb, GLA_HEADS)
    nk = GLA_HEADS
    kspec = lambda off: pl.BlockSpec((ts, bb, HEAD_DK), lambda i, h, off=off: (0, i, off + h))
    vspec = lambda off: pl.BlockSpec((ts, bb, HEAD_DV), lambda i, h, off=off: (0, i, off + h))
    hspec = pl.BlockSpec((None, 1, HEAD_DV), lambda i, h: (h, 0, 0))
    sspec = pl.BlockSpec((bb, None, HEAD_DK, HEAD_DV), lambda i, h: (i, h, 0, 0))
    ospec = pl.BlockSpec((ts, bb, HEAD_DV), lambda i, h: (0, i, h))
    out_shape = [
        jax.ShapeDtypeStruct((ts, bsz, GLA_HEADS * HEAD_DV), BF16),
        jax.ShapeDtypeStruct((bsz, GLA_HEADS, HEAD_DK, HEAD_DV), F32),
    ]
    params = _cparams(("parallel", "parallel"), 40)
    mg, s_gla = pl.pallas_call(
        _gla_sample_kernel,
        grid=grid,
        in_specs=[kspec(0), kspec(nk), vspec(nk), vspec(2 * nk),
                  pl.BlockSpec((ts, bb, HEAD_DK), lambda i, h: (0, i, h)),
                  hspec, sspec],
        out_specs=[ospec, sspec],
        out_shape=out_shape,
        compiler_params=params,
        name="gla_sample",
    )(proj, proj, proj, proj, glog, gla_norm, s0_gla)
    tspec = pl.BlockSpec((ts, HEAD_DK), lambda i, h: (0, 0))
    mr, s_ret = pl.pallas_call(
        _ret_sample_kernel,
        grid=grid,
        in_specs=[kspec(6 * nk), kspec(7 * nk), vspec(4 * nk), vspec(5 * nk),
                  tspec, tspec,
                  pl.BlockSpec((None, 1, LANES), lambda i, h: (h, 0, 0)),
                  hspec, sspec],
        out_specs=[ospec, sspec],
        out_shape=out_shape,
        compiler_params=params,
        name="ret_sample",
    )(proj, proj, proj, proj, cosf, sinf, ret_lg, ret_norm, s0_ret)
    return mg, mr, s_gla, s_ret


def _residual_out(x_ref, y, g_ref, mod_ref, outs, d):
    x_new = _gated_residual(x_ref[...], y, g_ref[...], mod_ref, d)
    if len(outs) == 1:
        outs[0][...] = x_new
    else:
        modn_ref, gn_ref, o_ref, h_ref = outs
        o_ref[...] = x_new
        h_ref[...] = _norm_mod(x_new, gn_ref[...], modn_ref, d).astype(BF16)


def _next_specs(nxt, tm, d):
    if nxt is None:
        return [], [], [], []
    mod_next, g_next, shape = nxt
    return ([_mod_spec(mod_next.shape[1], tm, 3 * d, 2), pl.BlockSpec((1, d), lambda g, i: (0, 0))],
            [pl.BlockSpec((None, tm, d), lambda g, i: (g, i, 0))],
            [jax.ShapeDtypeStruct(shape, BF16)], [mod_next, g_next])


def _outproj_kernel(x_ref, mod_ref, g_ref, mg_ref, mr_ref, wo_ref, *outs, d):
    half = mg_ref.shape[1]
    y = _dot(mg_ref[...], wo_ref[0:half, :]) + _dot(mr_ref[...], wo_ref[half:2 * half, :])
    _residual_out(x_ref, y, g_ref, mod_ref, outs, d)


def _outproj(x3, mod3, g_post, mg, mr, w_out, tm, nxt=None):
    gn, t, d = x3.shape
    r = mod3.shape[1]
    half = mg.shape[2]
    n_in, n_out, n_shape, n_ops = _next_specs(None if nxt is None else (*nxt, x3.shape), tm, d)
    res = pl.pallas_call(
        functools.partial(_outproj_kernel, d=d),
        grid=(gn, t // tm),
        in_specs=[
            pl.BlockSpec((None, tm, d), lambda g, i: (g, i, 0)),
            _mod_spec(r, tm, 3 * d, 2),
            pl.BlockSpec((1, d), lambda g, i: (0, 0)),
            pl.BlockSpec((None, tm, half), lambda g, i: (g, i, 0)),
            pl.BlockSpec((None, tm, half), lambda g, i: (g, i, 0)),
            pl.BlockSpec((2 * half, d), lambda g, i: (0, 0)),
        ] + n_in,
        out_specs=[pl.BlockSpec((None, tm, d), lambda g, i: (g, i, 0))] + n_out,
        out_shape=[jax.ShapeDtypeStruct((gn, t, d), F32)] + n_shape,
        compiler_params=_cparams(("parallel", "parallel"), 56),
        name="outproj",
    )(x3, mod3, g_post, mg, mr, w_out, *n_ops)
    return res if nxt is not None else (res[0], None)


def _mlp_kernel(x_ref, mod_ref, gpre_ref, gpost_ref, wup_ref, wdn_ref, o_ref, h_scr, acc_scr, *, d):
    j = pl.program_id(2)

    @pl.when(j == 0)
    def _():
        h_scr[...] = _norm_mod(x_ref[...], gpre_ref[...], mod_ref, d).astype(BF16)
        acc_scr[...] = jnp.zeros_like(acc_scr)

    u = jnp.maximum(_dot(h_scr[...], wup_ref[...]), 0.0)
    acc_scr[...] += _dot((u * u).astype(BF16), wdn_ref[...])

    @pl.when(j == pl.num_programs(2) - 1)
    def _():
        o_ref[...] = _gated_residual(x_ref[...], acc_scr[...], gpost_ref[...], mod_ref, d)


def _mlp_cast_kernel(x_ref, mod_ref, gpre_ref, gpost_ref, wup_ref, wdn_ref,
                     o_ref, wupb_ref, wdnb_ref, h_scr, acc_scr, *, d):
    wupb_ref[...] = wup_ref[...].astype(BF16)
    wdnb_ref[...] = wdn_ref[...].astype(BF16)
    _mlp_kernel(x_ref, mod_ref, gpre_ref, gpost_ref, wupb_ref, wdnb_ref, o_ref, h_scr, acc_scr, d=d)


def _mlp_cast(x3, mod3, g_pre, g_post, w_up_all, w_down_all, layer, tf=512):
    gn, t, d = x3.shape
    assert gn == 1
    r = mod3.shape[1]
    f = w_up_all.shape[2]
    return pl.pallas_call(
        functools.partial(_mlp_cast_kernel, d=d),
        grid=(1, 1, f // tf),
        in_specs=[
            pl.BlockSpec((None, t, d), lambda g, i, j: (0, 0, 0)),
            _mod_spec(r, t, 3 * d, 3),
            pl.BlockSpec((1, d), lambda g, i, j: (0, 0)),
            pl.BlockSpec((1, d), lambda g, i, j: (0, 0)),
            pl.BlockSpec((None, d, tf), lambda g, i, j: (layer, 0, j)),
            pl.BlockSpec((None, tf, d), lambda g, i, j: (layer, j, 0)),
        ],
        out_specs=[
            pl.BlockSpec((None, t, d), lambda g, i, j: (0, 0, 0)),
            pl.BlockSpec((d, tf), lambda g, i, j: (0, j)),
            pl.BlockSpec((tf, d), lambda g, i, j: (j, 0)),
        ],
        out_shape=[
            jax.ShapeDtypeStruct((1, t, d), F32),
            jax.ShapeDtypeStruct((d, f), BF16),
            jax.ShapeDtypeStruct((f, d), BF16),
        ],
        scratch_shapes=[pltpu.VMEM((t, d), BF16), pltpu.VMEM((t, d), F32)],
        compiler_params=_cparams(("arbitrary", "arbitrary", "arbitrary"), 56),
        name="mlp_cast",
    )(x3, mod3, g_pre, g_post, w_up_all, w_down_all)


def _mlp(x3, mod3, g_pre, g_post, w_up, w_down, tf=TF_MLP):
    gn, t, d = x3.shape
    assert gn == 1
    r = mod3.shape[1]
    f = w_up.shape[1]
    return pl.pallas_call(
        functools.partial(_mlp_kernel, d=d),
        grid=(1, 1, f // tf),
        in_specs=[
            pl.BlockSpec((None, t, d), lambda g, i, j: (0, 0, 0)),
            _mod_spec(r, t, 3 * d, 3),
            pl.BlockSpec((1, d), lambda g, i, j: (0, 0)),
            pl.BlockSpec((1, d), lambda g, i, j: (0, 0)),
            pl.BlockSpec((d, tf), lambda g, i, j: (0, j)),
            pl.BlockSpec((tf, d), lambda g, i, j: (j, 0)),
        ],
        out_specs=pl.BlockSpec((None, t, d), lambda g, i, j: (0, 0, 0)),
        out_shape=jax.ShapeDtypeStruct((1, t, d), F32),
        scratch_shapes=[pltpu.VMEM((t, d), BF16), pltpu.VMEM((t, d), F32)],
        compiler_params=_cparams(("arbitrary", "arbitrary", "arbitrary"), 56),
        name="mlp_rows",
    )(x3, mod3, g_pre, g_post, w_up, w_down)


def _mlp_h_kernel(x_ref, h_ref, mod_ref, gpost_ref, wup_ref, wdn_ref, o_ref, acc_scr, *, d):
    j = pl.program_id(2)

    @pl.when(j == 0)
    def _():
        acc_scr[...] = jnp.zeros_like(acc_scr)

    u = jnp.maximum(_dot(h_ref[...], wup_ref[...]), 0.0)
    acc_scr[...] += _dot((u * u).astype(BF16), wdn_ref[...])

    @pl.when(j == pl.num_programs(2) - 1)
    def _():
        o_ref[...] = _gated_residual(x_ref[...], acc_scr[...], gpost_ref[...], mod_ref, d)


def _mlp_h(x3, h3, mod3, g_post, w_up, w_down, tm, tf=TF_MLP):
    gn, t, d = x3.shape
    r = mod3.shape[1]
    f = w_up.shape[1]
    return pl.pallas_call(
        functools.partial(_mlp_h_kernel, d=d),
        grid=(gn, t // tm, f // tf),
        in_specs=[
            pl.BlockSpec((None, tm, d), lambda g, i, j: (g, i, 0)),
            pl.BlockSpec((None, tm, d), lambda g, i, j: (g, i, 0)),
            _mod_spec(r, tm, 3 * d, 3),
            pl.BlockSpec((1, d), lambda g, i, j: (0, 0)),
            pl.BlockSpec((d, tf), lambda g, i, j: (0, j)),
            pl.BlockSpec((tf, d), lambda g, i, j: (j, 0)),
        ],
        out_specs=pl.BlockSpec((None, tm, d), lambda g, i, j: (g, i, 0)),
        out_shape=jax.ShapeDtypeStruct((gn, t, d), F32),
        scratch_shapes=[pltpu.VMEM((tm, d), F32)],
        compiler_params=_cparams(("parallel", "parallel", "arbitrary"), 56),
        name="mlp",
    )(x3, h3, mod3, g_post, w_up, w_down)


def _diag_blocks(rows, reps, row_shift, col_shift):
    tiled = jnp.concatenate([rows] * reps, axis=1)
    rg = lax.broadcasted_iota(jnp.int32, tiled.shape, 0) >> row_shift
    cg = lax.broadcasted_iota(jnp.int32, tiled.shape, 1) >> col_shift
    return jnp.where(rg == cg, tiled, 0.0)


def _s5_disc_kernel(lr_ref, li_ref, ldt_ref, br_ref, bi_ref, ctr_ref, cti_ref,
                    pwr_ref, pwi_ref, bblk_ref, cblk_ref, *, seg_len):
    lr, li = lr_ref[...], li_ref[...]
    dt = jnp.exp(ldt_ref[...])
    mag = jnp.exp(lr * dt)
    lb_re, lb_im = mag * jnp.cos(li * dt), mag * jnp.sin(li * dt)
    nr, ni = lb_re - 1.0, lb_im
    den = lr * lr + li * li
    f_re = (nr * lr + ni * li) / den
    f_im = (ni * lr - nr * li) / den
    br, bi = br_ref[...], bi_ref[...]
    bb_re = f_re * br - f_im * bi
    bb_im = f_re * bi + f_im * br
    ncb, nrow, ncol2 = bblk_ref.shape
    gpb = nrow // S5_GROUP
    rs, cs = S5_GROUP.bit_length() - 1, S5_STATE.bit_length() - 1
    for cb in range(ncb):
        grp = slice(cb * gpb, (cb + 1) * gpb)
        b_parts = [_diag_blocks(a[grp].reshape(nrow, LANES), ncol2 // 2 // LANES, rs, cs)
                   for a in (bb_re, bb_im)]
        bblk_ref[cb] = jnp.concatenate(b_parts, axis=1).astype(BF16)
        c_parts = [_diag_blocks(r[grp].reshape(ncol2 // 2, LANES), nrow // LANES, cs, rs)
                   for r in (ctr_ref, cti_ref)]
        cblk_ref[cb] = jnp.concatenate([c_parts[0], -c_parts[1]], axis=0).astype(BF16)
    pwr_ref[0] = lb_re
    pwi_ref[0] = lb_im
    qr, qi = None, None
    sr, si = lb_re, lb_im
    e = seg_len
    while e:
        if e & 1:
            qr, qi = (sr, si) if qr is None else (qr * sr - qi * si, qr * si + qi * sr)
        e >>= 1
        if e:
            sr, si = sr * sr - si * si, 2.0 * sr * si
    pr, pi = qr, qi
    for n in range(SUBLANES):
        pwr_ref[1 + n] = pr
        pwi_ref[1 + n] = pi
        pr, pi = pr * qr - pi * qi, pr * qi + pi * qr


def _s5_discretize(lam_re, lam_im, log_dt, bt_re, bt_im, ct_re, ct_im, seg_len):
    g = lam_re.shape[0]
    ncb = g // S5_GPB
    assert S5_GROUP & (S5_GROUP - 1) == 0 and S5_STATE & (S5_STATE - 1) == 0
    return pl.pallas_call(
        functools.partial(_s5_disc_kernel, seg_len=seg_len),
        out_shape=[
            jax.ShapeDtypeStruct((1 + SUBLANES, g, 1, LANES), F32),
            jax.ShapeDtypeStruct((1 + SUBLANES, g, 1, LANES), F32),
            jax.ShapeDtypeStruct((ncb, S5_UW, 2 * S5_GPB * S5_STATE), BF16),
            jax.ShapeDtypeStruct((ncb, 2 * S5_GPB * S5_STATE, S5_UW), BF16),
        ],
        compiler_params=pltpu.CompilerParams(vmem_limit_bytes=48 * MIB),
        name="s5_discretize",
    )(lam_re, lam_im, log_dt, bt_re, bt_im, ct_re, ct_im)


def _gelu_tanh(x):
    c0 = math.sqrt(2.0 / math.pi)
    return x * (0.5 * (1.0 + jnp.tanh(c0 * (x + 0.044715 * (x * x * x)))))


def _cmul_add(ar, ai, xr, xi, yr, yi):
    return yr + ar * xr - ai * xi, yi + ar * xi + ai * xr


def _s5_seq_kernel(x_ref, mod_ref, gpre_ref, bblk_ref, cblk_ref, lam_ref, dskip_ref, s0r_ref, s0i_ref,
                   z_ref, sr_ref, si_ref, h_scr, xr_scr, xi_scr, *, d):
    cb = pl.program_id(0)
    ncb, tm, uw = h_scr.shape
    cw = xr_scr.shape[1]
    seg = s0r_ref.shape[0]

    @pl.when(cb == 0)
    def _():
        h = _norm_mod(x_ref[...], gpre_ref[...], mod_ref, d)
        for c in range(ncb):
            h_scr[c] = h[:, c * uw:(c + 1) * uw]

    u = h_scr[cb]
    bu = _dot(u.astype(BF16), bblk_ref[...])
    xr_scr[...] = bu[:, 0:cw]
    xi_scr[...] = bu[:, cw:2 * cw]
    car_r, car_i = s0r_ref[...], s0i_ref[...]
    l_r, l_i = lam_ref[0:1, :], lam_ref[1:2, :]
    for t in range(tm // seg):
        rows = slice(t * seg, (t + 1) * seg)
        car_r, car_i = _cmul_add(l_r, l_i, car_r, car_i, xr_scr[rows, :], xi_scr[rows, :])
        xr_scr[rows, :] = car_r
        xi_scr[rows, :] = car_i
    sr_ref[...] = car_r
    si_ref[...] = car_i
    xs = jnp.concatenate([xr_scr[...].astype(BF16), xi_scr[...].astype(BF16)], axis=1)
    y = _dot(xs, cblk_ref[...]) + dskip_ref[...] * u
    z_ref[...] = _gelu_tanh(y).astype(BF16)


def _s5_seq(x2, mod2, g_pre, bblk, cblk, lam2, dskip, s0_re, s0_im):
    tm, d = x2.shape
    seg, nst = s0_re.shape
    ncb, uw, cw2 = bblk.shape
    cw = cw2 // 2
    sspec = pl.BlockSpec((seg, cw), lambda c: (0, c))
    return pl.pallas_call(
        functools.partial(_s5_seq_kernel, d=d),
        grid=(ncb,),
        in_specs=[
            pl.BlockSpec((tm, d), lambda c: (0, 0)),
            pl.BlockSpec((seg, 3 * d), lambda c: (0, 0)),
            pl.BlockSpec((1, d), lambda c: (0, 0)),
            pl.BlockSpec((None, uw, cw2), lambda c: (c, 0, 0)),
            pl.BlockSpec((None, cw2, uw), lambda c: (c, 0, 0)),
            pl.BlockSpec((2, cw), lambda c: (0, c)),
            pl.BlockSpec((1, uw), lambda c: (0, c)),
            sspec, sspec,
        ],
        out_specs=[pl.BlockSpec((tm, uw), lambda c: (0, c)), sspec, sspec],
        out_shape=[
            jax.ShapeDtypeStruct((tm, d), BF16),
            jax.ShapeDtypeStruct((seg, nst), F32),
            jax.ShapeDtypeStruct((seg, nst), F32),
        ],
        scratch_shapes=[
            pltpu.VMEM((ncb, tm, uw), F32),
            pltpu.VMEM((tm, cw), F32),
            pltpu.VMEM((tm, cw), F32),
        ],
        compiler_params=_cparams(("arbitrary",), 48),
        name="s5_seq",
    )(x2, mod2, g_pre, bblk, cblk, lam2, dskip, s0_re, s0_im)


def _s5_rows_kernel(*refs, d, ncast):
    x_ref, mod_ref, gpre_ref, bblk_ref, cblk_ref, tbl_ref, dskip_ref = refs[:7]
    cast_in = refs[7:7 + ncast]
    z_ref, sr_ref, si_ref = refs[7 + ncast:10 + ncast]
    cast_out = refs[10 + ncast:10 + 2 * ncast]
    h_scr, xr_scr, xi_scr, cr_scr, ci_scr = refs[10 + 2 * ncast:]
    for src, dst in zip(cast_in, cast_out):
        dst[...] = src[...].astype(BF16)

    ncol, tm, _ = xr_scr.shape
    ncb, uw, _ = bblk_ref.shape
    sl = tm // SUBLANES

    @pl.when(pl.program_id(1) == 0)
    def _():
        cr_scr[...] = jnp.zeros_like(cr_scr)
        ci_scr[...] = jnp.zeros_like(ci_scr)

    h_scr[...] = _norm_mod(x_ref[...], gpre_ref[...], mod_ref, d)
    row0 = lax.broadcasted_iota(jnp.int32, (ncol, SUBLANES, LANES), 1) == 0

    for c in range(ncb):
        us = slice(c * uw, (c + 1) * uw)
        cols = slice(c * ncol, (c + 1) * ncol)
        u = h_scr[:, us]
        bu = _dot(u.astype(BF16), bblk_ref[c])
        for j in range(ncol):
            xr_scr[j] = bu[:, j * LANES:(j + 1) * LANES]
            xi_scr[j] = bu[:, (ncol + j) * LANES:(ncol + j + 1) * LANES]
        l_r, l_i = tbl_ref[0, cols], tbl_ref[1, cols]

        def local(i, s):
            rows = pl.ds(pl.multiple_of(i * SUBLANES, SUBLANES), SUBLANES)
            return _cmul_add(l_r, l_i, s[0], s[1], xr_scr[:, rows, :], xi_scr[:, rows, :])

        zero = jnp.zeros((ncol, SUBLANES, LANES), F32)
        g_r, g_i = lax.fori_loop(0, sl, local, (zero, zero), unroll=True)
        for n in range(3):
            g_r, g_i = _cmul_add(tbl_ref[2 + 2 * n, cols], tbl_ref[3 + 2 * n, cols],
                                 pltpu.roll(g_r, 1 << n, 1), pltpu.roll(g_i, 1 << n, 1), g_r, g_i)
        car_r, car_i = cr_scr[cols], ci_scr[cols]
        g_r, g_i = _cmul_add(tbl_ref[8, cols], tbl_ref[9, cols], car_r, car_i, g_r, g_i)
        in_r = jnp.where(row0, car_r, pltpu.roll(g_r, 1, 1))
        in_i = jnp.where(row0, car_i, pltpu.roll(g_i, 1, 1))

        def full(i, s):
            rows = pl.ds(pl.multiple_of(i * SUBLANES, SUBLANES), SUBLANES)
            s_r, s_i = _cmul_add(l_r, l_i, s[0], s[1], xr_scr[:, rows, :], xi_scr[:, rows, :])
            xr_scr[:, rows, :] = s_r
            xi_scr[:, rows, :] = s_i
            return s_r, s_i

        e_r, e_i = lax.fori_loop(0, sl, full, (in_r, in_i), unroll=True)
        cr_scr[cols] = e_r[:, SUBLANES - 1:SUBLANES, :]
        ci_scr[cols] = e_i[:, SUBLANES - 1:SUBLANES, :]
        xs = jnp.concatenate([xr_scr[j].astype(BF16) for j in range(ncol)]
                             + [xi_scr[j].astype(BF16) for j in range(ncol)], axis=1)
        y = _dot(xs, cblk_ref[c]) + dskip_ref[:, us] * u
        z_ref[:, us] = _gelu_tanh(y).astype(BF16)

    sr_ref[...] = cr_scr[...]
    si_ref[...] = ci_scr[...]


def _s5_rows(x3, mod3, g_pre, bblk, cblk, tbl, dskip, tm, casts=()):
    gn, t, d = x3.shape
    ncb, uw, cw2 = bblk.shape
    ncol = cw2 // 2 // LANES
    nct = ncb * ncol
    nt = t // tm
    nstep = gn * nt
    const = lambda shape: pl.BlockSpec(shape, lambda g, i: (0,) * len(shape), pipeline_mode=pl.Buffered(1))
    ospec = pl.BlockSpec((None, None, nct, 1, LANES), lambda g, i: (g, i, 0, 0, 0))
    c_in, c_out, c_shape = [], [], []
    for w, layer in casts:
        _, rows, cols = w.shape
        slab = rows // nstep
        assert slab * nstep == rows and slab % (2 * SUBLANES) == 0
        c_in.append(pl.BlockSpec((None, slab, cols), lambda g, i, layer=layer: (layer, g * nt + i, 0)))
        c_out.append(pl.BlockSpec((slab, cols), lambda g, i: (g * nt + i, 0)))
        c_shape.append(jax.ShapeDtypeStruct((rows, cols), BF16))
    res = pl.pallas_call(
        functools.partial(_s5_rows_kernel, d=d, ncast=len(casts)),
        grid=(gn, nt),
        in_specs=[
            pl.BlockSpec((None, tm, d), lambda g, i: (g, i, 0)),
            _mod_spec(1, tm, 3 * d, 2),
            pl.BlockSpec((1, d), lambda g, i: (0, 0)),
            const(bblk.shape), const(cblk.shape), const(tbl.shape), const(dskip.shape),
        ] + c_in,
        out_specs=[pl.BlockSpec((None, tm, d), lambda g, i: (g, i, 0)), ospec, ospec] + c_out,
        out_shape=[
            jax.ShapeDtypeStruct((gn, t, d), BF16),
            jax.ShapeDtypeStruct((gn, nt, nct, 1, LANES), F32),
            jax.ShapeDtypeStruct((gn, nt, nct, 1, LANES), F32),
        ] + c_shape,
        scratch_shapes=[
            pltpu.VMEM((tm, d), F32),
            pltpu.VMEM((ncol, tm, LANES), F32),
            pltpu.VMEM((ncol, tm, LANES), F32),
            pltpu.VMEM((nct, 1, LANES), F32),
            pltpu.VMEM((nct, 1, LANES), F32),
        ],
        compiler_params=_cparams(("arbitrary", "arbitrary"), 56),
        name="s5_rows",
    )(x3, mod3, g_pre, bblk, cblk, tbl, dskip, *[w for w, _ in casts])
    z3, s_re, s_im = res[:3]
    last = lambda s: s[:, nt - 1].reshape(gn, 1, nct * LANES)
    return z3, last(s_re), last(s_im), list(res[3:])


def _glu_kernel(x_ref, mod_ref, g_ref, z_ref, wa_ref, wb_ref, *outs, d):
    z = z_ref[...]
    y = _dot(z, wa_ref[...]) * jax.nn.sigmoid(_dot(z, wb_ref[...]))
    _residual_out(x_ref, y, g_ref, mod_ref, outs, d)


def _glu(x3, mod3, g_post, z3, w_a, w_b, tm, nxt=None):
    gn, t, d = x3.shape
    r = mod3.shape[1]
    wspec = pl.BlockSpec((d, d), lambda g, i: (0, 0), pipeline_mode=pl.Buffered(1))
    n_in, n_out, n_shape, n_ops = _next_specs(None if nxt is None else (*nxt, x3.shape), tm, d)
    res = pl.pallas_call(
        functools.partial(_glu_kernel, d=d),
        grid=(gn, t // tm),
        in_specs=[
            pl.BlockSpec((None, tm, d), lambda g, i: (g, i, 0)),
            _mod_spec(r, tm, 3 * d, 2),
            pl.BlockSpec((1, d), lambda g, i: (0, 0)),
            pl.BlockSpec((None, tm, d), lambda g, i: (g, i, 0)),
            wspec, wspec,
        ] + n_in,
        out_specs=[pl.BlockSpec((None, tm, d), lambda g, i: (g, i, 0))] + n_out,
        out_shape=[jax.ShapeDtypeStruct((gn, t, d), F32)] + n_shape,
        compiler_params=_cparams(("parallel", "parallel"), 56),
        name="glu",
    )(x3, mod3, g_post, z3, w_a, w_b, *n_ops)
    return res if nxt is not None else (res[0], None)


def _rope_tables(pos):
    half = HEAD_DK // 2
    inv = ROPE_BASE ** (-jnp.arange(half, dtype=F32) / half)
    ang = pos.astype(F32)[:, None] * inv[None, :]
    cos, sin = jnp.cos(ang), jnp.sin(ang)
    return jnp.concatenate([cos, cos], axis=-1), jnp.concatenate([-sin, sin], axis=-1)


def _s5_tables(pw_re, pw_im):
    n = pw_re.shape[0]
    flat_r = pw_re.reshape(n, -1)
    flat_i = pw_im.reshape(n, -1)
    row = jnp.arange(SUBLANES)[:, None]
    tabs = [jnp.broadcast_to(flat_r[0], (SUBLANES, flat_r.shape[1])),
            jnp.broadcast_to(flat_i[0], (SUBLANES, flat_i.shape[1]))]
    for s in (1, 2, 4):
        mask = row >= s
        tabs.append(jnp.where(mask, flat_r[s][None, :], 0.0))
        tabs.append(jnp.where(mask, flat_i[s][None, :], 0.0))
    tabs += [flat_r[1:], flat_i[1:]]
    tbl = jnp.stack(tabs)
    tbl = tbl.reshape(tbl.shape[0], SUBLANES, -1, LANES).transpose(0, 2, 1, 3)
    return tbl, jnp.stack([flat_r[0], flat_i[0]])


def kernel(x_prompt, x_sample, state_gla, state_ret, state_s5_re, state_s5_im, c_prompt, c_sample,
           w_ada, b_ada, norm_pre, norm_post, w_in_mix, w_gla_gk, b_gla_gk, gla_head_norm,
           ret_head_norm, w_out_mix, s5_lam_re, s5_lam_im, s5_log_dt, s5_b_re, s5_b_im,
           s5_c_re, s5_c_im, s5_d, w_glu_a, w_glu_b, w_mlp_up, w_mlp_down):
    bp, tp, d = x_prompt.shape
    bs, ts, _ = x_sample.shape
    depth = w_ada.shape[0]

    nrow = -(-(bs + bp) // SUBLANES) * SUBLANES
    c_all = jnp.concatenate([c_sample, c_prompt, jnp.zeros((nrow - bs - bp, d), F32)], axis=0)
    mod_all = _adaln(c_all, w_ada.reshape(depth * 2, d, 3 * d), b_ada.reshape(depth * 2, 1, 3 * d))
    mod_s = [mod_all[k, 0:bs][None] for k in range(depth * 2)]
    mod_p = [mod_all[k, bs:bs + bp][:, None, :] for k in range(depth * 2)]

    w_in_t = jnp.swapaxes(w_in_mix, 1, 2)
    w_gk = jnp.pad(w_gla_gk[0], ((0, LANES - GLA_RANK), (0, 0))).astype(BF16)
    b_gk = b_gla_gk[0][None, :]
    w_out = w_out_mix[0].astype(BF16)
    gla_norm = gla_head_norm[0][:, None, :]
    ret_norm = ret_head_norm[0][:, None, :]
    gamma_log = jnp.log1p(-jnp.power(2.0, -5.0 - jnp.arange(RET_HEADS, dtype=F32)))
    ret_lg = jnp.broadcast_to(gamma_log[:, None, None], (RET_HEADS, 1, LANES))

    ng = s5_lam_re.shape[1]
    tm5 = min(TM_S5, tp)
    per_state = lambda a: jnp.tile(a, (1,) * (a.ndim - 1) + (LANES // S5_STATE,))
    bt = lambda a: per_state(jnp.swapaxes(a[0], 1, 2))
    ct = lambda a: jnp.tile(jnp.swapaxes(a[0], 1, 2), (1, 1, LANES // S5_GROUP))
    pw_re, pw_im, bblk, cblk = _s5_discretize(
        per_state(s5_lam_re[0])[:, None, :], per_state(s5_lam_im[0])[:, None, :],
        s5_log_dt[0][:, None, None], bt(s5_b_re), bt(s5_b_im), ct(s5_c_re), ct(s5_c_im),
        tm5 // SUBLANES)
    tbl, lam2 = _s5_tables(pw_re[:, :, 0, :S5_STATE], pw_im[:, :, 0, :S5_STATE])
    dskip = s5_d[0][None, :]

    nxt = lambda mods, l, emit_h: (mods[2 * l + 1], norm_pre[l, 1][None]) if emit_h else None

    def layer0(x3, mods, tm, inproj, attn, mlp, emit_h):
        proj, glog = inproj(x3, mods[0])
        mg, mr, s_gla, s_ret = attn(proj, glog)
        x3, h3 = _outproj(x3, mods[0], norm_post[0, 0][None], mg, mr, w_out, tm, nxt(mods, 0, emit_h))
        return mlp(0, x3, h3, mods[1]), s_gla, s_ret

    def layer1_tail(x3, z3, mods, tm, glu_w, mlp, emit_h):
        x3, h3 = _glu(x3, mods[2], norm_post[1, 0][None], z3, glu_w[0], glu_w[1], tm, nxt(mods, 1, emit_h))
        return mlp(1, x3, h3, mods[3])

    cos_s, sin_s = _rope_tables(PAST_LEN + jnp.arange(ts, dtype=F32))

    def attn_s(proj, glog):
        tm_rows = lambda a: a.reshape(ts, bs, a.shape[-1])
        mg, mr, s_gla, s_ret = _attn_sample(tm_rows(proj), tm_rows(glog), gla_norm, ret_norm, cos_s, sin_s,
                                            ret_lg, state_gla[0], state_ret[0])
        flat = lambda a: a.reshape(1, ts * bs, a.shape[-1])
        return flat(mg), flat(mr), s_gla, s_ret

    xs3 = jnp.swapaxes(x_sample, 0, 1).reshape(1, ts * bs, d)

    def s5_s(x3, mod3):
        z2, s_re, s_im = _s5_seq(x3[0], mod3[0], norm_pre[1, 0][None], bblk, cblk, lam2, dskip,
                                 state_s5_re[0].reshape(bs, -1), state_s5_im[0].reshape(bs, -1))
        return z2[None], s_re, s_im

    w_up, w_dn, w_main = {}, {}, {}

    def inproj_s(x3, mod3):
        proj, glog, w_main[0], w_main["lr"] = _inproj_cast(x3, mod3, norm_pre[0, 0][None], w_in_t, 0,
                                                            w_gk, b_gk)
        return proj, glog

    def mlp_s(l, x3, h3, mod3):
        g_pre, g_post = norm_pre[l, 1][None], norm_post[l, 1][None]
        if l in w_up:
            return _mlp(x3, mod3, g_pre, g_post, w_up[l], w_dn[l])
        x3, w_up[l], w_dn[l] = _mlp_cast(x3, mod3, g_pre, g_post, w_mlp_up, w_mlp_down, l)
        return x3

    tm_s = ts * bs
    xs1, gla_s, ret_s = layer0(xs3, mod_s, tm_s, inproj_s, attn_s, mlp_s, False)

    cos_p, sin_p = _rope_tables(jnp.arange(tp, dtype=F32))
    zeros_att = jnp.zeros((bp, GLA_HEADS, HEAD_DK, HEAD_DV), F32)
    tm_p = min(TM_DENSE, tp)

    def attn_p(proj, glog):
        return _attn_prompt(proj, glog, gla_norm, ret_norm, cos_p, sin_p, ret_lg, zeros_att, zeros_att)

    def mlp_p(l, x3, h3, mod3):
        return _mlp_h(x3, h3, mod3, norm_post[l, 1][None], w_up[l], w_dn[l], tm_p)

    def inproj_p(x3, mod3):
        return _inproj(x3, mod3, norm_pre[0, 0][None], w_main[0], w_main["lr"], w_gk, b_gk, tm_p)

    xp1, gla_p, ret_p = layer0(x_prompt, mod_p, tm_p, inproj_p, attn_p, mlp_p, True)

    sl = tm5 // SUBLANES
    xpp = jnp.swapaxes(xp1.reshape(bp, tp // tm5, SUBLANES, sl, d), 2, 3).reshape(bp, tp, d)
    zpp, re_p, im_p, (w_up[1], w_dn[1], w_ga, w_gb) = _s5_rows(
        xpp, mod_p[2], norm_pre[1, 0][None], bblk, cblk, tbl, dskip, tm5,
        casts=((w_mlp_up, 1), (w_mlp_down, 1), (w_glu_a, 0), (w_glu_b, 0)))
    zp3 = jnp.swapaxes(zpp.reshape(bp, tp // tm5, sl, SUBLANES, d), 2, 3).reshape(bp, tp, d)

    zs3, re_s, im_s = s5_s(xs1, mod_s[2])
    y_s = layer1_tail(xs1, zs3, mod_s, tm_s, (w_ga, w_gb), mlp_s, False)
    y_s = jnp.swapaxes(y_s.reshape(ts, bs, d), 0, 1)
    y_p = layer1_tail(xp1, zp3, mod_p, tm_p, (w_ga, w_gb), mlp_p, True)

    st = lambda a, b_: a.reshape(1, b_, ng, S5_STATE)
    return (y_p, y_s, gla_p[None], gla_s[None], ret_p[None], ret_s[None],
            st(re_p, bp), st(re_s, bs), st(im_p, bp), st(im_s, bs))
```

```python
import functools
import math

import jax
import jax.numpy as jnp
import numpy as np
from jax import lax
from jax.experimental import pallas as pl
from jax.experimental.pallas import tpu as pltpu

F32 = jnp.float32
BF16 = jnp.bfloat16

EPS = 1e-6
LANES = 128
SUBLANES = 8
MIB = 1024 * 1024

GLA_HEADS = 4
RET_HEADS = 4
HEAD_DK = 128
HEAD_DV = 256
GLA_RANK = 16
GLA_LOGIT_NORM = 16.0
ROPE_BASE = 10000.0
PAST_LEN = 16384
S5_GROUP = 16
S5_STATE = 64
S5_GPB = 16
S5_UW = S5_GPB * S5_GROUP
ATT_CHUNK = 128
GLA_SUB = 16
TM_DENSE = 512
TM_S5 = 256
TN_INPROJ = 1024
TF_MLP = 1024


def _cparams(sem, vmem_mib):
    return pltpu.CompilerParams(dimension_semantics=sem, vmem_limit_bytes=vmem_mib * MIB)


def _dot(a, b):
    return jnp.dot(a, b, preferred_element_type=F32)


def _dot_nt(a, b):
    return lax.dot_general(a, b, (((1,), (1,)), ((), ())), preferred_element_type=F32)


def _rms(x, g):
    return x * lax.rsqrt(jnp.mean(x * x, axis=-1, keepdims=True) + EPS) * g


def _rows_affine(y, a, b=None):
    tm, d = y.shape
    r = a.shape[0]
    if r == 1 or r == tm:
        out = y * a
        return out if b is None else out + b
    y3 = y.reshape(tm // r, r, d)
    out = y3 * a[None]
    if b is not None:
        out = out + b[None]
    return out.reshape(tm, d)


def _norm_mod(x, g, mod_ref, d):
    return _rows_affine(_rms(x, g), 1.0 + mod_ref[:, d:2 * d], mod_ref[:, 0:d])


def _gated_residual(x, y, g, mod_ref, d):
    return x + _rows_affine(_rms(y, g), mod_ref[:, 2 * d:3 * d])


class _ModSlab:
    def __init__(self, arr, lead, rows):
        self.arr, self.lead = arr, lead
        self.shape = (1, rows, arr.shape[2])


def _arr(mod):
    return mod.arr if isinstance(mod, _ModSlab) else mod


def _mod_spec(mod, tm, width, ngrid):
    r = mod.shape[1]
    if isinstance(mod, _ModSlab):
        lead = mod.lead
        index = (lambda g, i: (lead, 0, 0)) if ngrid == 2 else (lambda g, i, j: (lead, 0, 0))
    else:
        index = (lambda g, i: (g, 0, 0)) if ngrid == 2 else (lambda g, i, j: (g, 0, 0))
    return pl.BlockSpec((None, r, width), index)


def _adaln_kernel(c_ref, w_ref, b_ref, o_ref):
    c = c_ref[...]
    sc = (c * jax.nn.sigmoid(c)).astype(BF16)
    o_ref[...] = _dot(sc, w_ref[...].astype(BF16)) + b_ref[...]


def _adaln(c_all, w_ada, b_ada, tn=1024):
    ls, d, n = w_ada.shape
    rows = c_all.shape[0]
    return pl.pallas_call(
        _adaln_kernel,
        grid=(ls, n // tn),
        in_specs=[
            pl.BlockSpec((rows, d), lambda l, j: (0, 0)),
            pl.BlockSpec((None, d, tn), lambda l, j: (l, 0, j)),
            pl.BlockSpec((None, 1, tn), lambda l, j: (l, 0, j)),
        ],
        out_specs=pl.BlockSpec((None, rows, tn), lambda l, j: (l, 0, j)),
        out_shape=jax.ShapeDtypeStruct((ls, rows, n), F32),
        compiler_params=_cparams(("parallel", "parallel"), 40),
        name="adaln",
    )(c_all, w_ada, b_ada)


def _log_sigmoid(x):
    return jnp.minimum(x, 0.0) - jnp.log1p(jnp.exp(-jnp.abs(x)))


def _gate_logits(hb, wlr_t, wgk_ref, bgk_ref, glog_ref):
    glr = _dot_nt(hb, wlr_t)
    logit = _dot(glr.astype(BF16), wgk_ref[...]) + bgk_ref[...]
    glog_ref[...] = _log_sigmoid(logit) * (1.0 / GLA_LOGIT_NORM)


def _inproj_kernel(x_ref, mod_ref, g_ref, w_ref, wlr_ref, wgk_ref, bgk_ref,
                   proj_ref, glog_ref, h_scr, *, d, tps):
    j = pl.program_id(2)

    @pl.when(j == 0)
    def _():
        hb = _norm_mod(x_ref[...], g_ref[...], mod_ref, d).astype(BF16)
        h_scr[...] = hb
        _gate_logits(hb, wlr_ref[...], wgk_ref, bgk_ref, glog_ref)

    tn = w_ref.shape[2]
    for k in range(tps):
        proj_ref[:, k * tn:(k + 1) * tn] = _dot(h_scr[...], w_ref[j * tps + k])


def _inproj_cast_kernel(x_ref, mod_ref, g_ref, wa_ref, wb_ref, wgk_ref, bgk_ref,
                        proj_ref, glog_ref, wout_ref, wlr_ref, h_scr, *, d, n_lo):
    j = pl.program_id(2)

    @pl.when(j == 0)
    def _():
        h_scr[...] = _norm_mod(x_ref[...], g_ref[...], mod_ref, d).astype(BF16)

    @pl.when(j < n_lo)
    def _():
        wout_ref[...] = jnp.transpose(wa_ref[...]).astype(BF16)

    @pl.when(j >= n_lo)
    def _():
        w = jnp.concatenate([wa_ref[GLA_RANK:, :], wb_ref[:GLA_RANK, :]], axis=0)
        wout_ref[...] = jnp.transpose(w).astype(BF16)

    @pl.when(j == n_lo)
    def _():
        wlr_t = _pad_rows(wa_ref[:GLA_RANK, :], LANES).astype(BF16)
        wlr_ref[...] = wlr_t
        _gate_logits(h_scr[...], wlr_t, wgk_ref, bgk_ref, glog_ref)

    proj_ref[...] = _dot(h_scr[...], wout_ref[...])


def _inproj_cast(x3, mod3, g_pre, w_raw_t, layer, w_gk, b_gk, tn=TN_INPROJ):
    gn, t, d = x3.shape
    assert gn == 1
    sec = (w_raw_t.shape[1] - GLA_RANK) // 2
    assert sec % tn == 0 and tn % LANES == 0
    nj = 2 * sec // tn
    n = nj * tn
    gkey = w_gk.shape[1]
    return pl.pallas_call(
        functools.partial(_inproj_cast_kernel, d=d, n_lo=sec // tn),
        grid=(1, 1, nj),
        in_specs=[
            pl.BlockSpec((None, t, d), lambda g, i, j: (0, 0, 0)),
            _mod_spec(mod3, t, 3 * d, 3),
            pl.BlockSpec((1, d), lambda g, i, j: (0, 0)),
            pl.BlockSpec((None, tn, d), lambda g, i, j: (layer, j, 0)),
            pl.BlockSpec((None, LANES, d), lambda g, i, j: (layer, (j + 1) * (tn // LANES), 0)),
            pl.BlockSpec((LANES, gkey), lambda g, i, j: (0, 0)),
            pl.BlockSpec((1, gkey), lambda g, i, j: (0, 0)),
        ],
        out_specs=[
            pl.BlockSpec((None, t, tn), lambda g, i, j: (0, 0, j)),
            pl.BlockSpec((None, t, gkey), lambda g, i, j: (0, 0, 0)),
            pl.BlockSpec((None, d, tn), lambda g, i, j: (j, 0, 0)),
            pl.BlockSpec((LANES, d), lambda g, i, j: (0, 0)),
        ],
        out_shape=[
            jax.ShapeDtypeStruct((1, t, n), F32),
            jax.ShapeDtypeStruct((1, t, gkey), F32),
            jax.ShapeDtypeStruct((nj, d, tn), BF16),
            jax.ShapeDtypeStruct((LANES, d), BF16),
        ],
        scratch_shapes=[pltpu.VMEM((t, d), BF16)],
        compiler_params=_cparams(("arbitrary", "arbitrary", "arbitrary"), 56),
        name="inproj_cast",
    )(x3, _arr(mod3),g_pre, w_raw_t, w_raw_t, w_gk, b_gk)


def _inproj(x3, mod3, g_pre, w_main, w_lr_t, w_gk, b_gk, tm, tps=3):
    gn, t, d = x3.shape
    nj, _, tn = w_main.shape
    n = nj * tn
    gkey = w_gk.shape[1]
    return pl.pallas_call(
        functools.partial(_inproj_kernel, d=d, tps=tps),
        grid=(gn, t // tm, nj // tps),
        in_specs=[
            pl.BlockSpec((None, tm, d), lambda g, i, j: (g, i, 0)),
            _mod_spec(mod3, tm, 3 * d, 3),
            pl.BlockSpec((1, d), lambda g, i, j: (0, 0)),
            pl.BlockSpec((nj, d, tn), lambda g, i, j: (0, 0, 0), pipeline_mode=pl.Buffered(1)),
            pl.BlockSpec((LANES, d), lambda g, i, j: (0, 0)),
            pl.BlockSpec((LANES, gkey), lambda g, i, j: (0, 0)),
            pl.BlockSpec((1, gkey), lambda g, i, j: (0, 0)),
        ],
        out_specs=[
            pl.BlockSpec((None, tm, tps * tn), lambda g, i, j: (g, i, j)),
            pl.BlockSpec((None, tm, gkey), lambda g, i, j: (g, i, 0)),
        ],
        out_shape=[
            jax.ShapeDtypeStruct((gn, t, n), F32),
            jax.ShapeDtypeStruct((gn, t, gkey), F32),
        ],
        scratch_shapes=[pltpu.VMEM((tm, d), BF16)],
        compiler_params=_cparams(("parallel", "parallel", "arbitrary"), 56),
        name="inproj",
    )(x3, _arr(mod3),g_pre, w_main, w_lr_t, w_gk, b_gk)


def _cumsum_rows(g):
    c = g.shape[0]
    row = lax.broadcasted_iota(jnp.int32, g.shape, 0)
    s = 1
    while s < c:
        g = g + jnp.where(row >= s, pltpu.roll(g, s, 0), 0.0)
        s *= 2
    return g


def _pad_rows(a, rows):
    if a.shape[0] == rows:
        return a
    return jnp.concatenate([a, jnp.zeros((rows - a.shape[0], a.shape[1]), a.dtype)], axis=0)


def _col_bcast(row, width):
    sq = jnp.transpose(jnp.broadcast_to(row, (LANES, LANES)))
    return jnp.concatenate([sq] * (width // LANES), axis=1)


def _gla_core(q, k, v, g, s, sub):
    cq = q.shape[0]
    ck = max(cq, LANES)
    b = _cumsum_rows(g)
    be = b - g
    bk = _pad_rows(b, ck)
    kp = _pad_rows(k, ck)
    vp = _pad_rows(v, ck).astype(BF16)
    rowj = lax.broadcasted_iota(jnp.int32, (ck, 1), 0)
    att_rows = []
    for blk in range(cq // sub):
        lo, hi = blk * sub, (blk + 1) * sub
        base = be[lo:lo + 1, :]
        qs = q[lo:hi] * jnp.exp(b[lo:hi] - base)
        ks = jnp.where(rowj < hi, kp * jnp.exp(base - bk), 0.0)
        att_rows.append(_dot_nt(qs.astype(BF16), ks.astype(BF16)))
    att = att_rows[0] if len(att_rows) == 1 else jnp.concatenate(att_rows, axis=0)
    ri = lax.broadcasted_iota(jnp.int32, (cq, ck), 0)
    cj = lax.broadcasted_iota(jnp.int32, (cq, ck), 1)
    att = jnp.where(ri >= cj, att, 0.0)
    o = _dot(att.astype(BF16), vp) + _dot((q * jnp.exp(b)).astype(BF16), s.astype(BF16))
    b_last = b[cq - 1:cq, :]
    k_out = kp * jnp.exp(b_last - bk)
    s_new = s * _col_bcast(jnp.exp(b_last), s.shape[1]) + _dot(jnp.transpose(k_out).astype(BF16), vp)
    return o, s_new


def _ret_core(q, k, v, s, lg, dmat, valid):
    cq = q.shape[0]
    ck = max(cq, LANES)
    kp = _pad_rows(k, ck)
    vp = _pad_rows(v, ck).astype(BF16)
    ti = lax.broadcasted_iota(jnp.int32, (cq, 1), 0).astype(F32)
    tj = lax.broadcasted_iota(jnp.int32, (ck, 1), 0).astype(F32)
    att = _dot_nt(q.astype(BF16), kp.astype(BF16)) * dmat
    q_in = q * jnp.exp((ti + 1.0) * lg)
    o = _dot(att.astype(BF16), vp) + _dot(q_in.astype(BF16), s.astype(BF16))
    k_out = kp * jnp.exp((float(valid - 1) - tj) * lg)
    s_new = s * jnp.exp(float(valid) * lg) + _dot(jnp.transpose(k_out).astype(BF16), vp)
    return o, s_new


def _decay_matrix(cq, ck, lg):
    ri = lax.broadcasted_iota(jnp.int32, (cq, ck), 0)
    cj = lax.broadcasted_iota(jnp.int32, (cq, ck), 1)
    diff = (ri - cj).astype(F32)
    return jnp.where(ri >= cj, jnp.exp(diff * lg), 0.0)


def _rope(x, cosf, sinf):
    return x * cosf + pltpu.roll(x, x.shape[1] // 2, 1) * sinf


def _silu(x):
    return x * jax.nn.sigmoid(x)


def _gla_finish(o, gate, gn):
    o = o * lax.rsqrt(jnp.mean(o * o, axis=-1, keepdims=True) + EPS) * gn
    return (o * _silu(gate)).astype(BF16)


def _ret_finish(o, gate, gn):
    oc = o - jnp.mean(o, axis=-1, keepdims=True)
    oc = oc * lax.rsqrt(jnp.mean(oc * oc, axis=-1, keepdims=True) + EPS) * gn
    return (oc * _silu(gate)).astype(BF16)


def _head(ref, h, width):
    return ref[:, h * width:(h + 1) * width]


def _attn_prompt_kernel(gq_ref, gk_ref, gv_ref, gg_ref, gl_ref, ggn_ref, gs0_ref,
                        rq_ref, rk_ref, rv_ref, rg_ref, cos_ref, sin_ref, lg_ref, rgn_ref, rs0_ref,
                        go_ref, gs_ref, ro_ref, rs_ref, d_scr):
    @pl.when(pl.program_id(1) == 0)
    def _():
        gs_ref[...] = gs0_ref[...]
        rs_ref[...] = rs0_ref[...]
        for h in range(d_scr.shape[0]):
            d_scr[h] = _decay_matrix(d_scr.shape[1], d_scr.shape[2], lg_ref[h][:, 0:1])

    cosf, sinf = cos_ref[...], sin_ref[...]
    for h in range(gs_ref.shape[0]):
        vcols = slice(h * HEAD_DV, (h + 1) * HEAD_DV)
        q = _head(gq_ref, h, HEAD_DK) * (HEAD_DK ** -0.5)
        o, s_new = _gla_core(q, _head(gk_ref, h, HEAD_DK), _head(gv_ref, h, HEAD_DV),
                             _head(gl_ref, h, HEAD_DK), gs_ref[h], GLA_SUB)
        gs_ref[h] = s_new
        go_ref[:, vcols] = _gla_finish(o, _head(gg_ref, h, HEAD_DV), ggn_ref[h])
        q = _rope(_head(rq_ref, h, HEAD_DK), cosf, sinf)
        k = _rope(_head(rk_ref, h, HEAD_DK), cosf, sinf) * (HEAD_DK ** -0.5)
        o, s_new = _ret_core(q, k, _head(rv_ref, h, HEAD_DV), rs_ref[h], lg_ref[h][:, 0:1], d_scr[h],
                             q.shape[0])
        rs_ref[h] = s_new
        ro_ref[:, vcols] = _ret_finish(o, _head(rg_ref, h, HEAD_DV), rgn_ref[h])


def _attn_prompt(proj, glog, gla_norm, ret_norm, cosf, sinf, ret_lg, s0_gla, s0_ret):
    bsz, t, _ = proj.shape
    c = ATT_CHUNK
    nh = GLA_HEADS
    kw, vw = nh * HEAD_DK, nh * HEAD_DV
    kspec = lambda blk: pl.BlockSpec((None, c, kw), lambda b, i, blk=blk: (b, i, blk))
    vspec = lambda blk: pl.BlockSpec((None, c, vw), lambda b, i, blk=blk: (b, i, blk))
    hspec = pl.BlockSpec((nh, 1, HEAD_DV), lambda b, i: (0, 0, 0))
    sspec = pl.BlockSpec((None, nh, HEAD_DK, HEAD_DV), lambda b, i: (b, 0, 0, 0))
    ospec = pl.BlockSpec((None, c, vw), lambda b, i: (b, i, 0))
    tspec = pl.BlockSpec((c, HEAD_DK), lambda b, i: (i, 0))
    o_shape = jax.ShapeDtypeStruct((bsz, t, vw), BF16)
    s_shape = jax.ShapeDtypeStruct((bsz, nh, HEAD_DK, HEAD_DV), F32)
    mg, s_gla, mr, s_ret = pl.pallas_call(
        _attn_prompt_kernel,
        grid=(bsz, t // c),
        in_specs=[kspec(0), kspec(1), vspec(1), vspec(2), kspec(0), hspec, sspec,
                  kspec(6), kspec(7), vspec(4), vspec(5), tspec, tspec,
                  pl.BlockSpec((nh, 1, LANES), lambda b, i: (0, 0, 0)), hspec, sspec],
        out_specs=[ospec, sspec, ospec, sspec],
        out_shape=[o_shape, s_shape, o_shape, s_shape],
        scratch_shapes=[pltpu.VMEM((nh, c, c), F32)],
        compiler_params=_cparams(("parallel", "arbitrary"), 32),
        name="attn_prompt",
    )(proj, proj, proj, proj, glog, gla_norm, s0_gla,
      proj, proj, proj, proj, cosf, sinf, ret_lg, ret_norm, s0_ret)
    return mg, mr, s_gla, s_ret


def _seq_rows(ref):
    ts, bb, w = ref.shape
    return ref[...].reshape(ts * bb, w)


def _seq_masks(n, bb):
    r = lax.broadcasted_iota(jnp.int32, (n, n), 0)
    c = lax.broadcasted_iota(jnp.int32, (n, n), 1)
    return (r % bb == c % bb) & (r >= c), (r - c).astype(F32) * (1.0 / bb)


def _seq_state_terms(q_in, k_out, v, s0, bb):
    n, dk = q_in.shape
    rown = lax.broadcasted_iota(jnp.int32, (n, 1), 0) % bb
    q_bd = jnp.concatenate([jnp.where(rown == j, q_in, 0.0) for j in range(bb)], axis=1)
    o_inter = _dot(q_bd.astype(BF16), s0.astype(BF16))
    k_t = jnp.transpose(_pad_rows(k_out, LANES))
    coln = lax.broadcasted_iota(jnp.int32, (1, LANES), 1) % bb
    k_bd = jnp.concatenate([jnp.where(coln == j, k_t, 0.0) for j in range(bb)], axis=0)
    ds = _dot(k_bd.astype(BF16), _pad_rows(v, LANES).astype(BF16))
    return o_inter, ds


def _gla_sample_kernel(q_ref, k_ref, v_ref, gg_ref, gl_ref, gn_ref, s0_ref, o_ref, s_ref):
    ts, bb, dk = q_ref.shape
    dv = v_ref.shape[2]
    n = ts * bb
    q = _seq_rows(q_ref) * (dk ** -0.5)
    k, v, g = _seq_rows(k_ref), _seq_rows(v_ref), _seq_rows(gl_ref)
    steps = [g[0:bb]]
    for t in range(1, ts):
        steps.append(steps[-1] + g[t * bb:(t + 1) * bb])
    b = jnp.concatenate(steps, axis=0)
    b_last = steps[-1]
    q_in = q * jnp.exp(b)
    mask, _ = _seq_masks(n, bb)
    att = jnp.where(mask, _dot_nt(q_in.astype(BF16), (k * jnp.exp(-b)).astype(BF16)), 0.0)
    k_out = k * jnp.exp(jnp.concatenate([b_last] * ts, axis=0) - b)
    s0 = s0_ref[...].reshape(bb * dk, dv)
    o_inter, ds = _seq_state_terms(q_in, k_out, v, s0, bb)
    o = _dot(att.astype(BF16), v.astype(BF16)) + o_inter
    e_last = jnp.exp(b_last)
    dec = jnp.concatenate([_col_bcast(e_last[j:j + 1, :], dv) for j in range(bb)], axis=0)
    s_ref[...] = (s0 * dec + ds).reshape(bb, dk, dv)
    o_ref[...] = _gla_finish(o, _seq_rows(gg_ref), gn_ref[...]).reshape(ts, bb, dv)


def _ret_sample_kernel(q_ref, k_ref, v_ref, rg_ref, cos_ref, sin_ref, lg_ref, gn_ref, s0_ref,
                       o_ref, s_ref):
    ts, bb, dk = q_ref.shape
    dv = v_ref.shape[2]
    n = ts * bb
    lg = lg_ref[:, 0:1]
    rows = lambda tab: jnp.concatenate(
        [jnp.broadcast_to(tab[t:t + 1, :], (bb, dk)) for t in range(ts)], axis=0)
    cosf, sinf = rows(cos_ref[...]), rows(sin_ref[...])
    q = _rope(_seq_rows(q_ref), cosf, sinf)
    k = _rope(_seq_rows(k_ref), cosf, sinf) * (dk ** -0.5)
    v = _seq_rows(v_ref)
    mask, dt = _seq_masks(n, bb)
    att = _dot_nt(q.astype(BF16), k.astype(BF16)) * jnp.where(mask, jnp.exp(dt * lg), 0.0)
    tt = (lax.broadcasted_iota(jnp.int32, (n, 1), 0) // bb).astype(F32)
    q_in = q * jnp.exp((tt + 1.0) * lg)
    k_out = k * jnp.exp((float(ts - 1) - tt) * lg)
    s0 = s0_ref[...].reshape(bb * dk, dv)
    o_inter, ds = _seq_state_terms(q_in, k_out, v, s0, bb)
    o = _dot(att.astype(BF16), v.astype(BF16)) + o_inter
    s_ref[...] = (s0 * jnp.exp(float(ts) * lg) + ds).reshape(bb, dk, dv)
    o_ref[...] = _ret_finish(o, _seq_rows(rg_ref), gn_ref[...]).reshape(ts, bb, dv)


def _attn_sample(proj, glog, gla_norm, ret_norm, cosf, sinf, ret_lg, s0_gla, s0_ret, bb=16):
    ts, bsz, _ = proj.shape
    grid = (bsz // bb, GLA_HEADS)
    nk = GLA_HEADS
    kspec = lambda off: pl.BlockSpec((ts, bb, HEAD_DK), lambda i, h, off=off: (0, i, off + h))
    vspec = lambda off: pl.BlockSpec((ts, bb, HEAD_DV), lambda i, h, off=off: (0, i, off + h))
    hspec = pl.BlockSpec((None, 1, HEAD_DV), lambda i, h: (h, 0, 0))
    sspec = pl.BlockSpec((bb, None, HEAD_DK, HEAD_DV), lambda i, h: (i, h, 0, 0))
    ospec = pl.BlockSpec((ts, bb, HEAD_DV), lambda i, h: (0, i, h))
    out_shape = [
        jax.ShapeDtypeStruct((ts, bsz, GLA_HEADS * HEAD_DV), BF16),
        jax.ShapeDtypeStruct((bsz, GLA_HEADS, HEAD_DK, HEAD_DV), F32),
    ]
    params = _cparams(("parallel", "parallel"), 40)
    mg, s_gla = pl.pallas_call(
        _gla_sample_kernel,
        grid=grid,
        in_specs=[kspec(0), kspec(nk), vspec(nk), vspec(2 * nk),
                  pl.BlockSpec((ts, bb, HEAD_DK), lambda i, h: (0, i, h)),
                  hspec, sspec],
        out_specs=[ospec, sspec],
        out_shape=out_shape,
        compiler_params=params,
        name="gla_sample",
    )(proj, proj, proj, proj, glog, gla_norm, s0_gla)
    tspec = pl.BlockSpec((ts, HEAD_DK), lambda i, h: (0, 0))
    mr, s_ret = pl.pallas_call(
        _ret_sample_kernel,
        grid=grid,
        in_specs=[kspec(6 * nk), kspec(7 * nk), vspec(4 * nk), vspec(5 * nk),
                  tspec, tspec,
                  pl.BlockSpec((None, 1, LANES), lambda i, h: (h, 0, 0)),
                  hspec, sspec],
        out_specs=[ospec, sspec],
        out_shape=out_shape,
        compiler_params=params,
        name="ret_sample",
    )(proj, proj, proj, proj, cosf, sinf, ret_lg, ret_norm, s0_ret)
    return mg, mr, s_gla, s_ret


ROW_SPLITS = 2


def _row_parts(tm):
    step = tm // ROW_SPLITS
    return [slice(k * step, (k + 1) * step) for k in range(ROW_SPLITS)]


def _residual_out(x_ref, y, g_ref, mod_ref, outs, d, rows):
    x_new = _gated_residual(x_ref[rows, :], y, g_ref[...], mod_ref, d)
    if len(outs) == 1:
        outs[0][rows, :] = x_new
    else:
        modn_ref, gn_ref, o_ref, h_ref = outs
        o_ref[rows, :] = x_new
        h_ref[rows, :] = _norm_mod(x_new, gn_ref[...], modn_ref, d).astype(BF16)


def _next_specs(nxt, tm, d):
    if nxt is None:
        return [], [], [], []
    mod_next, g_next, shape = nxt
    return ([_mod_spec(mod_next, tm, 3 * d, 2), pl.BlockSpec((1, d), lambda g, i: (0, 0))],
            [pl.BlockSpec((None, tm, d), lambda g, i: (g, i, 0))],
            [jax.ShapeDtypeStruct(shape, BF16)], [_arr(mod_next), g_next])


def _outproj_kernel(x_ref, mod_ref, g_ref, mg_ref, mr_ref, wo_ref, *outs, d):
    half = mg_ref.shape[1]
    for rows in _row_parts(x_ref.shape[0]):
        y = _dot(mg_ref[rows, :], wo_ref[0:half, :]) + _dot(mr_ref[rows, :], wo_ref[half:2 * half, :])
        _residual_out(x_ref, y, g_ref, mod_ref, outs, d, rows)


def _outproj(x3, mod3, g_post, mg, mr, w_out, tm, nxt=None):
    gn, t, d = x3.shape
    half = mg.shape[2]
    n_in, n_out, n_shape, n_ops = _next_specs(None if nxt is None else (*nxt, x3.shape), tm, d)
    res = pl.pallas_call(
        functools.partial(_outproj_kernel, d=d),
        grid=(gn, t // tm),
        in_specs=[
            pl.BlockSpec((None, tm, d), lambda g, i: (g, i, 0)),
            _mod_spec(mod3, tm, 3 * d, 2),
            pl.BlockSpec((1, d), lambda g, i: (0, 0)),
            pl.BlockSpec((None, tm, half), lambda g, i: (g, i, 0)),
            pl.BlockSpec((None, tm, half), lambda g, i: (g, i, 0)),
            pl.BlockSpec((2 * half, d), lambda g, i: (0, 0)),
        ] + n_in,
        out_specs=[pl.BlockSpec((None, tm, d), lambda g, i: (g, i, 0))] + n_out,
        out_shape=[jax.ShapeDtypeStruct((gn, t, d), F32)] + n_shape,
        compiler_params=_cparams(("parallel", "parallel"), 56),
        name="outproj",
    )(x3, _arr(mod3),g_post, mg, mr, w_out, *n_ops)
    return res if nxt is not None else (res[0], None)


def _mlp_kernel(x_ref, mod_ref, gpre_ref, gpost_ref, wup_ref, wdn_ref, o_ref, h_scr, acc_scr, *, d):
    j = pl.program_id(2)

    @pl.when(j == 0)
    def _():
        h_scr[...] = _norm_mod(x_ref[...], gpre_ref[...], mod_ref, d).astype(BF16)
        acc_scr[...] = jnp.zeros_like(acc_scr)

    u = jnp.maximum(_dot(h_scr[...], wup_ref[...]), 0.0)
    acc_scr[...] += _dot((u * u).astype(BF16), wdn_ref[...])

    @pl.when(j == pl.num_programs(2) - 1)
    def _():
        o_ref[...] = _gated_residual(x_ref[...], acc_scr[...], gpost_ref[...], mod_ref, d)


def _mlp_cast_kernel(x_ref, mod_ref, gpre_ref, gpost_ref, wup_ref, wdn_ref,
                     o_ref, wupb_ref, wdnb_ref, h_scr, acc_scr, *, d):
    wupb_ref[...] = wup_ref[...].astype(BF16)
    wdnb_ref[...] = wdn_ref[...].astype(BF16)
    _mlp_kernel(x_ref, mod_ref, gpre_ref, gpost_ref, wupb_ref, wdnb_ref, o_ref, h_scr, acc_scr, d=d)


def _mlp_cast(x3, mod3, g_pre, g_post, w_up_all, w_down_all, layer, tf=512):
    gn, t, d = x3.shape
    assert gn == 1
    f = w_up_all.shape[2]
    return pl.pallas_call(
        functools.partial(_mlp_cast_kernel, d=d),
        grid=(1, 1, f // tf),
        in_specs=[
            pl.BlockSpec((None, t, d), lambda g, i, j: (0, 0, 0)),
            _mod_spec(mod3, t, 3 * d, 3),
            pl.BlockSpec((1, d), lambda g, i, j: (0, 0)),
            pl.BlockSpec((1, d), lambda g, i, j: (0, 0)),
            pl.BlockSpec((None, d, tf), lambda g, i, j: (layer, 0, j)),
            pl.BlockSpec((None, tf, d), lambda g, i, j: (layer, j, 0)),
        ],
        out_specs=[
            pl.BlockSpec((None, t, d), lambda g, i, j: (0, 0, 0)),
            pl.BlockSpec((d, tf), lambda g, i, j: (0, j)),
            pl.BlockSpec((tf, d), lambda g, i, j: (j, 0)),
        ],
        out_shape=[
            jax.ShapeDtypeStruct((1, t, d), F32),
            jax.ShapeDtypeStruct((d, f), BF16),
            jax.ShapeDtypeStruct((f, d), BF16),
        ],
        scratch_shapes=[pltpu.VMEM((t, d), BF16), pltpu.VMEM((t, d), F32)],
        compiler_params=_cparams(("arbitrary", "arbitrary", "arbitrary"), 56),
        name="mlp_cast",
    )(x3, _arr(mod3),g_pre, g_post, w_up_all, w_down_all)


def _mlp(x3, mod3, g_pre, g_post, w_up, w_down, tf=TF_MLP):
    gn, t, d = x3.shape
    assert gn == 1
    f = w_up.shape[1]
    return pl.pallas_call(
        functools.partial(_mlp_kernel, d=d),
        grid=(1, 1, f // tf),
        in_specs=[
            pl.BlockSpec((None, t, d), lambda g, i, j: (0, 0, 0)),
            _mod_spec(mod3, t, 3 * d, 3),
            pl.BlockSpec((1, d), lambda g, i, j: (0, 0)),
            pl.BlockSpec((1, d), lambda g, i, j: (0, 0)),
            pl.BlockSpec((d, tf), lambda g, i, j: (0, j)),
            pl.BlockSpec((tf, d), lambda g, i, j: (j, 0)),
        ],
        out_specs=pl.BlockSpec((None, t, d), lambda g, i, j: (0, 0, 0)),
        out_shape=jax.ShapeDtypeStruct((1, t, d), F32),
        scratch_shapes=[pltpu.VMEM((t, d), BF16), pltpu.VMEM((t, d), F32)],
        compiler_params=_cparams(("arbitrary", "arbitrary", "arbitrary"), 56),
        name="mlp_rows",
    )(x3, _arr(mod3),g_pre, g_post, w_up, w_down)


def _mlp_h_kernel(x_ref, h_ref, mod_ref, gpost_ref, wup_ref, wdn_ref, o_ref, acc_scr, *, d):
    j = pl.program_id(2)

    @pl.when(j == 0)
    def _():
        acc_scr[...] = jnp.zeros_like(acc_scr)

    u = jnp.maximum(_dot(h_ref[...], wup_ref[...]), 0.0)
    acc_scr[...] += _dot((u * u).astype(BF16), wdn_ref[...])

    @pl.when(j == pl.num_programs(2) - 1)
    def _():
        o_ref[...] = _gated_residual(x_ref[...], acc_scr[...], gpost_ref[...], mod_ref, d)


def _mlp_h(x3, h3, mod3, g_post, w_up, w_down, tm, tf=TF_MLP):
    gn, t, d = x3.shape
    f = w_up.shape[1]
    return pl.pallas_call(
        functools.partial(_mlp_h_kernel, d=d),
        grid=(gn, t // tm, f // tf),
        in_specs=[
            pl.BlockSpec((None, tm, d), lambda g, i, j: (g, i, 0)),
            pl.BlockSpec((None, tm, d), lambda g, i, j: (g, i, 0)),
            _mod_spec(mod3, tm, 3 * d, 3),
            pl.BlockSpec((1, d), lambda g, i, j: (0, 0)),
            pl.BlockSpec((d, tf), lambda g, i, j: (0, j)),
            pl.BlockSpec((tf, d), lambda g, i, j: (j, 0)),
        ],
        out_specs=pl.BlockSpec((None, tm, d), lambda g, i, j: (g, i, 0)),
        out_shape=jax.ShapeDtypeStruct((gn, t, d), F32),
        scratch_shapes=[pltpu.VMEM((tm, d), F32)],
        compiler_params=_cparams(("parallel", "parallel", "arbitrary"), 56),
        name="mlp",
    )(x3, h3, _arr(mod3), g_post, w_up, w_down)


def _diag_blocks(rows, reps, row_shift, col_shift):
    tiled = jnp.concatenate([rows] * reps, axis=1)
    rg = lax.broadcasted_iota(jnp.int32, tiled.shape, 0) >> row_shift
    cg = lax.broadcasted_iota(jnp.int32, tiled.shape, 1) >> col_shift
    return jnp.where(rg == cg, tiled, 0.0)


def _s5_disc_kernel(lr_ref, li_ref, ldt_ref, br_ref, bi_ref, ctr_ref, cti_ref,
                    pwr_ref, pwi_ref, bblk_ref, cblk_ref, *, seg_len):
    lr, li = lr_ref[...], li_ref[...]
    dt = jnp.exp(ldt_ref[...])
    mag = jnp.exp(lr * dt)
    lb_re, lb_im = mag * jnp.cos(li * dt), mag * jnp.sin(li * dt)
    nr, ni = lb_re - 1.0, lb_im
    den = lr * lr + li * li
    f_re = (nr * lr + ni * li) / den
    f_im = (ni * lr - nr * li) / den
    br, bi = br_ref[...], bi_ref[...]
    bb_re = f_re * br - f_im * bi
    bb_im = f_re * bi + f_im * br
    nrow, ncol2 = bblk_ref.shape
    rs, cs = S5_GROUP.bit_length() - 1, S5_STATE.bit_length() - 1
    b_parts = [_diag_blocks(a.reshape(nrow, LANES), ncol2 // 2 // LANES, rs, cs) for a in (bb_re, bb_im)]
    bblk_ref[...] = jnp.concatenate(b_parts, axis=1).astype(BF16)
    c_parts = [_diag_blocks(r[...].reshape(ncol2 // 2, LANES), nrow // LANES, cs, rs)
               for r in (ctr_ref, cti_ref)]
    cblk_ref[...] = jnp.concatenate([c_parts[0], -c_parts[1]], axis=0).astype(BF16)
    pwr_ref[0] = lb_re
    pwi_ref[0] = lb_im
    qr, qi = None, None
    sr, si = lb_re, lb_im
    e = seg_len
    while e:
        if e & 1:
            qr, qi = (sr, si) if qr is None else (qr * sr - qi * si, qr * si + qi * sr)
        e >>= 1
        if e:
            sr, si = sr * sr - si * si, 2.0 * sr * si
    pr, pi = qr, qi
    for n in range(SUBLANES):
        pwr_ref[1 + n] = pr
        pwi_ref[1 + n] = pi
        pr, pi = pr * qr - pi * qi, pr * qi + pi * qr


def _s5_discretize(lam_re, lam_im, log_dt, bt_re, bt_im, ct_re, ct_im, seg_len):
    g = lam_re.shape[0]
    ncb = g // S5_GPB
    assert S5_GROUP & (S5_GROUP - 1) == 0 and S5_STATE & (S5_STATE - 1) == 0
    grp = lambda *dims: pl.BlockSpec((S5_GPB,) + dims, lambda c: (c, 0, 0))
    pw_spec = pl.BlockSpec((1 + SUBLANES, S5_GPB, 1, LANES), lambda c: (0, c, 0, 0))
    return pl.pallas_call(
        functools.partial(_s5_disc_kernel, seg_len=seg_len),
        grid=(ncb,),
        in_specs=[grp(1, LANES), grp(1, LANES), grp(1, 1), grp(S5_GROUP, LANES), grp(S5_GROUP, LANES),
                  grp(S5_STATE, LANES), grp(S5_STATE, LANES)],
        out_specs=[pw_spec, pw_spec,
                   pl.BlockSpec((None, S5_UW, 2 * S5_GPB * S5_STATE), lambda c: (c, 0, 0)),
                   pl.BlockSpec((None, 2 * S5_GPB * S5_STATE, S5_UW), lambda c: (c, 0, 0))],
        out_shape=[
            jax.ShapeDtypeStruct((1 + SUBLANES, g, 1, LANES), F32),
            jax.ShapeDtypeStruct((1 + SUBLANES, g, 1, LANES), F32),
            jax.ShapeDtypeStruct((ncb, S5_UW, 2 * S5_GPB * S5_STATE), BF16),
            jax.ShapeDtypeStruct((ncb, 2 * S5_GPB * S5_STATE, S5_UW), BF16),
        ],
        compiler_params=_cparams(("arbitrary",), 32),
        name="s5_discretize",
    )(lam_re, lam_im, log_dt, bt_re, bt_im, ct_re, ct_im)


def _gelu_tanh(x):
    c0 = math.sqrt(2.0 / math.pi)
    return x * (0.5 * (1.0 + jnp.tanh(c0 * (x + 0.044715 * (x * x * x)))))


def _cmul_add(ar, ai, xr, xi, yr, yi):
    return yr + ar * xr - ai * xi, yi + ar * xi + ai * xr


def _s5_seq_kernel(x_ref, mod_ref, gpre_ref, bblk_ref, cblk_ref, lam_ref, dskip_ref, s0r_ref, s0i_ref,
                   z_ref, sr_ref, si_ref, h_scr, xr_scr, xi_scr, *, d):
    cb = pl.program_id(0)
    ncb, tm, uw = h_scr.shape
    cw = xr_scr.shape[1]
    seg = s0r_ref.shape[0]

    @pl.when(cb == 0)
    def _():
        h = _norm_mod(x_ref[...], gpre_ref[...], mod_ref, d)
        for c in range(ncb):
            h_scr[c] = h[:, c * uw:(c + 1) * uw]

    u = h_scr[cb]
    bu = _dot(u.astype(BF16), bblk_ref[...])
    xr_scr[...] = bu[:, 0:cw]
    xi_scr[...] = bu[:, cw:2 * cw]
    car_r, car_i = s0r_ref[...], s0i_ref[...]
    l_r, l_i = lam_ref[0:1, :], lam_ref[1:2, :]
    for t in range(tm // seg):
        rows = slice(t * seg, (t + 1) * seg)
        car_r, car_i = _cmul_add(l_r, l_i, car_r, car_i, xr_scr[rows, :], xi_scr[rows, :])
        xr_scr[rows, :] = car_r
        xi_scr[rows, :] = car_i
    sr_ref[...] = car_r
    si_ref[...] = car_i
    xs = jnp.concatenate([xr_scr[...].astype(BF16), xi_scr[...].astype(BF16)], axis=1)
    y = _dot(xs, cblk_ref[...]) + dskip_ref[...] * u
    z_ref[...] = _gelu_tanh(y).astype(BF16)


def _s5_seq(x2, mod3, g_pre, bblk, cblk, lam2, dskip, s0_re, s0_im):
    tm, d = x2.shape
    seg, nst = s0_re.shape
    ncb, uw, cw2 = bblk.shape
    cw = cw2 // 2
    sspec = pl.BlockSpec((seg, cw), lambda c: (0, c))
    return pl.pallas_call(
        functools.partial(_s5_seq_kernel, d=d),
        grid=(ncb,),
        in_specs=[
            pl.BlockSpec((tm, d), lambda c: (0, 0)),
            pl.BlockSpec((None, seg, 3 * d), lambda c, lead=getattr(mod3, "lead", 0): (lead, 0, 0)),
            pl.BlockSpec((1, d), lambda c: (0, 0)),
            pl.BlockSpec((None, uw, cw2), lambda c: (c, 0, 0)),
            pl.BlockSpec((None, cw2, uw), lambda c: (c, 0, 0)),
            pl.BlockSpec((2, cw), lambda c: (0, c)),
            pl.BlockSpec((1, uw), lambda c: (0, c)),
            sspec, sspec,
        ],
        out_specs=[pl.BlockSpec((tm, uw), lambda c: (0, c)), sspec, sspec],
        out_shape=[
            jax.ShapeDtypeStruct((tm, d), BF16),
            jax.ShapeDtypeStruct((seg, nst), F32),
            jax.ShapeDtypeStruct((seg, nst), F32),
        ],
        scratch_shapes=[
            pltpu.VMEM((ncb, tm, uw), F32),
            pltpu.VMEM((tm, cw), F32),
            pltpu.VMEM((tm, cw), F32),
        ],
        compiler_params=_cparams(("arbitrary",), 48),
        name="s5_seq",
    )(x2, _arr(mod3), g_pre, bblk, cblk, lam2, dskip, s0_re, s0_im)


def _s5_rows_kernel(*refs, d, ncast):
    x_ref, mod_ref, gpre_ref, bblk_ref, cblk_ref, tbl_ref, dskip_ref = refs[:7]
    cast_in = refs[7:7 + ncast]
    z_ref, sr_ref, si_ref = refs[7 + ncast:10 + ncast]
    cast_out = refs[10 + ncast:10 + 2 * ncast]
    h_scr, xr_scr, xi_scr, cr_scr, ci_scr = refs[10 + 2 * ncast:]
    for src, dst in zip(cast_in, cast_out):
        dst[...] = src[...].astype(BF16)

    ncol, tm, _ = xr_scr.shape
    ncb, uw, _ = bblk_ref.shape
    sl = tm // SUBLANES

    @pl.when(pl.program_id(1) == 0)
    def _():
        cr_scr[...] = jnp.zeros_like(cr_scr)
        ci_scr[...] = jnp.zeros_like(ci_scr)

    h_scr[...] = _norm_mod(x_ref[...], gpre_ref[...], mod_ref, d)
    row0 = lax.broadcasted_iota(jnp.int32, (ncol, SUBLANES, LANES), 1) == 0

    for c in range(ncb):
        us = slice(c * uw, (c + 1) * uw)
        cols = slice(c * ncol, (c + 1) * ncol)
        u = h_scr[:, us]
        bu = _dot(u.astype(BF16), bblk_ref[c])
        for j in range(ncol):
            xr_scr[j] = bu[:, j * LANES:(j + 1) * LANES]
            xi_scr[j] = bu[:, (ncol + j) * LANES:(ncol + j + 1) * LANES]
        l_r, l_i = tbl_ref[0, cols], tbl_ref[1, cols]

        def local(i, s):
            rows = pl.ds(pl.multiple_of(i * SUBLANES, SUBLANES), SUBLANES)
            return _cmul_add(l_r, l_i, s[0], s[1], xr_scr[:, rows, :], xi_scr[:, rows, :])

        zero = jnp.zeros((ncol, SUBLANES, LANES), F32)
        g_r, g_i = lax.fori_loop(0, sl, local, (zero, zero), unroll=True)
        for n in range(3):
            g_r, g_i = _cmul_add(tbl_ref[2 + 2 * n, cols], tbl_ref[3 + 2 * n, cols],
                                 pltpu.roll(g_r, 1 << n, 1), pltpu.roll(g_i, 1 << n, 1), g_r, g_i)
        car_r, car_i = cr_scr[cols], ci_scr[cols]
        g_r, g_i = _cmul_add(tbl_ref[8, cols], tbl_ref[9, cols], car_r, car_i, g_r, g_i)
        in_r = jnp.where(row0, car_r, pltpu.roll(g_r, 1, 1))
        in_i = jnp.where(row0, car_i, pltpu.roll(g_i, 1, 1))

        def full(i, s):
            rows = pl.ds(pl.multiple_of(i * SUBLANES, SUBLANES), SUBLANES)
            s_r, s_i = _cmul_add(l_r, l_i, s[0], s[1], xr_scr[:, rows, :], xi_scr[:, rows, :])
            xr_scr[:, rows, :] = s_r
            xi_scr[:, rows, :] = s_i
            return s_r, s_i

        e_r, e_i = lax.fori_loop(0, sl, full, (in_r, in_i), unroll=True)
        cr_scr[cols] = e_r[:, SUBLANES - 1:SUBLANES, :]
        ci_scr[cols] = e_i[:, SUBLANES - 1:SUBLANES, :]
        xs = jnp.concatenate([xr_scr[j].astype(BF16) for j in range(ncol)]
                             + [xi_scr[j].astype(BF16) for j in range(ncol)], axis=1)
        y = _dot(xs, cblk_ref[c]) + dskip_ref[:, us] * u
        z_ref[:, us] = _gelu_tanh(y).astype(BF16)

    sr_ref[...] = cr_scr[...]
    si_ref[...] = ci_scr[...]


def _s5_rows(x3, mod3, g_pre, bblk, cblk, tbl, dskip, tm, casts=()):
    gn, t, d = x3.shape
    ncb, uw, cw2 = bblk.shape
    ncol = cw2 // 2 // LANES
    nct = ncb * ncol
    nt = t // tm
    nstep = gn * nt
    const = lambda shape: pl.BlockSpec(shape, lambda g, i: (0,) * len(shape), pipeline_mode=pl.Buffered(1))
    ospec = pl.BlockSpec((None, None, nct, 1, LANES), lambda g, i: (g, i, 0, 0, 0))
    c_in, c_out, c_shape = [], [], []
    for w, layer in casts:
        _, rows, cols = w.shape
        slab = rows // nstep
        assert slab * nstep == rows and slab % (2 * SUBLANES) == 0
        c_in.append(pl.BlockSpec((None, slab, cols), lambda g, i, layer=layer: (layer, g * nt + i, 0)))
        c_out.append(pl.BlockSpec((slab, cols), lambda g, i: (g * nt + i, 0)))
        c_shape.append(jax.ShapeDtypeStruct((rows, cols), BF16))
    res = pl.pallas_call(
        functools.partial(_s5_rows_kernel, d=d, ncast=len(casts)),
        grid=(gn, nt),
        in_specs=[
            pl.BlockSpec((None, tm, d), lambda g, i: (g, i, 0)),
            _mod_spec(mod3, tm, 3 * d, 2),
            pl.BlockSpec((1, d), lambda g, i: (0, 0)),
            const(bblk.shape), const(cblk.shape), const(tbl.shape), const(dskip.shape),
        ] + c_in,
        out_specs=[pl.BlockSpec((None, tm, d), lambda g, i: (g, i, 0)), ospec, ospec] + c_out,
        out_shape=[
            jax.ShapeDtypeStruct((gn, t, d), BF16),
            jax.ShapeDtypeStruct((gn, nt, nct, 1, LANES), F32),
            jax.ShapeDtypeStruct((gn, nt, nct, 1, LANES), F32),
        ] + c_shape,
        scratch_shapes=[
            pltpu.VMEM((tm, d), F32),
            pltpu.VMEM((ncol, tm, LANES), F32),
            pltpu.VMEM((ncol, tm, LANES), F32),
            pltpu.VMEM((nct, 1, LANES), F32),
            pltpu.VMEM((nct, 1, LANES), F32),
        ],
        compiler_params=_cparams(("arbitrary", "arbitrary"), 56),
        name="s5_rows",
    )(x3, _arr(mod3),g_pre, bblk, cblk, tbl, dskip, *[w for w, _ in casts])
    z3, s_re, s_im = res[:3]
    last = lambda s: s[:, nt - 1].reshape(gn, 1, nct * LANES)
    return z3, last(s_re), last(s_im), list(res[3:])


def _glu_kernel(x_ref, mod_ref, g_ref, z_ref, wa_ref, wb_ref, *outs, d):
    for rows in _row_parts(x_ref.shape[0]):
        z = z_ref[rows, :]
        y = _dot(z, wa_ref[...]) * jax.nn.sigmoid(_dot(z, wb_ref[...]))
        _residual_out(x_ref, y, g_ref, mod_ref, outs, d, rows)


def _glu(x3, mod3, g_post, z3, w_a, w_b, tm, nxt=None):
    gn, t, d = x3.shape
    wspec = pl.BlockSpec((d, d), lambda g, i: (0, 0), pipeline_mode=pl.Buffered(1))
    n_in, n_out, n_shape, n_ops = _next_specs(None if nxt is None else (*nxt, x3.shape), tm, d)
    res = pl.pallas_call(
        functools.partial(_glu_kernel, d=d),
        grid=(gn, t // tm),
        in_specs=[
            pl.BlockSpec((None, tm, d), lambda g, i: (g, i, 0)),
            _mod_spec(mod3, tm, 3 * d, 2),
            pl.BlockSpec((1, d), lambda g, i: (0, 0)),
            pl.BlockSpec((None, tm, d), lambda g, i: (g, i, 0)),
            wspec, wspec,
        ] + n_in,
        out_specs=[pl.BlockSpec((None, tm, d), lambda g, i: (g, i, 0))] + n_out,
        out_shape=[jax.ShapeDtypeStruct((gn, t, d), F32)] + n_shape,
        compiler_params=_cparams(("parallel", "parallel"), 56),
        name="glu",
    )(x3, _arr(mod3),g_post, z3, w_a, w_b, *n_ops)
    return res if nxt is not None else (res[0], None)


def _rope_tables(pos):
    half = HEAD_DK // 2
    inv = ROPE_BASE ** (-jnp.arange(half, dtype=F32) / half)
    ang = pos.astype(F32)[:, None] * inv[None, :]
    cos, sin = jnp.cos(ang), jnp.sin(ang)
    return jnp.concatenate([cos, cos], axis=-1), jnp.concatenate([-sin, sin], axis=-1)


def _s5_tables(pw_re, pw_im):
    n = pw_re.shape[0]
    flat_r = pw_re.reshape(n, -1)
    flat_i = pw_im.reshape(n, -1)
    row = jnp.arange(SUBLANES)[:, None]
    tabs = [jnp.broadcast_to(flat_r[0], (SUBLANES, flat_r.shape[1])),
            jnp.broadcast_to(flat_i[0], (SUBLANES, flat_i.shape[1]))]
    for s in (1, 2, 4):
        mask = row >= s
        tabs.append(jnp.where(mask, flat_r[s][None, :], 0.0))
        tabs.append(jnp.where(mask, flat_i[s][None, :], 0.0))
    tabs += [flat_r[1:], flat_i[1:]]
    tbl = jnp.stack(tabs)
    tbl = tbl.reshape(tbl.shape[0], SUBLANES, -1, LANES).transpose(0, 2, 1, 3)
    return tbl, jnp.stack([flat_r[0], flat_i[0]])


def kernel(x_prompt, x_sample, state_gla, state_ret, state_s5_re, state_s5_im, c_prompt, c_sample,
           w_ada, b_ada, norm_pre, norm_post, w_in_mix, w_gla_gk, b_gla_gk, gla_head_norm,
           ret_head_norm, w_out_mix, s5_lam_re, s5_lam_im, s5_log_dt, s5_b_re, s5_b_im,
           s5_c_re, s5_c_im, s5_d, w_glu_a, w_glu_b, w_mlp_up, w_mlp_down):
    bp, tp, d = x_prompt.shape
    bs, ts, _ = x_sample.shape
    depth = w_ada.shape[0]

    nrow = -(-(bs + bp) // SUBLANES) * SUBLANES
    c_all = jnp.concatenate([c_sample, c_prompt, jnp.zeros((nrow - bs - bp, d), F32)], axis=0)
    mod_all = _adaln(c_all, w_ada.reshape(depth * 2, d, 3 * d), b_ada.reshape(depth * 2, 1, 3 * d))
    mod_s = [_ModSlab(mod_all, k, bs) for k in range(depth * 2)]
    mod_p = [mod_all[k, bs:bs + bp][:, None, :] for k in range(depth * 2)]

    w_in_t = jnp.swapaxes(w_in_mix, 1, 2)
    w_gk = jnp.pad(w_gla_gk[0], ((0, LANES - GLA_RANK), (0, 0))).astype(BF16)
    b_gk = b_gla_gk[0][None, :]
    w_out = w_out_mix[0].astype(BF16)
    gla_norm = gla_head_norm[0][:, None, :]
    ret_norm = ret_head_norm[0][:, None, :]
    gamma_log = jnp.log1p(-jnp.power(2.0, -5.0 - jnp.arange(RET_HEADS, dtype=F32)))
    ret_lg = jnp.broadcast_to(gamma_log[:, None, None], (RET_HEADS, 1, LANES))

    ng = s5_lam_re.shape[1]
    tm5 = min(TM_S5, tp)
    per_state = lambda a: jnp.tile(a, (1,) * (a.ndim - 1) + (LANES // S5_STATE,))
    bt = lambda a: per_state(jnp.swapaxes(a[0], 1, 2))
    ct = lambda a: jnp.tile(jnp.swapaxes(a[0], 1, 2), (1, 1, LANES // S5_GROUP))
    pw_re, pw_im, bblk, cblk = _s5_discretize(
        per_state(s5_lam_re[0])[:, None, :], per_state(s5_lam_im[0])[:, None, :],
        s5_log_dt[0][:, None, None], bt(s5_b_re), bt(s5_b_im), ct(s5_c_re), ct(s5_c_im),
        tm5 // SUBLANES)
    tbl, lam2 = _s5_tables(pw_re[:, :, 0, :S5_STATE], pw_im[:, :, 0, :S5_STATE])
    dskip = s5_d[0][None, :]

    nxt = lambda mods, l, emit_h: (mods[2 * l + 1], norm_pre[l, 1][None]) if emit_h else None

    def layer0(x3, mods, tm, inproj, attn, mlp, emit_h):
        proj, glog = inproj(x3, mods[0])
        mg, mr, s_gla, s_ret = attn(proj, glog)
        x3, h3 = _outproj(x3, mods[0], norm_post[0, 0][None], mg, mr, w_out, tm, nxt(mods, 0, emit_h))
        return mlp(0, x3, h3, mods[1]), s_gla, s_ret

    def layer1_tail(x3, z3, mods, tm, glu_w, mlp, emit_h):
        x3, h3 = _glu(x3, mods[2], norm_post[1, 0][None], z3, glu_w[0], glu_w[1], tm, nxt(mods, 1, emit_h))
        return mlp(1, x3, h3, mods[3])

    cos_s, sin_s = _rope_tables(PAST_LEN + jnp.arange(ts, dtype=F32))

    def attn_s(proj, glog):
        tm_rows = lambda a: a.reshape(ts, bs, a.shape[-1])
        mg, mr, s_gla, s_ret = _attn_sample(tm_rows(proj), tm_rows(glog), gla_norm, ret_norm, cos_s, sin_s,
                                            ret_lg, state_gla[0], state_ret[0])
        flat = lambda a: a.reshape(1, ts * bs, a.shape[-1])
        return flat(mg), flat(mr), s_gla, s_ret

    xs3 = jnp.swapaxes(x_sample, 0, 1).reshape(1, ts * bs, d)

    def s5_s(x3, mod3):
        z2, s_re, s_im = _s5_seq(x3[0], mod3, norm_pre[1, 0][None], bblk, cblk, lam2, dskip,
                                 state_s5_re[0].reshape(bs, -1), state_s5_im[0].reshape(bs, -1))
        return z2[None], s_re, s_im

    w_up, w_dn, w_main = {}, {}, {}

    def inproj_s(x3, mod3):
        proj, glog, w_main[0], w_main["lr"] = _inproj_cast(x3, mod3, norm_pre[0, 0][None], w_in_t, 0,
                                                            w_gk, b_gk)
        return proj, glog

    def mlp_s(l, x3, h3, mod3):
        g_pre, g_post = norm_pre[l, 1][None], norm_post[l, 1][None]
        if l in w_up:
            return _mlp(x3, mod3, g_pre, g_post, w_up[l], w_dn[l])
        x3, w_up[l], w_dn[l] = _mlp_cast(x3, mod3, g_pre, g_post, w_mlp_up, w_mlp_down, l)
        return x3

    tm_s = ts * bs
    xs1, gla_s, ret_s = layer0(xs3, mod_s, tm_s, inproj_s, attn_s, mlp_s, False)

    cos_p, sin_p = _rope_tables(jnp.arange(tp, dtype=F32))
    zeros_att = jnp.zeros((bp, GLA_HEADS, HEAD_DK, HEAD_DV), F32)
    tm_p = min(TM_DENSE, tp)

    def attn_p(proj, glog):
        return _attn_prompt(proj, glog, gla_norm, ret_norm, cos_p, sin_p, ret_lg, zeros_att, zeros_att)

    def mlp_p(l, x3, h3, mod3):
        return _mlp_h(x3, h3, mod3, norm_post[l, 1][None], w_up[l], w_dn[l], tm_p)

    def inproj_p(x3, mod3):
        return _inproj(x3, mod3, norm_pre[0, 0][None], w_main[0], w_main["lr"], w_gk, b_gk, tm_p)

    xp1, gla_p, ret_p = layer0(x_prompt, mod_p, tm_p, inproj_p, attn_p, mlp_p, True)

    sl = tm5 // SUBLANES
    xpp = jnp.swapaxes(xp1.reshape(bp, tp // tm5, SUBLANES, sl, d), 2, 3).reshape(bp, tp, d)
    zpp, re_p, im_p, (w_up[1], w_dn[1], w_ga, w_gb) = _s5_rows(
        xpp, mod_p[2], norm_pre[1, 0][None], bblk, cblk, tbl, dskip, tm5,
        casts=((w_mlp_up, 1), (w_mlp_down, 1), (w_glu_a, 0), (w_glu_b, 0)))
    zp3 = jnp.swapaxes(zpp.reshape(bp, tp // tm5, sl, SUBLANES, d), 2, 3).reshape(bp, tp, d)

    zs3, re_s, im_s = s5_s(xs1, mod_s[2])
    y_s = layer1_tail(xs1, zs3, mod_s, tm_s, (w_ga, w_gb), mlp_s, False)
    y_s = jnp.swapaxes(y_s.reshape(ts, bs, d), 0, 1)
    y_p = layer1_tail(xp1, zp3, mod_p, tm_p, (w_ga, w_gb), mlp_p, True)

    st = lambda a, b_: a.reshape(1, b_, ng, S5_STATE)
    return (y_p, y_s, gla_p[None], gla_s[None], ret_p[None], ret_s[None],
            st(re_p, bp), st(re_s, bs), st(im_p, bp), st(im_s, bs))
```

```python
import functools
import math

import jax
import jax.numpy as jnp
import numpy as np
from jax import lax
from jax.experimental import pallas as pl
from jax.experimental.pallas import tpu as pltpu

F32 = jnp.float32
BF16 = jnp.bfloat16

EPS = 1e-6
LANES = 128
SUBLANES = 8
MIB = 1024 * 1024

GLA_HEADS = 4
RET_HEADS = 4
HEAD_DK = 128
HEAD_DV = 256
GLA_RANK = 16
GLA_LOGIT_NORM = 16.0
ROPE_BASE = 10000.0
PAST_LEN = 16384
S5_GROUP = 16
S5_STATE = 64
S5_GPB = 16
S5_UW = S5_GPB * S5_GROUP
ATT_CHUNK = 128
GLA_SUB = 16
TM_DENSE = 512
TM_S5 = 256
TN_INPROJ = 1024
TF_MLP = 1024


def _cparams(sem, vmem_mib):
    return pltpu.CompilerParams(dimension_semantics=sem, vmem_limit_bytes=vmem_mib * MIB)


def _dot(a, b):
    return jnp.dot(a, b, preferred_element_type=F32)


def _dot_nt(a, b):
    return lax.dot_general(a, b, (((1,), (1,)), ((), ())), preferred_element_type=F32)


def _rms(x, g):
    return x * lax.rsqrt(jnp.mean(x * x, axis=-1, keepdims=True) + EPS) * g


def _rows_affine(y, a, b=None):
    tm, d = y.shape
    r = a.shape[0]
    if r == 1 or r == tm:
        out = y * a
        return out if b is None else out + b
    y3 = y.reshape(tm // r, r, d)
    out = y3 * a[None]
    if b is not None:
        out = out + b[None]
    return out.reshape(tm, d)


def _norm_mod(x, g, mod_ref, d):
    return _rows_affine(_rms(x, g), 1.0 + mod_ref[:, d:2 * d], mod_ref[:, 0:d])


def _gated_residual(x, y, g, mod_ref, d):
    return x + _rows_affine(_rms(y, g), mod_ref[:, 2 * d:3 * d])


class _ModSlab:
    def __init__(self, arr, lead, rows):
        self.arr, self.lead = arr, lead
        self.shape = (1, rows, arr.shape[2])


def _arr(mod):
    return mod.arr if isinstance(mod, _ModSlab) else mod


def _mod_spec(mod, tm, width, ngrid):
    r = mod.shape[1]
    if isinstance(mod, _ModSlab):
        lead = mod.lead
        index = (lambda g, i: (lead, 0, 0)) if ngrid == 2 else (lambda g, i, j: (lead, 0, 0))
    else:
        index = (lambda g, i: (g, 0, 0)) if ngrid == 2 else (lambda g, i, j: (g, 0, 0))
    return pl.BlockSpec((None, r, width), index)


def _adaln_kernel(c_ref, w_ref, b_ref, o_ref):
    c = c_ref[...]
    sc = (c * jax.nn.sigmoid(c)).astype(BF16)
    o_ref[...] = _dot(sc, w_ref[...].astype(BF16)) + b_ref[...]


def _adaln(c_all, w_ada, b_ada, tn=1024):
    ls, d, n = w_ada.shape
    rows = c_all.shape[0]
    return pl.pallas_call(
        _adaln_kernel,
        grid=(ls, n // tn),
        in_specs=[
            pl.BlockSpec((rows, d), lambda l, j: (0, 0)),
            pl.BlockSpec((None, d, tn), lambda l, j: (l, 0, j)),
            pl.BlockSpec((None, 1, tn), lambda l, j: (l, 0, j)),
        ],
        out_specs=pl.BlockSpec((None, rows, tn), lambda l, j: (l, 0, j)),
        out_shape=jax.ShapeDtypeStruct((ls, rows, n), F32),
        compiler_params=_cparams(("parallel", "parallel"), 40),
        name="adaln",
    )(c_all, w_ada, b_ada)


def _log_sigmoid(x):
    return jnp.minimum(x, 0.0) - jnp.log1p(jnp.exp(-jnp.abs(x)))


def _gate_logits(hb, wlr_t, wgk_ref, bgk_ref, glog_ref):
    glr = _dot_nt(hb, wlr_t)
    logit = _dot(glr.astype(BF16), wgk_ref[...]) + bgk_ref[...]
    glog_ref[...] = _log_sigmoid(logit) * (1.0 / GLA_LOGIT_NORM)


def _inproj_kernel(x_ref, mod_ref, g_ref, w_ref, wlr_ref, wgk_ref, bgk_ref,
                   proj_ref, glog_ref, h_scr, *, d, tps):
    j = pl.program_id(2)

    @pl.when(j == 0)
    def _():
        hb = _norm_mod(x_ref[...], g_ref[...], mod_ref, d).astype(BF16)
        h_scr[...] = hb
        _gate_logits(hb, wlr_ref[...], wgk_ref, bgk_ref, glog_ref)

    tn = w_ref.shape[2]
    for k in range(tps):
        proj_ref[:, k * tn:(k + 1) * tn] = _dot(h_scr[...], w_ref[j * tps + k])


def _inproj_cast_kernel(x_ref, mod_ref, g_ref, wa_ref, wb_ref, wgk_ref, bgk_ref,
                        proj_ref, glog_ref, wout_ref, wlr_ref, h_scr, *, d, n_lo):
    j = pl.program_id(2)

    @pl.when(j == 0)
    def _():
        h_scr[...] = _norm_mod(x_ref[...], g_ref[...], mod_ref, d).astype(BF16)

    @pl.when(j < n_lo)
    def _():
        wout_ref[...] = jnp.transpose(wa_ref[...]).astype(BF16)

    @pl.when(j >= n_lo)
    def _():
        w = jnp.concatenate([wa_ref[GLA_RANK:, :], wb_ref[:GLA_RANK, :]], axis=0)
        wout_ref[...] = jnp.transpose(w).astype(BF16)

    @pl.when(j == n_lo)
    def _():
        wlr_t = _pad_rows(wa_ref[:GLA_RANK, :], LANES).astype(BF16)
        wlr_ref[...] = wlr_t
        _gate_logits(h_scr[...], wlr_t, wgk_ref, bgk_ref, glog_ref)

    proj_ref[...] = _dot(h_scr[...], wout_ref[...])


def _inproj_cast(x3, mod3, g_pre, w_raw_t, layer, w_gk, b_gk, tn=TN_INPROJ):
    gn, t, d = x3.shape
    assert gn == 1
    sec = (w_raw_t.shape[1] - GLA_RANK) // 2
    assert sec % tn == 0 and tn % LANES == 0
    nj = 2 * sec // tn
    n = nj * tn
    gkey = w_gk.shape[1]
    return pl.pallas_call(
        functools.partial(_inproj_cast_kernel, d=d, n_lo=sec // tn),
        grid=(1, 1, nj),
        in_specs=[
            pl.BlockSpec((None, t, d), lambda g, i, j: (0, 0, 0)),
            _mod_spec(mod3, t, 3 * d, 3),
            pl.BlockSpec((1, d), lambda g, i, j: (0, 0)),
            pl.BlockSpec((None, tn, d), lambda g, i, j: (layer, j, 0)),
            pl.BlockSpec((None, LANES, d), lambda g, i, j: (layer, (j + 1) * (tn // LANES), 0)),
            pl.BlockSpec((LANES, gkey), lambda g, i, j: (0, 0)),
            pl.BlockSpec((1, gkey), lambda g, i, j: (0, 0)),
        ],
        out_specs=[
            pl.BlockSpec((None, t, tn), lambda g, i, j: (0, 0, j)),
            pl.BlockSpec((None, t, gkey), lambda g, i, j: (0, 0, 0)),
            pl.BlockSpec((None, d, tn), lambda g, i, j: (j, 0, 0)),
            pl.BlockSpec((LANES, d), lambda g, i, j: (0, 0)),
        ],
        out_shape=[
            jax.ShapeDtypeStruct((1, t, n), F32),
            jax.ShapeDtypeStruct((1, t, gkey), F32),
            jax.ShapeDtypeStruct((nj, d, tn), BF16),
            jax.ShapeDtypeStruct((LANES, d), BF16),
        ],
        scratch_shapes=[pltpu.VMEM((t, d), BF16)],
        compiler_params=_cparams(("arbitrary", "arbitrary", "arbitrary"), 56),
        name="inproj_cast",
    )(x3, _arr(mod3),g_pre, w_raw_t, w_raw_t, w_gk, b_gk)


def _inproj(x3, mod3, g_pre, w_main, w_lr_t, w_gk, b_gk, tm, tps=3):
    gn, t, d = x3.shape
    nj, _, tn = w_main.shape
    n = nj * tn
    gkey = w_gk.shape[1]
    return pl.pallas_call(
        functools.partial(_inproj_kernel, d=d, tps=tps),
        grid=(gn, t // tm, nj // tps),
        in_specs=[
            pl.BlockSpec((None, tm, d), lambda g, i, j: (g, i, 0)),
            _mod_spec(mod3, tm, 3 * d, 3),
            pl.BlockSpec((1, d), lambda g, i, j: (0, 0)),
            pl.BlockSpec((nj, d, tn), lambda g, i, j: (0, 0, 0), pipeline_mode=pl.Buffered(1)),
            pl.BlockSpec((LANES, d), lambda g, i, j: (0, 0)),
            pl.BlockSpec((LANES, gkey), lambda g, i, j: (0, 0)),
            pl.BlockSpec((1, gkey), lambda g, i, j: (0, 0)),
        ],
        out_specs=[
            pl.BlockSpec((None, tm, tps * tn), lambda g, i, j: (g, i, j)),
            pl.BlockSpec((None, tm, gkey), lambda g, i, j: (g, i, 0)),
        ],
        out_shape=[
            jax.ShapeDtypeStruct((gn, t, n), F32),
            jax.ShapeDtypeStruct((gn, t, gkey), F32),
        ],
        scratch_shapes=[pltpu.VMEM((tm, d), BF16)],
        compiler_params=_cparams(("parallel", "parallel", "arbitrary"), 56),
        name="inproj",
    )(x3, _arr(mod3),g_pre, w_main, w_lr_t, w_gk, b_gk)


def _cumsum_rows(g):
    c = g.shape[0]
    row = lax.broadcasted_iota(jnp.int32, g.shape, 0)
    s = 1
    while s < c:
        g = g + jnp.where(row >= s, pltpu.roll(g, s, 0), 0.0)
        s *= 2
    return g


def _pad_rows(a, rows):
    if a.shape[0] == rows:
        return a
    return jnp.concatenate([a, jnp.zeros((rows - a.shape[0], a.shape[1]), a.dtype)], axis=0)


def _col_bcast(row, width):
    sq = jnp.transpose(jnp.broadcast_to(row, (LANES, LANES)))
    return jnp.concatenate([sq] * (width // LANES), axis=1)


def _gla_core(q, k, v, g, s, sub):
    cq = q.shape[0]
    ck = max(cq, LANES)
    b = _cumsum_rows(g)
    be = b - g
    bk = _pad_rows(b, ck)
    kp = _pad_rows(k, ck)
    vp = _pad_rows(v, ck).astype(BF16)
    rowj = lax.broadcasted_iota(jnp.int32, (ck, 1), 0)
    att_rows = []
    for blk in range(cq // sub):
        lo, hi = blk * sub, (blk + 1) * sub
        base = be[lo:lo + 1, :]
        qs = q[lo:hi] * jnp.exp(b[lo:hi] - base)
        ks = jnp.where(rowj < hi, kp * jnp.exp(base - bk), 0.0)
        att_rows.append(_dot_nt(qs.astype(BF16), ks.astype(BF16)))
    att = att_rows[0] if len(att_rows) == 1 else jnp.concatenate(att_rows, axis=0)
    ri = lax.broadcasted_iota(jnp.int32, (cq, ck), 0)
    cj = lax.broadcasted_iota(jnp.int32, (cq, ck), 1)
    att = jnp.where(ri >= cj, att, 0.0)
    o = _dot(att.astype(BF16), vp) + _dot((q * jnp.exp(b)).astype(BF16), s.astype(BF16))
    b_last = b[cq - 1:cq, :]
    k_out = kp * jnp.exp(b_last - bk)
    s_new = s * _col_bcast(jnp.exp(b_last), s.shape[1]) + _dot(jnp.transpose(k_out).astype(BF16), vp)
    return o, s_new


def _ret_core(q, k, v, s, lg, dmat, valid):
    cq = q.shape[0]
    ck = max(cq, LANES)
    kp = _pad_rows(k, ck)
    vp = _pad_rows(v, ck).astype(BF16)
    ti = lax.broadcasted_iota(jnp.int32, (cq, 1), 0).astype(F32)
    tj = lax.broadcasted_iota(jnp.int32, (ck, 1), 0).astype(F32)
    att = _dot_nt(q.astype(BF16), kp.astype(BF16)) * dmat
    q_in = q * jnp.exp((ti + 1.0) * lg)
    o = _dot(att.astype(BF16), vp) + _dot(q_in.astype(BF16), s.astype(BF16))
    k_out = kp * jnp.exp((float(valid - 1) - tj) * lg)
    s_new = s * jnp.exp(float(valid) * lg) + _dot(jnp.transpose(k_out).astype(BF16), vp)
    return o, s_new


def _decay_matrix(cq, ck, lg):
    ri = lax.broadcasted_iota(jnp.int32, (cq, ck), 0)
    cj = lax.broadcasted_iota(jnp.int32, (cq, ck), 1)
    diff = (ri - cj).astype(F32)
    return jnp.where(ri >= cj, jnp.exp(diff * lg), 0.0)


def _rope(x, cosf, sinf):
    return x * cosf + pltpu.roll(x, x.shape[1] // 2, 1) * sinf


def _silu(x):
    return x * jax.nn.sigmoid(x)


def _gla_finish(o, gate, gn):
    o = o * lax.rsqrt(jnp.mean(o * o, axis=-1, keepdims=True) + EPS) * gn
    return (o * _silu(gate)).astype(BF16)


def _ret_finish(o, gate, gn):
    oc = o - jnp.mean(o, axis=-1, keepdims=True)
    oc = oc * lax.rsqrt(jnp.mean(oc * oc, axis=-1, keepdims=True) + EPS) * gn
    return (oc * _silu(gate)).astype(BF16)


def _head(ref, h, width):
    return ref[:, h * width:(h + 1) * width]


def _attn_prompt_kernel(gq_ref, gk_ref, gv_ref, gg_ref, gl_ref, ggn_ref, gs0_ref,
                        rq_ref, rk_ref, rv_ref, rg_ref, cos_ref, sin_ref, lg_ref, rgn_ref, rs0_ref,
                        go_ref, gs_ref, ro_ref, rs_ref, d_scr):
    @pl.when(pl.program_id(1) == 0)
    def _():
        gs_ref[...] = gs0_ref[...]
        rs_ref[...] = rs0_ref[...]
        for h in range(d_scr.shape[0]):
            d_scr[h] = _decay_matrix(d_scr.shape[1], d_scr.shape[2], lg_ref[h][:, 0:1])

    cosf, sinf = cos_ref[...], sin_ref[...]
    for h in range(gs_ref.shape[0]):
        vcols = slice(h * HEAD_DV, (h + 1) * HEAD_DV)
        q = _head(gq_ref, h, HEAD_DK) * (HEAD_DK ** -0.5)
        o, s_new = _gla_core(q, _head(gk_ref, h, HEAD_DK), _head(gv_ref, h, HEAD_DV),
                             _head(gl_ref, h, HEAD_DK), gs_ref[h], GLA_SUB)
        gs_ref[h] = s_new
        go_ref[:, vcols] = _gla_finish(o, _head(gg_ref, h, HEAD_DV), ggn_ref[h])
        q = _rope(_head(rq_ref, h, HEAD_DK), cosf, sinf)
        k = _rope(_head(rk_ref, h, HEAD_DK), cosf, sinf) * (HEAD_DK ** -0.5)
        o, s_new = _ret_core(q, k, _head(rv_ref, h, HEAD_DV), rs_ref[h], lg_ref[h][:, 0:1], d_scr[h],
                             q.shape[0])
        rs_ref[h] = s_new
        ro_ref[:, vcols] = _ret_finish(o, _head(rg_ref, h, HEAD_DV), rgn_ref[h])


def _attn_prompt(proj, glog, gla_norm, ret_norm, cosf, sinf, ret_lg, s0_gla, s0_ret):
    bsz, t, _ = proj.shape
    c = ATT_CHUNK
    nh = GLA_HEADS
    kw, vw = nh * HEAD_DK, nh * HEAD_DV
    kspec = lambda blk: pl.BlockSpec((None, c, kw), lambda b, i, blk=blk: (b, i, blk))
    vspec = lambda blk: pl.BlockSpec((None, c, vw), lambda b, i, blk=blk: (b, i, blk))
    hspec = pl.BlockSpec((nh, 1, HEAD_DV), lambda b, i: (0, 0, 0))
    sspec = pl.BlockSpec((None, nh, HEAD_DK, HEAD_DV), lambda b, i: (b, 0, 0, 0))
    ospec = pl.BlockSpec((None, c, vw), lambda b, i: (b, i, 0))
    tspec = pl.BlockSpec((c, HEAD_DK), lambda b, i: (i, 0))
    o_shape = jax.ShapeDtypeStruct((bsz, t, vw), BF16)
    s_shape = jax.ShapeDtypeStruct((bsz, nh, HEAD_DK, HEAD_DV), F32)
    mg, s_gla, mr, s_ret = pl.pallas_call(
        _attn_prompt_kernel,
        grid=(bsz, t // c),
        in_specs=[kspec(0), kspec(1), vspec(1), vspec(2), kspec(0), hspec, sspec,
                  kspec(6), kspec(7), vspec(4), vspec(5), tspec, tspec,
                  pl.BlockSpec((nh, 1, LANES), lambda b, i: (0, 0, 0)), hspec, sspec],
        out_specs=[ospec, sspec, ospec, sspec],
        out_shape=[o_shape, s_shape, o_shape, s_shape],
        scratch_shapes=[pltpu.VMEM((nh, c, c), F32)],
        compiler_params=_cparams(("parallel", "arbitrary"), 32),
        name="attn_prompt",
    )(proj, proj, proj, proj, glog, gla_norm, s0_gla,
      proj, proj, proj, proj, cosf, sinf, ret_lg, ret_norm, s0_ret)
    return mg, mr, s_gla, s_ret


def _seq_rows(ref):
    ts, bb, w = ref.shape
    return ref[...].reshape(ts * bb, w)


def _seq_masks(n, bb):
    r = lax.broadcasted_iota(jnp.int32, (n, n), 0)
    c = lax.broadcasted_iota(jnp.int32, (n, n), 1)
    return (r % bb == c % bb) & (r >= c), (r - c).astype(F32) * (1.0 / bb)


def _seq_state_terms(q_in, k_out, v, s0, bb):
    n, dk = q_in.shape
    rown = lax.broadcasted_iota(jnp.int32, (n, 1), 0) % bb
    q_bd = jnp.concatenate([jnp.where(rown == j, q_in, 0.0) for j in range(bb)], axis=1)
    o_inter = _dot(q_bd.astype(BF16), s0.astype(BF16))
    k_t = jnp.transpose(_pad_rows(k_out, LANES))
    coln = lax.broadcasted_iota(jnp.int32, (1, LANES), 1) % bb
    k_bd = jnp.concatenate([jnp.where(coln == j, k_t, 0.0) for j in range(bb)], axis=0)
    ds = _dot(k_bd.astype(BF16), _pad_rows(v, LANES).astype(BF16))
    return o_inter, ds


def _gla_sample_kernel(q_ref, k_ref, v_ref, gg_ref, gl_ref, gn_ref, s0_ref, o_ref, s_ref):
    ts, bb, dk = q_ref.shape
    dv = v_ref.shape[2]
    n = ts * bb
    q = _seq_rows(q_ref) * (dk ** -0.5)
    k, v, g = _seq_rows(k_ref), _seq_rows(v_ref), _seq_rows(gl_ref)
    steps = [g[0:bb]]
    for t in range(1, ts):
        steps.append(steps[-1] + g[t * bb:(t + 1) * bb])
    b = jnp.concatenate(steps, axis=0)
    b_last = steps[-1]
    q_in = q * jnp.exp(b)
    mask, _ = _seq_masks(n, bb)
    att = jnp.where(mask, _dot_nt(q_in.astype(BF16), (k * jnp.exp(-b)).astype(BF16)), 0.0)
    k_out = k * jnp.exp(jnp.concatenate([b_last] * ts, axis=0) - b)
    s0 = s0_ref[...].reshape(bb * dk, dv)
    o_inter, ds = _seq_state_terms(q_in, k_out, v, s0, bb)
    o = _dot(att.astype(BF16), v.astype(BF16)) + o_inter
    e_last = jnp.exp(b_last)
    dec = jnp.concatenate([_col_bcast(e_last[j:j + 1, :], dv) for j in range(bb)], axis=0)
    s_ref[...] = (s0 * dec + ds).reshape(bb, dk, dv)
    o_ref[...] = _gla_finish(o, _seq_rows(gg_ref), gn_ref[...]).reshape(ts, bb, dv)


def _ret_sample_kernel(q_ref, k_ref, v_ref, rg_ref, cos_ref, sin_ref, lg_ref, gn_ref, s0_ref,
                       o_ref, s_ref):
    ts, bb, dk = q_ref.shape
    dv = v_ref.shape[2]
    n = ts * bb
    lg = lg_ref[:, 0:1]
    rows = lambda tab: jnp.concatenate(
        [jnp.broadcast_to(tab[t:t + 1, :], (bb, dk)) for t in range(ts)], axis=0)
    cosf, sinf = rows(cos_ref[...]), rows(sin_ref[...])
    q = _rope(_seq_rows(q_ref), cosf, sinf)
    k = _rope(_seq_rows(k_ref), cosf, sinf) * (dk ** -0.5)
    v = _seq_rows(v_ref)
    mask, dt = _seq_masks(n, bb)
    att = _dot_nt(q.astype(BF16), k.astype(BF16)) * jnp.where(mask, jnp.exp(dt * lg), 0.0)
    tt = (lax.broadcasted_iota(jnp.int32, (n, 1), 0) // bb).astype(F32)
    q_in = q * jnp.exp((tt + 1.0) * lg)
    k_out = k * jnp.exp((float(ts - 1) - tt) * lg)
    s0 = s0_ref[...].reshape(bb * dk, dv)
    o_inter, ds = _seq_state_terms(q_in, k_out, v, s0, bb)
    o = _dot(att.astype(BF16), v.astype(BF16)) + o_inter
    s_ref[...] = (s0 * jnp.exp(float(ts) * lg) + ds).reshape(bb, dk, dv)
    o_ref[...] = _ret_finish(o, _seq_rows(rg_ref), gn_ref[...]).reshape(ts, bb, dv)


def _attn_sample(proj, glog, gla_norm, ret_norm, cosf, sinf, ret_lg, s0_gla, s0_ret, bb=16):
    ts, bsz, _ = proj.shape
    grid = (bsz // bb, GLA_HEADS)
    nk = GLA_HEADS
    kspec = lambda off: pl.BlockSpec((ts, bb, HEAD_DK), lambda i, h, off=off: (0, i, off + h))
    vspec = lambda off: pl.BlockSpec((ts, bb, HEAD_DV), lambda i, h, off=off: (0, i, off + h))
    hspec = pl.BlockSpec((None, 1, HEAD_DV), lambda i, h: (h, 0, 0))
    sspec = pl.BlockSpec((bb, None, HEAD_DK, HEAD_DV), lambda i, h: (i, h, 0, 0))
    ospec = pl.BlockSpec((ts, bb, HEAD_DV), lambda i, h: (0, i, h))
    out_shape = [
        jax.ShapeDtypeStruct((ts, bsz, GLA_HEADS * HEAD_DV), BF16),
        jax.ShapeDtypeStruct((bsz, GLA_HEADS, HEAD_DK, HEAD_DV), F32),
    ]
    params = _cparams(("parallel", "parallel"), 40)
    mg, s_gla = pl.pallas_call(
        _gla_sample_kernel,
        grid=grid,
        in_specs=[kspec(0), kspec(nk), vspec(nk), vspec(2 * nk),
                  pl.BlockSpec((ts, bb, HEAD_DK), lambda i, h: (0, i, h)),
                  hspec, sspec],
        out_specs=[ospec, sspec],
        out_shape=out_shape,
        compiler_params=params,
        name="gla_sample",
    )(proj, proj, proj, proj, glog, gla_norm, s0_gla)
    tspec = pl.BlockSpec((ts, HEAD_DK), lambda i, h: (0, 0))
    mr, s_ret = pl.pallas_call(
        _ret_sample_kernel,
        grid=grid,
        in_specs=[kspec(6 * nk), kspec(7 * nk), vspec(4 * nk), vspec(5 * nk),
                  tspec, tspec,
                  pl.BlockSpec((None, 1, LANES), lambda i, h: (h, 0, 0)),
                  hspec, sspec],
        out_specs=[ospec, sspec],
        out_shape=out_shape,
        compiler_params=params,
        name="ret_sample",
    )(proj, proj, proj, proj, cosf, sinf, ret_lg, ret_norm, s0_ret)
    return mg, mr, s_gla, s_ret


ROW_SPLITS = 2


def _row_parts(tm):
    step = tm // ROW_SPLITS
    return [slice(k * step, (k + 1) * step) for k in range(ROW_SPLITS)]


def _residual_out(x_ref, y, g_ref, mod_ref, outs, d, rows):
    x_new = _gated_residual(x_ref[rows, :], y, g_ref[...], mod_ref, d)
    if len(outs) == 1:
        outs[0][rows, :] = x_new
    else:
        modn_ref, gn_ref, o_ref, h_ref = outs
        o_ref[rows, :] = x_new
        h_ref[rows, :] = _norm_mod(x_new, gn_ref[...], modn_ref, d).astype(BF16)


def _next_specs(nxt, tm, d):
    if nxt is None:
        return [], [], [], []
    mod_next, g_next, shape = nxt
    return ([_mod_spec(mod_next, tm, 3 * d, 2), pl.BlockSpec((1, d), lambda g, i: (0, 0))],
            [pl.BlockSpec((None, tm, d), lambda g, i: (g, i, 0))],
            [jax.ShapeDtypeStruct(shape, BF16)], [_arr(mod_next), g_next])


def _outproj_kernel(x_ref, mod_ref, g_ref, mg_ref, mr_ref, wo_ref, *outs, d):
    half = mg_ref.shape[1]
    for rows in _row_parts(x_ref.shape[0]):
        y = _dot(mg_ref[rows, :], wo_ref[0:half, :]) + _dot(mr_ref[rows, :], wo_ref[half:2 * half, :])
        _residual_out(x_ref, y, g_ref, mod_ref, outs, d, rows)


def _outproj(x3, mod3, g_post, mg, mr, w_out, tm, nxt=None):
    gn, t, d = x3.shape
    half = mg.shape[2]
    n_in, n_out, n_shape, n_ops = _next_specs(None if nxt is None else (*nxt, x3.shape), tm, d)
    res = pl.pallas_call(
        functools.partial(_outproj_kernel, d=d),
        grid=(gn, t // tm),
        in_specs=[
            pl.BlockSpec((None, tm, d), lambda g, i: (g, i, 0)),
            _mod_spec(mod3, tm, 3 * d, 2),
            pl.BlockSpec((1, d), lambda g, i: (0, 0)),
            pl.BlockSpec((None, tm, half), lambda g, i: (g, i, 0)),
            pl.BlockSpec((None, tm, half), lambda g, i: (g, i, 0)),
            pl.BlockSpec((2 * half, d), lambda g, i: (0, 0)),
        ] + n_in,
        out_specs=[pl.BlockSpec((None, tm, d), lambda g, i: (g, i, 0))] + n_out,
        out_shape=[jax.ShapeDtypeStruct((gn, t, d), F32)] + n_shape,
        compiler_params=_cparams(("parallel", "parallel"), 56),
        name="outproj",
    )(x3, _arr(mod3),g_post, mg, mr, w_out, *n_ops)
    return res if nxt is not None else (res[0], None)


def _mlp_kernel(x_ref, mod_ref, gpre_ref, gpost_ref, wup_ref, wdn_ref, o_ref, h_scr, acc_scr, *, d):
    j = pl.program_id(2)

    @pl.when(j == 0)
    def _():
        h_scr[...] = _norm_mod(x_ref[...], gpre_ref[...], mod_ref, d).astype(BF16)
        acc_scr[...] = jnp.zeros_like(acc_scr)

    u = jnp.maximum(_dot(h_scr[...], wup_ref[...]), 0.0)
    acc_scr[...] += _dot((u * u).astype(BF16), wdn_ref[...])

    @pl.when(j == pl.num_programs(2) - 1)
    def _():
        o_ref[...] = _gated_residual(x_ref[...], acc_scr[...], gpost_ref[...], mod_ref, d)


def _mlp_cast_kernel(x_ref, mod_ref, gpre_ref, gpost_ref, wup_ref, wdn_ref,
                     o_ref, wupb_ref, wdnb_ref, h_scr, acc_scr, *, d):
    wupb_ref[...] = wup_ref[...].astype(BF16)
    wdnb_ref[...] = wdn_ref[...].astype(BF16)
    _mlp_kernel(x_ref, mod_ref, gpre_ref, gpost_ref, wupb_ref, wdnb_ref, o_ref, h_scr, acc_scr, d=d)


def _mlp_cast(x3, mod3, g_pre, g_post, w_up_all, w_down_all, layer, tf=512):
    gn, t, d = x3.shape
    assert gn == 1
    f = w_up_all.shape[2]
    return pl.pallas_call(
        functools.partial(_mlp_cast_kernel, d=d),
        grid=(1, 1, f // tf),
        in_specs=[
            pl.BlockSpec((None, t, d), lambda g, i, j: (0, 0, 0)),
            _mod_spec(mod3, t, 3 * d, 3),
            pl.BlockSpec((1, d), lambda g, i, j: (0, 0)),
            pl.BlockSpec((1, d), lambda g, i, j: (0, 0)),
            pl.BlockSpec((None, d, tf), lambda g, i, j: (layer, 0, j)),
            pl.BlockSpec((None, tf, d), lambda g, i, j: (layer, j, 0)),
        ],
        out_specs=[
            pl.BlockSpec((None, t, d), lambda g, i, j: (0, 0, 0)),
            pl.BlockSpec((d, tf), lambda g, i, j: (0, j)),
            pl.BlockSpec((tf, d), lambda g, i, j: (j, 0)),
        ],
        out_shape=[
            jax.ShapeDtypeStruct((1, t, d), F32),
            jax.ShapeDtypeStruct((d, f), BF16),
            jax.ShapeDtypeStruct((f, d), BF16),
        ],
        scratch_shapes=[pltpu.VMEM((t, d), BF16), pltpu.VMEM((t, d), F32)],
        compiler_params=_cparams(("arbitrary", "arbitrary", "arbitrary"), 56),
        name="mlp_cast",
    )(x3, _arr(mod3),g_pre, g_post, w_up_all, w_down_all)


def _mlp(x3, mod3, g_pre, g_post, w_up, w_down, tf=TF_MLP):
    gn, t, d = x3.shape
    assert gn == 1
    f = w_up.shape[1]
    return pl.pallas_call(
        functools.partial(_mlp_kernel, d=d),
        grid=(1, 1, f // tf),
        in_specs=[
            pl.BlockSpec((None, t, d), lambda g, i, j: (0, 0, 0)),
            _mod_spec(mod3, t, 3 * d, 3),
            pl.BlockSpec((1, d), lambda g, i, j: (0, 0)),
            pl.BlockSpec((1, d), lambda g, i, j: (0, 0)),
            pl.BlockSpec((d, tf), lambda g, i, j: (0, j)),
            pl.BlockSpec((tf, d), lambda g, i, j: (j, 0)),
        ],
        out_specs=pl.BlockSpec((None, t, d), lambda g, i, j: (0, 0, 0)),
        out_shape=jax.ShapeDtypeStruct((1, t, d), F32),
        scratch_shapes=[pltpu.VMEM((t, d), BF16), pltpu.VMEM((t, d), F32)],
        compiler_params=_cparams(("arbitrary", "arbitrary", "arbitrary"), 56),
        name="mlp_rows",
    )(x3, _arr(mod3),g_pre, g_post, w_up, w_down)


def _mlp_h_kernel(x_ref, h_ref, mod_ref, gpost_ref, wup_ref, wdn_ref, o_ref, acc_scr, *, d):
    j = pl.program_id(2)
    last = pl.num_programs(2) - 1

    @pl.when(j == 0)
    def _():
        acc_scr[...] = jnp.zeros_like(acc_scr)

    def hidden(rows):
        u = jnp.maximum(_dot(h_ref[rows, :], wup_ref[...]), 0.0)
        return _dot((u * u).astype(BF16), wdn_ref[...])

    @pl.when(j < last)
    def _():
        acc_scr[...] += hidden(slice(None))

    @pl.when(j == last)
    def _():
        for rows in _row_parts(x_ref.shape[0]):
            y = acc_scr[rows, :] + hidden(rows)
            o_ref[rows, :] = _gated_residual(x_ref[rows, :], y, gpost_ref[...], mod_ref, d)


def _mlp_h(x3, h3, mod3, g_post, w_up, w_down, tm, tf=TF_MLP):
    gn, t, d = x3.shape
    f = w_up.shape[1]
    return pl.pallas_call(
        functools.partial(_mlp_h_kernel, d=d),
        grid=(gn, t // tm, f // tf),
        in_specs=[
            pl.BlockSpec((None, tm, d), lambda g, i, j: (g, i, 0)),
            pl.BlockSpec((None, tm, d), lambda g, i, j: (g, i, 0)),
            _mod_spec(mod3, tm, 3 * d, 3),
            pl.BlockSpec((1, d), lambda g, i, j: (0, 0)),
            pl.BlockSpec((d, tf), lambda g, i, j: (0, j)),
            pl.BlockSpec((tf, d), lambda g, i, j: (j, 0)),
        ],
        out_specs=pl.BlockSpec((None, tm, d), lambda g, i, j: (g, i, 0)),
        out_shape=jax.ShapeDtypeStruct((gn, t, d), F32),
        scratch_shapes=[pltpu.VMEM((tm, d), F32)],
        compiler_params=_cparams(("parallel", "parallel", "arbitrary"), 56),
        name="mlp",
    )(x3, h3, _arr(mod3), g_post, w_up, w_down)


def _diag_blocks(rows, reps, row_shift, col_shift):
    tiled = jnp.concatenate([rows] * reps, axis=1)
    rg = lax.broadcasted_iota(jnp.int32, tiled.shape, 0) >> row_shift
    cg = lax.broadcasted_iota(jnp.int32, tiled.shape, 1) >> col_shift
    return jnp.where(rg == cg, tiled, 0.0)


def _s5_disc_kernel(lr_ref, li_ref, ldt_ref, br_ref, bi_ref, cr_ref, ci_ref,
                    pwr_ref, pwi_ref, bblk_ref, cblk_ref, *, seg_len):
    lr, li = lr_ref[...], li_ref[...]
    dt = jnp.exp(ldt_ref[...])
    mag = jnp.exp(lr * dt)
    lb_re, lb_im = mag * jnp.cos(li * dt), mag * jnp.sin(li * dt)
    nr, ni = lb_re - 1.0, lb_im
    den = lr * lr + li * li
    f_re = (nr * lr + ni * li) / den
    f_im = (ni * lr - nr * li) / den
    br, bi = br_ref[...], bi_ref[...]
    bb_re = f_re * br - f_im * bi
    bb_im = f_re * bi + f_im * br
    nrow, ncol2 = bblk_ref.shape
    rs, cs = S5_GROUP.bit_length() - 1, S5_STATE.bit_length() - 1
    blocks = lambda a: _diag_blocks(a.reshape(nrow, LANES), ncol2 // 2 // LANES, rs, cs)
    bblk_ref[...] = jnp.concatenate([blocks(bb_re), blocks(bb_im)], axis=1).astype(BF16)
    cblk_ref[...] = jnp.concatenate([blocks(cr_ref[...]), -blocks(ci_ref[...])], axis=1).astype(BF16)
    pwr_ref[0] = lb_re
    pwi_ref[0] = lb_im
    qr, qi = None, None
    sr, si = lb_re, lb_im
    e = seg_len
    while e:
        if e & 1:
            qr, qi = (sr, si) if qr is None else (qr * sr - qi * si, qr * si + qi * sr)
        e >>= 1
        if e:
            sr, si = sr * sr - si * si, 2.0 * sr * si
    pr, pi = qr, qi
    for n in range(SUBLANES):
        pwr_ref[1 + n] = pr
        pwi_ref[1 + n] = pi
        pr, pi = pr * qr - pi * qi, pr * qi + pi * qr


def _s5_discretize(lam_re, lam_im, log_dt, bt_re, bt_im, c_re, c_im, seg_len):
    g = lam_re.shape[0]
    ncb = g // S5_GPB
    assert S5_GROUP & (S5_GROUP - 1) == 0 and S5_STATE & (S5_STATE - 1) == 0
    grp = lambda *dims: pl.BlockSpec((S5_GPB,) + dims, lambda c: (c, 0, 0))
    pw_spec = pl.BlockSpec((1 + SUBLANES, S5_GPB, 1, LANES), lambda c: (0, c, 0, 0))
    return pl.pallas_call(
        functools.partial(_s5_disc_kernel, seg_len=seg_len),
        grid=(ncb,),
        in_specs=[grp(1, LANES), grp(1, LANES), grp(1, 1), grp(S5_GROUP, LANES), grp(S5_GROUP, LANES),
                  grp(S5_GROUP, LANES), grp(S5_GROUP, LANES)],
        out_specs=[pw_spec, pw_spec,
                   pl.BlockSpec((None, S5_UW, 2 * S5_GPB * S5_STATE), lambda c: (c, 0, 0)),
                   pl.BlockSpec((None, S5_UW, 2 * S5_GPB * S5_STATE), lambda c: (c, 0, 0))],
        out_shape=[
            jax.ShapeDtypeStruct((1 + SUBLANES, g, 1, LANES), F32),
            jax.ShapeDtypeStruct((1 + SUBLANES, g, 1, LANES), F32),
            jax.ShapeDtypeStruct((ncb, S5_UW, 2 * S5_GPB * S5_STATE), BF16),
            jax.ShapeDtypeStruct((ncb, S5_UW, 2 * S5_GPB * S5_STATE), BF16),
        ],
        compiler_params=_cparams(("arbitrary",), 32),
        name="s5_discretize",
    )(lam_re, lam_im, log_dt, bt_re, bt_im, c_re, c_im)


def _gelu_tanh(x):
    c0 = math.sqrt(2.0 / math.pi)
    return x * (0.5 * (1.0 + jnp.tanh(c0 * (x + 0.044715 * (x * x * x)))))


def _cmul_add(ar, ai, xr, xi, yr, yi):
    return yr + ar * xr - ai * xi, yi + ar * xi + ai * xr


def _s5_seq_kernel(x_ref, mod_ref, gpre_ref, bblk_ref, cblk_ref, lam_ref, dskip_ref, s0r_ref, s0i_ref,
                   z_ref, sr_ref, si_ref, h_scr, xr_scr, xi_scr, *, d):
    cb = pl.program_id(0)
    ncb, tm, uw = h_scr.shape
    cw = xr_scr.shape[1]
    seg = s0r_ref.shape[0]

    @pl.when(cb == 0)
    def _():
        h = _norm_mod(x_ref[...], gpre_ref[...], mod_ref, d)
        for c in range(ncb):
            h_scr[c] = h[:, c * uw:(c + 1) * uw]

    u = h_scr[cb]
    bu = _dot(u.astype(BF16), bblk_ref[...])
    xr_scr[...] = bu[:, 0:cw]
    xi_scr[...] = bu[:, cw:2 * cw]
    car_r, car_i = s0r_ref[...], s0i_ref[...]
    l_r, l_i = lam_ref[0:1, :], lam_ref[1:2, :]
    for t in range(tm // seg):
        rows = slice(t * seg, (t + 1) * seg)
        car_r, car_i = _cmul_add(l_r, l_i, car_r, car_i, xr_scr[rows, :], xi_scr[rows, :])
        xr_scr[rows, :] = car_r
        xi_scr[rows, :] = car_i
    sr_ref[...] = car_r
    si_ref[...] = car_i
    xs = jnp.concatenate([xr_scr[...].astype(BF16), xi_scr[...].astype(BF16)], axis=1)
    y = _dot_nt(xs, cblk_ref[...]) + dskip_ref[...] * u
    z_ref[...] = _gelu_tanh(y).astype(BF16)


def _s5_seq(x2, mod3, g_pre, bblk, cblk, lam2, dskip, s0_re, s0_im):
    tm, d = x2.shape
    seg, nst = s0_re.shape
    ncb, uw, cw2 = bblk.shape
    cw = cw2 // 2
    sspec = pl.BlockSpec((seg, cw), lambda c: (0, c))
    return pl.pallas_call(
        functools.partial(_s5_seq_kernel, d=d),
        grid=(ncb,),
        in_specs=[
            pl.BlockSpec((tm, d), lambda c: (0, 0)),
            pl.BlockSpec((None, seg, 3 * d), lambda c, lead=getattr(mod3, "lead", 0): (lead, 0, 0)),
            pl.BlockSpec((1, d), lambda c: (0, 0)),
            pl.BlockSpec((None, uw, cw2), lambda c: (c, 0, 0)),
            pl.BlockSpec((None, uw, cw2), lambda c: (c, 0, 0)),
            pl.BlockSpec((2, cw), lambda c: (0, c)),
            pl.BlockSpec((1, uw), lambda c: (0, c)),
            sspec, sspec,
        ],
        out_specs=[pl.BlockSpec((tm, uw), lambda c: (0, c)), sspec, sspec],
        out_shape=[
            jax.ShapeDtypeStruct((tm, d), BF16),
            jax.ShapeDtypeStruct((seg, nst), F32),
            jax.ShapeDtypeStruct((seg, nst), F32),
        ],
        scratch_shapes=[
            pltpu.VMEM((ncb, tm, uw), F32),
            pltpu.VMEM((tm, cw), F32),
            pltpu.VMEM((tm, cw), F32),
        ],
        compiler_params=_cparams(("arbitrary",), 48),
        name="s5_seq",
    )(x2, _arr(mod3), g_pre, bblk, cblk, lam2, dskip, s0_re, s0_im)


def _s5_rows_kernel(*refs, d, ncast):
    x_ref, mod_ref, gpre_ref, bblk_ref, cblk_ref, tbl_ref, dskip_ref = refs[:7]
    cast_in = refs[7:7 + ncast]
    z_ref, sr_ref, si_ref = refs[7 + ncast:10 + ncast]
    cast_out = refs[10 + ncast:10 + 2 * ncast]
    h_scr, xr_scr, xi_scr, cr_scr, ci_scr = refs[10 + 2 * ncast:]
    for src, dst in zip(cast_in, cast_out):
        dst[...] = src[...].astype(BF16)

    ncol, tm, _ = xr_scr.shape
    ncb, uw, _ = bblk_ref.shape
    sl = tm // SUBLANES

    @pl.when(pl.program_id(1) == 0)
    def _():
        cr_scr[...] = jnp.zeros_like(cr_scr)
        ci_scr[...] = jnp.zeros_like(ci_scr)

    h_scr[...] = _norm_mod(x_ref[...], gpre_ref[...], mod_ref, d)
    row0 = lax.broadcasted_iota(jnp.int32, (ncol, SUBLANES, LANES), 1) == 0

    for c in range(ncb):
        us = slice(c * uw, (c + 1) * uw)
        cols = slice(c * ncol, (c + 1) * ncol)
        u = h_scr[:, us]
        bu = _dot(u.astype(BF16), bblk_ref[c])
        for j in range(ncol):
            xr_scr[j] = bu[:, j * LANES:(j + 1) * LANES]
            xi_scr[j] = bu[:, (ncol + j) * LANES:(ncol + j + 1) * LANES]
        l_r, l_i = tbl_ref[0, cols], tbl_ref[1, cols]

        def local(i, s):
            rows = pl.ds(pl.multiple_of(i * SUBLANES, SUBLANES), SUBLANES)
            return _cmul_add(l_r, l_i, s[0], s[1], xr_scr[:, rows, :], xi_scr[:, rows, :])

        zero = jnp.zeros((ncol, SUBLANES, LANES), F32)
        g_r, g_i = lax.fori_loop(0, sl, local, (zero, zero), unroll=True)
        for n in range(3):
            g_r, g_i = _cmul_add(tbl_ref[2 + 2 * n, cols], tbl_ref[3 + 2 * n, cols],
                                 pltpu.roll(g_r, 1 << n, 1), pltpu.roll(g_i, 1 << n, 1), g_r, g_i)
        car_r, car_i = cr_scr[cols], ci_scr[cols]
        g_r, g_i = _cmul_add(tbl_ref[8, cols], tbl_ref[9, cols], car_r, car_i, g_r, g_i)
        in_r = jnp.where(row0, car_r, pltpu.roll(g_r, 1, 1))
        in_i = jnp.where(row0, car_i, pltpu.roll(g_i, 1, 1))

        def full(i, s):
            rows = pl.ds(pl.multiple_of(i * SUBLANES, SUBLANES), SUBLANES)
            s_r, s_i = _cmul_add(l_r, l_i, s[0], s[1], xr_scr[:, rows, :], xi_scr[:, rows, :])
            xr_scr[:, rows, :] = s_r
            xi_scr[:, rows, :] = s_i
            return s_r, s_i

        e_r, e_i = lax.fori_loop(0, sl, full, (in_r, in_i), unroll=True)
        cr_scr[cols] = e_r[:, SUBLANES - 1:SUBLANES, :]
        ci_scr[cols] = e_i[:, SUBLANES - 1:SUBLANES, :]
        xs = jnp.concatenate([xr_scr[j].astype(BF16) for j in range(ncol)]
                             + [xi_scr[j].astype(BF16) for j in range(ncol)], axis=1)
        y = _dot_nt(xs, cblk_ref[c]) + dskip_ref[:, us] * u
        z_ref[:, us] = _gelu_tanh(y).astype(BF16)

    sr_ref[...] = cr_scr[...]
    si_ref[...] = ci_scr[...]


def _s5_rows(x3, mod3, g_pre, bblk, cblk, tbl, dskip, tm, casts=()):
    gn, t, d = x3.shape
    ncb, uw, cw2 = bblk.shape
    ncol = cw2 // 2 // LANES
    nct = ncb * ncol
    nt = t // tm
    nstep = gn * nt
    const = lambda shape: pl.BlockSpec(shape, lambda g, i: (0,) * len(shape), pipeline_mode=pl.Buffered(1))
    ospec = pl.BlockSpec((None, None, nct, 1, LANES), lambda g, i: (g, i, 0, 0, 0))
    c_in, c_out, c_shape = [], [], []
    for w, layer in casts:
        _, rows, cols = w.shape
        slab = rows // nstep
        assert slab * nstep == rows and slab % (2 * SUBLANES) == 0
        c_in.append(pl.BlockSpec((None, slab, cols), lambda g, i, layer=layer: (layer, g * nt + i, 0)))
        c_out.append(pl.BlockSpec((slab, cols), lambda g, i: (g * nt + i, 0)))
        c_shape.append(jax.ShapeDtypeStruct((rows, cols), BF16))
    res = pl.pallas_call(
        functools.partial(_s5_rows_kernel, d=d, ncast=len(casts)),
        grid=(gn, nt),
        in_specs=[
            pl.BlockSpec((None, tm, d), lambda g, i: (g, i, 0)),
            _mod_spec(mod3, tm, 3 * d, 2),
            pl.BlockSpec((1, d), lambda g, i: (0, 0)),
            const(bblk.shape), const(cblk.shape), const(tbl.shape), const(dskip.shape),
        ] + c_in,
        out_specs=[pl.BlockSpec((None, tm, d), lambda g, i: (g, i, 0)), ospec, ospec] + c_out,
        out_shape=[
            jax.ShapeDtypeStruct((gn, t, d), BF16),
            jax.ShapeDtypeStruct((gn, nt, nct, 1, LANES), F32),
            jax.ShapeDtypeStruct((gn, nt, nct, 1, LANES), F32),
        ] + c_shape,
        scratch_shapes=[
            pltpu.VMEM((tm, d), F32),
            pltpu.VMEM((ncol, tm, LANES), F32),
            pltpu.VMEM((ncol, tm, LANES), F32),
            pltpu.VMEM((nct, 1, LANES), F32),
            pltpu.VMEM((nct, 1, LANES), F32),
        ],
        compiler_params=_cparams(("arbitrary", "arbitrary"), 56),
        name="s5_rows",
    )(x3, _arr(mod3),g_pre, bblk, cblk, tbl, dskip, *[w for w, _ in casts])
    z3, s_re, s_im = res[:3]
    last = lambda s: s[:, nt - 1].reshape(gn, 1, nct * LANES)
    return z3, last(s_re), last(s_im), list(res[3:])


def _glu_kernel(x_ref, mod_ref, g_ref, z_ref, wa_ref, wb_ref, *outs, d):
    for rows in _row_parts(x_ref.shape[0]):
        z = z_ref[rows, :]
        y = _dot(z, wa_ref[...]) * jax.nn.sigmoid(_dot(z, wb_ref[...]))
        _residual_out(x_ref, y, g_ref, mod_ref, outs, d, rows)


def _glu(x3, mod3, g_post, z3, w_a, w_b, tm, nxt=None):
    gn, t, d = x3.shape
    wspec = pl.BlockSpec((d, d), lambda g, i: (0, 0), pipeline_mode=pl.Buffered(1))
    n_in, n_out, n_shape, n_ops = _next_specs(None if nxt is None else (*nxt, x3.shape), tm, d)
    res = pl.pallas_call(
        functools.partial(_glu_kernel, d=d),
        grid=(gn, t // tm),
        in_specs=[
            pl.BlockSpec((None, tm, d), lambda g, i: (g, i, 0)),
            _mod_spec(mod3, tm, 3 * d, 2),
            pl.BlockSpec((1, d), lambda g, i: (0, 0)),
            pl.BlockSpec((None, tm, d), lambda g, i: (g, i, 0)),
            wspec, wspec,
        ] + n_in,
        out_specs=[pl.BlockSpec((None, tm, d), lambda g, i: (g, i, 0))] + n_out,
        out_shape=[jax.ShapeDtypeStruct((gn, t, d), F32)] + n_shape,
        compiler_params=_cparams(("parallel", "parallel"), 56),
        name="glu",
    )(x3, _arr(mod3),g_post, z3, w_a, w_b, *n_ops)
    return res if nxt is not None else (res[0], None)


def _rope_tables(pos):
    half = HEAD_DK // 2
    inv = ROPE_BASE ** (-jnp.arange(half, dtype=F32) / half)
    ang = pos.astype(F32)[:, None] * inv[None, :]
    cos, sin = jnp.cos(ang), jnp.sin(ang)
    return jnp.concatenate([cos, cos], axis=-1), jnp.concatenate([-sin, sin], axis=-1)


def _s5_tables(pw_re, pw_im):
    n = pw_re.shape[0]
    flat_r = pw_re.reshape(n, -1)
    flat_i = pw_im.reshape(n, -1)
    row = jnp.arange(SUBLANES)[:, None]
    tabs = [jnp.broadcast_to(flat_r[0], (SUBLANES, flat_r.shape[1])),
            jnp.broadcast_to(flat_i[0], (SUBLANES, flat_i.shape[1]))]
    for s in (1, 2, 4):
        mask = row >= s
        tabs.append(jnp.where(mask, flat_r[s][None, :], 0.0))
        tabs.append(jnp.where(mask, flat_i[s][None, :], 0.0))
    tabs += [flat_r[1:], flat_i[1:]]
    tbl = jnp.stack(tabs)
    tbl = tbl.reshape(tbl.shape[0], SUBLANES, -1, LANES).transpose(0, 2, 1, 3)
    return tbl, jnp.stack([flat_r[0], flat_i[0]])


def kernel(x_prompt, x_sample, state_gla, state_ret, state_s5_re, state_s5_im, c_prompt, c_sample,
           w_ada, b_ada, norm_pre, norm_post, w_in_mix, w_gla_gk, b_gla_gk, gla_head_norm,
           ret_head_norm, w_out_mix, s5_lam_re, s5_lam_im, s5_log_dt, s5_b_re, s5_b_im,
           s5_c_re, s5_c_im, s5_d, w_glu_a, w_glu_b, w_mlp_up, w_mlp_down):
    bp, tp, d = x_prompt.shape
    bs, ts, _ = x_sample.shape
    depth = w_ada.shape[0]

    nrow = -(-(bs + bp) // SUBLANES) * SUBLANES
    c_all = jnp.concatenate([c_sample, c_prompt, jnp.zeros((nrow - bs - bp, d), F32)], axis=0)
    mod_all = _adaln(c_all, w_ada.reshape(depth * 2, d, 3 * d), b_ada.reshape(depth * 2, 1, 3 * d))
    mod_s = [_ModSlab(mod_all, k, bs) for k in range(depth * 2)]
    mod_p = [mod_all[k, bs:bs + bp][:, None, :] for k in range(depth * 2)]

    w_in_t = jnp.swapaxes(w_in_mix, 1, 2)
    w_gk = jnp.pad(w_gla_gk[0], ((0, LANES - GLA_RANK), (0, 0))).astype(BF16)
    b_gk = b_gla_gk[0][None, :]
    w_out = w_out_mix[0].astype(BF16)
    gla_norm = gla_head_norm[0][:, None, :]
    ret_norm = ret_head_norm[0][:, None, :]
    gamma_log = jnp.log1p(-jnp.power(2.0, -5.0 - jnp.arange(RET_HEADS, dtype=F32)))
    ret_lg = jnp.broadcast_to(gamma_log[:, None, None], (RET_HEADS, 1, LANES))

    ng = s5_lam_re.shape[1]
    tm5 = min(TM_S5, tp)
    per_state = lambda a: jnp.tile(a, (1,) * (a.ndim - 1) + (LANES // S5_STATE,))
    bt = lambda a: per_state(jnp.swapaxes(a[0], 1, 2))
    pw_re, pw_im, bblk, cblk = _s5_discretize(
        per_state(s5_lam_re[0])[:, None, :], per_state(s5_lam_im[0])[:, None, :],
        s5_log_dt[0][:, None, None], bt(s5_b_re), bt(s5_b_im), per_state(s5_c_re[0]), per_state(s5_c_im[0]),
        tm5 // SUBLANES)
    tbl, lam2 = _s5_tables(pw_re[:, :, 0, :S5_STATE], pw_im[:, :, 0, :S5_STATE])
    dskip = s5_d[0][None, :]

    nxt = lambda mods, l, emit_h: (mods[2 * l + 1], norm_pre[l, 1][None]) if emit_h else None

    def layer0(x3, mods, tm, inproj, attn, mlp, emit_h):
        proj, glog = inproj(x3, mods[0])
        mg, mr, s_gla, s_ret = attn(proj, glog)
        x3, h3 = _outproj(x3, mods[0], norm_post[0, 0][None], mg, mr, w_out, tm, nxt(mods, 0, emit_h))
        return mlp(0, x3, h3, mods[1]), s_gla, s_ret

    def layer1_tail(x3, z3, mods, tm, glu_w, mlp, emit_h):
        x3, h3 = _glu(x3, mods[2], norm_post[1, 0][None], z3, glu_w[0], glu_w[1], tm, nxt(mods, 1, emit_h))
        return mlp(1, x3, h3, mods[3])

    cos_s, sin_s = _rope_tables(PAST_LEN + jnp.arange(ts, dtype=F32))

    def attn_s(proj, glog):
        tm_rows = lambda a: a.reshape(ts, bs, a.shape[-1])
        mg, mr, s_gla, s_ret = _attn_sample(tm_rows(proj), tm_rows(glog), gla_norm, ret_norm, cos_s, sin_s,
                                            ret_lg, state_gla[0], state_ret[0])
        flat = lambda a: a.reshape(1, ts * bs, a.shape[-1])
        return flat(mg), flat(mr), s_gla, s_ret

    xs3 = jnp.swapaxes(x_sample, 0, 1).reshape(1, ts * bs, d)

    def s5_s(x3, mod3):
        z2, s_re, s_im = _s5_seq(x3[0], mod3, norm_pre[1, 0][None], bblk, cblk, lam2, dskip,
                                 state_s5_re[0].reshape(bs, -1), state_s5_im[0].reshape(bs, -1))
        return z2[None], s_re, s_im

    w_up, w_dn, w_main = {}, {}, {}

    def inproj_s(x3, mod3):
        proj, glog, w_main[0], w_main["lr"] = _inproj_cast(x3, mod3, norm_pre[0, 0][None], w_in_t, 0,
                                                            w_gk, b_gk)
        return proj, glog

    def mlp_s(l, x3, h3, mod3):
        g_pre, g_post = norm_pre[l, 1][None], norm_post[l, 1][None]
        if l in w_up:
            return _mlp(x3, mod3, g_pre, g_post, w_up[l], w_dn[l])
        x3, w_up[l], w_dn[l] = _mlp_cast(x3, mod3, g_pre, g_post, w_mlp_up, w_mlp_down, l)
        return x3

    tm_s = ts * bs
    xs1, gla_s, ret_s = layer0(xs3, mod_s, tm_s, inproj_s, attn_s, mlp_s, False)

    cos_p, sin_p = _rope_tables(jnp.arange(tp, dtype=F32))
    zeros_att = jnp.zeros((bp, GLA_HEADS, HEAD_DK, HEAD_DV), F32)
    tm_p = min(TM_DENSE, tp)

    def attn_p(proj, glog):
        return _attn_prompt(proj, glog, gla_norm, ret_norm, cos_p, sin_p, ret_lg, zeros_att, zeros_att)

    def mlp_p(l, x3, h3, mod3):
        return _mlp_h(x3, h3, mod3, norm_post[l, 1][None], w_up[l], w_dn[l], tm_p)

    def inproj_p(x3, mod3):
        return _inproj(x3, mod3, norm_pre[0, 0][None], w_main[0], w_main["lr"], w_gk, b_gk, tm_p)

    xp1, gla_p, ret_p = layer0(x_prompt, mod_p, tm_p, inproj_p, attn_p, mlp_p, True)

    sl = tm5 // SUBLANES
    xpp = jnp.swapaxes(xp1.reshape(bp, tp // tm5, SUBLANES, sl, d), 2, 3).reshape(bp, tp, d)
    zpp, re_p, im_p, (w_up[1], w_dn[1], w_ga, w_gb) = _s5_rows(
        xpp, mod_p[2], norm_pre[1, 0][None], bblk, cblk, tbl, dskip, tm5,
        casts=((w_mlp_up, 1), (w_mlp_down, 1), (w_glu_a, 0), (w_glu_b, 0)))
    zp3 = jnp.swapaxes(zpp.reshape(bp, tp // tm5, sl, SUBLANES, d), 2, 3).reshape(bp, tp, d)

    zs3, re_s, im_s = s5_s(xs1, mod_s[2])
    y_s = layer1_tail(xs1, zs3, mod_s, tm_s, (w_ga, w_gb), mlp_s, False)
    y_s = jnp.swapaxes(y_s.reshape(ts, bs, d), 0, 1)
    y_p = layer1_tail(xp1, zp3, mod_p, tm_p, (w_ga, w_gb), mlp_p, True)

    st = lambda a, b_: a.reshape(1, b_, ng, S5_STATE)
    return (y_p, y_s, gla_p[None], gla_s[None], ret_p[None], ret_s[None],
            st(re_p, bp), st(re_s, bs), st(im_p, bp), st(im_s, bs))
```

```python
import functools
import math

import jax
import jax.numpy as jnp
import numpy as np
from jax import lax
from jax.experimental import pallas as pl
from jax.experimental.pallas import tpu as pltpu

F32 = jnp.float32
BF16 = jnp.bfloat16

EPS = 1e-6
LANES = 128
SUBLANES = 8
MIB = 1024 * 1024

GLA_HEADS = 4
RET_HEADS = 4
HEAD_DK = 128
HEAD_DV = 256
GLA_RANK = 16
GLA_LOGIT_NORM = 16.0
ROPE_BASE = 10000.0
PAST_LEN = 16384
S5_GROUP = 16
S5_STATE = 64
S5_GPB = 16
S5_UW = S5_GPB * S5_GROUP
ATT_CHUNK = 128
GLA_SUB = 16
TM_DENSE = 512
TM_S5 = 256
TN_INPROJ = 1024
TF_MLP = 1024


def _cparams(sem, vmem_mib):
    return pltpu.CompilerParams(dimension_semantics=sem, vmem_limit_bytes=vmem_mib * MIB)


def _dot(a, b):
    return jnp.dot(a, b, preferred_element_type=F32)


def _dot_nt(a, b):
    return lax.dot_general(a, b, (((1,), (1,)), ((), ())), preferred_element_type=F32)


def _rms(x, g):
    return x * lax.rsqrt(jnp.mean(x * x, axis=-1, keepdims=True) + EPS) * g


def _rows_affine(y, a, b=None):
    tm, d = y.shape
    r = a.shape[0]
    if r == 1 or r == tm:
        out = y * a
        return out if b is None else out + b
    y3 = y.reshape(tm // r, r, d)
    out = y3 * a[None]
    if b is not None:
        out = out + b[None]
    return out.reshape(tm, d)


def _norm_mod(x, g, mod_ref, d):
    return _rows_affine(_rms(x, g), 1.0 + mod_ref[:, d:2 * d], mod_ref[:, 0:d])


def _gated_residual(x, y, g, mod_ref, d):
    return x + _rows_affine(_rms(y, g), mod_ref[:, 2 * d:3 * d])


class _ModSlab:
    def __init__(self, arr, lead, rows):
        self.arr, self.lead = arr, lead
        self.shape = (1, rows, arr.shape[2])


def _arr(mod):
    return mod.arr if isinstance(mod, _ModSlab) else mod


def _mod_spec(mod, tm, width, ngrid):
    r = mod.shape[1]
    if isinstance(mod, _ModSlab):
        lead = mod.lead
        index = (lambda g, i: (lead, 0, 0)) if ngrid == 2 else (lambda g, i, j: (lead, 0, 0))
    else:
        index = (lambda g, i: (g, 0, 0)) if ngrid == 2 else (lambda g, i, j: (g, 0, 0))
    return pl.BlockSpec((None, r, width), index)


def _adaln_kernel(c_ref, w_ref, b_ref, o_ref):
    c = c_ref[...]
    sc = (c * jax.nn.sigmoid(c)).astype(BF16)
    o_ref[...] = _dot(sc, w_ref[...].astype(BF16)) + b_ref[...]


def _adaln(c_all, w_ada, b_ada, tn=1024):
    ls, d, n = w_ada.shape
    rows = c_all.shape[0]
    return pl.pallas_call(
        _adaln_kernel,
        grid=(ls, n // tn),
        in_specs=[
            pl.BlockSpec((rows, d), lambda l, j: (0, 0)),
            pl.BlockSpec((None, d, tn), lambda l, j: (l, 0, j)),
            pl.BlockSpec((None, 1, tn), lambda l, j: (l, 0, j)),
        ],
        out_specs=pl.BlockSpec((None, rows, tn), lambda l, j: (l, 0, j)),
        out_shape=jax.ShapeDtypeStruct((ls, rows, n), F32),
        compiler_params=_cparams(("parallel", "parallel"), 40),
        name="adaln",
    )(c_all, w_ada, b_ada)


def _log_sigmoid(x):
    return jnp.minimum(x, 0.0) - jnp.log1p(jnp.exp(-jnp.abs(x)))


def _gate_logits(hb, wlr_t, wgk_ref, bgk_ref, glog_ref):
    glr = _dot_nt(hb, wlr_t)
    logit = _dot(glr.astype(BF16), wgk_ref[...]) + bgk_ref[...]
    glog_ref[...] = _log_sigmoid(logit) * (1.0 / GLA_LOGIT_NORM)


def _inproj_kernel(x_ref, mod_ref, g_ref, w_ref, wlr_ref, wgk_ref, bgk_ref,
                   proj_ref, glog_ref, h_scr, *, d, tps):
    j = pl.program_id(2)

    @pl.when(j == 0)
    def _():
        hb = _norm_mod(x_ref[...], g_ref[...], mod_ref, d).astype(BF16)
        h_scr[...] = hb
        _gate_logits(hb, wlr_ref[...], wgk_ref, bgk_ref, glog_ref)

    tn = w_ref.shape[2]
    for k in range(tps):
        proj_ref[:, k * tn:(k + 1) * tn] = _dot(h_scr[...], w_ref[j * tps + k])


def _inproj_cast_kernel(x_ref, mod_ref, g_ref, wa_ref, wb_ref, wgk_ref, bgk_ref,
                        proj_ref, glog_ref, wout_ref, wlr_ref, h_scr, *, d, n_lo):
    j = pl.program_id(2)

    @pl.when(j == 0)
    def _():
        h_scr[...] = _norm_mod(x_ref[...], g_ref[...], mod_ref, d).astype(BF16)

    @pl.when(j < n_lo)
    def _():
        wout_ref[...] = jnp.transpose(wa_ref[...]).astype(BF16)

    @pl.when(j >= n_lo)
    def _():
        w = jnp.concatenate([wa_ref[GLA_RANK:, :], wb_ref[:GLA_RANK, :]], axis=0)
        wout_ref[...] = jnp.transpose(w).astype(BF16)

    @pl.when(j == n_lo)
    def _():
        wlr_t = _pad_rows(wa_ref[:GLA_RANK, :], LANES).astype(BF16)
        wlr_ref[...] = wlr_t
        _gate_logits(h_scr[...], wlr_t, wgk_ref, bgk_ref, glog_ref)

    proj_ref[...] = _dot(h_scr[...], wout_ref[...])


def _inproj_cast(x3, mod3, g_pre, w_raw_t, layer, w_gk, b_gk, tn=TN_INPROJ):
    gn, t, d = x3.shape
    assert gn == 1
    sec = (w_raw_t.shape[1] - GLA_RANK) // 2
    assert sec % tn == 0 and tn % LANES == 0
    nj = 2 * sec // tn
    n = nj * tn
    gkey = w_gk.shape[1]
    return pl.pallas_call(
        functools.partial(_inproj_cast_kernel, d=d, n_lo=sec // tn),
        grid=(1, 1, nj),
        in_specs=[
            pl.BlockSpec((None, t, d), lambda g, i, j: (0, 0, 0)),
            _mod_spec(mod3, t, 3 * d, 3),
            pl.BlockSpec((1, d), lambda g, i, j: (0, 0)),
            pl.BlockSpec((None, tn, d), lambda g, i, j: (layer, j, 0)),
            pl.BlockSpec((None, LANES, d), lambda g, i, j: (layer, (j + 1) * (tn // LANES), 0)),
            pl.BlockSpec((LANES, gkey), lambda g, i, j: (0, 0)),
            pl.BlockSpec((1, gkey), lambda g, i, j: (0, 0)),
        ],
        out_specs=[
            pl.BlockSpec((None, t, tn), lambda g, i, j: (0, 0, j)),
            pl.BlockSpec((None, t, gkey), lambda g, i, j: (0, 0, 0)),
            pl.BlockSpec((None, d, tn), lambda g, i, j: (j, 0, 0)),
            pl.BlockSpec((LANES, d), lambda g, i, j: (0, 0)),
        ],
        out_shape=[
            jax.ShapeDtypeStruct((1, t, n), F32),
            jax.ShapeDtypeStruct((1, t, gkey), F32),
            jax.ShapeDtypeStruct((nj, d, tn), BF16),
            jax.ShapeDtypeStruct((LANES, d), BF16),
        ],
        scratch_shapes=[pltpu.VMEM((t, d), BF16)],
        compiler_params=_cparams(("arbitrary", "arbitrary", "arbitrary"), 56),
        name="inproj_cast",
    )(x3, _arr(mod3),g_pre, w_raw_t, w_raw_t, w_gk, b_gk)


def _inproj(x3, mod3, g_pre, w_main, w_lr_t, w_gk, b_gk, tm, tps=3):
    gn, t, d = x3.shape
    nj, _, tn = w_main.shape
    n = nj * tn
    gkey = w_gk.shape[1]
    return pl.pallas_call(
        functools.partial(_inproj_kernel, d=d, tps=tps),
        grid=(gn, t // tm, nj // tps),
        in_specs=[
            pl.BlockSpec((None, tm, d), lambda g, i, j: (g, i, 0)),
            _mod_spec(mod3, tm, 3 * d, 3),
            pl.BlockSpec((1, d), lambda g, i, j: (0, 0)),
            pl.BlockSpec((nj, d, tn), lambda g, i, j: (0, 0, 0), pipeline_mode=pl.Buffered(1)),
            pl.BlockSpec((LANES, d), lambda g, i, j: (0, 0)),
            pl.BlockSpec((LANES, gkey), lambda g, i, j: (0, 0)),
            pl.BlockSpec((1, gkey), lambda g, i, j: (0, 0)),
        ],
        out_specs=[
            pl.BlockSpec((None, tm, tps * tn), lambda g, i, j: (g, i, j)),
            pl.BlockSpec((None, tm, gkey), lambda g, i, j: (g, i, 0)),
        ],
        out_shape=[
            jax.ShapeDtypeStruct((gn, t, n), F32),
            jax.ShapeDtypeStruct((gn, t, gkey), F32),
        ],
        scratch_shapes=[pltpu.VMEM((tm, d), BF16)],
        compiler_params=_cparams(("parallel", "parallel", "arbitrary"), 56),
        name="inproj",
    )(x3, _arr(mod3),g_pre, w_main, w_lr_t, w_gk, b_gk)


def _cumsum_rows(g):
    c = g.shape[0]
    row = lax.broadcasted_iota(jnp.int32, g.shape, 0)
    s = 1
    while s < c:
        g = g + jnp.where(row >= s, pltpu.roll(g, s, 0), 0.0)
        s *= 2
    return g


def _pad_rows(a, rows):
    if a.shape[0] == rows:
        return a
    return jnp.concatenate([a, jnp.zeros((rows - a.shape[0], a.shape[1]), a.dtype)], axis=0)


def _col_bcast(row, width):
    sq = jnp.transpose(jnp.broadcast_to(row, (LANES, LANES)))
    return jnp.concatenate([sq] * (width // LANES), axis=1)


def _gla_core(q, k, v, g, s, sub):
    cq = q.shape[0]
    ck = max(cq, LANES)
    b = _cumsum_rows(g)
    be = b - g
    bk = _pad_rows(b, ck)
    kp = _pad_rows(k, ck)
    vp = _pad_rows(v, ck).astype(BF16)
    rowj = lax.broadcasted_iota(jnp.int32, (ck, 1), 0)
    att_rows = []
    for blk in range(cq // sub):
        lo, hi = blk * sub, (blk + 1) * sub
        base = be[lo:lo + 1, :]
        qs = q[lo:hi] * jnp.exp(b[lo:hi] - base)
        ks = jnp.where(rowj < hi, kp * jnp.exp(base - bk), 0.0)
        att_rows.append(_dot_nt(qs.astype(BF16), ks.astype(BF16)))
    att = att_rows[0] if len(att_rows) == 1 else jnp.concatenate(att_rows, axis=0)
    ri = lax.broadcasted_iota(jnp.int32, (cq, ck), 0)
    cj = lax.broadcasted_iota(jnp.int32, (cq, ck), 1)
    att = jnp.where(ri >= cj, att, 0.0)
    o = _dot(att.astype(BF16), vp) + _dot((q * jnp.exp(b)).astype(BF16), s.astype(BF16))
    b_last = b[cq - 1:cq, :]
    k_out = kp * jnp.exp(b_last - bk)
    s_new = s * _col_bcast(jnp.exp(b_last), s.shape[1]) + _dot(jnp.transpose(k_out).astype(BF16), vp)
    return o, s_new


def _ret_core(q, k, v, s, lg, dmat, valid):
    cq = q.shape[0]
    ck = max(cq, LANES)
    kp = _pad_rows(k, ck)
    vp = _pad_rows(v, ck).astype(BF16)
    ti = lax.broadcasted_iota(jnp.int32, (cq, 1), 0).astype(F32)
    tj = lax.broadcasted_iota(jnp.int32, (ck, 1), 0).astype(F32)
    att = _dot_nt(q.astype(BF16), kp.astype(BF16)) * dmat
    q_in = q * jnp.exp((ti + 1.0) * lg)
    o = _dot(att.astype(BF16), vp) + _dot(q_in.astype(BF16), s.astype(BF16))
    k_out = kp * jnp.exp((float(valid - 1) - tj) * lg)
    s_new = s * jnp.exp(float(valid) * lg) + _dot(jnp.transpose(k_out).astype(BF16), vp)
    return o, s_new


def _decay_matrix(cq, ck, lg):
    ri = lax.broadcasted_iota(jnp.int32, (cq, ck), 0)
    cj = lax.broadcasted_iota(jnp.int32, (cq, ck), 1)
    diff = (ri - cj).astype(F32)
    return jnp.where(ri >= cj, jnp.exp(diff * lg), 0.0)


def _rope(x, cosf, sinf):
    return x * cosf + pltpu.roll(x, x.shape[1] // 2, 1) * sinf


def _silu(x):
    return x * jax.nn.sigmoid(x)


def _gla_finish(o, gate, gn):
    o = o * lax.rsqrt(jnp.mean(o * o, axis=-1, keepdims=True) + EPS) * gn
    return (o * _silu(gate)).astype(BF16)


def _ret_finish(o, gate, gn):
    oc = o - jnp.mean(o, axis=-1, keepdims=True)
    oc = oc * lax.rsqrt(jnp.mean(oc * oc, axis=-1, keepdims=True) + EPS) * gn
    return (oc * _silu(gate)).astype(BF16)


def _head(ref, h, width):
    return ref[:, h * width:(h + 1) * width]


def _attn_prompt_kernel(gq_ref, gk_ref, gv_ref, gg_ref, gl_ref, ggn_ref, gs0_ref,
                        rq_ref, rk_ref, rv_ref, rg_ref, cos_ref, sin_ref, lg_ref, rgn_ref, rs0_ref,
                        go_ref, gs_ref, ro_ref, rs_ref, d_scr):
    @pl.when(pl.program_id(1) == 0)
    def _():
        gs_ref[...] = gs0_ref[...]
        rs_ref[...] = rs0_ref[...]
        for h in range(d_scr.shape[0]):
            d_scr[h] = _decay_matrix(d_scr.shape[1], d_scr.shape[2], lg_ref[h][:, 0:1])

    cosf, sinf = cos_ref[...], sin_ref[...]
    for h in range(gs_ref.shape[0]):
        vcols = slice(h * HEAD_DV, (h + 1) * HEAD_DV)
        q = _head(gq_ref, h, HEAD_DK) * (HEAD_DK ** -0.5)
        o, s_new = _gla_core(q, _head(gk_ref, h, HEAD_DK), _head(gv_ref, h, HEAD_DV),
                             _head(gl_ref, h, HEAD_DK), gs_ref[h], GLA_SUB)
        gs_ref[h] = s_new
        go_ref[:, vcols] = _gla_finish(o, _head(gg_ref, h, HEAD_DV), ggn_ref[h])
        q = _rope(_head(rq_ref, h, HEAD_DK), cosf, sinf)
        k = _rope(_head(rk_ref, h, HEAD_DK), cosf, sinf) * (HEAD_DK ** -0.5)
        o, s_new = _ret_core(q, k, _head(rv_ref, h, HEAD_DV), rs_ref[h], lg_ref[h][:, 0:1], d_scr[h],
                             q.shape[0])
        rs_ref[h] = s_new
        ro_ref[:, vcols] = _ret_finish(o, _head(rg_ref, h, HEAD_DV), rgn_ref[h])


def _attn_prompt(proj, glog, gla_norm, ret_norm, cosf, sinf, ret_lg, s0_gla, s0_ret):
    bsz, t, _ = proj.shape
    c = ATT_CHUNK
    nh = GLA_HEADS
    kw, vw = nh * HEAD_DK, nh * HEAD_DV
    kspec = lambda blk: pl.BlockSpec((None, c, kw), lambda b, i, blk=blk: (b, i, blk))
    vspec = lambda blk: pl.BlockSpec((None, c, vw), lambda b, i, blk=blk: (b, i, blk))
    hspec = pl.BlockSpec((nh, 1, HEAD_DV), lambda b, i: (0, 0, 0))
    sspec = pl.BlockSpec((None, nh, HEAD_DK, HEAD_DV), lambda b, i: (b, 0, 0, 0))
    ospec = pl.BlockSpec((None, c, vw), lambda b, i: (b, i, 0))
    tspec = pl.BlockSpec((c, HEAD_DK), lambda b, i: (i, 0))
    o_shape = jax.ShapeDtypeStruct((bsz, t, vw), BF16)
    s_shape = jax.ShapeDtypeStruct((bsz, nh, HEAD_DK, HEAD_DV), F32)
    mg, s_gla, mr, s_ret = pl.pallas_call(
        _attn_prompt_kernel,
        grid=(bsz, t // c),
        in_specs=[kspec(0), kspec(1), vspec(1), vspec(2), kspec(0), hspec, sspec,
                  kspec(6), kspec(7), vspec(4), vspec(5), tspec, tspec,
                  pl.BlockSpec((nh, 1, LANES), lambda b, i: (0, 0, 0)), hspec, sspec],
        out_specs=[ospec, sspec, ospec, sspec],
        out_shape=[o_shape, s_shape, o_shape, s_shape],
        scratch_shapes=[pltpu.VMEM((nh, c, c), F32)],
        compiler_params=_cparams(("parallel", "arbitrary"), 32),
        name="attn_prompt",
    )(proj, proj, proj, proj, glog, gla_norm, s0_gla,
      proj, proj, proj, proj, cosf, sinf, ret_lg, ret_norm, s0_ret)
    return mg, mr, s_gla, s_ret


def _seq_rows(ref):
    ts, bb, w = ref.shape
    return ref[...].reshape(ts * bb, w)


def _seq_masks(n, bb):
    r = lax.broadcasted_iota(jnp.int32, (n, n), 0)
    c = lax.broadcasted_iota(jnp.int32, (n, n), 1)
    return (r % bb == c % bb) & (r >= c), (r - c).astype(F32) * (1.0 / bb)


def _seq_state_terms(q_in, k_out, v, s0, bb):
    n, dk = q_in.shape
    rown = lax.broadcasted_iota(jnp.int32, (n, 1), 0) % bb
    q_bd = jnp.concatenate([jnp.where(rown == j, q_in, 0.0) for j in range(bb)], axis=1)
    o_inter = _dot(q_bd.astype(BF16), s0.astype(BF16))
    k_t = jnp.transpose(_pad_rows(k_out, LANES))
    coln = lax.broadcasted_iota(jnp.int32, (1, LANES), 1) % bb
    k_bd = jnp.concatenate([jnp.where(coln == j, k_t, 0.0) for j in range(bb)], axis=0)
    ds = _dot(k_bd.astype(BF16), _pad_rows(v, LANES).astype(BF16))
    return o_inter, ds


def _gla_sample_kernel(q_ref, k_ref, v_ref, gg_ref, gl_ref, gn_ref, s0_ref, o_ref, s_ref):
    ts, bb, dk = q_ref.shape
    dv = v_ref.shape[2]
    n = ts * bb
    q = _seq_rows(q_ref) * (dk ** -0.5)
    k, v, g = _seq_rows(k_ref), _seq_rows(v_ref), _seq_rows(gl_ref)
    steps = [g[0:bb]]
    for t in range(1, ts):
        steps.append(steps[-1] + g[t * bb:(t + 1) * bb])
    b = jnp.concatenate(steps, axis=0)
    b_last = steps[-1]
    q_in = q * jnp.exp(b)
    mask, _ = _seq_masks(n, bb)
    att = jnp.where(mask, _dot_nt(q_in.astype(BF16), (k * jnp.exp(-b)).astype(BF16)), 0.0)
    k_out = k * jnp.exp(jnp.concatenate([b_last] * ts, axis=0) - b)
    s0 = s0_ref[...].reshape(bb * dk, dv)
    o_inter, ds = _seq_state_terms(q_in, k_out, v, s0, bb)
    o = _dot(att.astype(BF16), v.astype(BF16)) + o_inter
    e_last = jnp.exp(b_last)
    dec = jnp.concatenate([_col_bcast(e_last[j:j + 1, :], dv) for j in range(bb)], axis=0)
    s_ref[...] = (s0 * dec + ds).reshape(bb, dk, dv)
    o_ref[...] = _gla_finish(o, _seq_rows(gg_ref), gn_ref[...]).reshape(ts, bb, dv)


def _ret_sample_kernel(q_ref, k_ref, v_ref, rg_ref, cos_ref, sin_ref, lg_ref, gn_ref, s0_ref,
                       o_ref, s_ref):
    ts, bb, dk = q_ref.shape
    dv = v_ref.shape[2]
    n = ts * bb
    lg = lg_ref[:, 0:1]
    rows = lambda tab: jnp.concatenate(
        [jnp.broadcast_to(tab[t:t + 1, :], (bb, dk)) for t in range(ts)], axis=0)
    cosf, sinf = rows(cos_ref[...]), rows(sin_ref[...])
    q = _rope(_seq_rows(q_ref), cosf, sinf)
    k = _rope(_seq_rows(k_ref), cosf, sinf) * (dk ** -0.5)
    v = _seq_rows(v_ref)
    mask, dt = _seq_masks(n, bb)
    att = _dot_nt(q.astype(BF16), k.astype(BF16)) * jnp.where(mask, jnp.exp(dt * lg), 0.0)
    tt = (lax.broadcasted_iota(jnp.int32, (n, 1), 0) // bb).astype(F32)
    q_in = q * jnp.exp((tt + 1.0) * lg)
    k_out = k * jnp.exp((float(ts - 1) - tt) * lg)
    s0 = s0_ref[...].reshape(bb * dk, dv)
    o_inter, ds = _seq_state_terms(q_in, k_out, v, s0, bb)
    o = _dot(att.astype(BF16), v.astype(BF16)) + o_inter
    s_ref[...] = (s0 * jnp.exp(float(ts) * lg) + ds).reshape(bb, dk, dv)
    o_ref[...] = _ret_finish(o, _seq_rows(rg_ref), gn_ref[...]).reshape(ts, bb, dv)


def _attn_sample(proj, glog, gla_norm, ret_norm, cosf, sinf, ret_lg, s0_gla, s0_ret, bb=16):
    ts, bsz, _ = proj.shape
    grid = (bsz // bb, GLA_HEADS)
    nk = GLA_HEADS
    kspec = lambda off: pl.BlockSpec((ts, bb, HEAD_DK), lambda i, h, off=off: (0, i, off + h))
    vspec = lambda off: pl.BlockSpec((ts, bb, HEAD_DV), lambda i, h, off=off: (0, i, off + h))
    hspec = pl.BlockSpec((None, 1, HEAD_DV), lambda i, h: (h, 0, 0))
    sspec = pl.BlockSpec((bb, None, HEAD_DK, HEAD_DV), lambda i, h: (i, h, 0, 0))
    ospec = pl.BlockSpec((ts, bb, HEAD_DV), lambda i, h: (0, i, h))
    out_shape = [
        jax.ShapeDtypeStruct((ts, bsz, GLA_HEADS * HEAD_DV), BF16),
        jax.ShapeDtypeStruct((bsz, GLA_HEADS, HEAD_DK, HEAD_DV), F32),
    ]
    params = _cparams(("parallel", "parallel"), 40)
    mg, s_gla = pl.pallas_call(
        _gla_sample_kernel,
        grid=grid,
        in_specs=[kspec(0), kspec(nk), vspec(nk), vspec(2 * nk),
                  pl.BlockSpec((ts, bb, HEAD_DK), lambda i, h: (0, i, h)),
                  hspec, sspec],
        out_specs=[ospec, sspec],
        out_shape=out_shape,
        compiler_params=params,
        name="gla_sample",
    )(proj, proj, proj, proj, glog, gla_norm, s0_gla)
    tspec = pl.BlockSpec((ts, HEAD_DK), lambda i, h: (0, 0))
    mr, s_ret = pl.pallas_call(
        _ret_sample_kernel,
        grid=grid,
        in_specs=[kspec(6 * nk), kspec(7 * nk), vspec(4 * nk), vspec(5 * nk),
                  tspec, tspec,
                  pl.BlockSpec((None, 1, LANES), lambda i, h: (h, 0, 0)),
                  hspec, sspec],
        out_specs=[ospec, sspec],
        out_shape=out_shape,
        compiler_params=params,
        name="ret_sample",
    )(proj, proj, proj, proj, cosf, sinf, ret_lg, ret_norm, s0_ret)
    return mg, mr, s_gla, s_ret


ROW_SPLITS = 2


def _row_parts(tm):
    step = tm // ROW_SPLITS
    return [slice(k * step, (k + 1) * step) for k in range(ROW_SPLITS)]


def _residual_out(x_ref, y, g_ref, mod_ref, outs, d, rows):
    x_new = _gated_residual(x_ref[rows, :], y, g_ref[...], mod_ref, d)
    if len(outs) == 1:
        outs[0][rows, :] = x_new
    else:
        modn_ref, gn_ref, o_ref, h_ref = outs
        o_ref[rows, :] = x_new
        h_ref[rows, :] = _norm_mod(x_new, gn_ref[...], modn_ref, d).astype(BF16)


def _next_specs(nxt, tm, d):
    if nxt is None:
        return [], [], [], []
    mod_next, g_next, shape = nxt
    return ([_mod_spec(mod_next, tm, 3 * d, 2), pl.BlockSpec((1, d), lambda g, i: (0, 0))],
            [pl.BlockSpec((None, tm, d), lambda g, i: (g, i, 0))],
            [jax.ShapeDtypeStruct(shape, BF16)], [_arr(mod_next), g_next])


def _outproj_kernel(x_ref, mod_ref, g_ref, mg_ref, mr_ref, wo_ref, *outs, d):
    half = mg_ref.shape[1]
    for rows in _row_parts(x_ref.shape[0]):
        y = _dot(mg_ref[rows, :], wo_ref[0:half, :]) + _dot(mr_ref[rows, :], wo_ref[half:2 * half, :])
        _residual_out(x_ref, y, g_ref, mod_ref, outs, d, rows)


def _outproj(x3, mod3, g_post, mg, mr, w_out, tm, nxt=None):
    gn, t, d = x3.shape
    half = mg.shape[2]
    n_in, n_out, n_shape, n_ops = _next_specs(None if nxt is None else (*nxt, x3.shape), tm, d)
    res = pl.pallas_call(
        functools.partial(_outproj_kernel, d=d),
        grid=(gn, t // tm),
        in_specs=[
            pl.BlockSpec((None, tm, d), lambda g, i: (g, i, 0)),
            _mod_spec(mod3, tm, 3 * d, 2),
            pl.BlockSpec((1, d), lambda g, i: (0, 0)),
            pl.BlockSpec((None, tm, half), lambda g, i: (g, i, 0)),
            pl.BlockSpec((None, tm, half), lambda g, i: (g, i, 0)),
            pl.BlockSpec((2 * half, d), lambda g, i: (0, 0)),
        ] + n_in,
        out_specs=[pl.BlockSpec((None, tm, d), lambda g, i: (g, i, 0))] + n_out,
        out_shape=[jax.ShapeDtypeStruct((gn, t, d), F32)] + n_shape,
        compiler_params=_cparams(("parallel", "parallel"), 56),
        name="outproj",
    )(x3, _arr(mod3),g_post, mg, mr, w_out, *n_ops)
    return res if nxt is not None else (res[0], None)


def _mlp_kernel(x_ref, mod_ref, gpre_ref, gpost_ref, wup_ref, wdn_ref, o_ref, h_scr, acc_scr, *, d):
    j = pl.program_id(2)

    @pl.when(j == 0)
    def _():
        h_scr[...] = _norm_mod(x_ref[...], gpre_ref[...], mod_ref, d).astype(BF16)
        acc_scr[...] = jnp.zeros_like(acc_scr)

    u = jnp.maximum(_dot(h_scr[...], wup_ref[...]), 0.0)
    acc_scr[...] += _dot((u * u).astype(BF16), wdn_ref[...])

    @pl.when(j == pl.num_programs(2) - 1)
    def _():
        o_ref[...] = _gated_residual(x_ref[...], acc_scr[...], gpost_ref[...], mod_ref, d)


def _mlp_cast_kernel(x_ref, mod_ref, gpre_ref, gpost_ref, wup_ref, wdn_ref,
                     o_ref, wupb_ref, wdnb_ref, h_scr, acc_scr, *, d):
    wupb_ref[...] = wup_ref[...].astype(BF16)
    wdnb_ref[...] = wdn_ref[...].astype(BF16)
    _mlp_kernel(x_ref, mod_ref, gpre_ref, gpost_ref, wupb_ref, wdnb_ref, o_ref, h_scr, acc_scr, d=d)


def _mlp_cast(x3, mod3, g_pre, g_post, w_up_all, w_down_all, layer, tf=512):
    gn, t, d = x3.shape
    assert gn == 1
    f = w_up_all.shape[2]
    return pl.pallas_call(
        functools.partial(_mlp_cast_kernel, d=d),
        grid=(1, 1, f // tf),
        in_specs=[
            pl.BlockSpec((None, t, d), lambda g, i, j: (0, 0, 0)),
            _mod_spec(mod3, t, 3 * d, 3),
            pl.BlockSpec((1, d), lambda g, i, j: (0, 0)),
            pl.BlockSpec((1, d), lambda g, i, j: (0, 0)),
            pl.BlockSpec((None, d, tf), lambda g, i, j: (layer, 0, j)),
            pl.BlockSpec((None, tf, d), lambda g, i, j: (layer, j, 0)),
        ],
        out_specs=[
            pl.BlockSpec((None, t, d), lambda g, i, j: (0, 0, 0)),
            pl.BlockSpec((d, tf), lambda g, i, j: (0, j)),
            pl.BlockSpec((tf, d), lambda g, i, j: (j, 0)),
        ],
        out_shape=[
            jax.ShapeDtypeStruct((1, t, d), F32),
            jax.ShapeDtypeStruct((d, f), BF16),
            jax.ShapeDtypeStruct((f, d), BF16),
        ],
        scratch_shapes=[pltpu.VMEM((t, d), BF16), pltpu.VMEM((t, d), F32)],
        compiler_params=_cparams(("arbitrary", "arbitrary", "arbitrary"), 56),
        name="mlp_cast",
    )(x3, _arr(mod3),g_pre, g_post, w_up_all, w_down_all)


def _mlp(x3, mod3, g_pre, g_post, w_up, w_down, tf=TF_MLP):
    gn, t, d = x3.shape
    assert gn == 1
    f = w_up.shape[1]
    return pl.pallas_call(
        functools.partial(_mlp_kernel, d=d),
        grid=(1, 1, f // tf),
        in_specs=[
            pl.BlockSpec((None, t, d), lambda g, i, j: (0, 0, 0)),
            _mod_spec(mod3, t, 3 * d, 3),
            pl.BlockSpec((1, d), lambda g, i, j: (0, 0)),
            pl.BlockSpec((1, d), lambda g, i, j: (0, 0)),
            pl.BlockSpec((d, tf), lambda g, i, j: (0, j)),
            pl.BlockSpec((tf, d), lambda g, i, j: (j, 0)),
        ],
        out_specs=pl.BlockSpec((None, t, d), lambda g, i, j: (0, 0, 0)),
        out_shape=jax.ShapeDtypeStruct((1, t, d), F32),
        scratch_shapes=[pltpu.VMEM((t, d), BF16), pltpu.VMEM((t, d), F32)],
        compiler_params=_cparams(("arbitrary", "arbitrary", "arbitrary"), 56),
        name="mlp_rows",
    )(x3, _arr(mod3),g_pre, g_post, w_up, w_down)


def _mlp_h_kernel(x_ref, h_ref, mod_ref, gpost_ref, wup_ref, wdn_ref, o_ref, acc_scr, *, d):
    j = pl.program_id(2)
    last = pl.num_programs(2) - 1

    def hidden(rows):
        u = jnp.maximum(_dot(h_ref[rows, :], wup_ref[...]), 0.0)
        return _dot((u * u).astype(BF16), wdn_ref[...])

    @pl.when(j == 0)
    def _():
        acc_scr[...] = hidden(slice(None))

    @pl.when((j > 0) & (j < last))
    def _():
        acc_scr[...] += hidden(slice(None))

    @pl.when(j == last)
    def _():
        for rows in _row_parts(x_ref.shape[0]):
            y = acc_scr[rows, :] + hidden(rows)
            o_ref[rows, :] = _gated_residual(x_ref[rows, :], y, gpost_ref[...], mod_ref, d)


def _mlp_h(x3, h3, mod3, g_post, w_up, w_down, tm, tf=TF_MLP):
    gn, t, d = x3.shape
    f = w_up.shape[1]
    assert f // tf >= 2
    return pl.pallas_call(
        functools.partial(_mlp_h_kernel, d=d),
        grid=(gn, t // tm, f // tf),
        in_specs=[
            pl.BlockSpec((None, tm, d), lambda g, i, j: (g, i, 0)),
            pl.BlockSpec((None, tm, d), lambda g, i, j: (g, i, 0)),
            _mod_spec(mod3, tm, 3 * d, 3),
            pl.BlockSpec((1, d), lambda g, i, j: (0, 0)),
            pl.BlockSpec((d, tf), lambda g, i, j: (0, j)),
            pl.BlockSpec((tf, d), lambda g, i, j: (j, 0)),
        ],
        out_specs=pl.BlockSpec((None, tm, d), lambda g, i, j: (g, i, 0)),
        out_shape=jax.ShapeDtypeStruct((gn, t, d), F32),
        scratch_shapes=[pltpu.VMEM((tm, d), F32)],
        compiler_params=_cparams(("parallel", "parallel", "arbitrary"), 56),
        name="mlp",
    )(x3, h3, _arr(mod3), g_post, w_up, w_down)


def _diag_blocks(rows, reps, row_shift, col_shift):
    tiled = jnp.concatenate([rows] * reps, axis=1)
    rg = lax.broadcasted_iota(jnp.int32, tiled.shape, 0) >> row_shift
    cg = lax.broadcasted_iota(jnp.int32, tiled.shape, 1) >> col_shift
    return jnp.where(rg == cg, tiled, 0.0)


def _s5_disc_kernel(lr_ref, li_ref, ldt_ref, br_ref, bi_ref, cr_ref, ci_ref,
                    pwr_ref, pwi_ref, bblk_ref, cblk_ref, *, seg_len):
    lr, li = lr_ref[...], li_ref[...]
    dt = jnp.exp(ldt_ref[...])
    mag = jnp.exp(lr * dt)
    lb_re, lb_im = mag * jnp.cos(li * dt), mag * jnp.sin(li * dt)
    nr, ni = lb_re - 1.0, lb_im
    den = lr * lr + li * li
    f_re = (nr * lr + ni * li) / den
    f_im = (ni * lr - nr * li) / den
    br, bi = br_ref[...], bi_ref[...]
    bb_re = f_re * br - f_im * bi
    bb_im = f_re * bi + f_im * br
    nrow, ncol2 = bblk_ref.shape
    rs, cs = S5_GROUP.bit_length() - 1, S5_STATE.bit_length() - 1
    blocks = lambda a: _diag_blocks(a.reshape(nrow, LANES), ncol2 // 2 // LANES, rs, cs)
    bblk_ref[...] = jnp.concatenate([blocks(bb_re), blocks(bb_im)], axis=1).astype(BF16)
    cblk_ref[...] = jnp.concatenate([blocks(cr_ref[...]), -blocks(ci_ref[...])], axis=1).astype(BF16)
    pwr_ref[0] = lb_re
    pwi_ref[0] = lb_im
    qr, qi = None, None
    sr, si = lb_re, lb_im
    e = seg_len
    while e:
        if e & 1:
            qr, qi = (sr, si) if qr is None else (qr * sr - qi * si, qr * si + qi * sr)
        e >>= 1
        if e:
            sr, si = sr * sr - si * si, 2.0 * sr * si
    pr, pi = qr, qi
    for n in range(SUBLANES):
        pwr_ref[1 + n] = pr
        pwi_ref[1 + n] = pi
        pr, pi = pr * qr - pi * qi, pr * qi + pi * qr


def _s5_discretize(lam_re, lam_im, log_dt, bt_re, bt_im, c_re, c_im, seg_len):
    g = lam_re.shape[0]
    ncb = g // S5_GPB
    assert S5_GROUP & (S5_GROUP - 1) == 0 and S5_STATE & (S5_STATE - 1) == 0
    grp = lambda *dims: pl.BlockSpec((S5_GPB,) + dims, lambda c: (c, 0, 0))
    pw_spec = pl.BlockSpec((1 + SUBLANES, S5_GPB, 1, LANES), lambda c: (0, c, 0, 0))
    return pl.pallas_call(
        functools.partial(_s5_disc_kernel, seg_len=seg_len),
        grid=(ncb,),
        in_specs=[grp(1, LANES), grp(1, LANES), grp(1, 1), grp(S5_GROUP, LANES), grp(S5_GROUP, LANES),
                  grp(S5_GROUP, LANES), grp(S5_GROUP, LANES)],
        out_specs=[pw_spec, pw_spec,
                   pl.BlockSpec((None, S5_UW, 2 * S5_GPB * S5_STATE), lambda c: (c, 0, 0)),
                   pl.BlockSpec((None, S5_UW, 2 * S5_GPB * S5_STATE), lambda c: (c, 0, 0))],
        out_shape=[
            jax.ShapeDtypeStruct((1 + SUBLANES, g, 1, LANES), F32),
            jax.ShapeDtypeStruct((1 + SUBLANES, g, 1, LANES), F32),
            jax.ShapeDtypeStruct((ncb, S5_UW, 2 * S5_GPB * S5_STATE), BF16),
            jax.ShapeDtypeStruct((ncb, S5_UW, 2 * S5_GPB * S5_STATE), BF16),
        ],
        compiler_params=_cparams(("arbitrary",), 32),
        name="s5_discretize",
    )(lam_re, lam_im, log_dt, bt_re, bt_im, c_re, c_im)


def _gelu_tanh(x):
    c0 = math.sqrt(2.0 / math.pi)
    return x * (0.5 * (1.0 + jnp.tanh(c0 * (x + 0.044715 * (x * x * x)))))


def _cmul_add(ar, ai, xr, xi, yr, yi):
    return yr + ar * xr - ai * xi, yi + ar * xi + ai * xr


def _s5_seq_kernel(x_ref, mod_ref, gpre_ref, bblk_ref, cblk_ref, lam_ref, dskip_ref, s0r_ref, s0i_ref,
                   z_ref, sr_ref, si_ref, h_scr, xr_scr, xi_scr, *, d):
    cb = pl.program_id(0)
    ncb, tm, uw = h_scr.shape
    cw = xr_scr.shape[1]
    seg = s0r_ref.shape[0]

    @pl.when(cb == 0)
    def _():
        h = _norm_mod(x_ref[...], gpre_ref[...], mod_ref, d)
        for c in range(ncb):
            h_scr[c] = h[:, c * uw:(c + 1) * uw]

    u = h_scr[cb]
    bu = _dot(u.astype(BF16), bblk_ref[...])
    xr_scr[...] = bu[:, 0:cw]
    xi_scr[...] = bu[:, cw:2 * cw]
    car_r, car_i = s0r_ref[...], s0i_ref[...]
    l_r, l_i = lam_ref[0:1, :], lam_ref[1:2, :]
    for t in range(tm // seg):
        rows = slice(t * seg, (t + 1) * seg)
        car_r, car_i = _cmul_add(l_r, l_i, car_r, car_i, xr_scr[rows, :], xi_scr[rows, :])
        xr_scr[rows, :] = car_r
        xi_scr[rows, :] = car_i
    sr_ref[...] = car_r
    si_ref[...] = car_i
    xs = jnp.concatenate([xr_scr[...].astype(BF16), xi_scr[...].astype(BF16)], axis=1)
    y = _dot_nt(xs, cblk_ref[...]) + dskip_ref[...] * u
    z_ref[...] = _gelu_tanh(y).astype(BF16)


def _s5_seq(x2, mod3, g_pre, bblk, cblk, lam2, dskip, s0_re, s0_im):
    tm, d = x2.shape
    seg, nst = s0_re.shape
    ncb, uw, cw2 = bblk.shape
    cw = cw2 // 2
    sspec = pl.BlockSpec((seg, cw), lambda c: (0, c))
    return pl.pallas_call(
        functools.partial(_s5_seq_kernel, d=d),
        grid=(ncb,),
        in_specs=[
            pl.BlockSpec((tm, d), lambda c: (0, 0)),
            pl.BlockSpec((None, seg, 3 * d), lambda c, lead=getattr(mod3, "lead", 0): (lead, 0, 0)),
            pl.BlockSpec((1, d), lambda c: (0, 0)),
            pl.BlockSpec((None, uw, cw2), lambda c: (c, 0, 0)),
            pl.BlockSpec((None, uw, cw2), lambda c: (c, 0, 0)),
            pl.BlockSpec((2, cw), lambda c: (0, c)),
            pl.BlockSpec((1, uw), lambda c: (0, c)),
            sspec, sspec,
        ],
        out_specs=[pl.BlockSpec((tm, uw), lambda c: (0, c)), sspec, sspec],
        out_shape=[
            jax.ShapeDtypeStruct((tm, d), BF16),
            jax.ShapeDtypeStruct((seg, nst), F32),
            jax.ShapeDtypeStruct((seg, nst), F32),
        ],
        scratch_shapes=[
            pltpu.VMEM((ncb, tm, uw), F32),
            pltpu.VMEM((tm, cw), F32),
            pltpu.VMEM((tm, cw), F32),
        ],
        compiler_params=_cparams(("arbitrary",), 48),
        name="s5_seq",
    )(x2, _arr(mod3), g_pre, bblk, cblk, lam2, dskip, s0_re, s0_im)


def _s5_rows_kernel(*refs, d, ncast):
    x_ref, mod_ref, gpre_ref, bblk_ref, cblk_ref, tbl_ref, dskip_ref = refs[:7]
    cast_in = refs[7:7 + ncast]
    z_ref, sr_ref, si_ref = refs[7 + ncast:10 + ncast]
    cast_out = refs[10 + ncast:10 + 2 * ncast]
    h_scr, xr_scr, xi_scr, cr_scr, ci_scr = refs[10 + 2 * ncast:]
    for src, dst in zip(cast_in, cast_out):
        dst[...] = src[...].astype(BF16)

    ncol, tm, _ = xr_scr.shape
    ncb, uw, _ = bblk_ref.shape
    sl = tm // SUBLANES

    @pl.when(pl.program_id(1) == 0)
    def _():
        cr_scr[...] = jnp.zeros_like(cr_scr)
        ci_scr[...] = jnp.zeros_like(ci_scr)

    h_scr[...] = _norm_mod(x_ref[...], gpre_ref[...], mod_ref, d)
    row0 = lax.broadcasted_iota(jnp.int32, (ncol, SUBLANES, LANES), 1) == 0

    for c in range(ncb):
        us = slice(c * uw, (c + 1) * uw)
        cols = slice(c * ncol, (c + 1) * ncol)
        u = h_scr[:, us]
        bu = _dot(u.astype(BF16), bblk_ref[c])
        for j in range(ncol):
            xr_scr[j] = bu[:, j * LANES:(j + 1) * LANES]
            xi_scr[j] = bu[:, (ncol + j) * LANES:(ncol + j + 1) * LANES]
        l_r, l_i = tbl_ref[0, cols], tbl_ref[1, cols]

        def local(i, s):
            rows = pl.ds(pl.multiple_of(i * SUBLANES, SUBLANES), SUBLANES)
            return _cmul_add(l_r, l_i, s[0], s[1], xr_scr[:, rows, :], xi_scr[:, rows, :])

        zero = jnp.zeros((ncol, SUBLANES, LANES), F32)
        g_r, g_i = lax.fori_loop(0, sl, local, (zero, zero), unroll=True)
        for n in range(3):
            g_r, g_i = _cmul_add(tbl_ref[2 + 2 * n, cols], tbl_ref[3 + 2 * n, cols],
                                 pltpu.roll(g_r, 1 << n, 1), pltpu.roll(g_i, 1 << n, 1), g_r, g_i)
        car_r, car_i = cr_scr[cols], ci_scr[cols]
        g_r, g_i = _cmul_add(tbl_ref[8, cols], tbl_ref[9, cols], car_r, car_i, g_r, g_i)
        in_r = jnp.where(row0, car_r, pltpu.roll(g_r, 1, 1))
        in_i = jnp.where(row0, car_i, pltpu.roll(g_i, 1, 1))

        def full(i, s):
            rows = pl.ds(pl.multiple_of(i * SUBLANES, SUBLANES), SUBLANES)
            s_r, s_i = _cmul_add(l_r, l_i, s[0], s[1], xr_scr[:, rows, :], xi_scr[:, rows, :])
            xr_scr[:, rows, :] = s_r
            xi_scr[:, rows, :] = s_i
            return s_r, s_i

        e_r, e_i = lax.fori_loop(0, sl, full, (in_r, in_i), unroll=True)
        cr_scr[cols] = e_r[:, SUBLANES - 1:SUBLANES, :]
        ci_scr[cols] = e_i[:, SUBLANES - 1:SUBLANES, :]
        xs = jnp.concatenate([xr_scr[j].astype(BF16) for j in range(ncol)]
                             + [xi_scr[j].astype(BF16) for j in range(ncol)], axis=1)
        y = _dot_nt(xs, cblk_ref[c]) + dskip_ref[:, us] * u
        z_ref[:, us] = _gelu_tanh(y).astype(BF16)

    sr_ref[...] = cr_scr[...]
    si_ref[...] = ci_scr[...]


def _s5_rows(x3, mod3, g_pre, bblk, cblk, tbl, dskip, tm, casts=()):
    gn, t, d = x3.shape
    ncb, uw, cw2 = bblk.shape
    ncol = cw2 // 2 // LANES
    nct = ncb * ncol
    nt = t // tm
    nstep = gn * nt
    const = lambda shape: pl.BlockSpec(shape, lambda g, i: (0,) * len(shape), pipeline_mode=pl.Buffered(1))
    ospec = pl.BlockSpec((None, None, nct, 1, LANES), lambda g, i: (g, i, 0, 0, 0))
    c_in, c_out, c_shape = [], [], []
    for w, layer in casts:
        _, rows, cols = w.shape
        slab = rows // nstep
        assert slab * nstep == rows and slab % (2 * SUBLANES) == 0
        c_in.append(pl.BlockSpec((None, slab, cols), lambda g, i, layer=layer: (layer, g * nt + i, 0)))
        c_out.append(pl.BlockSpec((slab, cols), lambda g, i: (g * nt + i, 0)))
        c_shape.append(jax.ShapeDtypeStruct((rows, cols), BF16))
    res = pl.pallas_call(
        functools.partial(_s5_rows_kernel, d=d, ncast=len(casts)),
        grid=(gn, nt),
        in_specs=[
            pl.BlockSpec((None, tm, d), lambda g, i: (g, i, 0)),
            _mod_spec(mod3, tm, 3 * d, 2),
            pl.BlockSpec((1, d), lambda g, i: (0, 0)),
            const(bblk.shape), const(cblk.shape), const(tbl.shape), const(dskip.shape),
        ] + c_in,
        out_specs=[pl.BlockSpec((None, tm, d), lambda g, i: (g, i, 0)), ospec, ospec] + c_out,
        out_shape=[
            jax.ShapeDtypeStruct((gn, t, d), BF16),
            jax.ShapeDtypeStruct((gn, nt, nct, 1, LANES), F32),
            jax.ShapeDtypeStruct((gn, nt, nct, 1, LANES), F32),
        ] + c_shape,
        scratch_shapes=[
            pltpu.VMEM((tm, d), F32),
            pltpu.VMEM((ncol, tm, LANES), F32),
            pltpu.VMEM((ncol, tm, LANES), F32),
            pltpu.VMEM((nct, 1, LANES), F32),
            pltpu.VMEM((nct, 1, LANES), F32),
        ],
        compiler_params=_cparams(("arbitrary", "arbitrary"), 56),
        name="s5_rows",
    )(x3, _arr(mod3),g_pre, bblk, cblk, tbl, dskip, *[w for w, _ in casts])
    z3, s_re, s_im = res[:3]
    last = lambda s: s[:, nt - 1].reshape(gn, 1, nct * LANES)
    return z3, last(s_re), last(s_im), list(res[3:])


def _glu_kernel(x_ref, mod_ref, g_ref, z_ref, wa_ref, wb_ref, *outs, d):
    for rows in _row_parts(x_ref.shape[0]):
        z = z_ref[rows, :]
        y = _dot(z, wa_ref[...]) * jax.nn.sigmoid(_dot(z, wb_ref[...]))
        _residual_out(x_ref, y, g_ref, mod_ref, outs, d, rows)


def _glu(x3, mod3, g_post, z3, w_a, w_b, tm, nxt=None):
    gn, t, d = x3.shape
    wspec = pl.BlockSpec((d, d), lambda g, i: (0, 0), pipeline_mode=pl.Buffered(1))
    n_in, n_out, n_shape, n_ops = _next_specs(None if nxt is None else (*nxt, x3.shape), tm, d)
    res = pl.pallas_call(
        functools.partial(_glu_kernel, d=d),
        grid=(gn, t // tm),
        in_specs=[
            pl.BlockSpec((None, tm, d), lambda g, i: (g, i, 0)),
            _mod_spec(mod3, tm, 3 * d, 2),
            pl.BlockSpec((1, d), lambda g, i: (0, 0)),
            pl.BlockSpec((None, tm, d), lambda g, i: (g, i, 0)),
            wspec, wspec,
        ] + n_in,
        out_specs=[pl.BlockSpec((None, tm, d), lambda g, i: (g, i, 0))] + n_out,
        out_shape=[jax.ShapeDtypeStruct((gn, t, d), F32)] + n_shape,
        compiler_params=_cparams(("parallel", "parallel"), 56),
        name="glu",
    )(x3, _arr(mod3),g_post, z3, w_a, w_b, *n_ops)
    return res if nxt is not None else (res[0], None)


def _rope_tables(pos):
    half = HEAD_DK // 2
    inv = ROPE_BASE ** (-jnp.arange(half, dtype=F32) / half)
    ang = pos.astype(F32)[:, None] * inv[None, :]
    cos, sin = jnp.cos(ang), jnp.sin(ang)
    return jnp.concatenate([cos, cos], axis=-1), jnp.concatenate([-sin, sin], axis=-1)


def _s5_tables(pw_re, pw_im):
    n = pw_re.shape[0]
    flat_r = pw_re.reshape(n, -1)
    flat_i = pw_im.reshape(n, -1)
    row = jnp.arange(SUBLANES)[:, None]
    tabs = [jnp.broadcast_to(flat_r[0], (SUBLANES, flat_r.shape[1])),
            jnp.broadcast_to(flat_i[0], (SUBLANES, flat_i.shape[1]))]
    for s in (1, 2, 4):
        mask = row >= s
        tabs.append(jnp.where(mask, flat_r[s][None, :], 0.0))
        tabs.append(jnp.where(mask, flat_i[s][None, :], 0.0))
    tabs += [flat_r[1:], flat_i[1:]]
    tbl = jnp.stack(tabs)
    tbl = tbl.reshape(tbl.shape[0], SUBLANES, -1, LANES).transpose(0, 2, 1, 3)
    return tbl, jnp.stack([flat_r[0], flat_i[0]])


def kernel(x_prompt, x_sample, state_gla, state_ret, state_s5_re, state_s5_im, c_prompt, c_sample,
           w_ada, b_ada, norm_pre, norm_post, w_in_mix, w_gla_gk, b_gla_gk, gla_head_norm,
           ret_head_norm, w_out_mix, s5_lam_re, s5_lam_im, s5_log_dt, s5_b_re, s5_b_im,
           s5_c_re, s5_c_im, s5_d, w_glu_a, w_glu_b, w_mlp_up, w_mlp_down):
    bp, tp, d = x_prompt.shape
    bs, ts, _ = x_sample.shape
    depth = w_ada.shape[0]

    nrow = -(-(bs + bp) // SUBLANES) * SUBLANES
    c_all = jnp.concatenate([c_sample, c_prompt, jnp.zeros((nrow - bs - bp, d), F32)], axis=0)
    mod_all = _adaln(c_all, w_ada.reshape(depth * 2, d, 3 * d), b_ada.reshape(depth * 2, 1, 3 * d))
    mod_s = [_ModSlab(mod_all, k, bs) for k in range(depth * 2)]
    mod_p = [mod_all[k, bs:bs + bp][:, None, :] for k in range(depth * 2)]

    w_in_t = jnp.swapaxes(w_in_mix, 1, 2)
    w_gk = jnp.pad(w_gla_gk[0], ((0, LANES - GLA_RANK), (0, 0))).astype(BF16)
    b_gk = b_gla_gk[0][None, :]
    w_out = w_out_mix[0].astype(BF16)
    gla_norm = gla_head_norm[0][:, None, :]
    ret_norm = ret_head_norm[0][:, None, :]
    gamma_log = jnp.log1p(-jnp.power(2.0, -5.0 - jnp.arange(RET_HEADS, dtype=F32)))
    ret_lg = jnp.broadcast_to(gamma_log[:, None, None], (RET_HEADS, 1, LANES))

    ng = s5_lam_re.shape[1]
    tm5 = min(TM_S5, tp)
    per_state = lambda a: jnp.tile(a, (1,) * (a.ndim - 1) + (LANES // S5_STATE,))
    bt = lambda a: per_state(jnp.swapaxes(a[0], 1, 2))
    pw_re, pw_im, bblk, cblk = _s5_discretize(
        per_state(s5_lam_re[0])[:, None, :], per_state(s5_lam_im[0])[:, None, :],
        s5_log_dt[0][:, None, None], bt(s5_b_re), bt(s5_b_im), per_state(s5_c_re[0]), per_state(s5_c_im[0]),
        tm5 // SUBLANES)
    tbl, lam2 = _s5_tables(pw_re[:, :, 0, :S5_STATE], pw_im[:, :, 0, :S5_STATE])
    dskip = s5_d[0][None, :]

    nxt = lambda mods, l, emit_h: (mods[2 * l + 1], norm_pre[l, 1][None]) if emit_h else None

    def layer0(x3, mods, tm, inproj, attn, mlp, emit_h):
        proj, glog = inproj(x3, mods[0])
        mg, mr, s_gla, s_ret = attn(proj, glog)
        x3, h3 = _outproj(x3, mods[0], norm_post[0, 0][None], mg, mr, w_out, tm, nxt(mods, 0, emit_h))
        return mlp(0, x3, h3, mods[1]), s_gla, s_ret

    def layer1_tail(x3, z3, mods, tm, glu_w, mlp, emit_h):
        x3, h3 = _glu(x3, mods[2], norm_post[1, 0][None], z3, glu_w[0], glu_w[1], tm, nxt(mods, 1, emit_h))
        return mlp(1, x3, h3, mods[3])

    cos_s, sin_s = _rope_tables(PAST_LEN + jnp.arange(ts, dtype=F32))

    def attn_s(proj, glog):
        tm_rows = lambda a: a.reshape(ts, bs, a.shape[-1])
        mg, mr, s_gla, s_ret = _attn_sample(tm_rows(proj), tm_rows(glog), gla_norm, ret_norm, cos_s, sin_s,
                                            ret_lg, state_gla[0], state_ret[0])
        flat = lambda a: a.reshape(1, ts * bs, a.shape[-1])
        return flat(mg), flat(mr), s_gla, s_ret

    xs3 = jnp.swapaxes(x_sample, 0, 1).reshape(1, ts * bs, d)

    def s5_s(x3, mod3):
        z2, s_re, s_im = _s5_seq(x3[0], mod3, norm_pre[1, 0][None], bblk, cblk, lam2, dskip,
                                 state_s5_re[0].reshape(bs, -1), state_s5_im[0].reshape(bs, -1))
        return z2[None], s_re, s_im

    w_up, w_dn, w_main = {}, {}, {}

    def inproj_s(x3, mod3):
        proj, glog, w_main[0], w_main["lr"] = _inproj_cast(x3, mod3, norm_pre[0, 0][None], w_in_t, 0,
                                                            w_gk, b_gk)
        return proj, glog

    def mlp_s(l, x3, h3, mod3):
        g_pre, g_post = norm_pre[l, 1][None], norm_post[l, 1][None]
        if l in w_up:
            return _mlp(x3, mod3, g_pre, g_post, w_up[l], w_dn[l])
        x3, w_up[l], w_dn[l] = _mlp_cast(x3, mod3, g_pre, g_post, w_mlp_up, w_mlp_down, l)
        return x3

    tm_s = ts * bs
    xs1, gla_s, ret_s = layer0(xs3, mod_s, tm_s, inproj_s, attn_s, mlp_s, False)

    cos_p, sin_p = _rope_tables(jnp.arange(tp, dtype=F32))
    zeros_att = jnp.zeros((bp, GLA_HEADS, HEAD_DK, HEAD_DV), F32)
    tm_p = min(TM_DENSE, tp)

    def attn_p(proj, glog):
        return _attn_prompt(proj, glog, gla_norm, ret_norm, cos_p, sin_p, ret_lg, zeros_att, zeros_att)

    def mlp_p(l, x3, h3, mod3):
        return _mlp_h(x3, h3, mod3, norm_post[l, 1][None], w_up[l], w_dn[l], tm_p)

    def inproj_p(x3, mod3):
        return _inproj(x3, mod3, norm_pre[0, 0][None], w_main[0], w_main["lr"], w_gk, b_gk, tm_p)

    xp1, gla_p, ret_p = layer0(x_prompt, mod_p, tm_p, inproj_p, attn_p, mlp_p, True)

    sl = tm5 // SUBLANES
    xpp = jnp.swapaxes(xp1.reshape(bp, tp // tm5, SUBLANES, sl, d), 2, 3).reshape(bp, tp, d)
    zpp, re_p, im_p, (w_up[1], w_dn[1], w_ga, w_gb) = _s5_rows(
        xpp, mod_p[2], norm_pre[1, 0][None], bblk, cblk, tbl, dskip, tm5,
        casts=((w_mlp_up, 1), (w_mlp_down, 1), (w_glu_a, 0), (w_glu_b, 0)))
    zp3 = jnp.swapaxes(zpp.reshape(bp, tp // tm5, sl, SUBLANES, d), 2, 3).reshape(bp, tp, d)

    zs3, re_s, im_s = s5_s(xs1, mod_s[2])
    y_s = layer1_tail(xs1, zs3, mod_s, tm_s, (w_ga, w_gb), mlp_s, False)
    y_s = jnp.swapaxes(y_s.reshape(ts, bs, d), 0, 1)
    y_p = layer1_tail(xp1, zp3, mod_p, tm_p, (w_ga, w_gb), mlp_p, True)

    st = lambda a, b_: a.reshape(1, b_, ng, S5_STATE)
    return (y_p, y_s, gla_p[None], gla_s[None], ret_p[None], ret_s[None],
            st(re_p, bp), st(re_s, bs), st(im_p, bp), st(im_s, bs))
```

```python
import functools
import math

import jax
import jax.numpy as jnp
import numpy as np
from jax import lax
from jax.experimental import pallas as pl
from jax.experimental.pallas import tpu as pltpu

F32 = jnp.float32
BF16 = jnp.bfloat16

EPS = 1e-6
LANES = 128
SUBLANES = 8
MIB = 1024 * 1024

GLA_HEADS = 4
RET_HEADS = 4
HEAD_DK = 128
HEAD_DV = 256
GLA_RANK = 16
GLA_LOGIT_NORM = 16.0
ROPE_BASE = 10000.0
PAST_LEN = 16384
S5_GROUP = 16
S5_STATE = 64
S5_GPB = 16
S5_UW = S5_GPB * S5_GROUP
ATT_CHUNK = 128
GLA_SUB = 16
TM_DENSE = 512
TM_S5 = 256
TN_INPROJ = 1024
TF_MLP = 1024


def _cparams(sem, vmem_mib):
    return pltpu.CompilerParams(dimension_semantics=sem, vmem_limit_bytes=vmem_mib * MIB)


def _dot(a, b):
    return jnp.dot(a, b, preferred_element_type=F32)


def _dot_nt(a, b):
    return lax.dot_general(a, b, (((1,), (1,)), ((), ())), preferred_element_type=F32)


def _rms(x, g):
    return x * lax.rsqrt(jnp.mean(x * x, axis=-1, keepdims=True) + EPS) * g


def _rows_affine(y, a, b=None):
    tm, d = y.shape
    r = a.shape[0]
    if r == 1 or r == tm:
        out = y * a
        return out if b is None else out + b
    y3 = y.reshape(tm // r, r, d)
    out = y3 * a[None]
    if b is not None:
        out = out + b[None]
    return out.reshape(tm, d)


def _norm_mod(x, g, mod_ref, d):
    return _rows_affine(_rms(x, g), 1.0 + mod_ref[:, d:2 * d], mod_ref[:, 0:d])


def _gated_residual(x, y, g, mod_ref, d):
    return x + _rows_affine(_rms(y, g), mod_ref[:, 2 * d:3 * d])


class _ModSlab:
    def __init__(self, arr, lead, rows):
        self.arr, self.lead = arr, lead
        self.shape = (1, rows, arr.shape[2])


def _arr(mod):
    return mod.arr if isinstance(mod, _ModSlab) else mod


def _mod_spec(mod, tm, width, ngrid):
    r = mod.shape[1]
    if isinstance(mod, _ModSlab):
        lead = mod.lead
        index = (lambda g, i: (lead, 0, 0)) if ngrid == 2 else (lambda g, i, j: (lead, 0, 0))
    else:
        index = (lambda g, i: (g, 0, 0)) if ngrid == 2 else (lambda g, i, j: (g, 0, 0))
    return pl.BlockSpec((None, r, width), index)


def _adaln_kernel(c_ref, w_ref, b_ref, o_ref):
    c = c_ref[...]
    sc = (c * jax.nn.sigmoid(c)).astype(BF16)
    o_ref[...] = _dot(sc, w_ref[...].astype(BF16)) + b_ref[...]


def _adaln(c_all, w_ada, b_ada, tn=1024):
    ls, d, n = w_ada.shape
    rows = c_all.shape[0]
    return pl.pallas_call(
        _adaln_kernel,
        grid=(ls, n // tn),
        in_specs=[
            pl.BlockSpec((rows, d), lambda l, j: (0, 0)),
            pl.BlockSpec((None, d, tn), lambda l, j: (l, 0, j)),
            pl.BlockSpec((None, 1, tn), lambda l, j: (l, 0, j)),
        ],
        out_specs=pl.BlockSpec((None, rows, tn), lambda l, j: (l, 0, j)),
        out_shape=jax.ShapeDtypeStruct((ls, rows, n), F32),
        compiler_params=_cparams(("parallel", "parallel"), 40),
        name="adaln",
    )(c_all, w_ada, b_ada)


def _log_sigmoid(x):
    return jnp.minimum(x, 0.0) - jnp.log1p(jnp.exp(-jnp.abs(x)))


def _gate_logits(hb, wlr_t, wgk_ref, bgk_ref, glog_ref):
    glr = _dot_nt(hb, wlr_t)
    logit = _dot(glr.astype(BF16), wgk_ref[...]) + bgk_ref[...]
    glog_ref[...] = _log_sigmoid(logit) * (1.0 / GLA_LOGIT_NORM)


def _inproj_kernel(x_ref, mod_ref, g_ref, w_ref, wlr_ref, wgk_ref, bgk_ref,
                   proj_ref, glog_ref, h_scr, *, d, tps):
    j = pl.program_id(2)

    @pl.when(j == 0)
    def _():
        hb = _norm_mod(x_ref[...], g_ref[...], mod_ref, d).astype(BF16)
        h_scr[...] = hb
        _gate_logits(hb, wlr_ref[...], wgk_ref, bgk_ref, glog_ref)

    tn = w_ref.shape[2]
    for k in range(tps):
        proj_ref[:, k * tn:(k + 1) * tn] = _dot(h_scr[...], w_ref[j * tps + k])


def _inproj_cast_kernel(x_ref, mod_ref, g_ref, wa_ref, wb_ref, wgk_ref, bgk_ref,
                        proj_ref, glog_ref, wout_ref, wlr_ref, h_scr, *, d, n_lo):
    j = pl.program_id(2)

    @pl.when(j == 0)
    def _():
        h_scr[...] = _norm_mod(x_ref[...], g_ref[...], mod_ref, d).astype(BF16)

    @pl.when(j < n_lo)
    def _():
        wout_ref[...] = jnp.transpose(wa_ref[...]).astype(BF16)

    @pl.when(j >= n_lo)
    def _():
        w = jnp.concatenate([wa_ref[GLA_RANK:, :], wb_ref[:GLA_RANK, :]], axis=0)
        wout_ref[...] = jnp.transpose(w).astype(BF16)

    @pl.when(j == n_lo)
    def _():
        wlr_t = _pad_rows(wa_ref[:GLA_RANK, :], LANES).astype(BF16)
        wlr_ref[...] = wlr_t
        _gate_logits(h_scr[...], wlr_t, wgk_ref, bgk_ref, glog_ref)

    proj_ref[...] = _dot(h_scr[...], wout_ref[...])


def _inproj_cast(x3, mod3, g_pre, w_raw_t, layer, w_gk, b_gk, tn=TN_INPROJ):
    gn, t, d = x3.shape
    assert gn == 1
    sec = (w_raw_t.shape[1] - GLA_RANK) // 2
    assert sec % tn == 0 and tn % LANES == 0
    nj = 2 * sec // tn
    n = nj * tn
    gkey = w_gk.shape[1]
    return pl.pallas_call(
        functools.partial(_inproj_cast_kernel, d=d, n_lo=sec // tn),
        grid=(1, 1, nj),
        in_specs=[
            pl.BlockSpec((None, t, d), lambda g, i, j: (0, 0, 0)),
            _mod_spec(mod3, t, 3 * d, 3),
            pl.BlockSpec((1, d), lambda g, i, j: (0, 0)),
            pl.BlockSpec((None, tn, d), lambda g, i, j: (layer, j, 0)),
            pl.BlockSpec((None, LANES, d), lambda g, i, j: (layer, (j + 1) * (tn // LANES), 0)),
            pl.BlockSpec((LANES, gkey), lambda g, i, j: (0, 0)),
            pl.BlockSpec((1, gkey), lambda g, i, j: (0, 0)),
        ],
        out_specs=[
            pl.BlockSpec((None, t, tn), lambda g, i, j: (0, 0, j)),
            pl.BlockSpec((None, t, gkey), lambda g, i, j: (0, 0, 0)),
            pl.BlockSpec((None, d, tn), lambda g, i, j: (j, 0, 0)),
            pl.BlockSpec((LANES, d), lambda g, i, j: (0, 0)),
        ],
        out_shape=[
            jax.ShapeDtypeStruct((1, t, n), F32),
            jax.ShapeDtypeStruct((1, t, gkey), F32),
            jax.ShapeDtypeStruct((nj, d, tn), BF16),
            jax.ShapeDtypeStruct((LANES, d), BF16),
        ],
        scratch_shapes=[pltpu.VMEM((t, d), BF16)],
        compiler_params=_cparams(("arbitrary", "arbitrary", "arbitrary"), 56),
        name="inproj_cast",
    )(x3, _arr(mod3),g_pre, w_raw_t, w_raw_t, w_gk, b_gk)


def _inproj(x3, mod3, g_pre, w_main, w_lr_t, w_gk, b_gk, tm, tps=3):
    gn, t, d = x3.shape
    nj, _, tn = w_main.shape
    n = nj * tn
    gkey = w_gk.shape[1]
    return pl.pallas_call(
        functools.partial(_inproj_kernel, d=d, tps=tps),
        grid=(gn, t // tm, nj // tps),
        in_specs=[
            pl.BlockSpec((None, tm, d), lambda g, i, j: (g, i, 0)),
            _mod_spec(mod3, tm, 3 * d, 3),
            pl.BlockSpec((1, d), lambda g, i, j: (0, 0)),
            pl.BlockSpec((nj, d, tn), lambda g, i, j: (0, 0, 0), pipeline_mode=pl.Buffered(1)),
            pl.BlockSpec((LANES, d), lambda g, i, j: (0, 0)),
            pl.BlockSpec((LANES, gkey), lambda g, i, j: (0, 0)),
            pl.BlockSpec((1, gkey), lambda g, i, j: (0, 0)),
        ],
        out_specs=[
            pl.BlockSpec((None, tm, tps * tn), lambda g, i, j: (g, i, j)),
            pl.BlockSpec((None, tm, gkey), lambda g, i, j: (g, i, 0)),
        ],
        out_shape=[
            jax.ShapeDtypeStruct((gn, t, n), F32),
            jax.ShapeDtypeStruct((gn, t, gkey), F32),
        ],
        scratch_shapes=[pltpu.VMEM((tm, d), BF16)],
        compiler_params=_cparams(("parallel", "parallel", "arbitrary"), 56),
        name="inproj",
    )(x3, _arr(mod3),g_pre, w_main, w_lr_t, w_gk, b_gk)


def _cumsum_rows(g):
    c = g.shape[0]
    row = lax.broadcasted_iota(jnp.int32, g.shape, 0)
    s = 1
    while s < c:
        g = g + jnp.where(row >= s, pltpu.roll(g, s, 0), 0.0)
        s *= 2
    return g


def _pad_rows(a, rows):
    if a.shape[0] == rows:
        return a
    return jnp.concatenate([a, jnp.zeros((rows - a.shape[0], a.shape[1]), a.dtype)], axis=0)


def _col_bcast(row, width):
    sq = jnp.transpose(jnp.broadcast_to(row, (LANES, LANES)))
    return jnp.concatenate([sq] * (width // LANES), axis=1)


def _gla_core(q, k, v, g, s, sub):
    cq = q.shape[0]
    ck = max(cq, LANES)
    b = _cumsum_rows(g)
    be = b - g
    bk = _pad_rows(b, ck)
    kp = _pad_rows(k, ck)
    vp = _pad_rows(v, ck).astype(BF16)
    rowj = lax.broadcasted_iota(jnp.int32, (ck, 1), 0)
    att_rows = []
    for blk in range(cq // sub):
        lo, hi = blk * sub, (blk + 1) * sub
        base = be[lo:lo + 1, :]
        qs = q[lo:hi] * jnp.exp(b[lo:hi] - base)
        ks = jnp.where(rowj < hi, kp * jnp.exp(base - bk), 0.0)
        att_rows.append(_dot_nt(qs.astype(BF16), ks.astype(BF16)))
    att = att_rows[0] if len(att_rows) == 1 else jnp.concatenate(att_rows, axis=0)
    ri = lax.broadcasted_iota(jnp.int32, (cq, ck), 0)
    cj = lax.broadcasted_iota(jnp.int32, (cq, ck), 1)
    att = jnp.where(ri >= cj, att, 0.0)
    o = _dot(att.astype(BF16), vp) + _dot((q * jnp.exp(b)).astype(BF16), s.astype(BF16))
    b_last = b[cq - 1:cq, :]
    k_out = kp * jnp.exp(b_last - bk)
    s_new = s * _col_bcast(jnp.exp(b_last), s.shape[1]) + _dot(jnp.transpose(k_out).astype(BF16), vp)
    return o, s_new


def _ret_core(q, k, v, s, lg, dmat, valid):
    cq = q.shape[0]
    ck = max(cq, LANES)
    kp = _pad_rows(k, ck)
    vp = _pad_rows(v, ck).astype(BF16)
    ti = lax.broadcasted_iota(jnp.int32, (cq, 1), 0).astype(F32)
    tj = lax.broadcasted_iota(jnp.int32, (ck, 1), 0).astype(F32)
    att = _dot_nt(q.astype(BF16), kp.astype(BF16)) * dmat
    q_in = q * jnp.exp((ti + 1.0) * lg)
    o = _dot(att.astype(BF16), vp) + _dot(q_in.astype(BF16), s.astype(BF16))
    k_out = kp * jnp.exp((float(valid - 1) - tj) * lg)
    s_new = s * jnp.exp(float(valid) * lg) + _dot(jnp.transpose(k_out).astype(BF16), vp)
    return o, s_new


def _decay_matrix(cq, ck, lg):
    ri = lax.broadcasted_iota(jnp.int32, (cq, ck), 0)
    cj = lax.broadcasted_iota(jnp.int32, (cq, ck), 1)
    diff = (ri - cj).astype(F32)
    return jnp.where(ri >= cj, jnp.exp(diff * lg), 0.0)


def _rope(x, cosf, sinf):
    return x * cosf + pltpu.roll(x, x.shape[1] // 2, 1) * sinf


def _silu(x):
    return x * jax.nn.sigmoid(x)


def _gla_finish(o, gate, gn):
    o = o * lax.rsqrt(jnp.mean(o * o, axis=-1, keepdims=True) + EPS) * gn
    return (o * _silu(gate)).astype(BF16)


def _ret_finish(o, gate, gn):
    oc = o - jnp.mean(o, axis=-1, keepdims=True)
    oc = oc * lax.rsqrt(jnp.mean(oc * oc, axis=-1, keepdims=True) + EPS) * gn
    return (oc * _silu(gate)).astype(BF16)


def _head(ref, h, width):
    return ref[:, h * width:(h + 1) * width]


def _attn_prompt_kernel(gq_ref, gk_ref, gv_ref, gg_ref, gl_ref, ggn_ref, gs0_ref,
                        rq_ref, rk_ref, rv_ref, rg_ref, cos_ref, sin_ref, lg_ref, rgn_ref, rs0_ref,
                        go_ref, gs_ref, ro_ref, rs_ref, d_scr):
    @pl.when(pl.program_id(1) == 0)
    def _():
        gs_ref[...] = gs0_ref[...]
        rs_ref[...] = rs0_ref[...]
        for h in range(d_scr.shape[0]):
            d_scr[h] = _decay_matrix(d_scr.shape[1], d_scr.shape[2], lg_ref[h][:, 0:1])

    cosf, sinf = cos_ref[...], sin_ref[...]
    for h in range(gs_ref.shape[0]):
        vcols = slice(h * HEAD_DV, (h + 1) * HEAD_DV)
        q = _head(gq_ref, h, HEAD_DK) * (HEAD_DK ** -0.5)
        o, s_new = _gla_core(q, _head(gk_ref, h, HEAD_DK), _head(gv_ref, h, HEAD_DV),
                             _head(gl_ref, h, HEAD_DK), gs_ref[h], GLA_SUB)
        gs_ref[h] = s_new
        go_ref[:, vcols] = _gla_finish(o, _head(gg_ref, h, HEAD_DV), ggn_ref[h])
        q = _rope(_head(rq_ref, h, HEAD_DK), cosf, sinf)
        k = _rope(_head(rk_ref, h, HEAD_DK), cosf, sinf) * (HEAD_DK ** -0.5)
        o, s_new = _ret_core(q, k, _head(rv_ref, h, HEAD_DV), rs_ref[h], lg_ref[h][:, 0:1], d_scr[h],
                             q.shape[0])
        rs_ref[h] = s_new
        ro_ref[:, vcols] = _ret_finish(o, _head(rg_ref, h, HEAD_DV), rgn_ref[h])


def _attn_prompt(proj, glog, gla_norm, ret_norm, cosf, sinf, ret_lg, s0_gla, s0_ret):
    bsz, t, _ = proj.shape
    c = ATT_CHUNK
    nh = GLA_HEADS
    kw, vw = nh * HEAD_DK, nh * HEAD_DV
    kspec = lambda blk: pl.BlockSpec((None, c, kw), lambda b, i, blk=blk: (b, i, blk))
    vspec = lambda blk: pl.BlockSpec((None, c, vw), lambda b, i, blk=blk: (b, i, blk))
    hspec = pl.BlockSpec((nh, 1, HEAD_DV), lambda b, i: (0, 0, 0))
    sspec = pl.BlockSpec((None, nh, HEAD_DK, HEAD_DV), lambda b, i: (b, 0, 0, 0))
    ospec = pl.BlockSpec((None, c, vw), lambda b, i: (b, i, 0))
    tspec = pl.BlockSpec((c, HEAD_DK), lambda b, i: (i, 0))
    o_shape = jax.ShapeDtypeStruct((bsz, t, vw), BF16)
    s_shape = jax.ShapeDtypeStruct((bsz, nh, HEAD_DK, HEAD_DV), F32)
    mg, s_gla, mr, s_ret = pl.pallas_call(
        _attn_prompt_kernel,
        grid=(bsz, t // c),
        in_specs=[kspec(0), kspec(1), vspec(1), vspec(2), kspec(0), hspec, sspec,
                  kspec(6), kspec(7), vspec(4), vspec(5), tspec, tspec,
                  pl.BlockSpec((nh, 1, LANES), lambda b, i: (0, 0, 0)), hspec, sspec],
        out_specs=[ospec, sspec, ospec, sspec],
        out_shape=[o_shape, s_shape, o_shape, s_shape],
        scratch_shapes=[pltpu.VMEM((nh, c, c), F32)],
        compiler_params=_cparams(("parallel", "arbitrary"), 32),
        name="attn_prompt",
    )(proj, proj, proj, proj, glog, gla_norm, s0_gla,
      proj, proj, proj, proj, cosf, sinf, ret_lg, ret_norm, s0_ret)
    return mg, mr, s_gla, s_ret


def _seq_rows(ref):
    ts, bb, w = ref.shape
    return ref[...].reshape(ts * bb, w)


def _seq_masks(n, bb):
    r = lax.broadcasted_iota(jnp.int32, (n, n), 0)
    c = lax.broadcasted_iota(jnp.int32, (n, n), 1)
    return (r % bb == c % bb) & (r >= c), (r - c).astype(F32) * (1.0 / bb)


def _seq_state_terms(q_in, k_out, v, s0, bb):
    n, dk = q_in.shape
    rown = lax.broadcasted_iota(jnp.int32, (n, 1), 0) % bb
    q_bd = jnp.concatenate([jnp.where(rown == j, q_in, 0.0) for j in range(bb)], axis=1)
    o_inter = _dot(q_bd.astype(BF16), s0.astype(BF16))
    k_t = jnp.transpose(_pad_rows(k_out, LANES))
    coln = lax.broadcasted_iota(jnp.int32, (1, LANES), 1) % bb
    k_bd = jnp.concatenate([jnp.where(coln == j, k_t, 0.0) for j in range(bb)], axis=0)
    ds = _dot(k_bd.astype(BF16), _pad_rows(v, LANES).astype(BF16))
    return o_inter, ds


def _gla_sample_kernel(q_ref, k_ref, v_ref, gg_ref, gl_ref, gn_ref, s0_ref, o_ref, s_ref):
    ts, bb, dk = q_ref.shape
    dv = v_ref.shape[2]
    n = ts * bb
    q = _seq_rows(q_ref) * (dk ** -0.5)
    k, v, g = _seq_rows(k_ref), _seq_rows(v_ref), _seq_rows(gl_ref)
    steps = [g[0:bb]]
    for t in range(1, ts):
        steps.append(steps[-1] + g[t * bb:(t + 1) * bb])
    b = jnp.concatenate(steps, axis=0)
    b_last = steps[-1]
    q_in = q * jnp.exp(b)
    mask, _ = _seq_masks(n, bb)
    att = jnp.where(mask, _dot_nt(q_in.astype(BF16), (k * jnp.exp(-b)).astype(BF16)), 0.0)
    k_out = k * jnp.exp(jnp.concatenate([b_last] * ts, axis=0) - b)
    s0 = s0_ref[...].reshape(bb * dk, dv)
    o_inter, ds = _seq_state_terms(q_in, k_out, v, s0, bb)
    o = _dot(att.astype(BF16), v.astype(BF16)) + o_inter
    e_last = jnp.exp(b_last)
    dec = jnp.concatenate([_col_bcast(e_last[j:j + 1, :], dv) for j in range(bb)], axis=0)
    s_ref[...] = (s0 * dec + ds).reshape(bb, dk, dv)
    o_ref[...] = _gla_finish(o, _seq_rows(gg_ref), gn_ref[...]).reshape(ts, bb, dv)


def _ret_sample_kernel(q_ref, k_ref, v_ref, rg_ref, cos_ref, sin_ref, lg_ref, gn_ref, s0_ref,
                       o_ref, s_ref):
    ts, bb, dk = q_ref.shape
    dv = v_ref.shape[2]
    n = ts * bb
    lg = lg_ref[:, 0:1]
    rows = lambda tab: jnp.concatenate(
        [jnp.broadcast_to(tab[t:t + 1, :], (bb, dk)) for t in range(ts)], axis=0)
    cosf, sinf = rows(cos_ref[...]), rows(sin_ref[...])
    q = _rope(_seq_rows(q_ref), cosf, sinf)
    k = _rope(_seq_rows(k_ref), cosf, sinf) * (dk ** -0.5)
    v = _seq_rows(v_ref)
    mask, dt = _seq_masks(n, bb)
    att = _dot_nt(q.astype(BF16), k.astype(BF16)) * jnp.where(mask, jnp.exp(dt * lg), 0.0)
    tt = (lax.broadcasted_iota(jnp.int32, (n, 1), 0) // bb).astype(F32)
    q_in = q * jnp.exp((tt + 1.0) * lg)
    k_out = k * jnp.exp((float(ts - 1) - tt) * lg)
    s0 = s0_ref[...].reshape(bb * dk, dv)
    o_inter, ds = _seq_state_terms(q_in, k_out, v, s0, bb)
    o = _dot(att.astype(BF16), v.astype(BF16)) + o_inter
    s_ref[...] = (s0 * jnp.exp(float(ts) * lg) + ds).reshape(bb, dk, dv)
    o_ref[...] = _ret_finish(o, _seq_rows(rg_ref), gn_ref[...]).reshape(ts, bb, dv)


def _attn_sample(proj, glog, gla_norm, ret_norm, cosf, sinf, ret_lg, s0_gla, s0_ret, bb=16):
    ts, bsz, _ = proj.shape
    grid = (bsz // bb, GLA_HEADS)
    nk = GLA_HEADS
    kspec = lambda off: pl.BlockSpec((ts, bb, HEAD_DK), lambda i, h, off=off: (0, i, off + h))
    vspec = lambda off: pl.BlockSpec((ts, bb, HEAD_DV), lambda i, h, off=off: (0, i, off + h))
    hspec = pl.BlockSpec((None, 1, HEAD_DV), lambda i, h: (h, 0, 0))
    sspec = pl.BlockSpec((bb, None, HEAD_DK, HEAD_DV), lambda i, h: (i, h, 0, 0))
    ospec = pl.BlockSpec((ts, bb, HEAD_DV), lambda i, h: (0, i, h))
    out_shape = [
        jax.ShapeDtypeStruct((ts, bsz, GLA_HEADS * HEAD_DV), BF16),
        jax.ShapeDtypeStruct((bsz, GLA_HEADS, HEAD_DK, HEAD_DV), F32),
    ]
    params = _cparams(("parallel", "parallel"), 40)
    mg, s_gla = pl.pallas_call(
        _gla_sample_kernel,
        grid=grid,
        in_specs=[kspec(0), kspec(nk), vspec(nk), vspec(2 * nk),
                  pl.BlockSpec((ts, bb, HEAD_DK), lambda i, h: (0, i, h)),
                  hspec, sspec],
        out_specs=[ospec, sspec],
        out_shape=out_shape,
        compiler_params=params,
        name="gla_sample",
    )(proj, proj, proj, proj, glog, gla_norm, s0_gla)
    tspec = pl.BlockSpec((ts, HEAD_DK), lambda i, h: (0, 0))
    mr, s_ret = pl.pallas_call(
        _ret_sample_kernel,
        grid=grid,
        in_specs=[kspec(6 * nk), kspec(7 * nk), vspec(4 * nk), vspec(5 * nk),
                  tspec, tspec,
                  pl.BlockSpec((None, 1, LANES), lambda i, h: (h, 0, 0)),
                  hspec, sspec],
        out_specs=[ospec, sspec],
        out_shape=out_shape,
        compiler_params=params,
        name="ret_sample",
    )(proj, proj, proj, proj, cosf, sinf, ret_lg, ret_norm, s0_ret)
    return mg, mr, s_gla, s_ret


ROW_SPLITS = 2


def _row_parts(tm):
    step = tm // ROW_SPLITS
    return [slice(k * step, (k + 1) * step) for k in range(ROW_SPLITS)]


def _residual_out(x_ref, y, g_ref, mod_ref, outs, d, rows):
    x_new = _gated_residual(x_ref[rows, :], y, g_ref[...], mod_ref, d)
    if len(outs) == 1:
        outs[0][rows, :] = x_new
    else:
        modn_ref, gn_ref, o_ref, h_ref = outs
        o_ref[rows, :] = x_new
        h_ref[rows, :] = _norm_mod(x_new, gn_ref[...], modn_ref, d).astype(BF16)


def _next_specs(nxt, tm, d):
    if nxt is None:
        return [], [], [], []
    mod_next, g_next, shape = nxt
    return ([_mod_spec(mod_next, tm, 3 * d, 2), pl.BlockSpec((1, d), lambda g, i: (0, 0))],
            [pl.BlockSpec((None, tm, d), lambda g, i: (g, i, 0))],
            [jax.ShapeDtypeStruct(shape, BF16)], [_arr(mod_next), g_next])


def _outproj_kernel(x_ref, mod_ref, g_ref, mg_ref, mr_ref, wo_ref, *outs, d):
    half = mg_ref.shape[1]
    for rows in _row_parts(x_ref.shape[0]):
        y = _dot(mg_ref[rows, :], wo_ref[0:half, :]) + _dot(mr_ref[rows, :], wo_ref[half:2 * half, :])
        _residual_out(x_ref, y, g_ref, mod_ref, outs, d, rows)


def _outproj(x3, mod3, g_post, mg, mr, w_out, tm, nxt=None):
    gn, t, d = x3.shape
    half = mg.shape[2]
    n_in, n_out, n_shape, n_ops = _next_specs(None if nxt is None else (*nxt, x3.shape), tm, d)
    res = pl.pallas_call(
        functools.partial(_outproj_kernel, d=d),
        grid=(gn, t // tm),
        in_specs=[
            pl.BlockSpec((None, tm, d), lambda g, i: (g, i, 0)),
            _mod_spec(mod3, tm, 3 * d, 2),
            pl.BlockSpec((1, d), lambda g, i: (0, 0)),
            pl.BlockSpec((None, tm, half), lambda g, i: (g, i, 0)),
            pl.BlockSpec((None, tm, half), lambda g, i: (g, i, 0)),
            pl.BlockSpec((2 * half, d), lambda g, i: (0, 0)),
        ] + n_in,
        out_specs=[pl.BlockSpec((None, tm, d), lambda g, i: (g, i, 0))] + n_out,
        out_shape=[jax.ShapeDtypeStruct((gn, t, d), F32)] + n_shape,
        compiler_params=_cparams(("parallel", "parallel"), 56),
        name="outproj",
    )(x3, _arr(mod3),g_post, mg, mr, w_out, *n_ops)
    return res if nxt is not None else (res[0], None)


def _mlp_kernel(x_ref, mod_ref, gpre_ref, gpost_ref, wup_ref, wdn_ref, o_ref, h_scr, acc_scr, *, d):
    j = pl.program_id(2)

    @pl.when(j == 0)
    def _():
        h_scr[...] = _norm_mod(x_ref[...], gpre_ref[...], mod_ref, d).astype(BF16)
        acc_scr[...] = jnp.zeros_like(acc_scr)

    u = jnp.maximum(_dot(h_scr[...], wup_ref[...]), 0.0)
    acc_scr[...] += _dot((u * u).astype(BF16), wdn_ref[...])

    @pl.when(j == pl.num_programs(2) - 1)
    def _():
        o_ref[...] = _gated_residual(x_ref[...], acc_scr[...], gpost_ref[...], mod_ref, d)


def _mlp_cast_kernel(x_ref, mod_ref, gpre_ref, gpost_ref, wup_ref, wdn_ref,
                     o_ref, wupb_ref, wdnb_ref, h_scr, acc_scr, *, d):
    wupb_ref[...] = wup_ref[...].astype(BF16)
    wdnb_ref[...] = wdn_ref[...].astype(BF16)
    _mlp_kernel(x_ref, mod_ref, gpre_ref, gpost_ref, wupb_ref, wdnb_ref, o_ref, h_scr, acc_scr, d=d)


def _mlp_cast(x3, mod3, g_pre, g_post, w_up_all, w_down_all, layer, tf=512):
    gn, t, d = x3.shape
    assert gn == 1
    f = w_up_all.shape[2]
    return pl.pallas_call(
        functools.partial(_mlp_cast_kernel, d=d),
        grid=(1, 1, f // tf),
        in_specs=[
            pl.BlockSpec((None, t, d), lambda g, i, j: (0, 0, 0)),
            _mod_spec(mod3, t, 3 * d, 3),
            pl.BlockSpec((1, d), lambda g, i, j: (0, 0)),
            pl.BlockSpec((1, d), lambda g, i, j: (0, 0)),
            pl.BlockSpec((None, d, tf), lambda g, i, j: (layer, 0, j)),
            pl.BlockSpec((None, tf, d), lambda g, i, j: (layer, j, 0)),
        ],
        out_specs=[
            pl.BlockSpec((None, t, d), lambda g, i, j: (0, 0, 0)),
            pl.BlockSpec((d, tf), lambda g, i, j: (0, j)),
            pl.BlockSpec((tf, d), lambda g, i, j: (j, 0)),
        ],
        out_shape=[
            jax.ShapeDtypeStruct((1, t, d), F32),
            jax.ShapeDtypeStruct((d, f), BF16),
            jax.ShapeDtypeStruct((f, d), BF16),
        ],
        scratch_shapes=[pltpu.VMEM((t, d), BF16), pltpu.VMEM((t, d), F32)],
        compiler_params=_cparams(("arbitrary", "arbitrary", "arbitrary"), 56),
        name="mlp_cast",
    )(x3, _arr(mod3),g_pre, g_post, w_up_all, w_down_all)


def _mlp(x3, mod3, g_pre, g_post, w_up, w_down, tf=TF_MLP):
    gn, t, d = x3.shape
    assert gn == 1
    f = w_up.shape[1]
    return pl.pallas_call(
        functools.partial(_mlp_kernel, d=d),
        grid=(1, 1, f // tf),
        in_specs=[
            pl.BlockSpec((None, t, d), lambda g, i, j: (0, 0, 0)),
            _mod_spec(mod3, t, 3 * d, 3),
            pl.BlockSpec((1, d), lambda g, i, j: (0, 0)),
            pl.BlockSpec((1, d), lambda g, i, j: (0, 0)),
            pl.BlockSpec((d, tf), lambda g, i, j: (0, j)),
            pl.BlockSpec((tf, d), lambda g, i, j: (j, 0)),
        ],
        out_specs=pl.BlockSpec((None, t, d), lambda g, i, j: (0, 0, 0)),
        out_shape=jax.ShapeDtypeStruct((1, t, d), F32),
        scratch_shapes=[pltpu.VMEM((t, d), BF16), pltpu.VMEM((t, d), F32)],
        compiler_params=_cparams(("arbitrary", "arbitrary", "arbitrary"), 56),
        name="mlp_rows",
    )(x3, _arr(mod3),g_pre, g_post, w_up, w_down)


def _mlp_h_kernel(x_ref, h_ref, mod_ref, gpost_ref, wup_ref, wdn_ref, o_ref, acc_scr, *, d):
    j = pl.program_id(2)
    last = pl.num_programs(2) - 1

    def hidden(rows):
        u = jnp.maximum(_dot(h_ref[rows, :], wup_ref[...]), 0.0)
        return _dot((u * u).astype(BF16), wdn_ref[...])

    @pl.when(j == 0)
    def _():
        acc_scr[...] = hidden(slice(None))

    @pl.when((j > 0) & (j < last))
    def _():
        acc_scr[...] += hidden(slice(None))

    @pl.when(j == last)
    def _():
        for rows in _row_parts(x_ref.shape[0]):
            y = acc_scr[rows, :] + hidden(rows)
            o_ref[rows, :] = _gated_residual(x_ref[rows, :], y, gpost_ref[...], mod_ref, d)


def _mlp_h(x3, h3, mod3, g_post, w_up, w_down, tm, tf=TF_MLP):
    gn, t, d = x3.shape
    f = w_up.shape[1]
    assert f // tf >= 2
    return pl.pallas_call(
        functools.partial(_mlp_h_kernel, d=d),
        grid=(gn, t // tm, f // tf),
        in_specs=[
            pl.BlockSpec((None, tm, d), lambda g, i, j: (g, i, 0)),
            pl.BlockSpec((None, tm, d), lambda g, i, j: (g, i, 0)),
            _mod_spec(mod3, tm, 3 * d, 3),
            pl.BlockSpec((1, d), lambda g, i, j: (0, 0)),
            pl.BlockSpec((d, tf), lambda g, i, j: (0, j)),
            pl.BlockSpec((tf, d), lambda g, i, j: (j, 0)),
        ],
        out_specs=pl.BlockSpec((None, tm, d), lambda g, i, j: (g, i, 0)),
        out_shape=jax.ShapeDtypeStruct((gn, t, d), F32),
        scratch_shapes=[pltpu.VMEM((tm, d), F32)],
        compiler_params=_cparams(("parallel", "parallel", "arbitrary"), 56),
        name="mlp",
    )(x3, h3, _arr(mod3), g_post, w_up, w_down)


def _diag_blocks(rows, reps, row_shift, col_shift):
    tiled = jnp.concatenate([rows] * reps, axis=1)
    rg = lax.broadcasted_iota(jnp.int32, tiled.shape, 0) >> row_shift
    cg = lax.broadcasted_iota(jnp.int32, tiled.shape, 1) >> col_shift
    return jnp.where(rg == cg, tiled, 0.0)


def _s5_disc_kernel(lr_ref, li_ref, ldt_ref, br_ref, bi_ref, cr_ref, ci_ref,
                    pwr_ref, pwi_ref, bblk_ref, cblk_ref, *, seg_len):
    lr, li = lr_ref[...], li_ref[...]
    dt = jnp.exp(ldt_ref[...])
    mag = jnp.exp(lr * dt)
    lb_re, lb_im = mag * jnp.cos(li * dt), mag * jnp.sin(li * dt)
    nr, ni = lb_re - 1.0, lb_im
    den = lr * lr + li * li
    f_re = (nr * lr + ni * li) / den
    f_im = (ni * lr - nr * li) / den
    br, bi = br_ref[...], bi_ref[...]
    bb_re = f_re * br - f_im * bi
    bb_im = f_re * bi + f_im * br
    nrow, ncol2 = bblk_ref.shape
    rs, cs = S5_GROUP.bit_length() - 1, S5_STATE.bit_length() - 1
    blocks = lambda a: _diag_blocks(a.reshape(nrow, LANES), ncol2 // 2 // LANES, rs, cs)
    bblk_ref[...] = jnp.concatenate([blocks(bb_re), blocks(bb_im)], axis=1).astype(BF16)
    c_t = jnp.concatenate([blocks(cr_ref[...]), -blocks(ci_ref[...])], axis=1)
    cblk_ref[...] = jnp.transpose(c_t).astype(BF16)
    pwr_ref[0] = lb_re
    pwi_ref[0] = lb_im
    qr, qi = None, None
    sr, si = lb_re, lb_im
    e = seg_len
    while e:
        if e & 1:
            qr, qi = (sr, si) if qr is None else (qr * sr - qi * si, qr * si + qi * sr)
        e >>= 1
        if e:
            sr, si = sr * sr - si * si, 2.0 * sr * si
    pr, pi = qr, qi
    for n in range(SUBLANES):
        pwr_ref[1 + n] = pr
        pwi_ref[1 + n] = pi
        pr, pi = pr * qr - pi * qi, pr * qi + pi * qr


def _s5_discretize(lam_re, lam_im, log_dt, bt_re, bt_im, c_re, c_im, seg_len):
    g = lam_re.shape[0]
    ncb = g // S5_GPB
    assert S5_GROUP & (S5_GROUP - 1) == 0 and S5_STATE & (S5_STATE - 1) == 0
    grp = lambda *dims: pl.BlockSpec((S5_GPB,) + dims, lambda c: (c, 0, 0))
    pw_spec = pl.BlockSpec((1 + SUBLANES, S5_GPB, 1, LANES), lambda c: (0, c, 0, 0))
    return pl.pallas_call(
        functools.partial(_s5_disc_kernel, seg_len=seg_len),
        grid=(ncb,),
        in_specs=[grp(1, LANES), grp(1, LANES), grp(1, 1), grp(S5_GROUP, LANES), grp(S5_GROUP, LANES),
                  grp(S5_GROUP, LANES), grp(S5_GROUP, LANES)],
        out_specs=[pw_spec, pw_spec,
                   pl.BlockSpec((None, S5_UW, 2 * S5_GPB * S5_STATE), lambda c: (c, 0, 0)),
                   pl.BlockSpec((None, 2 * S5_GPB * S5_STATE, S5_UW), lambda c: (c, 0, 0))],
        out_shape=[
            jax.ShapeDtypeStruct((1 + SUBLANES, g, 1, LANES), F32),
            jax.ShapeDtypeStruct((1 + SUBLANES, g, 1, LANES), F32),
            jax.ShapeDtypeStruct((ncb, S5_UW, 2 * S5_GPB * S5_STATE), BF16),
            jax.ShapeDtypeStruct((ncb, 2 * S5_GPB * S5_STATE, S5_UW), BF16),
        ],
        compiler_params=_cparams(("arbitrary",), 32),
        name="s5_discretize",
    )(lam_re, lam_im, log_dt, bt_re, bt_im, c_re, c_im)


def _gelu_tanh(x):
    c0 = math.sqrt(2.0 / math.pi)
    return x * (0.5 * (1.0 + jnp.tanh(c0 * (x + 0.044715 * (x * x * x)))))


def _cmul_add(ar, ai, xr, xi, yr, yi):
    return yr + ar * xr - ai * xi, yi + ar * xi + ai * xr


def _s5_seq_kernel(x_ref, mod_ref, gpre_ref, bblk_ref, cblk_ref, lam_ref, dskip_ref, s0r_ref, s0i_ref,
                   z_ref, sr_ref, si_ref, h_scr, xr_scr, xi_scr, *, d):
    cb = pl.program_id(0)
    ncb, tm, uw = h_scr.shape
    cw = xr_scr.shape[1]
    seg = s0r_ref.shape[0]

    @pl.when(cb == 0)
    def _():
        h = _norm_mod(x_ref[...], gpre_ref[...], mod_ref, d)
        for c in range(ncb):
            h_scr[c] = h[:, c * uw:(c + 1) * uw]

    u = h_scr[cb]
    bu = _dot(u.astype(BF16), bblk_ref[...])
    xr_scr[...] = bu[:, 0:cw]
    xi_scr[...] = bu[:, cw:2 * cw]
    car_r, car_i = s0r_ref[...], s0i_ref[...]
    l_r, l_i = lam_ref[0:1, :], lam_ref[1:2, :]
    for t in range(tm // seg):
        rows = slice(t * seg, (t + 1) * seg)
        car_r, car_i = _cmul_add(l_r, l_i, car_r, car_i, xr_scr[rows, :], xi_scr[rows, :])
        xr_scr[rows, :] = car_r
        xi_scr[rows, :] = car_i
    sr_ref[...] = car_r
    si_ref[...] = car_i
    xs = jnp.concatenate([xr_scr[...].astype(BF16), xi_scr[...].astype(BF16)], axis=1)
    y = _dot(xs, cblk_ref[...]) + dskip_ref[...] * u
    z_ref[...] = _gelu_tanh(y).astype(BF16)


def _s5_seq(x2, mod3, g_pre, bblk, cblk, lam2, dskip, s0_re, s0_im):
    tm, d = x2.shape
    seg, nst = s0_re.shape
    ncb, uw, cw2 = bblk.shape
    cw = cw2 // 2
    sspec = pl.BlockSpec((seg, cw), lambda c: (0, c))
    return pl.pallas_call(
        functools.partial(_s5_seq_kernel, d=d),
        grid=(ncb,),
        in_specs=[
            pl.BlockSpec((tm, d), lambda c: (0, 0)),
            pl.BlockSpec((None, seg, 3 * d), lambda c, lead=getattr(mod3, "lead", 0): (lead, 0, 0)),
            pl.BlockSpec((1, d), lambda c: (0, 0)),
            pl.BlockSpec((None, uw, cw2), lambda c: (c, 0, 0)),
            pl.BlockSpec((None, cw2, uw), lambda c: (c, 0, 0)),
            pl.BlockSpec((2, cw), lambda c: (0, c)),
            pl.BlockSpec((1, uw), lambda c: (0, c)),
            sspec, sspec,
        ],
        out_specs=[pl.BlockSpec((tm, uw), lambda c: (0, c)), sspec, sspec],
        out_shape=[
            jax.ShapeDtypeStruct((tm, d), BF16),
            jax.ShapeDtypeStruct((seg, nst), F32),
            jax.ShapeDtypeStruct((seg, nst), F32),
        ],
        scratch_shapes=[
            pltpu.VMEM((ncb, tm, uw), F32),
            pltpu.VMEM((tm, cw), F32),
            pltpu.VMEM((tm, cw), F32),
        ],
        compiler_params=_cparams(("arbitrary",), 48),
        name="s5_seq",
    )(x2, _arr(mod3), g_pre, bblk, cblk, lam2, dskip, s0_re, s0_im)


def _s5_rows_kernel(*refs, d, ncast):
    x_ref, mod_ref, gpre_ref, bblk_ref, cblk_ref, tbl_ref, dskip_ref = refs[:7]
    cast_in = refs[7:7 + ncast]
    z_ref, sr_ref, si_ref = refs[7 + ncast:10 + ncast]
    cast_out = refs[10 + ncast:10 + 2 * ncast]
    h_scr, xr_scr, xi_scr, cr_scr, ci_scr = refs[10 + 2 * ncast:]
    for src, dst in zip(cast_in, cast_out):
        dst[...] = src[...].astype(BF16)

    ncol, tm, _ = xr_scr.shape
    ncb, uw, _ = bblk_ref.shape
    sl = tm // SUBLANES

    @pl.when(pl.program_id(1) == 0)
    def _():
        cr_scr[...] = jnp.zeros_like(cr_scr)
        ci_scr[...] = jnp.zeros_like(ci_scr)

    h_scr[...] = _norm_mod(x_ref[...], gpre_ref[...], mod_ref, d)
    row0 = lax.broadcasted_iota(jnp.int32, (ncol, SUBLANES, LANES), 1) == 0

    for c in range(ncb):
        us = slice(c * uw, (c + 1) * uw)
        cols = slice(c * ncol, (c + 1) * ncol)
        u = h_scr[:, us]
        bu = _dot(u.astype(BF16), bblk_ref[c])
        for j in range(ncol):
            xr_scr[j] = bu[:, j * LANES:(j + 1) * LANES]
            xi_scr[j] = bu[:, (ncol + j) * LANES:(ncol + j + 1) * LANES]
        l_r, l_i = tbl_ref[0, cols], tbl_ref[1, cols]

        def local(i, s):
            rows = pl.ds(pl.multiple_of(i * SUBLANES, SUBLANES), SUBLANES)
            return _cmul_add(l_r, l_i, s[0], s[1], xr_scr[:, rows, :], xi_scr[:, rows, :])

        zero = jnp.zeros((ncol, SUBLANES, LANES), F32)
        g_r, g_i = lax.fori_loop(0, sl, local, (zero, zero), unroll=True)
        for n in range(3):
            g_r, g_i = _cmul_add(tbl_ref[2 + 2 * n, cols], tbl_ref[3 + 2 * n, cols],
                                 pltpu.roll(g_r, 1 << n, 1), pltpu.roll(g_i, 1 << n, 1), g_r, g_i)
        car_r, car_i = cr_scr[cols], ci_scr[cols]
        g_r, g_i = _cmul_add(tbl_ref[8, cols], tbl_ref[9, cols], car_r, car_i, g_r, g_i)
        in_r = jnp.where(row0, car_r, pltpu.roll(g_r, 1, 1))
        in_i = jnp.where(row0, car_i, pltpu.roll(g_i, 1, 1))

        def full(i, s):
            rows = pl.ds(pl.multiple_of(i * SUBLANES, SUBLANES), SUBLANES)
            s_r, s_i = _cmul_add(l_r, l_i, s[0], s[1], xr_scr[:, rows, :], xi_scr[:, rows, :])
            xr_scr[:, rows, :] = s_r
            xi_scr[:, rows, :] = s_i
            return s_r, s_i

        e_r, e_i = lax.fori_loop(0, sl, full, (in_r, in_i), unroll=True)
        cr_scr[cols] = e_r[:, SUBLANES - 1:SUBLANES, :]
        ci_scr[cols] = e_i[:, SUBLANES - 1:SUBLANES, :]
        xs = jnp.concatenate([xr_scr[j].astype(BF16) for j in range(ncol)]
                             + [xi_scr[j].astype(BF16) for j in range(ncol)], axis=1)
        y = _dot(xs, cblk_ref[c]) + dskip_ref[:, us] * u
        z_ref[:, us] = _gelu_tanh(y).astype(BF16)

    sr_ref[...] = cr_scr[...]
    si_ref[...] = ci_scr[...]


def _s5_rows(x3, mod3, g_pre, bblk, cblk, tbl, dskip, tm, casts=()):
    gn, t, d = x3.shape
    ncb, uw, cw2 = bblk.shape
    ncol = cw2 // 2 // LANES
    nct = ncb * ncol
    nt = t // tm
    nstep = gn * nt
    const = lambda shape: pl.BlockSpec(shape, lambda g, i: (0,) * len(shape), pipeline_mode=pl.Buffered(1))
    ospec = pl.BlockSpec((None, None, nct, 1, LANES), lambda g, i: (g, i, 0, 0, 0))
    c_in, c_out, c_shape = [], [], []
    for w, layer in casts:
        _, rows, cols = w.shape
        slab = rows // nstep
        assert slab * nstep == rows and slab % (2 * SUBLANES) == 0
        c_in.append(pl.BlockSpec((None, slab, cols), lambda g, i, layer=layer: (layer, g * nt + i, 0)))
        c_out.append(pl.BlockSpec((slab, cols), lambda g, i: (g * nt + i, 0)))
        c_shape.append(jax.ShapeDtypeStruct((rows, cols), BF16))
    res = pl.pallas_call(
        functools.partial(_s5_rows_kernel, d=d, ncast=len(casts)),
        grid=(gn, nt),
        in_specs=[
            pl.BlockSpec((None, tm, d), lambda g, i: (g, i, 0)),
            _mod_spec(mod3, tm, 3 * d, 2),
            pl.BlockSpec((1, d), lambda g, i: (0, 0)),
            const(bblk.shape), const(cblk.shape), const(tbl.shape), const(dskip.shape),
        ] + c_in,
        out_specs=[pl.BlockSpec((None, tm, d), lambda g, i: (g, i, 0)), ospec, ospec] + c_out,
        out_shape=[
            jax.ShapeDtypeStruct((gn, t, d), BF16),
            jax.ShapeDtypeStruct((gn, nt, nct, 1, LANES), F32),
            jax.ShapeDtypeStruct((gn, nt, nct, 1, LANES), F32),
        ] + c_shape,
        scratch_shapes=[
            pltpu.VMEM((tm, d), F32),
            pltpu.VMEM((ncol, tm, LANES), F32),
            pltpu.VMEM((ncol, tm, LANES), F32),
            pltpu.VMEM((nct, 1, LANES), F32),
            pltpu.VMEM((nct, 1, LANES), F32),
        ],
        compiler_params=_cparams(("arbitrary", "arbitrary"), 56),
        name="s5_rows",
    )(x3, _arr(mod3),g_pre, bblk, cblk, tbl, dskip, *[w for w, _ in casts])
    z3, s_re, s_im = res[:3]
    last = lambda s: s[:, nt - 1].reshape(gn, 1, nct * LANES)
    return z3, last(s_re), last(s_im), list(res[3:])


def _glu_kernel(x_ref, mod_ref, g_ref, z_ref, wa_ref, wb_ref, *outs, d):
    for rows in _row_parts(x_ref.shape[0]):
        z = z_ref[rows, :]
        y = _dot(z, wa_ref[...]) * jax.nn.sigmoid(_dot(z, wb_ref[...]))
        _residual_out(x_ref, y, g_ref, mod_ref, outs, d, rows)


def _glu(x3, mod3, g_post, z3, w_a, w_b, tm, nxt=None):
    gn, t, d = x3.shape
    wspec = pl.BlockSpec((d, d), lambda g, i: (0, 0), pipeline_mode=pl.Buffered(1))
    n_in, n_out, n_shape, n_ops = _next_specs(None if nxt is None else (*nxt, x3.shape), tm, d)
    res = pl.pallas_call(
        functools.partial(_glu_kernel, d=d),
        grid=(gn, t // tm),
        in_specs=[
            pl.BlockSpec((None, tm, d), lambda g, i: (g, i, 0)),
            _mod_spec(mod3, tm, 3 * d, 2),
            pl.BlockSpec((1, d), lambda g, i: (0, 0)),
            pl.BlockSpec((None, tm, d), lambda g, i: (g, i, 0)),
            wspec, wspec,
        ] + n_in,
        out_specs=[pl.BlockSpec((None, tm, d), lambda g, i: (g, i, 0))] + n_out,
        out_shape=[jax.ShapeDtypeStruct((gn, t, d), F32)] + n_shape,
        compiler_params=_cparams(("parallel", "parallel"), 56),
        name="glu",
    )(x3, _arr(mod3),g_post, z3, w_a, w_b, *n_ops)
    return res if nxt is not None else (res[0], None)


def _rope_tables(pos):
    half = HEAD_DK // 2
    inv = ROPE_BASE ** (-jnp.arange(half, dtype=F32) / half)
    ang = pos.astype(F32)[:, None] * inv[None, :]
    cos, sin = jnp.cos(ang), jnp.sin(ang)
    return jnp.concatenate([cos, cos], axis=-1), jnp.concatenate([-sin, sin], axis=-1)


def _s5_tables(pw_re, pw_im):
    n = pw_re.shape[0]
    flat_r = pw_re.reshape(n, -1)
    flat_i = pw_im.reshape(n, -1)
    row = jnp.arange(SUBLANES)[:, None]
    tabs = [jnp.broadcast_to(flat_r[0], (SUBLANES, flat_r.shape[1])),
            jnp.broadcast_to(flat_i[0], (SUBLANES, flat_i.shape[1]))]
    for s in (1, 2, 4):
        mask = row >= s
        tabs.append(jnp.where(mask, flat_r[s][None, :], 0.0))
        tabs.append(jnp.where(mask, flat_i[s][None, :], 0.0))
    tabs += [flat_r[1:], flat_i[1:]]
    tbl = jnp.stack(tabs)
    tbl = tbl.reshape(tbl.shape[0], SUBLANES, -1, LANES).transpose(0, 2, 1, 3)
    return tbl, jnp.stack([flat_r[0], flat_i[0]])


def kernel(x_prompt, x_sample, state_gla, state_ret, state_s5_re, state_s5_im, c_prompt, c_sample,
           w_ada, b_ada, norm_pre, norm_post, w_in_mix, w_gla_gk, b_gla_gk, gla_head_norm,
           ret_head_norm, w_out_mix, s5_lam_re, s5_lam_im, s5_log_dt, s5_b_re, s5_b_im,
           s5_c_re, s5_c_im, s5_d, w_glu_a, w_glu_b, w_mlp_up, w_mlp_down):
    bp, tp, d = x_prompt.shape
    bs, ts, _ = x_sample.shape
    depth = w_ada.shape[0]

    nrow = -(-(bs + bp) // SUBLANES) * SUBLANES
    c_all = jnp.concatenate([c_sample, c_prompt, jnp.zeros((nrow - bs - bp, d), F32)], axis=0)
    mod_all = _adaln(c_all, w_ada.reshape(depth * 2, d, 3 * d), b_ada.reshape(depth * 2, 1, 3 * d))
    mod_s = [_ModSlab(mod_all, k, bs) for k in range(depth * 2)]
    mod_p = [mod_all[k, bs:bs + bp][:, None, :] for k in range(depth * 2)]

    w_in_t = jnp.swapaxes(w_in_mix, 1, 2)
    w_gk = jnp.pad(w_gla_gk[0], ((0, LANES - GLA_RANK), (0, 0))).astype(BF16)
    b_gk = b_gla_gk[0][None, :]
    w_out = w_out_mix[0].astype(BF16)
    gla_norm = gla_head_norm[0][:, None, :]
    ret_norm = ret_head_norm[0][:, None, :]
    gamma_log = jnp.log1p(-jnp.power(2.0, -5.0 - jnp.arange(RET_HEADS, dtype=F32)))
    ret_lg = jnp.broadcast_to(gamma_log[:, None, None], (RET_HEADS, 1, LANES))

    ng = s5_lam_re.shape[1]
    tm5 = min(TM_S5, tp)
    per_state = lambda a: jnp.tile(a, (1,) * (a.ndim - 1) + (LANES // S5_STATE,))
    bt = lambda a: per_state(jnp.swapaxes(a[0], 1, 2))
    pw_re, pw_im, bblk, cblk = _s5_discretize(
        per_state(s5_lam_re[0])[:, None, :], per_state(s5_lam_im[0])[:, None, :],
        s5_log_dt[0][:, None, None], bt(s5_b_re), bt(s5_b_im), per_state(s5_c_re[0]), per_state(s5_c_im[0]),
        tm5 // SUBLANES)
    tbl, lam2 = _s5_tables(pw_re[:, :, 0, :S5_STATE], pw_im[:, :, 0, :S5_STATE])
    dskip = s5_d[0][None, :]

    nxt = lambda mods, l, emit_h: (mods[2 * l + 1], norm_pre[l, 1][None]) if emit_h else None

    def layer0(x3, mods, tm, inproj, attn, mlp, emit_h):
        proj, glog = inproj(x3, mods[0])
        mg, mr, s_gla, s_ret = attn(proj, glog)
        x3, h3 = _outproj(x3, mods[0], norm_post[0, 0][None], mg, mr, w_out, tm, nxt(mods, 0, emit_h))
        return mlp(0, x3, h3, mods[1]), s_gla, s_ret

    def layer1_tail(x3, z3, mods, tm, glu_w, mlp, emit_h):
        x3, h3 = _glu(x3, mods[2], norm_post[1, 0][None], z3, glu_w[0], glu_w[1], tm, nxt(mods, 1, emit_h))
        return mlp(1, x3, h3, mods[3])

    cos_s, sin_s = _rope_tables(PAST_LEN + jnp.arange(ts, dtype=F32))

    def attn_s(proj, glog):
        tm_rows = lambda a: a.reshape(ts, bs, a.shape[-1])
        mg, mr, s_gla, s_ret = _attn_sample(tm_rows(proj), tm_rows(glog), gla_norm, ret_norm, cos_s, sin_s,
                                            ret_lg, state_gla[0], state_ret[0])
        flat = lambda a: a.reshape(1, ts * bs, a.shape[-1])
        return flat(mg), flat(mr), s_gla, s_ret

    xs3 = jnp.swapaxes(x_sample, 0, 1).reshape(1, ts * bs, d)

    def s5_s(x3, mod3):
        z2, s_re, s_im = _s5_seq(x3[0], mod3, norm_pre[1, 0][None], bblk, cblk, lam2, dskip,
                                 state_s5_re[0].reshape(bs, -1), state_s5_im[0].reshape(bs, -1))
        return z2[None], s_re, s_im

    w_up, w_dn, w_main = {}, {}, {}

    def inproj_s(x3, mod3):
        proj, glog, w_main[0], w_main["lr"] = _inproj_cast(x3, mod3, norm_pre[0, 0][None], w_in_t, 0,
                                                            w_gk, b_gk)
        return proj, glog

    def mlp_s(l, x3, h3, mod3):
        g_pre, g_post = norm_pre[l, 1][None], norm_post[l, 1][None]
        if l in w_up:
            return _mlp(x3, mod3, g_pre, g_post, w_up[l], w_dn[l])
        x3, w_up[l], w_dn[l] = _mlp_cast(x3, mod3, g_pre, g_post, w_mlp_up, w_mlp_down, l)
        return x3

    tm_s = ts * bs
    xs1, gla_s, ret_s = layer0(xs3, mod_s, tm_s, inproj_s, attn_s, mlp_s, False)

    cos_p, sin_p = _rope_tables(jnp.arange(tp, dtype=F32))
    zeros_att = jnp.zeros((bp, GLA_HEADS, HEAD_DK, HEAD_DV), F32)
    tm_p = min(TM_DENSE, tp)

    def attn_p(proj, glog):
        return _attn_prompt(proj, glog, gla_norm, ret_norm, cos_p, sin_p, ret_lg, zeros_att, zeros_att)

    def mlp_p(l, x3, h3, mod3):
        return _mlp_h(x3, h3, mod3, norm_post[l, 1][None], w_up[l], w_dn[l], tm_p)

    def inproj_p(x3, mod3):
        return _inproj(x3, mod3, norm_pre[0, 0][None], w_main[0], w_main["lr"], w_gk, b_gk, tm_p)

    xp1, gla_p, ret_p = layer0(x_prompt, mod_p, tm_p, inproj_p, attn_p, mlp_p, True)

    sl = tm5 // SUBLANES
    xpp = jnp.swapaxes(xp1.reshape(bp, tp // tm5, SUBLANES, sl, d), 2, 3).reshape(bp, tp, d)
    zpp, re_p, im_p, (w_up[1], w_dn[1], w_ga, w_gb) = _s5_rows(
        xpp, mod_p[2], norm_pre[1, 0][None], bblk, cblk, tbl, dskip, tm5,
        casts=((w_mlp_up, 1), (w_mlp_down, 1), (w_glu_a, 0), (w_glu_b, 0)))
    zp3 = jnp.swapaxes(zpp.reshape(bp, tp // tm5, sl, SUBLANES, d), 2, 3).reshape(bp, tp, d)

    zs3, re_s, im_s = s5_s(xs1, mod_s[2])
    y_s = layer1_tail(xs1, zs3, mod_s, tm_s, (w_ga, w_gb), mlp_s, False)
    y_s = jnp.swapaxes(y_s.reshape(ts, bs, d), 0, 1)
    y_p = layer1_tail(xp1, zp3, mod_p, tm_p, (w_ga, w_gb), mlp_p, True)

    st = lambda a, b_: a.reshape(1, b_, ng, S5_STATE)
    return (y_p, y_s, gla_p[None], gla_s[None], ret_p[None], ret_s[None],
            st(re_p, bp), st(re_s, bs), st(im_p, bp), st(im_s, bs))
```

```python
import functools
import math

import jax
import jax.numpy as jnp
import numpy as np
from jax import lax
from jax.experimental import pallas as pl
from jax.experimental.pallas import tpu as pltpu

F32 = jnp.float32
BF16 = jnp.bfloat16

EPS = 1e-6
LANES = 128
SUBLANES = 8
MIB = 1024 * 1024

GLA_HEADS = 4
RET_HEADS = 4
HEAD_DK = 128
HEAD_DV = 256
GLA_RANK = 16
GLA_LOGIT_NORM = 16.0
ROPE_BASE = 10000.0
PAST_LEN = 16384
S5_GROUP = 16
S5_STATE = 64
S5_GPB = 16
S5_UW = S5_GPB * S5_GROUP
ATT_CHUNK = 256
GLA_SUB = 16
TM_DENSE = 512
TM_S5 = 256
TN_INPROJ = 1024
TF_MLP = 1024


def _cparams(sem, vmem_mib):
    return pltpu.CompilerParams(dimension_semantics=sem, vmem_limit_bytes=vmem_mib * MIB)


def _dot(a, b):
    return jnp.dot(a, b, preferred_element_type=F32)


def _dot_nt(a, b):
    return lax.dot_general(a, b, (((1,), (1,)), ((), ())), preferred_element_type=F32)


def _rms(x, g):
    return x * lax.rsqrt(jnp.mean(x * x, axis=-1, keepdims=True) + EPS) * g


def _rows_affine(y, a, b=None):
    tm, d = y.shape
    r = a.shape[0]
    if r == 1 or r == tm:
        out = y * a
        return out if b is None else out + b
    y3 = y.reshape(tm // r, r, d)
    out = y3 * a[None]
    if b is not None:
        out = out + b[None]
    return out.reshape(tm, d)


def _norm_mod(x, g, mod_ref, d):
    return _rows_affine(_rms(x, g), 1.0 + mod_ref[:, d:2 * d], mod_ref[:, 0:d])


def _gated_residual(x, y, g, mod_ref, d):
    return x + _rows_affine(_rms(y, g), mod_ref[:, 2 * d:3 * d])


class _ModSlab:
    def __init__(self, arr, lead, rows):
        self.arr, self.lead = arr, lead
        self.shape = (1, rows, arr.shape[2])


def _arr(mod):
    return mod.arr if isinstance(mod, _ModSlab) else mod


def _mod_spec(mod, tm, width, ngrid):
    r = mod.shape[1]
    if isinstance(mod, _ModSlab):
        lead = mod.lead
        index = (lambda g, i: (lead, 0, 0)) if ngrid == 2 else (lambda g, i, j: (lead, 0, 0))
    else:
        index = (lambda g, i: (g, 0, 0)) if ngrid == 2 else (lambda g, i, j: (g, 0, 0))
    return pl.BlockSpec((None, r, width), index)


def _adaln_kernel(c_ref, w_ref, b_ref, o_ref):
    c = c_ref[...]
    sc = (c * jax.nn.sigmoid(c)).astype(BF16)
    o_ref[...] = _dot(sc, w_ref[...].astype(BF16)) + b_ref[...]


def _adaln(c_all, w_ada, b_ada, tn=2048):
    ls, d, n = w_ada.shape
    rows = c_all.shape[0]
    return pl.pallas_call(
        _adaln_kernel,
        grid=(ls, n // tn),
        in_specs=[
            pl.BlockSpec((rows, d), lambda l, j: (0, 0)),
            pl.BlockSpec((None, d, tn), lambda l, j: (l, 0, j)),
            pl.BlockSpec((None, 1, tn), lambda l, j: (l, 0, j)),
        ],
        out_specs=pl.BlockSpec((None, rows, tn), lambda l, j: (l, 0, j)),
        out_shape=jax.ShapeDtypeStruct((ls, rows, n), F32),
        compiler_params=_cparams(("parallel", "parallel"), 52),
        name="adaln",
    )(c_all, w_ada, b_ada)


def _log_sigmoid(x):
    return jnp.minimum(x, 0.0) - jnp.log1p(jnp.exp(-jnp.abs(x)))


def _gate_logits(hb, wlr_t, wgk_ref, bgk_ref, glog_ref):
    glr = _dot_nt(hb, wlr_t)
    logit = _dot(glr.astype(BF16), wgk_ref[...]) + bgk_ref[...]
    glog_ref[...] = _log_sigmoid(logit) * (1.0 / GLA_LOGIT_NORM)


def _inproj_kernel(x_ref, mod_ref, g_ref, w_ref, wlr_ref, wgk_ref, bgk_ref,
                   proj_ref, glog_ref, h_scr, *, d, tps):
    j = pl.program_id(2)

    @pl.when(j == 0)
    def _():
        hb = _norm_mod(x_ref[...], g_ref[...], mod_ref, d).astype(BF16)
        h_scr[...] = hb
        _gate_logits(hb, wlr_ref[...], wgk_ref, bgk_ref, glog_ref)

    tn = w_ref.shape[2]
    for k in range(tps):
        proj_ref[:, k * tn:(k + 1) * tn] = _dot(h_scr[...], w_ref[j * tps + k])


def _inproj_cast_kernel(x_ref, mod_ref, g_ref, wa_ref, wb_ref, wgk_ref, bgk_ref,
                        proj_ref, glog_ref, wout_ref, wlr_ref, h_scr, *, d, n_lo):
    j = pl.program_id(2)

    @pl.when(j == 0)
    def _():
        h_scr[...] = _norm_mod(x_ref[...], g_ref[...], mod_ref, d).astype(BF16)

    @pl.when(j < n_lo)
    def _():
        wout_ref[...] = jnp.transpose(wa_ref[...]).astype(BF16)

    @pl.when(j >= n_lo)
    def _():
        w = jnp.concatenate([wa_ref[GLA_RANK:, :], wb_ref[:GLA_RANK, :]], axis=0)
        wout_ref[...] = jnp.transpose(w).astype(BF16)

    @pl.when(j == n_lo)
    def _():
        wlr_t = _pad_rows(wa_ref[:GLA_RANK, :], LANES).astype(BF16)
        wlr_ref[...] = wlr_t
        _gate_logits(h_scr[...], wlr_t, wgk_ref, bgk_ref, glog_ref)

    proj_ref[...] = _dot(h_scr[...], wout_ref[...])


def _inproj_cast(x3, mod3, g_pre, w_raw_t, layer, w_gk, b_gk, tn=TN_INPROJ):
    gn, t, d = x3.shape
    assert gn == 1
    sec = (w_raw_t.shape[1] - GLA_RANK) // 2
    assert sec % tn == 0 and tn % LANES == 0
    nj = 2 * sec // tn
    n = nj * tn
    gkey = w_gk.shape[1]
    return pl.pallas_call(
        functools.partial(_inproj_cast_kernel, d=d, n_lo=sec // tn),
        grid=(1, 1, nj),
        in_specs=[
            pl.BlockSpec((None, t, d), lambda g, i, j: (0, 0, 0)),
            _mod_spec(mod3, t, 3 * d, 3),
            pl.BlockSpec((1, d), lambda g, i, j: (0, 0)),
            pl.BlockSpec((None, tn, d), lambda g, i, j: (layer, j, 0)),
            pl.BlockSpec((None, LANES, d), lambda g, i, j: (layer, (j + 1) * (tn // LANES), 0)),
            pl.BlockSpec((LANES, gkey), lambda g, i, j: (0, 0)),
            pl.BlockSpec((1, gkey), lambda g, i, j: (0, 0)),
        ],
        out_specs=[
            pl.BlockSpec((None, t, tn), lambda g, i, j: (0, 0, j)),
            pl.BlockSpec((None, t, gkey), lambda g, i, j: (0, 0, 0)),
            pl.BlockSpec((None, d, tn), lambda g, i, j: (j, 0, 0)),
            pl.BlockSpec((LANES, d), lambda g, i, j: (0, 0)),
        ],
        out_shape=[
            jax.ShapeDtypeStruct((1, t, n), F32),
            jax.ShapeDtypeStruct((1, t, gkey), F32),
            jax.ShapeDtypeStruct((nj, d, tn), BF16),
            jax.ShapeDtypeStruct((LANES, d), BF16),
        ],
        scratch_shapes=[pltpu.VMEM((t, d), BF16)],
        compiler_params=_cparams(("arbitrary", "arbitrary", "arbitrary"), 56),
        name="inproj_cast",
    )(x3, _arr(mod3),g_pre, w_raw_t, w_raw_t, w_gk, b_gk)


def _inproj(x3, mod3, g_pre, w_main, w_lr_t, w_gk, b_gk, tm, tps=3):
    gn, t, d = x3.shape
    nj, _, tn = w_main.shape
    n = nj * tn
    gkey = w_gk.shape[1]
    return pl.pallas_call(
        functools.partial(_inproj_kernel, d=d, tps=tps),
        grid=(gn, t // tm, nj // tps),
        in_specs=[
            pl.BlockSpec((None, tm, d), lambda g, i, j: (g, i, 0)),
            _mod_spec(mod3, tm, 3 * d, 3),
            pl.BlockSpec((1, d), lambda g, i, j: (0, 0)),
            pl.BlockSpec((nj, d, tn), lambda g, i, j: (0, 0, 0), pipeline_mode=pl.Buffered(1)),
            pl.BlockSpec((LANES, d), lambda g, i, j: (0, 0)),
            pl.BlockSpec((LANES, gkey), lambda g, i, j: (0, 0)),
            pl.BlockSpec((1, gkey), lambda g, i, j: (0, 0)),
        ],
        out_specs=[
            pl.BlockSpec((None, tm, tps * tn), lambda g, i, j: (g, i, j)),
            pl.BlockSpec((None, tm, gkey), lambda g, i, j: (g, i, 0)),
        ],
        out_shape=[
            jax.ShapeDtypeStruct((gn, t, n), F32),
            jax.ShapeDtypeStruct((gn, t, gkey), F32),
        ],
        scratch_shapes=[pltpu.VMEM((tm, d), BF16)],
        compiler_params=_cparams(("parallel", "parallel", "arbitrary"), 56),
        name="inproj",
    )(x3, _arr(mod3),g_pre, w_main, w_lr_t, w_gk, b_gk)


def _cumsum_rows(g):
    c = g.shape[0]
    row = lax.broadcasted_iota(jnp.int32, g.shape, 0)
    s = 1
    while s < c:
        g = g + jnp.where(row >= s, pltpu.roll(g, s, 0), 0.0)
        s *= 2
    return g


def _pad_rows(a, rows):
    if a.shape[0] == rows:
        return a
    return jnp.concatenate([a, jnp.zeros((rows - a.shape[0], a.shape[1]), a.dtype)], axis=0)


def _col_bcast(row, width):
    sq = jnp.transpose(jnp.broadcast_to(row, (LANES, LANES)))
    return jnp.concatenate([sq] * (width // LANES), axis=1)


def _gla_core(q, k, v, g, s, sub):
    cq = q.shape[0]
    ck = max(cq, LANES)
    b = _cumsum_rows(g)
    be = b - g
    bk = _pad_rows(b, ck)
    kp = _pad_rows(k, ck)
    vp = _pad_rows(v, ck).astype(BF16)
    rowj = lax.broadcasted_iota(jnp.int32, (ck, 1), 0)
    att_rows = []
    for blk in range(cq // sub):
        lo, hi = blk * sub, (blk + 1) * sub
        base = be[lo:lo + 1, :]
        qs = q[lo:hi] * jnp.exp(b[lo:hi] - base)
        ks = jnp.where(rowj < hi, kp * jnp.exp(base - bk), 0.0)
        att_rows.append(_dot_nt(qs.astype(BF16), ks.astype(BF16)))
    att = att_rows[0] if len(att_rows) == 1 else jnp.concatenate(att_rows, axis=0)
    ri = lax.broadcasted_iota(jnp.int32, (cq, ck), 0)
    cj = lax.broadcasted_iota(jnp.int32, (cq, ck), 1)
    att = jnp.where(ri >= cj, att, 0.0)
    o = _dot(att.astype(BF16), vp) + _dot((q * jnp.exp(b)).astype(BF16), s.astype(BF16))
    b_last = b[cq - 1:cq, :]
    k_out = kp * jnp.exp(b_last - bk)
    s_new = s * _col_bcast(jnp.exp(b_last), s.shape[1]) + _dot(jnp.transpose(k_out).astype(BF16), vp)
    return o, s_new


def _ret_core(q, k, v, s, lg, dmat, valid):
    cq = q.shape[0]
    ck = max(cq, LANES)
    kp = _pad_rows(k, ck)
    vp = _pad_rows(v, ck).astype(BF16)
    ti = lax.broadcasted_iota(jnp.int32, (cq, 1), 0).astype(F32)
    tj = lax.broadcasted_iota(jnp.int32, (ck, 1), 0).astype(F32)
    att = _dot_nt(q.astype(BF16), kp.astype(BF16)) * dmat
    q_in = q * jnp.exp((ti + 1.0) * lg)
    o = _dot(att.astype(BF16), vp) + _dot(q_in.astype(BF16), s.astype(BF16))
    k_out = kp * jnp.exp((float(valid - 1) - tj) * lg)
    s_new = s * jnp.exp(float(valid) * lg) + _dot(jnp.transpose(k_out).astype(BF16), vp)
    return o, s_new


def _decay_matrix(cq, ck, lg):
    ri = lax.broadcasted_iota(jnp.int32, (cq, ck), 0)
    cj = lax.broadcasted_iota(jnp.int32, (cq, ck), 1)
    diff = (ri - cj).astype(F32)
    return jnp.where(ri >= cj, jnp.exp(diff * lg), 0.0)


def _rope(x, cosf, sinf):
    return x * cosf + pltpu.roll(x, x.shape[1] // 2, 1) * sinf


def _silu(x):
    return x * jax.nn.sigmoid(x)


def _gla_finish(o, gate, gn):
    o = o * lax.rsqrt(jnp.mean(o * o, axis=-1, keepdims=True) + EPS) * gn
    return (o * _silu(gate)).astype(BF16)


def _ret_finish(o, gate, gn):
    oc = o - jnp.mean(o, axis=-1, keepdims=True)
    oc = oc * lax.rsqrt(jnp.mean(oc * oc, axis=-1, keepdims=True) + EPS) * gn
    return (oc * _silu(gate)).astype(BF16)


def _head(ref, h, width):
    return ref[:, h * width:(h + 1) * width]


def _attn_prompt_kernel(gq_ref, gk_ref, gv_ref, gg_ref, gl_ref, ggn_ref, gs0_ref,
                        rq_ref, rk_ref, rv_ref, rg_ref, cos_ref, sin_ref, lg_ref, rgn_ref, rs0_ref,
                        go_ref, gs_ref, ro_ref, rs_ref, d_scr):
    @pl.when(pl.program_id(1) == 0)
    def _():
        gs_ref[...] = gs0_ref[...]
        rs_ref[...] = rs0_ref[...]
        for h in range(d_scr.shape[0]):
            d_scr[h] = _decay_matrix(d_scr.shape[1], d_scr.shape[2], lg_ref[h][:, 0:1])

    cosf, sinf = cos_ref[...], sin_ref[...]
    for h in range(gs_ref.shape[0]):
        vcols = slice(h * HEAD_DV, (h + 1) * HEAD_DV)
        q = _head(gq_ref, h, HEAD_DK) * (HEAD_DK ** -0.5)
        o, s_new = _gla_core(q, _head(gk_ref, h, HEAD_DK), _head(gv_ref, h, HEAD_DV),
                             _head(gl_ref, h, HEAD_DK), gs_ref[h], GLA_SUB)
        gs_ref[h] = s_new
        go_ref[:, vcols] = _gla_finish(o, _head(gg_ref, h, HEAD_DV), ggn_ref[h])
        q = _rope(_head(rq_ref, h, HEAD_DK), cosf, sinf)
        k = _rope(_head(rk_ref, h, HEAD_DK), cosf, sinf) * (HEAD_DK ** -0.5)
        o, s_new = _ret_core(q, k, _head(rv_ref, h, HEAD_DV), rs_ref[h], lg_ref[h][:, 0:1], d_scr[h],
                             q.shape[0])
        rs_ref[h] = s_new
        ro_ref[:, vcols] = _ret_finish(o, _head(rg_ref, h, HEAD_DV), rgn_ref[h])


def _attn_prompt(proj, glog, gla_norm, ret_norm, cosf, sinf, ret_lg, s0_gla, s0_ret):
    bsz, t, _ = proj.shape
    c = ATT_CHUNK
    nh = GLA_HEADS
    kw, vw = nh * HEAD_DK, nh * HEAD_DV
    kspec = lambda blk: pl.BlockSpec((None, c, kw), lambda b, i, blk=blk: (b, i, blk))
    vspec = lambda blk: pl.BlockSpec((None, c, vw), lambda b, i, blk=blk: (b, i, blk))
    hspec = pl.BlockSpec((nh, 1, HEAD_DV), lambda b, i: (0, 0, 0))
    sspec = pl.BlockSpec((None, nh, HEAD_DK, HEAD_DV), lambda b, i: (b, 0, 0, 0))
    ospec = pl.BlockSpec((None, c, vw), lambda b, i: (b, i, 0))
    tspec = pl.BlockSpec((c, HEAD_DK), lambda b, i: (i, 0))
    o_shape = jax.ShapeDtypeStruct((bsz, t, vw), BF16)
    s_shape = jax.ShapeDtypeStruct((bsz, nh, HEAD_DK, HEAD_DV), F32)
    mg, s_gla, mr, s_ret = pl.pallas_call(
        _attn_prompt_kernel,
        grid=(bsz, t // c),
        in_specs=[kspec(0), kspec(1), vspec(1), vspec(2), kspec(0), hspec, sspec,
                  kspec(6), kspec(7), vspec(4), vspec(5), tspec, tspec,
                  pl.BlockSpec((nh, 1, LANES), lambda b, i: (0, 0, 0)), hspec, sspec],
        out_specs=[ospec, sspec, ospec, sspec],
        out_shape=[o_shape, s_shape, o_shape, s_shape],
        scratch_shapes=[pltpu.VMEM((nh, c, c), F32)],
        compiler_params=_cparams(("parallel", "arbitrary"), 32),
        name="attn_prompt",
    )(proj, proj, proj, proj, glog, gla_norm, s0_gla,
      proj, proj, proj, proj, cosf, sinf, ret_lg, ret_norm, s0_ret)
    return mg, mr, s_gla, s_ret


def _seq_rows(ref):
    ts, bb, w = ref.shape
    return ref[...].reshape(ts * bb, w)


def _seq_masks(n, bb):
    r = lax.broadcasted_iota(jnp.int32, (n, n), 0)
    c = lax.broadcasted_iota(jnp.int32, (n, n), 1)
    return (r % bb == c % bb) & (r >= c), (r - c).astype(F32) * (1.0 / bb)


def _seq_state_terms(q_in, k_out, v, s0, bb):
    n, dk = q_in.shape
    rown = lax.broadcasted_iota(jnp.int32, (n, 1), 0) % bb
    q_bd = jnp.concatenate([jnp.where(rown == j, q_in, 0.0) for j in range(bb)], axis=1)
    o_inter = _dot(q_bd.astype(BF16), s0.astype(BF16))
    k_t = jnp.transpose(_pad_rows(k_out, LANES))
    coln = lax.broadcasted_iota(jnp.int32, (1, LANES), 1) % bb
    k_bd = jnp.concatenate([jnp.where(coln == j, k_t, 0.0) for j in range(bb)], axis=0)
    ds = _dot(k_bd.astype(BF16), _pad_rows(v, LANES).astype(BF16))
    return o_inter, ds


def _gla_sample_kernel(q_ref, k_ref, v_ref, gg_ref, gl_ref, gn_ref, s0_ref, o_ref, s_ref):
    ts, bb, dk = q_ref.shape
    dv = v_ref.shape[2]
    n = ts * bb
    q = _seq_rows(q_ref) * (dk ** -0.5)
    k, v, g = _seq_rows(k_ref), _seq_rows(v_ref), _seq_rows(gl_ref)
    steps = [g[0:bb]]
    for t in range(1, ts):
        steps.append(steps[-1] + g[t * bb:(t + 1) * bb])
    b = jnp.concatenate(steps, axis=0)
    b_last = steps[-1]
    q_in = q * jnp.exp(b)
    mask, _ = _seq_masks(n, bb)
    att = jnp.where(mask, _dot_nt(q_in.astype(BF16), (k * jnp.exp(-b)).astype(BF16)), 0.0)
    k_out = k * jnp.exp(jnp.concatenate([b_last] * ts, axis=0) - b)
    s0 = s0_ref[...].reshape(bb * dk, dv)
    o_inter, ds = _seq_state_terms(q_in, k_out, v, s0, bb)
    o = _dot(att.astype(BF16), v.astype(BF16)) + o_inter
    e_last = jnp.exp(b_last)
    dec = jnp.concatenate([_col_bcast(e_last[j:j + 1, :], dv) for j in range(bb)], axis=0)
    s_ref[...] = (s0 * dec + ds).reshape(bb, dk, dv)
    o_ref[...] = _gla_finish(o, _seq_rows(gg_ref), gn_ref[...]).reshape(ts, bb, dv)


def _ret_sample_kernel(q_ref, k_ref, v_ref, rg_ref, cos_ref, sin_ref, lg_ref, gn_ref, s0_ref,
                       o_ref, s_ref):
    ts, bb, dk = q_ref.shape
    dv = v_ref.shape[2]
    n = ts * bb
    lg = lg_ref[:, 0:1]
    rows = lambda tab: jnp.concatenate(
        [jnp.broadcast_to(tab[t:t + 1, :], (bb, dk)) for t in range(ts)], axis=0)
    cosf, sinf = rows(cos_ref[...]), rows(sin_ref[...])
    q = _rope(_seq_rows(q_ref), cosf, sinf)
    k = _rope(_seq_rows(k_ref), cosf, sinf) * (dk ** -0.5)
    v = _seq_rows(v_ref)
    mask, dt = _seq_masks(n, bb)
    att = _dot_nt(q.astype(BF16), k.astype(BF16)) * jnp.where(mask, jnp.exp(dt * lg), 0.0)
    tt = (lax.broadcasted_iota(jnp.int32, (n, 1), 0) // bb).astype(F32)
    q_in = q * jnp.exp((tt + 1.0) * lg)
    k_out = k * jnp.exp((float(ts - 1) - tt) * lg)
    s0 = s0_ref[...].reshape(bb * dk, dv)
    o_inter, ds = _seq_state_terms(q_in, k_out, v, s0, bb)
    o = _dot(att.astype(BF16), v.astype(BF16)) + o_inter
    s_ref[...] = (s0 * jnp.exp(float(ts) * lg) + ds).reshape(bb, dk, dv)
    o_ref[...] = _ret_finish(o, _seq_rows(rg_ref), gn_ref[...]).reshape(ts, bb, dv)


def _attn_sample(proj, glog, gla_norm, ret_norm, cosf, sinf, ret_lg, s0_gla, s0_ret, bb=16):
    ts, bsz, _ = proj.shape
    grid = (bsz // bb, GLA_HEADS)
    nk = GLA_HEADS
    kspec = lambda off: pl.BlockSpec((ts, bb, HEAD_DK), lambda i, h, off=off: (0, i, off + h))
    vspec = lambda off: pl.BlockSpec((ts, bb, HEAD_DV), lambda i, h, off=off: (0, i, off + h))
    hspec = pl.BlockSpec((None, 1, HEAD_DV), lambda i, h: (h, 0, 0))
    sspec = pl.BlockSpec((bb, None, HEAD_DK, HEAD_DV), lambda i, h: (i, h, 0, 0))
    ospec = pl.BlockSpec((ts, bb, HEAD_DV), lambda i, h: (0, i, h))
    out_shape = [
        jax.ShapeDtypeStruct((ts, bsz, GLA_HEADS * HEAD_DV), BF16),
        jax.ShapeDtypeStruct((bsz, GLA_HEADS, HEAD_DK, HEAD_DV), F32),
    ]
    params = _cparams(("parallel", "parallel"), 40)
    mg, s_gla = pl.pallas_call(
        _gla_sample_kernel,
        grid=grid,
        in_specs=[kspec(0), kspec(nk), vspec(nk), vspec(2 * nk),
                  pl.BlockSpec((ts, bb, HEAD_DK), lambda i, h: (0, i, h)),
                  hspec, sspec],
        out_specs=[ospec, sspec],
        out_shape=out_shape,
        compiler_params=params,
        name="gla_sample",
    )(proj, proj, proj, proj, glog, gla_norm, s0_gla)
    tspec = pl.BlockSpec((ts, HEAD_DK), lambda i, h: (0, 0))
    mr, s_ret = pl.pallas_call(
        _ret_sample_kernel,
        grid=grid,
        in_specs=[kspec(6 * nk), kspec(7 * nk), vspec(4 * nk), vspec(5 * nk),
                  tspec, tspec,
                  pl.BlockSpec((None, 1, LANES), lambda i, h: (h, 0, 0)),
                  hspec, sspec],
        out_specs=[ospec, sspec],
        out_shape=out_shape,
        compiler_params=params,
        name="ret_sample",
    )(proj, proj, proj, proj, cosf, sinf, ret_lg, ret_norm, s0_ret)
    return mg, mr, s_gla, s_ret


ROW_SPLITS = 4


def _row_parts(tm, parts=ROW_SPLITS):
    step = tm // parts
    return [slice(k * step, (k + 1) * step) for k in range(parts)]


def _residual_out(x_ref, y, g_ref, mod_ref, outs, d, rows):
    x_new = _gated_residual(x_ref[rows, :], y, g_ref[...], mod_ref, d)
    if len(outs) == 1:
        outs[0][rows, :] = x_new
    else:
        modn_ref, gn_ref, o_ref, h_ref = outs
        o_ref[rows, :] = x_new
        h_ref[rows, :] = _norm_mod(x_new, gn_ref[...], modn_ref, d).astype(BF16)


def _next_specs(nxt, tm, d):
    if nxt is None:
        return [], [], [], []
    mod_next, g_next, shape = nxt
    return ([_mod_spec(mod_next, tm, 3 * d, 2), pl.BlockSpec((1, d), lambda g, i: (0, 0))],
            [pl.BlockSpec((None, tm, d), lambda g, i: (g, i, 0))],
            [jax.ShapeDtypeStruct(shape, BF16)], [_arr(mod_next), g_next])


def _outproj_kernel(x_ref, mod_ref, g_ref, mg_ref, mr_ref, wo_ref, *outs, d):
    half = mg_ref.shape[1]
    for rows in _row_parts(x_ref.shape[0]):
        y = _dot(mg_ref[rows, :], wo_ref[0:half, :]) + _dot(mr_ref[rows, :], wo_ref[half:2 * half, :])
        _residual_out(x_ref, y, g_ref, mod_ref, outs, d, rows)


def _outproj(x3, mod3, g_post, mg, mr, w_out, tm, nxt=None):
    gn, t, d = x3.shape
    half = mg.shape[2]
    n_in, n_out, n_shape, n_ops = _next_specs(None if nxt is None else (*nxt, x3.shape), tm, d)
    res = pl.pallas_call(
        functools.partial(_outproj_kernel, d=d),
        grid=(gn, t // tm),
        in_specs=[
            pl.BlockSpec((None, tm, d), lambda g, i: (g, i, 0)),
            _mod_spec(mod3, tm, 3 * d, 2),
            pl.BlockSpec((1, d), lambda g, i: (0, 0)),
            pl.BlockSpec((None, tm, half), lambda g, i: (g, i, 0)),
            pl.BlockSpec((None, tm, half), lambda g, i: (g, i, 0)),
            pl.BlockSpec((2 * half, d), lambda g, i: (0, 0)),
        ] + n_in,
        out_specs=[pl.BlockSpec((None, tm, d), lambda g, i: (g, i, 0))] + n_out,
        out_shape=[jax.ShapeDtypeStruct((gn, t, d), F32)] + n_shape,
        compiler_params=_cparams(("parallel", "parallel"), 56),
        name="outproj",
    )(x3, _arr(mod3),g_post, mg, mr, w_out, *n_ops)
    return res if nxt is not None else (res[0], None)


def _mlp_kernel(x_ref, mod_ref, gpre_ref, gpost_ref, wup_ref, wdn_ref, o_ref, h_scr, acc_scr, *, d):
    j = pl.program_id(2)

    @pl.when(j == 0)
    def _():
        h_scr[...] = _norm_mod(x_ref[...], gpre_ref[...], mod_ref, d).astype(BF16)
        acc_scr[...] = jnp.zeros_like(acc_scr)

    u = jnp.maximum(_dot(h_scr[...], wup_ref[...]), 0.0)
    acc_scr[...] += _dot((u * u).astype(BF16), wdn_ref[...])

    @pl.when(j == pl.num_programs(2) - 1)
    def _():
        o_ref[...] = _gated_residual(x_ref[...], acc_scr[...], gpost_ref[...], mod_ref, d)


def _mlp_cast_kernel(x_ref, mod_ref, gpre_ref, gpost_ref, wup_ref, wdn_ref,
                     o_ref, wupb_ref, wdnb_ref, h_scr, acc_scr, *, d):
    wupb_ref[...] = wup_ref[...].astype(BF16)
    wdnb_ref[...] = wdn_ref[...].astype(BF16)
    _mlp_kernel(x_ref, mod_ref, gpre_ref, gpost_ref, wupb_ref, wdnb_ref, o_ref, h_scr, acc_scr, d=d)


def _mlp_cast(x3, mod3, g_pre, g_post, w_up_all, w_down_all, layer, tf=512):
    gn, t, d = x3.shape
    assert gn == 1
    f = w_up_all.shape[2]
    return pl.pallas_call(
        functools.partial(_mlp_cast_kernel, d=d),
        grid=(1, 1, f // tf),
        in_specs=[
            pl.BlockSpec((None, t, d), lambda g, i, j: (0, 0, 0)),
            _mod_spec(mod3, t, 3 * d, 3),
            pl.BlockSpec((1, d), lambda g, i, j: (0, 0)),
            pl.BlockSpec((1, d), lambda g, i, j: (0, 0)),
            pl.BlockSpec((None, d, tf), lambda g, i, j: (layer, 0, j)),
            pl.BlockSpec((None, tf, d), lambda g, i, j: (layer, j, 0)),
        ],
        out_specs=[
            pl.BlockSpec((None, t, d), lambda g, i, j: (0, 0, 0)),
            pl.BlockSpec((d, tf), lambda g, i, j: (0, j)),
            pl.BlockSpec((tf, d), lambda g, i, j: (j, 0)),
        ],
        out_shape=[
            jax.ShapeDtypeStruct((1, t, d), F32),
            jax.ShapeDtypeStruct((d, f), BF16),
            jax.ShapeDtypeStruct((f, d), BF16),
        ],
        scratch_shapes=[pltpu.VMEM((t, d), BF16), pltpu.VMEM((t, d), F32)],
        compiler_params=_cparams(("arbitrary", "arbitrary", "arbitrary"), 56),
        name="mlp_cast",
    )(x3, _arr(mod3),g_pre, g_post, w_up_all, w_down_all)


def _mlp(x3, mod3, g_pre, g_post, w_up, w_down, tf=TF_MLP):
    gn, t, d = x3.shape
    assert gn == 1
    f = w_up.shape[1]
    return pl.pallas_call(
        functools.partial(_mlp_kernel, d=d),
        grid=(1, 1, f // tf),
        in_specs=[
            pl.BlockSpec((None, t, d), lambda g, i, j: (0, 0, 0)),
            _mod_spec(mod3, t, 3 * d, 3),
            pl.BlockSpec((1, d), lambda g, i, j: (0, 0)),
            pl.BlockSpec((1, d), lambda g, i, j: (0, 0)),
            pl.BlockSpec((d, tf), lambda g, i, j: (0, j)),
            pl.BlockSpec((tf, d), lambda g, i, j: (j, 0)),
        ],
        out_specs=pl.BlockSpec((None, t, d), lambda g, i, j: (0, 0, 0)),
        out_shape=jax.ShapeDtypeStruct((1, t, d), F32),
        scratch_shapes=[pltpu.VMEM((t, d), BF16), pltpu.VMEM((t, d), F32)],
        compiler_params=_cparams(("arbitrary", "arbitrary", "arbitrary"), 56),
        name="mlp_rows",
    )(x3, _arr(mod3),g_pre, g_post, w_up, w_down)


def _mlp_h_kernel(x_ref, h_ref, mod_ref, gpost_ref, wup_ref, wdn_ref, o_ref, acc_scr, *, d):
    j = pl.program_id(2)
    last = pl.num_programs(2) - 1

    def hidden(rows):
        u = jnp.maximum(_dot(h_ref[rows, :], wup_ref[...]), 0.0)
        return _dot((u * u).astype(BF16), wdn_ref[...])

    @pl.when(j == 0)
    def _():
        acc_scr[...] = hidden(slice(None))

    @pl.when((j > 0) & (j < last))
    def _():
        acc_scr[...] += hidden(slice(None))

    @pl.when(j == last)
    def _():
        for rows in _row_parts(x_ref.shape[0], 2):
            y = acc_scr[rows, :] + hidden(rows)
            o_ref[rows, :] = _gated_residual(x_ref[rows, :], y, gpost_ref[...], mod_ref, d)


def _mlp_h(x3, h3, mod3, g_post, w_up, w_down, tm, tf=TF_MLP):
    gn, t, d = x3.shape
    f = w_up.shape[1]
    assert f // tf >= 2
    return pl.pallas_call(
        functools.partial(_mlp_h_kernel, d=d),
        grid=(gn, t // tm, f // tf),
        in_specs=[
            pl.BlockSpec((None, tm, d), lambda g, i, j: (g, i, 0)),
            pl.BlockSpec((None, tm, d), lambda g, i, j: (g, i, 0)),
            _mod_spec(mod3, tm, 3 * d, 3),
            pl.BlockSpec((1, d), lambda g, i, j: (0, 0)),
            pl.BlockSpec((d, tf), lambda g, i, j: (0, j)),
            pl.BlockSpec((tf, d), lambda g, i, j: (j, 0)),
        ],
        out_specs=pl.BlockSpec((None, tm, d), lambda g, i, j: (g, i, 0)),
        out_shape=jax.ShapeDtypeStruct((gn, t, d), F32),
        scratch_shapes=[pltpu.VMEM((tm, d), F32)],
        compiler_params=_cparams(("parallel", "parallel", "arbitrary"), 56),
        name="mlp",
    )(x3, h3, _arr(mod3), g_post, w_up, w_down)


def _diag_blocks(rows, reps, row_shift, col_shift):
    tiled = jnp.concatenate([rows] * reps, axis=1)
    rg = lax.broadcasted_iota(jnp.int32, tiled.shape, 0) >> row_shift
    cg = lax.broadcasted_iota(jnp.int32, tiled.shape, 1) >> col_shift
    return jnp.where(rg == cg, tiled, 0.0)


def _s5_disc_kernel(lr_ref, li_ref, ldt_ref, br_ref, bi_ref, cr_ref, ci_ref,
                    pwr_ref, pwi_ref, bblk_ref, cblk_ref, *, seg_len):
    lr, li = lr_ref[...], li_ref[...]
    dt = jnp.exp(ldt_ref[...])
    mag = jnp.exp(lr * dt)
    lb_re, lb_im = mag * jnp.cos(li * dt), mag * jnp.sin(li * dt)
    nr, ni = lb_re - 1.0, lb_im
    den = lr * lr + li * li
    f_re = (nr * lr + ni * li) / den
    f_im = (ni * lr - nr * li) / den
    br, bi = br_ref[...], bi_ref[...]
    bb_re = f_re * br - f_im * bi
    bb_im = f_re * bi + f_im * br
    nrow, ncol2 = bblk_ref.shape
    rs, cs = S5_GROUP.bit_length() - 1, S5_STATE.bit_length() - 1
    blocks = lambda a: _diag_blocks(a.reshape(nrow, LANES), ncol2 // 2 // LANES, rs, cs)
    bblk_ref[...] = jnp.concatenate([blocks(bb_re), blocks(bb_im)], axis=1).astype(BF16)
    c_t = jnp.concatenate([blocks(cr_ref[...]), -blocks(ci_ref[...])], axis=1)
    cblk_ref[...] = jnp.transpose(c_t).astype(BF16)
    pwr_ref[0] = lb_re
    pwi_ref[0] = lb_im
    qr, qi = None, None
    sr, si = lb_re, lb_im
    e = seg_len
    while e:
        if e & 1:
            qr, qi = (sr, si) if qr is None else (qr * sr - qi * si, qr * si + qi * sr)
        e >>= 1
        if e:
            sr, si = sr * sr - si * si, 2.0 * sr * si
    pr, pi = qr, qi
    for n in range(SUBLANES):
        pwr_ref[1 + n] = pr
        pwi_ref[1 + n] = pi
        pr, pi = pr * qr - pi * qi, pr * qi + pi * qr


def _s5_discretize(lam_re, lam_im, log_dt, bt_re, bt_im, c_re, c_im, seg_len):
    g = lam_re.shape[0]
    ncb = g // S5_GPB
    assert S5_GROUP & (S5_GROUP - 1) == 0 and S5_STATE & (S5_STATE - 1) == 0
    grp = lambda *dims: pl.BlockSpec((S5_GPB,) + dims, lambda c: (c, 0, 0))
    pw_spec = pl.BlockSpec((1 + SUBLANES, S5_GPB, 1, LANES), lambda c: (0, c, 0, 0))
    return pl.pallas_call(
        functools.partial(_s5_disc_kernel, seg_len=seg_len),
        grid=(ncb,),
        in_specs=[grp(1, LANES), grp(1, LANES), grp(1, 1), grp(S5_GROUP, LANES), grp(S5_GROUP, LANES),
                  grp(S5_GROUP, LANES), grp(S5_GROUP, LANES)],
        out_specs=[pw_spec, pw_spec,
                   pl.BlockSpec((None, S5_UW, 2 * S5_GPB * S5_STATE), lambda c: (c, 0, 0)),
                   pl.BlockSpec((None, 2 * S5_GPB * S5_STATE, S5_UW), lambda c: (c, 0, 0))],
        out_shape=[
            jax.ShapeDtypeStruct((1 + SUBLANES, g, 1, LANES), F32),
            jax.ShapeDtypeStruct((1 + SUBLANES, g, 1, LANES), F32),
            jax.ShapeDtypeStruct((ncb, S5_UW, 2 * S5_GPB * S5_STATE), BF16),
            jax.ShapeDtypeStruct((ncb, 2 * S5_GPB * S5_STATE, S5_UW), BF16),
        ],
        compiler_params=_cparams(("arbitrary",), 32),
        name="s5_discretize",
    )(lam_re, lam_im, log_dt, bt_re, bt_im, c_re, c_im)


def _gelu_tanh(x):
    c0 = math.sqrt(2.0 / math.pi)
    return x * (0.5 * (1.0 + jnp.tanh(c0 * (x + 0.044715 * (x * x * x)))))


def _cmul_add(ar, ai, xr, xi, yr, yi):
    return yr + ar * xr - ai * xi, yi + ar * xi + ai * xr


def _s5_seq_kernel(x_ref, mod_ref, gpre_ref, bblk_ref, cblk_ref, lam_ref, dskip_ref, s0r_ref, s0i_ref,
                   z_ref, sr_ref, si_ref, h_scr, xr_scr, xi_scr, *, d):
    cb = pl.program_id(0)
    ncb, tm, uw = h_scr.shape
    cw = xr_scr.shape[1]
    seg = s0r_ref.shape[0]

    @pl.when(cb == 0)
    def _():
        h = _norm_mod(x_ref[...], gpre_ref[...], mod_ref, d)
        for c in range(ncb):
            h_scr[c] = h[:, c * uw:(c + 1) * uw]

    u = h_scr[cb]
    bu = _dot(u.astype(BF16), bblk_ref[...])
    xr_scr[...] = bu[:, 0:cw]
    xi_scr[...] = bu[:, cw:2 * cw]
    car_r, car_i = s0r_ref[...], s0i_ref[...]
    l_r, l_i = lam_ref[0:1, :], lam_ref[1:2, :]
    for t in range(tm // seg):
        rows = slice(t * seg, (t + 1) * seg)
        car_r, car_i = _cmul_add(l_r, l_i, car_r, car_i, xr_scr[rows, :], xi_scr[rows, :])
        xr_scr[rows, :] = car_r
        xi_scr[rows, :] = car_i
    sr_ref[...] = car_r
    si_ref[...] = car_i
    xs = jnp.concatenate([xr_scr[...].astype(BF16), xi_scr[...].astype(BF16)], axis=1)
    y = _dot(xs, cblk_ref[...]) + dskip_ref[...] * u
    z_ref[...] = _gelu_tanh(y).astype(BF16)


def _s5_seq(x2, mod3, g_pre, bblk, cblk, lam2, dskip, s0_re, s0_im):
    tm, d = x2.shape
    seg, nst = s0_re.shape
    ncb, uw, cw2 = bblk.shape
    cw = cw2 // 2
    sspec = pl.BlockSpec((seg, cw), lambda c: (0, c))
    return pl.pallas_call(
        functools.partial(_s5_seq_kernel, d=d),
        grid=(ncb,),
        in_specs=[
            pl.BlockSpec((tm, d), lambda c: (0, 0)),
            pl.BlockSpec((None, seg, 3 * d), lambda c, lead=getattr(mod3, "lead", 0): (lead, 0, 0)),
            pl.BlockSpec((1, d), lambda c: (0, 0)),
            pl.BlockSpec((None, uw, cw2), lambda c: (c, 0, 0)),
            pl.BlockSpec((None, cw2, uw), lambda c: (c, 0, 0)),
            pl.BlockSpec((2, cw), lambda c: (0, c)),
            pl.BlockSpec((1, uw), lambda c: (0, c)),
            sspec, sspec,
        ],
        out_specs=[pl.BlockSpec((tm, uw), lambda c: (0, c)), sspec, sspec],
        out_shape=[
            jax.ShapeDtypeStruct((tm, d), BF16),
            jax.ShapeDtypeStruct((seg, nst), F32),
            jax.ShapeDtypeStruct((seg, nst), F32),
        ],
        scratch_shapes=[
            pltpu.VMEM((ncb, tm, uw), F32),
            pltpu.VMEM((tm, cw), F32),
            pltpu.VMEM((tm, cw), F32),
        ],
        compiler_params=_cparams(("arbitrary",), 48),
        name="s5_seq",
    )(x2, _arr(mod3), g_pre, bblk, cblk, lam2, dskip, s0_re, s0_im)


def _s5_rows_kernel(*refs, d, ncast):
    x_ref, mod_ref, gpre_ref, bblk_ref, cblk_ref, tbl_ref, dskip_ref = refs[:7]
    cast_in = refs[7:7 + ncast]
    z_ref, sr_ref, si_ref = refs[7 + ncast:10 + ncast]
    cast_out = refs[10 + ncast:10 + 2 * ncast]
    h_scr, xr_scr, xi_scr, cr_scr, ci_scr = refs[10 + 2 * ncast:]
    for src, dst in zip(cast_in, cast_out):
        dst[...] = src[...].astype(BF16)

    ncol, tm, _ = xr_scr.shape
    ncb, uw, _ = bblk_ref.shape
    sl = tm // SUBLANES

    @pl.when(pl.program_id(1) == 0)
    def _():
        cr_scr[...] = jnp.zeros_like(cr_scr)
        ci_scr[...] = jnp.zeros_like(ci_scr)

    h_scr[...] = _norm_mod(x_ref[...], gpre_ref[...], mod_ref, d)
    row0 = lax.broadcasted_iota(jnp.int32, (ncol, SUBLANES, LANES), 1) == 0

    for c in range(ncb):
        us = slice(c * uw, (c + 1) * uw)
        cols = slice(c * ncol, (c + 1) * ncol)
        u = h_scr[:, us]
        bu = _dot(u.astype(BF16), bblk_ref[c])
        for j in range(ncol):
            xr_scr[j] = bu[:, j * LANES:(j + 1) * LANES]
            xi_scr[j] = bu[:, (ncol + j) * LANES:(ncol + j + 1) * LANES]
        l_r, l_i = tbl_ref[0, cols], tbl_ref[1, cols]

        def local(i, s):
            rows = pl.ds(pl.multiple_of(i * SUBLANES, SUBLANES), SUBLANES)
            return _cmul_add(l_r, l_i, s[0], s[1], xr_scr[:, rows, :], xi_scr[:, rows, :])

        zero = jnp.zeros((ncol, SUBLANES, LANES), F32)
        g_r, g_i = lax.fori_loop(0, sl, local, (zero, zero), unroll=True)
        for n in range(3):
            g_r, g_i = _cmul_add(tbl_ref[2 + 2 * n, cols], tbl_ref[3 + 2 * n, cols],
                                 pltpu.roll(g_r, 1 << n, 1), pltpu.roll(g_i, 1 << n, 1), g_r, g_i)
        car_r, car_i = cr_scr[cols], ci_scr[cols]
        g_r, g_i = _cmul_add(tbl_ref[8, cols], tbl_ref[9, cols], car_r, car_i, g_r, g_i)
        in_r = jnp.where(row0, car_r, pltpu.roll(g_r, 1, 1))
        in_i = jnp.where(row0, car_i, pltpu.roll(g_i, 1, 1))

        def full(i, s):
            rows = pl.ds(pl.multiple_of(i * SUBLANES, SUBLANES), SUBLANES)
            s_r, s_i = _cmul_add(l_r, l_i, s[0], s[1], xr_scr[:, rows, :], xi_scr[:, rows, :])
            xr_scr[:, rows, :] = s_r
            xi_scr[:, rows, :] = s_i
            return s_r, s_i

        e_r, e_i = lax.fori_loop(0, sl, full, (in_r, in_i), unroll=True)
        cr_scr[cols] = e_r[:, SUBLANES - 1:SUBLANES, :]
        ci_scr[cols] = e_i[:, SUBLANES - 1:SUBLANES, :]
        xs = jnp.concatenate([xr_scr[j].astype(BF16) for j in range(ncol)]
                             + [xi_scr[j].astype(BF16) for j in range(ncol)], axis=1)
        y = _dot(xs, cblk_ref[c]) + dskip_ref[:, us] * u
        z_ref[:, us] = _gelu_tanh(y).astype(BF16)

    sr_ref[...] = cr_scr[...]
    si_ref[...] = ci_scr[...]


def _s5_rows(x3, mod3, g_pre, bblk, cblk, tbl, dskip, tm, casts=()):
    gn, t, d = x3.shape
    ncb, uw, cw2 = bblk.shape
    ncol = cw2 // 2 // LANES
    nct = ncb * ncol
    nt = t // tm
    nstep = gn * nt
    const = lambda shape: pl.BlockSpec(shape, lambda g, i: (0,) * len(shape), pipeline_mode=pl.Buffered(1))
    ospec = pl.BlockSpec((None, None, nct, 1, LANES), lambda g, i: (g, i, 0, 0, 0))
    c_in, c_out, c_shape = [], [], []
    for w, layer in casts:
        _, rows, cols = w.shape
        slab = rows // nstep
        assert slab * nstep == rows and slab % (2 * SUBLANES) == 0
        c_in.append(pl.BlockSpec((None, slab, cols), lambda g, i, layer=layer: (layer, g * nt + i, 0)))
        c_out.append(pl.BlockSpec((slab, cols), lambda g, i: (g * nt + i, 0)))
        c_shape.append(jax.ShapeDtypeStruct((rows, cols), BF16))
    res = pl.pallas_call(
        functools.partial(_s5_rows_kernel, d=d, ncast=len(casts)),
        grid=(gn, nt),
        in_specs=[
            pl.BlockSpec((None, tm, d), lambda g, i: (g, i, 0)),
            _mod_spec(mod3, tm, 3 * d, 2),
            pl.BlockSpec((1, d), lambda g, i: (0, 0)),
            const(bblk.shape), const(cblk.shape), const(tbl.shape), const(dskip.shape),
        ] + c_in,
        out_specs=[pl.BlockSpec((None, tm, d), lambda g, i: (g, i, 0)), ospec, ospec] + c_out,
        out_shape=[
            jax.ShapeDtypeStruct((gn, t, d), BF16),
            jax.ShapeDtypeStruct((gn, nt, nct, 1, LANES), F32),
            jax.ShapeDtypeStruct((gn, nt, nct, 1, LANES), F32),
        ] + c_shape,
        scratch_shapes=[
            pltpu.VMEM((tm, d), F32),
            pltpu.VMEM((ncol, tm, LANES), F32),
            pltpu.VMEM((ncol, tm, LANES), F32),
            pltpu.VMEM((nct, 1, LANES), F32),
            pltpu.VMEM((nct, 1, LANES), F32),
        ],
        compiler_params=_cparams(("arbitrary", "arbitrary"), 56),
        name="s5_rows",
    )(x3, _arr(mod3),g_pre, bblk, cblk, tbl, dskip, *[w for w, _ in casts])
    z3, s_re, s_im = res[:3]
    last = lambda s: s[:, nt - 1].reshape(gn, 1, nct * LANES)
    return z3, last(s_re), last(s_im), list(res[3:])


def _glu_kernel(x_ref, mod_ref, g_ref, z_ref, wa_ref, wb_ref, *outs, d):
    for rows in _row_parts(x_ref.shape[0]):
        z = z_ref[rows, :]
        y = _dot(z, wa_ref[...]) * jax.nn.sigmoid(_dot(z, wb_ref[...]))
        _residual_out(x_ref, y, g_ref, mod_ref, outs, d, rows)


def _glu(x3, mod3, g_post, z3, w_a, w_b, tm, nxt=None):
    gn, t, d = x3.shape
    wspec = pl.BlockSpec((d, d), lambda g, i: (0, 0), pipeline_mode=pl.Buffered(1))
    n_in, n_out, n_shape, n_ops = _next_specs(None if nxt is None else (*nxt, x3.shape), tm, d)
    res = pl.pallas_call(
        functools.partial(_glu_kernel, d=d),
        grid=(gn, t // tm),
        in_specs=[
            pl.BlockSpec((None, tm, d), lambda g, i: (g, i, 0)),
            _mod_spec(mod3, tm, 3 * d, 2),
            pl.BlockSpec((1, d), lambda g, i: (0, 0)),
            pl.BlockSpec((None, tm, d), lambda g, i: (g, i, 0)),
            wspec, wspec,
        ] + n_in,
        out_specs=[pl.BlockSpec((None, tm, d), lambda g, i: (g, i, 0))] + n_out,
        out_shape=[jax.ShapeDtypeStruct((gn, t, d), F32)] + n_shape,
        compiler_params=_cparams(("parallel", "parallel"), 56),
        name="glu",
    )(x3, _arr(mod3),g_post, z3, w_a, w_b, *n_ops)
    return res if nxt is not None else (res[0], None)


def _rope_tables(pos):
    half = HEAD_DK // 2
    inv = ROPE_BASE ** (-jnp.arange(half, dtype=F32) / half)
    ang = pos.astype(F32)[:, None] * inv[None, :]
    cos, sin = jnp.cos(ang), jnp.sin(ang)
    return jnp.concatenate([cos, cos], axis=-1), jnp.concatenate([-sin, sin], axis=-1)


def _s5_tables(pw_re, pw_im):
    n = pw_re.shape[0]
    flat_r = pw_re.reshape(n, -1)
    flat_i = pw_im.reshape(n, -1)
    row = jnp.arange(SUBLANES)[:, None]
    tabs = [jnp.broadcast_to(flat_r[0], (SUBLANES, flat_r.shape[1])),
            jnp.broadcast_to(flat_i[0], (SUBLANES, flat_i.shape[1]))]
    for s in (1, 2, 4):
        mask = row >= s
        tabs.append(jnp.where(mask, flat_r[s][None, :], 0.0))
        tabs.append(jnp.where(mask, flat_i[s][None, :], 0.0))
    tabs += [flat_r[1:], flat_i[1:]]
    tbl = jnp.stack(tabs)
    tbl = tbl.reshape(tbl.shape[0], SUBLANES, -1, LANES).transpose(0, 2, 1, 3)
    return tbl, jnp.stack([flat_r[0], flat_i[0]])


def kernel(x_prompt, x_sample, state_gla, state_ret, state_s5_re, state_s5_im, c_prompt, c_sample,
           w_ada, b_ada, norm_pre, norm_post, w_in_mix, w_gla_gk, b_gla_gk, gla_head_norm,
           ret_head_norm, w_out_mix, s5_lam_re, s5_lam_im, s5_log_dt, s5_b_re, s5_b_im,
           s5_c_re, s5_c_im, s5_d, w_glu_a, w_glu_b, w_mlp_up, w_mlp_down):
    bp, tp, d = x_prompt.shape
    bs, ts, _ = x_sample.shape
    depth = w_ada.shape[0]

    nrow = -(-(bs + bp) // SUBLANES) * SUBLANES
    c_all = jnp.concatenate([c_sample, c_prompt, jnp.zeros((nrow - bs - bp, d), F32)], axis=0)
    mod_all = _adaln(c_all, w_ada.reshape(depth * 2, d, 3 * d), b_ada.reshape(depth * 2, 1, 3 * d))
    mod_s = [_ModSlab(mod_all, k, bs) for k in range(depth * 2)]
    mod_p = [mod_all[k, bs:bs + bp][:, None, :] for k in range(depth * 2)]

    w_in_t = jnp.swapaxes(w_in_mix, 1, 2)
    w_gk = jnp.pad(w_gla_gk[0], ((0, LANES - GLA_RANK), (0, 0))).astype(BF16)
    b_gk = b_gla_gk[0][None, :]
    w_out = w_out_mix[0].astype(BF16)
    gla_norm = gla_head_norm[0][:, None, :]
    ret_norm = ret_head_norm[0][:, None, :]
    gamma_log = jnp.log1p(-jnp.power(2.0, -5.0 - jnp.arange(RET_HEADS, dtype=F32)))
    ret_lg = jnp.broadcast_to(gamma_log[:, None, None], (RET_HEADS, 1, LANES))

    ng = s5_lam_re.shape[1]
    tm5 = min(TM_S5, tp)
    per_state = lambda a: jnp.tile(a, (1,) * (a.ndim - 1) + (LANES // S5_STATE,))
    bt = lambda a: per_state(jnp.swapaxes(a[0], 1, 2))
    pw_re, pw_im, bblk, cblk = _s5_discretize(
        per_state(s5_lam_re[0])[:, None, :], per_state(s5_lam_im[0])[:, None, :],
        s5_log_dt[0][:, None, None], bt(s5_b_re), bt(s5_b_im), per_state(s5_c_re[0]), per_state(s5_c_im[0]),
        tm5 // SUBLANES)
    tbl, lam2 = _s5_tables(pw_re[:, :, 0, :S5_STATE], pw_im[:, :, 0, :S5_STATE])
    dskip = s5_d[0][None, :]

    nxt = lambda mods, l, emit_h: (mods[2 * l + 1], norm_pre[l, 1][None]) if emit_h else None

    def layer0(x3, mods, tm, inproj, attn, mlp, emit_h):
        proj, glog = inproj(x3, mods[0])
        mg, mr, s_gla, s_ret = attn(proj, glog)
        x3, h3 = _outproj(x3, mods[0], norm_post[0, 0][None], mg, mr, w_out, tm, nxt(mods, 0, emit_h))
        return mlp(0, x3, h3, mods[1]), s_gla, s_ret

    def layer1_tail(x3, z3, mods, tm, glu_w, mlp, emit_h):
        x3, h3 = _glu(x3, mods[2], norm_post[1, 0][None], z3, glu_w[0], glu_w[1], tm, nxt(mods, 1, emit_h))
        return mlp(1, x3, h3, mods[3])

    cos_s, sin_s = _rope_tables(PAST_LEN + jnp.arange(ts, dtype=F32))

    def attn_s(proj, glog):
        tm_rows = lambda a: a.reshape(ts, bs, a.shape[-1])
        mg, mr, s_gla, s_ret = _attn_sample(tm_rows(proj), tm_rows(glog), gla_norm, ret_norm, cos_s, sin_s,
                                            ret_lg, state_gla[0], state_ret[0])
        flat = lambda a: a.reshape(1, ts * bs, a.shape[-1])
        return flat(mg), flat(mr), s_gla, s_ret

    xs3 = jnp.swapaxes(x_sample, 0, 1).reshape(1, ts * bs, d)

    def s5_s(x3, mod3):
        z2, s_re, s_im = _s5_seq(x3[0], mod3, norm_pre[1, 0][None], bblk, cblk, lam2, dskip,
                                 state_s5_re[0].reshape(bs, -1), state_s5_im[0].reshape(bs, -1))
        return z2[None], s_re, s_im

    w_up, w_dn, w_main = {}, {}, {}

    def inproj_s(x3, mod3):
        proj, glog, w_main[0], w_main["lr"] = _inproj_cast(x3, mod3, norm_pre[0, 0][None], w_in_t, 0,
                                                            w_gk, b_gk)
        return proj, glog

    def mlp_s(l, x3, h3, mod3):
        g_pre, g_post = norm_pre[l, 1][None], norm_post[l, 1][None]
        if l in w_up:
            return _mlp(x3, mod3, g_pre, g_post, w_up[l], w_dn[l])
        x3, w_up[l], w_dn[l] = _mlp_cast(x3, mod3, g_pre, g_post, w_mlp_up, w_mlp_down, l)
        return x3

    tm_s = ts * bs
    xs1, gla_s, ret_s = layer0(xs3, mod_s, tm_s, inproj_s, attn_s, mlp_s, False)

    cos_p, sin_p = _rope_tables(jnp.arange(tp, dtype=F32))
    zeros_att = jnp.zeros((bp, GLA_HEADS, HEAD_DK, HEAD_DV), F32)
    tm_p = min(TM_DENSE, tp)

    def attn_p(proj, glog):
        return _attn_prompt(proj, glog, gla_norm, ret_norm, cos_p, sin_p, ret_lg, zeros_att, zeros_att)

    def mlp_p(l, x3, h3, mod3):
        return _mlp_h(x3, h3, mod3, norm_post[l, 1][None], w_up[l], w_dn[l], tm_p)

    def inproj_p(x3, mod3):
        return _inproj(x3, mod3, norm_pre[0, 0][None], w_main[0], w_main["lr"], w_gk, b_gk, tm_p)

    xp1, gla_p, ret_p = layer0(x_prompt, mod_p, tm_p, inproj_p, attn_p, mlp_p, True)

    sl = tm5 // SUBLANES
    xpp = jnp.swapaxes(xp1.reshape(bp, tp // tm5, SUBLANES, sl, d), 2, 3).reshape(bp, tp, d)
    zpp, re_p, im_p, (w_up[1], w_dn[1], w_ga, w_gb) = _s5_rows(
        xpp, mod_p[2], norm_pre[1, 0][None], bblk, cblk, tbl, dskip, tm5,
        casts=((w_mlp_up, 1), (w_mlp_down, 1), (w_glu_a, 0), (w_glu_b, 0)))
    zp3 = jnp.swapaxes(zpp.reshape(bp, tp // tm5, sl, SUBLANES, d), 2, 3).reshape(bp, tp, d)

    zs3, re_s, im_s = s5_s(xs1, mod_s[2])
    y_s = layer1_tail(xs1, zs3, mod_s, tm_s, (w_ga, w_gb), mlp_s, False)
    y_s = jnp.swapaxes(y_s.reshape(ts, bs, d), 0, 1)
    y_p = layer1_tail(xp1, zp3, mod_p, tm_p, (w_ga, w_gb), mlp_p, True)

    st = lambda a, b_: a.reshape(1, b_, ng, S5_STATE)
    return (y_p, y_s, gla_p[None], gla_s[None], ret_p[None], ret_s[None],
            st(re_p, bp), st(re_s, bs), st(im_p, bp), st(im_s, bs))
```

```python
import functools
import math

import jax
import jax.numpy as jnp
import numpy as np
from jax import lax
from jax.experimental import pallas as pl
from jax.experimental.pallas import tpu as pltpu

F32 = jnp.float32
BF16 = jnp.bfloat16

EPS = 1e-6
LANES = 128
SUBLANES = 8
MIB = 1024 * 1024

GLA_HEADS = 4
RET_HEADS = 4
HEAD_DK = 128
HEAD_DV = 256
GLA_RANK = 16
GLA_LOGIT_NORM = 16.0
ROPE_BASE = 10000.0
PAST_LEN = 16384
S5_GROUP = 16
S5_STATE = 64
S5_GPB = 16
S5_UW = S5_GPB * S5_GROUP
ATT_CHUNK = 256
GLA_SUB = 16
TM_DENSE = 512
TM_S5 = 256
TN_INPROJ = 1024
TF_MLP = 1024


def _cparams(sem, vmem_mib):
    return pltpu.CompilerParams(dimension_semantics=sem, vmem_limit_bytes=vmem_mib * MIB)


def _dot(a, b):
    return jnp.dot(a, b, preferred_element_type=F32)


def _dot_nt(a, b):
    return lax.dot_general(a, b, (((1,), (1,)), ((), ())), preferred_element_type=F32)


def _rms(x, g):
    return x * lax.rsqrt(jnp.mean(x * x, axis=-1, keepdims=True) + EPS) * g


def _rows_affine(y, a, b=None):
    tm, d = y.shape
    r = a.shape[0]
    if r == 1 or r == tm:
        out = y * a
        return out if b is None else out + b
    y3 = y.reshape(tm // r, r, d)
    out = y3 * a[None]
    if b is not None:
        out = out + b[None]
    return out.reshape(tm, d)


def _norm_mod(x, g, mod_ref, d):
    return _rows_affine(_rms(x, g), 1.0 + mod_ref[:, d:2 * d], mod_ref[:, 0:d])


def _gated_residual(x, y, g, mod_ref, d):
    return x + _rows_affine(_rms(y, g), mod_ref[:, 2 * d:3 * d])


class _ModSlab:
    def __init__(self, arr, lead, rows):
        self.arr, self.lead = arr, lead
        self.shape = (1, rows, arr.shape[2])


def _arr(mod):
    return mod.arr if isinstance(mod, _ModSlab) else mod


def _mod_spec(mod, tm, width, ngrid):
    r = mod.shape[1]
    if isinstance(mod, _ModSlab):
        lead = mod.lead
        index = (lambda g, i: (lead, 0, 0)) if ngrid == 2 else (lambda g, i, j: (lead, 0, 0))
    else:
        index = (lambda g, i: (g, 0, 0)) if ngrid == 2 else (lambda g, i, j: (g, 0, 0))
    return pl.BlockSpec((None, r, width), index)


def _adaln_kernel(c_ref, w_ref, b_ref, o_ref):
    c = c_ref[...]
    sc = (c * jax.nn.sigmoid(c)).astype(BF16)
    o_ref[...] = _dot(sc, w_ref[...].astype(BF16)) + b_ref[...]


def _adaln(c_all, w_ada, b_ada, tn=1024):
    ls, d, n = w_ada.shape
    rows = c_all.shape[0]
    return pl.pallas_call(
        _adaln_kernel,
        grid=(ls, n // tn),
        in_specs=[
            pl.BlockSpec((rows, d), lambda l, j: (0, 0)),
            pl.BlockSpec((None, d, tn), lambda l, j: (l, 0, j)),
            pl.BlockSpec((None, 1, tn), lambda l, j: (l, 0, j)),
        ],
        out_specs=pl.BlockSpec((None, rows, tn), lambda l, j: (l, 0, j)),
        out_shape=jax.ShapeDtypeStruct((ls, rows, n), F32),
        compiler_params=_cparams(("parallel", "parallel"), 40),
        name="adaln",
    )(c_all, w_ada, b_ada)


def _log_sigmoid(x):
    return jnp.minimum(x, 0.0) - jnp.log1p(jnp.exp(-jnp.abs(x)))


def _gate_logits(hb, wlr_t, wgk_ref, bgk_ref, glog_ref):
    glr = _dot_nt(hb, wlr_t)
    logit = _dot(glr.astype(BF16), wgk_ref[...]) + bgk_ref[...]
    glog_ref[...] = _log_sigmoid(logit) * (1.0 / GLA_LOGIT_NORM)


def _inproj_kernel(x_ref, mod_ref, g_ref, w_ref, wlr_ref, wgk_ref, bgk_ref,
                   proj_ref, glog_ref, h_scr, *, d, tps):
    j = pl.program_id(2)

    @pl.when(j == 0)
    def _():
        hb = _norm_mod(x_ref[...], g_ref[...], mod_ref, d).astype(BF16)
        h_scr[...] = hb
        _gate_logits(hb, wlr_ref[...], wgk_ref, bgk_ref, glog_ref)

    tn = w_ref.shape[2]
    for k in range(tps):
        proj_ref[:, k * tn:(k + 1) * tn] = _dot(h_scr[...], w_ref[j * tps + k])


def _inproj_cast_kernel(x_ref, mod_ref, g_ref, wa_ref, wb_ref, wgk_ref, bgk_ref,
                        proj_ref, glog_ref, wout_ref, wlr_ref, h_scr, *, d, n_lo):
    j = pl.program_id(2)

    @pl.when(j == 0)
    def _():
        h_scr[...] = _norm_mod(x_ref[...], g_ref[...], mod_ref, d).astype(BF16)

    @pl.when(j < n_lo)
    def _():
        wout_ref[...] = jnp.transpose(wa_ref[...]).astype(BF16)

    @pl.when(j >= n_lo)
    def _():
        w = jnp.concatenate([wa_ref[GLA_RANK:, :], wb_ref[:GLA_RANK, :]], axis=0)
        wout_ref[...] = jnp.transpose(w).astype(BF16)

    @pl.when(j == n_lo)
    def _():
        wlr_t = _pad_rows(wa_ref[:GLA_RANK, :], LANES).astype(BF16)
        wlr_ref[...] = wlr_t
        _gate_logits(h_scr[...], wlr_t, wgk_ref, bgk_ref, glog_ref)

    proj_ref[...] = _dot(h_scr[...], wout_ref[...])


def _inproj_cast(x3, mod3, g_pre, w_raw_t, layer, w_gk, b_gk, tn=TN_INPROJ):
    gn, t, d = x3.shape
    assert gn == 1
    sec = (w_raw_t.shape[1] - GLA_RANK) // 2
    assert sec % tn == 0 and tn % LANES == 0
    nj = 2 * sec // tn
    n = nj * tn
    gkey = w_gk.shape[1]
    return pl.pallas_call(
        functools.partial(_inproj_cast_kernel, d=d, n_lo=sec // tn),
        grid=(1, 1, nj),
        in_specs=[
            pl.BlockSpec((None, t, d), lambda g, i, j: (0, 0, 0)),
            _mod_spec(mod3, t, 3 * d, 3),
            pl.BlockSpec((1, d), lambda g, i, j: (0, 0)),
            pl.BlockSpec((None, tn, d), lambda g, i, j: (layer, j, 0)),
            pl.BlockSpec((None, LANES, d), lambda g, i, j: (layer, (j + 1) * (tn // LANES), 0)),
            pl.BlockSpec((LANES, gkey), lambda g, i, j: (0, 0)),
            pl.BlockSpec((1, gkey), lambda g, i, j: (0, 0)),
        ],
        out_specs=[
            pl.BlockSpec((None, t, tn), lambda g, i, j: (0, 0, j)),
            pl.BlockSpec((None, t, gkey), lambda g, i, j: (0, 0, 0)),
            pl.BlockSpec((None, d, tn), lambda g, i, j: (j, 0, 0)),
            pl.BlockSpec((LANES, d), lambda g, i, j: (0, 0)),
        ],
        out_shape=[
            jax.ShapeDtypeStruct((1, t, n), F32),
            jax.ShapeDtypeStruct((1, t, gkey), F32),
            jax.ShapeDtypeStruct((nj, d, tn), BF16),
            jax.ShapeDtypeStruct((LANES, d), BF16),
        ],
        scratch_shapes=[pltpu.VMEM((t, d), BF16)],
        compiler_params=_cparams(("arbitrary", "arbitrary", "arbitrary"), 56),
        name="inproj_cast",
    )(x3, _arr(mod3),g_pre, w_raw_t, w_raw_t, w_gk, b_gk)


def _inproj(x3, mod3, g_pre, w_main, w_lr_t, w_gk, b_gk, tm, tps=3):
    gn, t, d = x3.shape
    nj, _, tn = w_main.shape
    n = nj * tn
    gkey = w_gk.shape[1]
    return pl.pallas_call(
        functools.partial(_inproj_kernel, d=d, tps=tps),
        grid=(gn, t // tm, nj // tps),
        in_specs=[
            pl.BlockSpec((None, tm, d), lambda g, i, j: (g, i, 0)),
            _mod_spec(mod3, tm, 3 * d, 3),
            pl.BlockSpec((1, d), lambda g, i, j: (0, 0)),
            pl.BlockSpec((nj, d, tn), lambda g, i, j: (0, 0, 0), pipeline_mode=pl.Buffered(1)),
            pl.BlockSpec((LANES, d), lambda g, i, j: (0, 0)),
            pl.BlockSpec((LANES, gkey), lambda g, i, j: (0, 0)),
            pl.BlockSpec((1, gkey), lambda g, i, j: (0, 0)),
        ],
        out_specs=[
            pl.BlockSpec((None, tm, tps * tn), lambda g, i, j: (g, i, j)),
            pl.BlockSpec((None, tm, gkey), lambda g, i, j: (g, i, 0)),
        ],
        out_shape=[
            jax.ShapeDtypeStruct((gn, t, n), F32),
            jax.ShapeDtypeStruct((gn, t, gkey), F32),
        ],
        scratch_shapes=[pltpu.VMEM((tm, d), BF16)],
        compiler_params=_cparams(("parallel", "parallel", "arbitrary"), 56),
        name="inproj",
    )(x3, _arr(mod3),g_pre, w_main, w_lr_t, w_gk, b_gk)


def _cumsum_rows(g):
    c = g.shape[0]
    row = lax.broadcasted_iota(jnp.int32, g.shape, 0)
    s = 1
    while s < c:
        g = g + jnp.where(row >= s, pltpu.roll(g, s, 0), 0.0)
        s *= 2
    return g


def _pad_rows(a, rows):
    if a.shape[0] == rows:
        return a
    return jnp.concatenate([a, jnp.zeros((rows - a.shape[0], a.shape[1]), a.dtype)], axis=0)


def _col_bcast(row, width):
    sq = jnp.transpose(jnp.broadcast_to(row, (LANES, LANES)))
    return jnp.concatenate([sq] * (width // LANES), axis=1)


def _gla_core(q, k, v, g, s, sub):
    cq = q.shape[0]
    ck = max(cq, LANES)
    b = _cumsum_rows(g)
    be = b - g
    bk = _pad_rows(b, ck)
    kp = _pad_rows(k, ck)
    vp = _pad_rows(v, ck).astype(BF16)
    rowj = lax.broadcasted_iota(jnp.int32, (ck, 1), 0)
    att_rows = []
    for blk in range(cq // sub):
        lo, hi = blk * sub, (blk + 1) * sub
        base = be[lo:lo + 1, :]
        qs = q[lo:hi] * jnp.exp(b[lo:hi] - base)
        ks = jnp.where(rowj < hi, kp * jnp.exp(base - bk), 0.0)
        att_rows.append(_dot_nt(qs.astype(BF16), ks.astype(BF16)))
    att = att_rows[0] if len(att_rows) == 1 else jnp.concatenate(att_rows, axis=0)
    ri = lax.broadcasted_iota(jnp.int32, (cq, ck), 0)
    cj = lax.broadcasted_iota(jnp.int32, (cq, ck), 1)
    att = jnp.where(ri >= cj, att, 0.0)
    o = _dot(att.astype(BF16), vp) + _dot((q * jnp.exp(b)).astype(BF16), s.astype(BF16))
    b_last = b[cq - 1:cq, :]
    k_out = kp * jnp.exp(b_last - bk)
    s_new = s * _col_bcast(jnp.exp(b_last), s.shape[1]) + _dot(jnp.transpose(k_out).astype(BF16), vp)
    return o, s_new


def _ret_core(q, k, v, s, lg, dmat, valid):
    cq = q.shape[0]
    ck = max(cq, LANES)
    kp = _pad_rows(k, ck)
    vp = _pad_rows(v, ck).astype(BF16)
    ti = lax.broadcasted_iota(jnp.int32, (cq, 1), 0).astype(F32)
    tj = lax.broadcasted_iota(jnp.int32, (ck, 1), 0).astype(F32)
    att = _dot_nt(q.astype(BF16), kp.astype(BF16)) * dmat
    q_in = q * jnp.exp((ti + 1.0) * lg)
    o = _dot(att.astype(BF16), vp) + _dot(q_in.astype(BF16), s.astype(BF16))
    k_out = kp * jnp.exp((float(valid - 1) - tj) * lg)
    s_new = s * jnp.exp(float(valid) * lg) + _dot(jnp.transpose(k_out).astype(BF16), vp)
    return o, s_new


def _decay_matrix(cq, ck, lg):
    ri = lax.broadcasted_iota(jnp.int32, (cq, ck), 0)
    cj = lax.broadcasted_iota(jnp.int32, (cq, ck), 1)
    diff = (ri - cj).astype(F32)
    return jnp.where(ri >= cj, jnp.exp(diff * lg), 0.0)


def _rope(x, cosf, sinf):
    return x * cosf + pltpu.roll(x, x.shape[1] // 2, 1) * sinf


def _silu(x):
    return x * jax.nn.sigmoid(x)


def _gla_finish(o, gate, gn):
    o = o * lax.rsqrt(jnp.mean(o * o, axis=-1, keepdims=True) + EPS) * gn
    return (o * _silu(gate)).astype(BF16)


def _ret_finish(o, gate, gn):
    oc = o - jnp.mean(o, axis=-1, keepdims=True)
    oc = oc * lax.rsqrt(jnp.mean(oc * oc, axis=-1, keepdims=True) + EPS) * gn
    return (oc * _silu(gate)).astype(BF16)


def _head(ref, h, width):
    return ref[:, h * width:(h + 1) * width]


def _attn_prompt_kernel(gq_ref, gk_ref, gv_ref, gg_ref, gl_ref, ggn_ref, gs0_ref,
                        rq_ref, rk_ref, rv_ref, rg_ref, cos_ref, sin_ref, lg_ref, rgn_ref, rs0_ref,
                        go_ref, gs_ref, ro_ref, rs_ref, d_scr):
    @pl.when(pl.program_id(1) == 0)
    def _():
        gs_ref[...] = gs0_ref[...]
        rs_ref[...] = rs0_ref[...]
        for h in range(d_scr.shape[0]):
            d_scr[h] = _decay_matrix(d_scr.shape[1], d_scr.shape[2], lg_ref[h][:, 0:1])

    cosf, sinf = cos_ref[...], sin_ref[...]
    for h in range(gs_ref.shape[0]):
        vcols = slice(h * HEAD_DV, (h + 1) * HEAD_DV)
        q = _head(gq_ref, h, HEAD_DK) * (HEAD_DK ** -0.5)
        o, s_new = _gla_core(q, _head(gk_ref, h, HEAD_DK), _head(gv_ref, h, HEAD_DV),
                             _head(gl_ref, h, HEAD_DK), gs_ref[h], GLA_SUB)
        gs_ref[h] = s_new
        go_ref[:, vcols] = _gla_finish(o, _head(gg_ref, h, HEAD_DV), ggn_ref[h])
        q = _rope(_head(rq_ref, h, HEAD_DK), cosf, sinf)
        k = _rope(_head(rk_ref, h, HEAD_DK), cosf, sinf) * (HEAD_DK ** -0.5)
        o, s_new = _ret_core(q, k, _head(rv_ref, h, HEAD_DV), rs_ref[h], lg_ref[h][:, 0:1], d_scr[h],
                             q.shape[0])
        rs_ref[h] = s_new
        ro_ref[:, vcols] = _ret_finish(o, _head(rg_ref, h, HEAD_DV), rgn_ref[h])


def _attn_prompt(proj, glog, gla_norm, ret_norm, cosf, sinf, ret_lg, s0_gla, s0_ret):
    bsz, t, _ = proj.shape
    c = ATT_CHUNK
    nh = GLA_HEADS
    kw, vw = nh * HEAD_DK, nh * HEAD_DV
    kspec = lambda blk: pl.BlockSpec((None, c, kw), lambda b, i, blk=blk: (b, i, blk))
    vspec = lambda blk: pl.BlockSpec((None, c, vw), lambda b, i, blk=blk: (b, i, blk))
    hspec = pl.BlockSpec((nh, 1, HEAD_DV), lambda b, i: (0, 0, 0))
    sspec = pl.BlockSpec((None, nh, HEAD_DK, HEAD_DV), lambda b, i: (b, 0, 0, 0))
    ospec = pl.BlockSpec((None, c, vw), lambda b, i: (b, i, 0))
    tspec = pl.BlockSpec((c, HEAD_DK), lambda b, i: (i, 0))
    o_shape = jax.ShapeDtypeStruct((bsz, t, vw), BF16)
    s_shape = jax.ShapeDtypeStruct((bsz, nh, HEAD_DK, HEAD_DV), F32)
    mg, s_gla, mr, s_ret = pl.pallas_call(
        _attn_prompt_kernel,
        grid=(bsz, t // c),
        in_specs=[kspec(0), kspec(1), vspec(1), vspec(2), kspec(0), hspec, sspec,
                  kspec(6), kspec(7), vspec(4), vspec(5), tspec, tspec,
                  pl.BlockSpec((nh, 1, LANES), lambda b, i: (0, 0, 0)), hspec, sspec],
        out_specs=[ospec, sspec, ospec, sspec],
        out_shape=[o_shape, s_shape, o_shape, s_shape],
        scratch_shapes=[pltpu.VMEM((nh, c, c), F32)],
        compiler_params=_cparams(("parallel", "arbitrary"), 32),
        name="attn_prompt",
    )(proj, proj, proj, proj, glog, gla_norm, s0_gla,
      proj, proj, proj, proj, cosf, sinf, ret_lg, ret_norm, s0_ret)
    return mg, mr, s_gla, s_ret


def _seq_rows(ref):
    ts, bb, w = ref.shape
    return ref[...].reshape(ts * bb, w)


def _seq_masks(n, bb):
    r = lax.broadcasted_iota(jnp.int32, (n, n), 0)
    c = lax.broadcasted_iota(jnp.int32, (n, n), 1)
    return (r % bb == c % bb) & (r >= c), (r - c).astype(F32) * (1.0 / bb)


def _seq_state_terms(q_in, k_out, v, s0, bb):
    n, dk = q_in.shape
    rown = lax.broadcasted_iota(jnp.int32, (n, 1), 0) % bb
    q_bd = jnp.concatenate([jnp.where(rown == j, q_in, 0.0) for j in range(bb)], axis=1)
    o_inter = _dot(q_bd.astype(BF16), s0.astype(BF16))
    k_t = jnp.transpose(_pad_rows(k_out, LANES))
    coln = lax.broadcasted_iota(jnp.int32, (1, LANES), 1) % bb
    k_bd = jnp.concatenate([jnp.where(coln == j, k_t, 0.0) for j in range(bb)], axis=0)
    ds = _dot(k_bd.astype(BF16), _pad_rows(v, LANES).astype(BF16))
    return o_inter, ds


def _gla_sample_kernel(q_ref, k_ref, v_ref, gg_ref, gl_ref, gn_ref, s0_ref, o_ref, s_ref):
    ts, bb, dk = q_ref.shape
    dv = v_ref.shape[2]
    n = ts * bb
    q = _seq_rows(q_ref) * (dk ** -0.5)
    k, v, g = _seq_rows(k_ref), _seq_rows(v_ref), _seq_rows(gl_ref)
    steps = [g[0:bb]]
    for t in range(1, ts):
        steps.append(steps[-1] + g[t * bb:(t + 1) * bb])
    b = jnp.concatenate(steps, axis=0)
    b_last = steps[-1]
    q_in = q * jnp.exp(b)
    mask, _ = _seq_masks(n, bb)
    att = jnp.where(mask, _dot_nt(q_in.astype(BF16), (k * jnp.exp(-b)).astype(BF16)), 0.0)
    k_out = k * jnp.exp(jnp.concatenate([b_last] * ts, axis=0) - b)
    s0 = s0_ref[...].reshape(bb * dk, dv)
    o_inter, ds = _seq_state_terms(q_in, k_out, v, s0, bb)
    o = _dot(att.astype(BF16), v.astype(BF16)) + o_inter
    e_last = jnp.exp(b_last)
    dec = jnp.concatenate([_col_bcast(e_last[j:j + 1, :], dv) for j in range(bb)], axis=0)
    s_ref[...] = (s0 * dec + ds).reshape(bb, dk, dv)
    o_ref[...] = _gla_finish(o, _seq_rows(gg_ref), gn_ref[...]).reshape(ts, bb, dv)


def _ret_sample_kernel(q_ref, k_ref, v_ref, rg_ref, cos_ref, sin_ref, lg_ref, gn_ref, s0_ref,
                       o_ref, s_ref):
    ts, bb, dk = q_ref.shape
    dv = v_ref.shape[2]
    n = ts * bb
    lg = lg_ref[:, 0:1]
    rows = lambda tab: jnp.concatenate(
        [jnp.broadcast_to(tab[t:t + 1, :], (bb, dk)) for t in range(ts)], axis=0)
    cosf, sinf = rows(cos_ref[...]), rows(sin_ref[...])
    q = _rope(_seq_rows(q_ref), cosf, sinf)
    k = _rope(_seq_rows(k_ref), cosf, sinf) * (dk ** -0.5)
    v = _seq_rows(v_ref)
    mask, dt = _seq_masks(n, bb)
    att = _dot_nt(q.astype(BF16), k.astype(BF16)) * jnp.where(mask, jnp.exp(dt * lg), 0.0)
    tt = (lax.broadcasted_iota(jnp.int32, (n, 1), 0) // bb).astype(F32)
    q_in = q * jnp.exp((tt + 1.0) * lg)
    k_out = k * jnp.exp((float(ts - 1) - tt) * lg)
    s0 = s0_ref[...].reshape(bb * dk, dv)
    o_inter, ds = _seq_state_terms(q_in, k_out, v, s0, bb)
    o = _dot(att.astype(BF16), v.astype(BF16)) + o_inter
    s_ref[...] = (s0 * jnp.exp(float(ts) * lg) + ds).reshape(bb, dk, dv)
    o_ref[...] = _ret_finish(o, _seq_rows(rg_ref), gn_ref[...]).reshape(ts, bb, dv)


def _attn_sample(proj, glog, gla_norm, ret_norm, cosf, sinf, ret_lg, s0_gla, s0_ret, bb=16):
    ts, bsz, _ = proj.shape
    grid = (bsz // bb, GLA_HEADS)
    nk = GLA_HEADS
    kspec = lambda off: pl.BlockSpec((ts, bb, HEAD_DK), lambda i, h, off=off: (0, i, off + h))
    vspec = lambda off: pl.BlockSpec((ts, bb, HEAD_DV), lambda i, h, off=off: (0, i, off + h))
    hspec = pl.BlockSpec((None, 1, HEAD_DV), lambda i, h: (h, 0, 0))
    sspec = pl.BlockSpec((bb, None, HEAD_DK, HEAD_DV), lambda i, h: (i, h, 0, 0))
    ospec = pl.BlockSpec((ts, bb, HEAD_DV), lambda i, h: (0, i, h))
    out_shape = [
        jax.ShapeDtypeStruct((ts, bsz, GLA_HEADS * HEAD_DV), BF16),
        jax.ShapeDtypeStruct((bsz, GLA_HEADS, HEAD_DK, HEAD_DV), F32),
    ]
    params = _cparams(("parallel", "parallel"), 40)
    mg, s_gla = pl.pallas_call(
        _gla_sample_kernel,
        grid=grid,
        in_specs=[kspec(0), kspec(nk), vspec(nk), vspec(2 * nk),
                  pl.BlockSpec((ts, bb, HEAD_DK), lambda i, h: (0, i, h)),
                  hspec, sspec],
        out_specs=[ospec, sspec],
        out_shape=out_shape,
        compiler_params=params,
        name="gla_sample",
    )(proj, proj, proj, proj, glog, gla_norm, s0_gla)
    tspec = pl.BlockSpec((ts, HEAD_DK), lambda i, h: (0, 0))
    mr, s_ret = pl.pallas_call(
        _ret_sample_kernel,
        grid=grid,
        in_specs=[kspec(6 * nk), kspec(7 * nk), vspec(4 * nk), vspec(5 * nk),
                  tspec, tspec,
                  pl.BlockSpec((None, 1, LANES), lambda i, h: (h, 0, 0)),
                  hspec, sspec],
        out_specs=[ospec, sspec],
        out_shape=out_shape,
        compiler_params=params,
        name="ret_sample",
    )(proj, proj, proj, proj, cosf, sinf, ret_lg, ret_norm, s0_ret)
    return mg, mr, s_gla, s_ret


ROW_SPLITS = 2


def _row_parts(tm):
    step = tm // ROW_SPLITS
    return [slice(k * step, (k + 1) * step) for k in range(ROW_SPLITS)]


def _residual_out(x_ref, y, g_ref, mod_ref, outs, d, rows):
    x_new = _gated_residual(x_ref[rows, :], y, g_ref[...], mod_ref, d)
    if len(outs) == 1:
        outs[0][rows, :] = x_new
    else:
        modn_ref, gn_ref, o_ref, h_ref = outs
        o_ref[rows, :] = x_new
        h_ref[rows, :] = _norm_mod(x_new, gn_ref[...], modn_ref, d).astype(BF16)


def _next_specs(nxt, tm, d):
    if nxt is None:
        return [], [], [], []
    mod_next, g_next, shape = nxt
    return ([_mod_spec(mod_next, tm, 3 * d, 2), pl.BlockSpec((1, d), lambda g, i: (0, 0))],
            [pl.BlockSpec((None, tm, d), lambda g, i: (g, i, 0))],
            [jax.ShapeDtypeStruct(shape, BF16)], [_arr(mod_next), g_next])


def _outproj_kernel(x_ref, mod_ref, g_ref, mg_ref, mr_ref, wo_ref, *outs, d):
    half = mg_ref.shape[1]
    for rows in _row_parts(x_ref.shape[0]):
        y = _dot(mg_ref[rows, :], wo_ref[0:half, :]) + _dot(mr_ref[rows, :], wo_ref[half:2 * half, :])
        _residual_out(x_ref, y, g_ref, mod_ref, outs, d, rows)


def _outproj(x3, mod3, g_post, mg, mr, w_out, tm, nxt=None):
    gn, t, d = x3.shape
    half = mg.shape[2]
    n_in, n_out, n_shape, n_ops = _next_specs(None if nxt is None else (*nxt, x3.shape), tm, d)
    res = pl.pallas_call(
        functools.partial(_outproj_kernel, d=d),
        grid=(gn, t // tm),
        in_specs=[
            pl.BlockSpec((None, tm, d), lambda g, i: (g, i, 0)),
            _mod_spec(mod3, tm, 3 * d, 2),
            pl.BlockSpec((1, d), lambda g, i: (0, 0)),
            pl.BlockSpec((None, tm, half), lambda g, i: (g, i, 0)),
            pl.BlockSpec((None, tm, half), lambda g, i: (g, i, 0)),
            pl.BlockSpec((2 * half, d), lambda g, i: (0, 0)),
        ] + n_in,
        out_specs=[pl.BlockSpec((None, tm, d), lambda g, i: (g, i, 0))] + n_out,
        out_shape=[jax.ShapeDtypeStruct((gn, t, d), F32)] + n_shape,
        compiler_params=_cparams(("parallel", "parallel"), 56),
        name="outproj",
    )(x3, _arr(mod3),g_post, mg, mr, w_out, *n_ops)
    return res if nxt is not None else (res[0], None)


def _mlp_kernel(x_ref, mod_ref, gpre_ref, gpost_ref, wup_ref, wdn_ref, o_ref, h_scr, acc_scr, *, d):
    j = pl.program_id(2)

    @pl.when(j == 0)
    def _():
        h_scr[...] = _norm_mod(x_ref[...], gpre_ref[...], mod_ref, d).astype(BF16)
        acc_scr[...] = jnp.zeros_like(acc_scr)

    u = jnp.maximum(_dot(h_scr[...], wup_ref[...]), 0.0)
    acc_scr[...] += _dot((u * u).astype(BF16), wdn_ref[...])

    @pl.when(j == pl.num_programs(2) - 1)
    def _():
        o_ref[...] = _gated_residual(x_ref[...], acc_scr[...], gpost_ref[...], mod_ref, d)


def _mlp_cast_kernel(x_ref, mod_ref, gpre_ref, gpost_ref, wup_ref, wdn_ref,
                     o_ref, wupb_ref, wdnb_ref, h_scr, acc_scr, *, d):
    wupb_ref[...] = wup_ref[...].astype(BF16)
    wdnb_ref[...] = wdn_ref[...].astype(BF16)
    _mlp_kernel(x_ref, mod_ref, gpre_ref, gpost_ref, wupb_ref, wdnb_ref, o_ref, h_scr, acc_scr, d=d)


def _mlp_cast(x3, mod3, g_pre, g_post, w_up_all, w_down_all, layer, tf=512):
    gn, t, d = x3.shape
    assert gn == 1
    f = w_up_all.shape[2]
    return pl.pallas_call(
        functools.partial(_mlp_cast_kernel, d=d),
        grid=(1, 1, f // tf),
        in_specs=[
            pl.BlockSpec((None, t, d), lambda g, i, j: (0, 0, 0)),
            _mod_spec(mod3, t, 3 * d, 3),
            pl.BlockSpec((1, d), lambda g, i, j: (0, 0)),
            pl.BlockSpec((1, d), lambda g, i, j: (0, 0)),
            pl.BlockSpec((None, d, tf), lambda g, i, j: (layer, 0, j)),
            pl.BlockSpec((None, tf, d), lambda g, i, j: (layer, j, 0)),
        ],
        out_specs=[
            pl.BlockSpec((None, t, d), lambda g, i, j: (0, 0, 0)),
            pl.BlockSpec((d, tf), lambda g, i, j: (0, j)),
            pl.BlockSpec((tf, d), lambda g, i, j: (j, 0)),
        ],
        out_shape=[
            jax.ShapeDtypeStruct((1, t, d), F32),
            jax.ShapeDtypeStruct((d, f), BF16),
            jax.ShapeDtypeStruct((f, d), BF16),
        ],
        scratch_shapes=[pltpu.VMEM((t, d), BF16), pltpu.VMEM((t, d), F32)],
        compiler_params=_cparams(("arbitrary", "arbitrary", "arbitrary"), 56),
        name="mlp_cast",
    )(x3, _arr(mod3),g_pre, g_post, w_up_all, w_down_all)


def _mlp(x3, mod3, g_pre, g_post, w_up, w_down, tf=TF_MLP):
    gn, t, d = x3.shape
    assert gn == 1
    f = w_up.shape[1]
    return pl.pallas_call(
        functools.partial(_mlp_kernel, d=d),
        grid=(1, 1, f // tf),
        in_specs=[
            pl.BlockSpec((None, t, d), lambda g, i, j: (0, 0, 0)),
            _mod_spec(mod3, t, 3 * d, 3),
            pl.BlockSpec((1, d), lambda g, i, j: (0, 0)),
            pl.BlockSpec((1, d), lambda g, i, j: (0, 0)),
            pl.BlockSpec((d, tf), lambda g, i, j: (0, j)),
            pl.BlockSpec((tf, d), lambda g, i, j: (j, 0)),
        ],
        out_specs=pl.BlockSpec((None, t, d), lambda g, i, j: (0, 0, 0)),
        out_shape=jax.ShapeDtypeStruct((1, t, d), F32),
        scratch_shapes=[pltpu.VMEM((t, d), BF16), pltpu.VMEM((t, d), F32)],
        compiler_params=_cparams(("arbitrary", "arbitrary", "arbitrary"), 56),
        name="mlp_rows",
    )(x3, _arr(mod3),g_pre, g_post, w_up, w_down)


def _mlp_h_kernel(x_ref, h_ref, mod_ref, gpost_ref, wup_ref, wdn_ref, o_ref, acc_scr, *, d):
    j = pl.program_id(2)
    last = pl.num_programs(2) - 1

    def hidden(rows):
        u = jnp.maximum(_dot(h_ref[rows, :], wup_ref[...]), 0.0)
        return _dot((u * u).astype(BF16), wdn_ref[...])

    @pl.when(j == 0)
    def _():
        acc_scr[...] = hidden(slice(None))

    @pl.when((j > 0) & (j < last))
    def _():
        acc_scr[...] += hidden(slice(None))

    @pl.when(j == last)
    def _():
        for rows in _row_parts(x_ref.shape[0]):
            y = acc_scr[rows, :] + hidden(rows)
            o_ref[rows, :] = _gated_residual(x_ref[rows, :], y, gpost_ref[...], mod_ref, d)


def _mlp_h(x3, h3, mod3, g_post, w_up, w_down, tm, tf=TF_MLP):
    gn, t, d = x3.shape
    f = w_up.shape[1]
    assert f // tf >= 2
    return pl.pallas_call(
        functools.partial(_mlp_h_kernel, d=d),
        grid=(gn, t // tm, f // tf),
        in_specs=[
            pl.BlockSpec((None, tm, d), lambda g, i, j: (g, i, 0)),
            pl.BlockSpec((None, tm, d), lambda g, i, j: (g, i, 0)),
            _mod_spec(mod3, tm, 3 * d, 3),
            pl.BlockSpec((1, d), lambda g, i, j: (0, 0)),
            pl.BlockSpec((d, tf), lambda g, i, j: (0, j)),
            pl.BlockSpec((tf, d), lambda g, i, j: (j, 0)),
        ],
        out_specs=pl.BlockSpec((None, tm, d), lambda g, i, j: (g, i, 0)),
        out_shape=jax.ShapeDtypeStruct((gn, t, d), F32),
        scratch_shapes=[pltpu.VMEM((tm, d), F32)],
        compiler_params=_cparams(("parallel", "parallel", "arbitrary"), 56),
        name="mlp",
    )(x3, h3, _arr(mod3), g_post, w_up, w_down)


def _diag_blocks(rows, reps, row_shift, col_shift):
    tiled = jnp.concatenate([rows] * reps, axis=1)
    rg = lax.broadcasted_iota(jnp.int32, tiled.shape, 0) >> row_shift
    cg = lax.broadcasted_iota(jnp.int32, tiled.shape, 1) >> col_shift
    return jnp.where(rg == cg, tiled, 0.0)


def _s5_disc_kernel(lr_ref, li_ref, ldt_ref, br_ref, bi_ref, cr_ref, ci_ref,
                    pwr_ref, pwi_ref, bblk_ref, cblk_ref, *, seg_len):
    lr, li = lr_ref[...], li_ref[...]
    dt = jnp.exp(ldt_ref[...])
    mag = jnp.exp(lr * dt)
    lb_re, lb_im = mag * jnp.cos(li * dt), mag * jnp.sin(li * dt)
    nr, ni = lb_re - 1.0, lb_im
    den = lr * lr + li * li
    f_re = (nr * lr + ni * li) / den
    f_im = (ni * lr - nr * li) / den
    br, bi = br_ref[...], bi_ref[...]
    bb_re = f_re * br - f_im * bi
    bb_im = f_re * bi + f_im * br
    nrow, ncol2 = bblk_ref.shape
    rs, cs = S5_GROUP.bit_length() - 1, S5_STATE.bit_length() - 1
    blocks = lambda a: _diag_blocks(a.reshape(nrow, LANES), ncol2 // 2 // LANES, rs, cs)
    bblk_ref[...] = jnp.concatenate([blocks(bb_re), blocks(bb_im)], axis=1).astype(BF16)
    c_t = jnp.concatenate([blocks(cr_ref[...]), -blocks(ci_ref[...])], axis=1)
    cblk_ref[...] = jnp.transpose(c_t).astype(BF16)
    pwr_ref[0] = lb_re
    pwi_ref[0] = lb_im
    qr, qi = None, None
    sr, si = lb_re, lb_im
    e = seg_len
    while e:
        if e & 1:
            qr, qi = (sr, si) if qr is None else (qr * sr - qi * si, qr * si + qi * sr)
        e >>= 1
        if e:
            sr, si = sr * sr - si * si, 2.0 * sr * si
    pr, pi = qr, qi
    for n in range(SUBLANES):
        pwr_ref[1 + n] = pr
        pwi_ref[1 + n] = pi
        pr, pi = pr * qr - pi * qi, pr * qi + pi * qr


def _s5_discretize(lam_re, lam_im, log_dt, bt_re, bt_im, c_re, c_im, seg_len):
    g = lam_re.shape[0]
    ncb = g // S5_GPB
    assert S5_GROUP & (S5_GROUP - 1) == 0 and S5_STATE & (S5_STATE - 1) == 0
    grp = lambda *dims: pl.BlockSpec((S5_GPB,) + dims, lambda c: (c, 0, 0))
    pw_spec = pl.BlockSpec((1 + SUBLANES, S5_GPB, 1, LANES), lambda c: (0, c, 0, 0))
    return pl.pallas_call(
        functools.partial(_s5_disc_kernel, seg_len=seg_len),
        grid=(ncb,),
        in_specs=[grp(1, LANES), grp(1, LANES), grp(1, 1), grp(S5_GROUP, LANES), grp(S5_GROUP, LANES),
                  grp(S5_GROUP, LANES), grp(S5_GROUP, LANES)],
        out_specs=[pw_spec, pw_spec,
                   pl.BlockSpec((None, S5_UW, 2 * S5_GPB * S5_STATE), lambda c: (c, 0, 0)),
                   pl.BlockSpec((None, 2 * S5_GPB * S5_STATE, S5_UW), lambda c: (c, 0, 0))],
        out_shape=[
            jax.ShapeDtypeStruct((1 + SUBLANES, g, 1, LANES), F32),
            jax.ShapeDtypeStruct((1 + SUBLANES, g, 1, LANES), F32),
            jax.ShapeDtypeStruct((ncb, S5_UW, 2 * S5_GPB * S5_STATE), BF16),
            jax.ShapeDtypeStruct((ncb, 2 * S5_GPB * S5_STATE, S5_UW), BF16),
        ],
        compiler_params=_cparams(("arbitrary",), 32),
        name="s5_discretize",
    )(lam_re, lam_im, log_dt, bt_re, bt_im, c_re, c_im)


def _gelu_tanh(x):
    c0 = math.sqrt(2.0 / math.pi)
    return x * (0.5 * (1.0 + jnp.tanh(c0 * (x + 0.044715 * (x * x * x)))))


def _cmul_add(ar, ai, xr, xi, yr, yi):
    return yr + ar * xr - ai * xi, yi + ar * xi + ai * xr


def _s5_seq_kernel(x_ref, mod_ref, gpre_ref, bblk_ref, cblk_ref, lam_ref, dskip_ref, s0r_ref, s0i_ref,
                   z_ref, sr_ref, si_ref, h_scr, xr_scr, xi_scr, *, d):
    cb = pl.program_id(0)
    ncb, tm, uw = h_scr.shape
    cw = xr_scr.shape[1]
    seg = s0r_ref.shape[0]

    @pl.when(cb == 0)
    def _():
        h = _norm_mod(x_ref[...], gpre_ref[...], mod_ref, d)
        for c in range(ncb):
            h_scr[c] = h[:, c * uw:(c + 1) * uw]

    u = h_scr[cb]
    bu = _dot(u.astype(BF16), bblk_ref[...])
    xr_scr[...] = bu[:, 0:cw]
    xi_scr[...] = bu[:, cw:2 * cw]
    car_r, car_i = s0r_ref[...], s0i_ref[...]
    l_r, l_i = lam_ref[0:1, :], lam_ref[1:2, :]
    for t in range(tm // seg):
        rows = slice(t * seg, (t + 1) * seg)
        car_r, car_i = _cmul_add(l_r, l_i, car_r, car_i, xr_scr[rows, :], xi_scr[rows, :])
        xr_scr[rows, :] = car_r
        xi_scr[rows, :] = car_i
    sr_ref[...] = car_r
    si_ref[...] = car_i
    xs = jnp.concatenate([xr_scr[...].astype(BF16), xi_scr[...].astype(BF16)], axis=1)
    y = _dot(xs, cblk_ref[...]) + dskip_ref[...] * u
    z_ref[...] = _gelu_tanh(y).astype(BF16)


def _s5_seq(x2, mod3, g_pre, bblk, cblk, lam2, dskip, s0_re, s0_im):
    tm, d = x2.shape
    seg, nst = s0_re.shape
    ncb, uw, cw2 = bblk.shape
    cw = cw2 // 2
    sspec = pl.BlockSpec((seg, cw), lambda c: (0, c))
    return pl.pallas_call(
        functools.partial(_s5_seq_kernel, d=d),
        grid=(ncb,),
        in_specs=[
            pl.BlockSpec((tm, d), lambda c: (0, 0)),
            pl.BlockSpec((None, seg, 3 * d), lambda c, lead=getattr(mod3, "lead", 0): (lead, 0, 0)),
            pl.BlockSpec((1, d), lambda c: (0, 0)),
            pl.BlockSpec((None, uw, cw2), lambda c: (c, 0, 0)),
            pl.BlockSpec((None, cw2, uw), lambda c: (c, 0, 0)),
            pl.BlockSpec((2, cw), lambda c: (0, c)),
            pl.BlockSpec((1, uw), lambda c: (0, c)),
            sspec, sspec,
        ],
        out_specs=[pl.BlockSpec((tm, uw), lambda c: (0, c)), sspec, sspec],
        out_shape=[
            jax.ShapeDtypeStruct((tm, d), BF16),
            jax.ShapeDtypeStruct((seg, nst), F32),
            jax.ShapeDtypeStruct((seg, nst), F32),
        ],
        scratch_shapes=[
            pltpu.VMEM((ncb, tm, uw), F32),
            pltpu.VMEM((tm, cw), F32),
            pltpu.VMEM((tm, cw), F32),
        ],
        compiler_params=_cparams(("arbitrary",), 48),
        name="s5_seq",
    )(x2, _arr(mod3), g_pre, bblk, cblk, lam2, dskip, s0_re, s0_im)


def _s5_rows_kernel(*refs, d, ncast):
    x_ref, mod_ref, gpre_ref, bblk_ref, cblk_ref, tbl_ref, dskip_ref = refs[:7]
    cast_in = refs[7:7 + ncast]
    z_ref, sr_ref, si_ref = refs[7 + ncast:10 + ncast]
    cast_out = refs[10 + ncast:10 + 2 * ncast]
    h_scr, xr_scr, xi_scr, cr_scr, ci_scr = refs[10 + 2 * ncast:]
    for src, dst in zip(cast_in, cast_out):
        dst[...] = src[...].astype(BF16)

    ncol, tm, _ = xr_scr.shape
    ncb, uw, _ = bblk_ref.shape
    sl = tm // SUBLANES

    @pl.when(pl.program_id(1) == 0)
    def _():
        cr_scr[...] = jnp.zeros_like(cr_scr)
        ci_scr[...] = jnp.zeros_like(ci_scr)

    h_scr[...] = _norm_mod(x_ref[...], gpre_ref[...], mod_ref, d)
    row0 = lax.broadcasted_iota(jnp.int32, (ncol, SUBLANES, LANES), 1) == 0

    for c in range(ncb):
        us = slice(c * uw, (c + 1) * uw)
        cols = slice(c * ncol, (c + 1) * ncol)
        u = h_scr[:, us]
        bu = _dot(u.astype(BF16), bblk_ref[c])
        for j in range(ncol):
            xr_scr[j] = bu[:, j * LANES:(j + 1) * LANES]
            xi_scr[j] = bu[:, (ncol + j) * LANES:(ncol + j + 1) * LANES]
        l_r, l_i = tbl_ref[0, cols], tbl_ref[1, cols]

        def local(i, s):
            rows = pl.ds(pl.multiple_of(i * SUBLANES, SUBLANES), SUBLANES)
            return _cmul_add(l_r, l_i, s[0], s[1], xr_scr[:, rows, :], xi_scr[:, rows, :])

        zero = jnp.zeros((ncol, SUBLANES, LANES), F32)
        g_r, g_i = lax.fori_loop(0, sl, local, (zero, zero), unroll=True)
        for n in range(3):
            g_r, g_i = _cmul_add(tbl_ref[2 + 2 * n, cols], tbl_ref[3 + 2 * n, cols],
                                 pltpu.roll(g_r, 1 << n, 1), pltpu.roll(g_i, 1 << n, 1), g_r, g_i)
        car_r, car_i = cr_scr[cols], ci_scr[cols]
        g_r, g_i = _cmul_add(tbl_ref[8, cols], tbl_ref[9, cols], car_r, car_i, g_r, g_i)
        in_r = jnp.where(row0, car_r, pltpu.roll(g_r, 1, 1))
        in_i = jnp.where(row0, car_i, pltpu.roll(g_i, 1, 1))

        def full(i, s):
            rows = pl.ds(pl.multiple_of(i * SUBLANES, SUBLANES), SUBLANES)
            s_r, s_i = _cmul_add(l_r, l_i, s[0], s[1], xr_scr[:, rows, :], xi_scr[:, rows, :])
            xr_scr[:, rows, :] = s_r
            xi_scr[:, rows, :] = s_i
            return s_r, s_i

        e_r, e_i = lax.fori_loop(0, sl, full, (in_r, in_i), unroll=True)
        cr_scr[cols] = e_r[:, SUBLANES - 1:SUBLANES, :]
        ci_scr[cols] = e_i[:, SUBLANES - 1:SUBLANES, :]
        xs = jnp.concatenate([xr_scr[j].astype(BF16) for j in range(ncol)]
                             + [xi_scr[j].astype(BF16) for j in range(ncol)], axis=1)
        y = _dot(xs, cblk_ref[c]) + dskip_ref[:, us] * u
        z_ref[:, us] = _gelu_tanh(y).astype(BF16)

    sr_ref[...] = cr_scr[...]
    si_ref[...] = ci_scr[...]


def _s5_rows(x3, mod3, g_pre, bblk, cblk, tbl, dskip, tm, casts=()):
    gn, t, d = x3.shape
    ncb, uw, cw2 = bblk.shape
    ncol = cw2 // 2 // LANES
    nct = ncb * ncol
    nt = t // tm
    nstep = gn * nt
    const = lambda shape: pl.BlockSpec(shape, lambda g, i: (0,) * len(shape), pipeline_mode=pl.Buffered(1))
    ospec = pl.BlockSpec((None, None, nct, 1, LANES), lambda g, i: (g, i, 0, 0, 0))
    c_in, c_out, c_shape = [], [], []
    for w, layer in casts:
        _, rows, cols = w.shape
        slab = rows // nstep
        assert slab * nstep == rows and slab % (2 * SUBLANES) == 0
        c_in.append(pl.BlockSpec((None, slab, cols), lambda g, i, layer=layer: (layer, g * nt + i, 0)))
        c_out.append(pl.BlockSpec((slab, cols), lambda g, i: (g * nt + i, 0)))
        c_shape.append(jax.ShapeDtypeStruct((rows, cols), BF16))
    res = pl.pallas_call(
        functools.partial(_s5_rows_kernel, d=d, ncast=len(casts)),
        grid=(gn, nt),
        in_specs=[
            pl.BlockSpec((None, tm, d), lambda g, i: (g, i, 0)),
            _mod_spec(mod3, tm, 3 * d, 2),
            pl.BlockSpec((1, d), lambda g, i: (0, 0)),
            const(bblk.shape), const(cblk.shape), const(tbl.shape), const(dskip.shape),
        ] + c_in,
        out_specs=[pl.BlockSpec((None, tm, d), lambda g, i: (g, i, 0)), ospec, ospec] + c_out,
        out_shape=[
            jax.ShapeDtypeStruct((gn, t, d), BF16),
            jax.ShapeDtypeStruct((gn, nt, nct, 1, LANES), F32),
            jax.ShapeDtypeStruct((gn, nt, nct, 1, LANES), F32),
        ] + c_shape,
        scratch_shapes=[
            pltpu.VMEM((tm, d), F32),
            pltpu.VMEM((ncol, tm, LANES), F32),
            pltpu.VMEM((ncol, tm, LANES), F32),
            pltpu.VMEM((nct, 1, LANES), F32),
            pltpu.VMEM((nct, 1, LANES), F32),
        ],
        compiler_params=_cparams(("arbitrary", "arbitrary"), 56),
        name="s5_rows",
    )(x3, _arr(mod3),g_pre, bblk, cblk, tbl, dskip, *[w for w, _ in casts])
    z3, s_re, s_im = res[:3]
    last = lambda s: s[:, nt - 1].reshape(gn, 1, nct * LANES)
    return z3, last(s_re), last(s_im), list(res[3:])


def _glu_kernel(x_ref, mod_ref, g_ref, z_ref, wa_ref, wb_ref, *outs, d):
    for rows in _row_parts(x_ref.shape[0]):
        z = z_ref[rows, :]
        y = _dot(z, wa_ref[...]) * jax.nn.sigmoid(_dot(z, wb_ref[...]))
        _residual_out(x_ref, y, g_ref, mod_ref, outs, d, rows)


def _glu(x3, mod3, g_post, z3, w_a, w_b, tm, nxt=None):
    gn, t, d = x3.shape
    wspec = pl.BlockSpec((d, d), lambda g, i: (0, 0), pipeline_mode=pl.Buffered(1))
    n_in, n_out, n_shape, n_ops = _next_specs(None if nxt is None else (*nxt, x3.shape), tm, d)
    res = pl.pallas_call(
        functools.partial(_glu_kernel, d=d),
        grid=(gn, t // tm),
        in_specs=[
            pl.BlockSpec((None, tm, d), lambda g, i: (g, i, 0)),
            _mod_spec(mod3, tm, 3 * d, 2),
            pl.BlockSpec((1, d), lambda g, i: (0, 0)),
            pl.BlockSpec((None, tm, d), lambda g, i: (g, i, 0)),
            wspec, wspec,
        ] + n_in,
        out_specs=[pl.BlockSpec((None, tm, d), lambda g, i: (g, i, 0))] + n_out,
        out_shape=[jax.ShapeDtypeStruct((gn, t, d), F32)] + n_shape,
        compiler_params=_cparams(("parallel", "parallel"), 56),
        name="glu",
    )(x3, _arr(mod3),g_post, z3, w_a, w_b, *n_ops)
    return res if nxt is not None else (res[0], None)


def _rope_tables(pos):
    half = HEAD_DK // 2
    inv = ROPE_BASE ** (-jnp.arange(half, dtype=F32) / half)
    ang = pos.astype(F32)[:, None] * inv[None, :]
    cos, sin = jnp.cos(ang), jnp.sin(ang)
    return jnp.concatenate([cos, cos], axis=-1), jnp.concatenate([-sin, sin], axis=-1)


def _s5_tables(pw_re, pw_im):
    n = pw_re.shape[0]
    flat_r = pw_re.reshape(n, -1)
    flat_i = pw_im.reshape(n, -1)
    row = jnp.arange(SUBLANES)[:, None]
    tabs = [jnp.broadcast_to(flat_r[0], (SUBLANES, flat_r.shape[1])),
            jnp.broadcast_to(flat_i[0], (SUBLANES, flat_i.shape[1]))]
    for s in (1, 2, 4):
        mask = row >= s
        tabs.append(jnp.where(mask, flat_r[s][None, :], 0.0))
        tabs.append(jnp.where(mask, flat_i[s][None, :], 0.0))
    tabs += [flat_r[1:], flat_i[1:]]
    tbl = jnp.stack(tabs)
    tbl = tbl.reshape(tbl.shape[0], SUBLANES, -1, LANES).transpose(0, 2, 1, 3)
    return tbl, jnp.stack([flat_r[0], flat_i[0]])


def kernel(x_prompt, x_sample, state_gla, state_ret, state_s5_re, state_s5_im, c_prompt, c_sample,
           w_ada, b_ada, norm_pre, norm_post, w_in_mix, w_gla_gk, b_gla_gk, gla_head_norm,
           ret_head_norm, w_out_mix, s5_lam_re, s5_lam_im, s5_log_dt, s5_b_re, s5_b_im,
           s5_c_re, s5_c_im, s5_d, w_glu_a, w_glu_b, w_mlp_up, w_mlp_down):
    bp, tp, d = x_prompt.shape
    bs, ts, _ = x_sample.shape
    depth = w_ada.shape[0]

    nrow = -(-(bs + bp) // SUBLANES) * SUBLANES
    c_all = jnp.concatenate([c_sample, c_prompt, jnp.zeros((nrow - bs - bp, d), F32)], axis=0)
    mod_all = _adaln(c_all, w_ada.reshape(depth * 2, d, 3 * d), b_ada.reshape(depth * 2, 1, 3 * d))
    mod_s = [_ModSlab(mod_all, k, bs) for k in range(depth * 2)]
    mod_p = [mod_all[k, bs:bs + bp][:, None, :] for k in range(depth * 2)]

    w_in_t = jnp.swapaxes(w_in_mix, 1, 2)
    w_gk = jnp.pad(w_gla_gk[0], ((0, LANES - GLA_RANK), (0, 0))).astype(BF16)
    b_gk = b_gla_gk[0][None, :]
    w_out = w_out_mix[0].astype(BF16)
    gla_norm = gla_head_norm[0][:, None, :]
    ret_norm = ret_head_norm[0][:, None, :]
    gamma_log = jnp.log1p(-jnp.power(2.0, -5.0 - jnp.arange(RET_HEADS, dtype=F32)))
    ret_lg = jnp.broadcast_to(gamma_log[:, None, None], (RET_HEADS, 1, LANES))

    ng = s5_lam_re.shape[1]
    tm5 = min(TM_S5, tp)
    per_state = lambda a: jnp.tile(a, (1,) * (a.ndim - 1) + (LANES // S5_STATE,))
    bt = lambda a: per_state(jnp.swapaxes(a[0], 1, 2))
    pw_re, pw_im, bblk, cblk = _s5_discretize(
        per_state(s5_lam_re[0])[:, None, :], per_state(s5_lam_im[0])[:, None, :],
        s5_log_dt[0][:, None, None], bt(s5_b_re), bt(s5_b_im), per_state(s5_c_re[0]), per_state(s5_c_im[0]),
        tm5 // SUBLANES)
    tbl, lam2 = _s5_tables(pw_re[:, :, 0, :S5_STATE], pw_im[:, :, 0, :S5_STATE])
    dskip = s5_d[0][None, :]

    nxt = lambda mods, l, emit_h: (mods[2 * l + 1], norm_pre[l, 1][None]) if emit_h else None

    def layer0(x3, mods, tm, inproj, attn, mlp, emit_h):
        proj, glog = inproj(x3, mods[0])
        mg, mr, s_gla, s_ret = attn(proj, glog)
        x3, h3 = _outproj(x3, mods[0], norm_post[0, 0][None], mg, mr, w_out, tm, nxt(mods, 0, emit_h))
        return mlp(0, x3, h3, mods[1]), s_gla, s_ret

    def layer1_tail(x3, z3, mods, tm, glu_w, mlp, emit_h):
        x3, h3 = _glu(x3, mods[2], norm_post[1, 0][None], z3, glu_w[0], glu_w[1], tm, nxt(mods, 1, emit_h))
        return mlp(1, x3, h3, mods[3])

    cos_s, sin_s = _rope_tables(PAST_LEN + jnp.arange(ts, dtype=F32))

    def attn_s(proj, glog):
        tm_rows = lambda a: a.reshape(ts, bs, a.shape[-1])
        mg, mr, s_gla, s_ret = _attn_sample(tm_rows(proj), tm_rows(glog), gla_norm, ret_norm, cos_s, sin_s,
                                            ret_lg, state_gla[0], state_ret[0])
        flat = lambda a: a.reshape(1, ts * bs, a.shape[-1])
        return flat(mg), flat(mr), s_gla, s_ret

    xs3 = jnp.swapaxes(x_sample, 0, 1).reshape(1, ts * bs, d)

    def s5_s(x3, mod3):
        z2, s_re, s_im = _s5_seq(x3[0], mod3, norm_pre[1, 0][None], bblk, cblk, lam2, dskip,
                                 state_s5_re[0].reshape(bs, -1), state_s5_im[0].reshape(bs, -1))
        return z2[None], s_re, s_im

    w_up, w_dn, w_main = {}, {}, {}

    def inproj_s(x3, mod3):
        proj, glog, w_main[0], w_main["lr"] = _inproj_cast(x3, mod3, norm_pre[0, 0][None], w_in_t, 0,
                                                            w_gk, b_gk)
        return proj, glog

    def mlp_s(l, x3, h3, mod3):
        g_pre, g_post = norm_pre[l, 1][None], norm_post[l, 1][None]
        if l in w_up:
            return _mlp(x3, mod3, g_pre, g_post, w_up[l], w_dn[l])
        x3, w_up[l], w_dn[l] = _mlp_cast(x3, mod3, g_pre, g_post, w_mlp_up, w_mlp_down, l)
        return x3

    tm_s = ts * bs
    xs1, gla_s, ret_s = layer0(xs3, mod_s, tm_s, inproj_s, attn_s, mlp_s, False)

    cos_p, sin_p = _rope_tables(jnp.arange(tp, dtype=F32))
    zeros_att = jnp.zeros((bp, GLA_HEADS, HEAD_DK, HEAD_DV), F32)
    tm_p = min(TM_DENSE, tp)

    def attn_p(proj, glog):
        return _attn_prompt(proj, glog, gla_norm, ret_norm, cos_p, sin_p, ret_lg, zeros_att, zeros_att)

    def mlp_p(l, x3, h3, mod3):
        return _mlp_h(x3, h3, mod3, norm_post[l, 1][None], w_up[l], w_dn[l], tm_p)

    def inproj_p(x3, mod3):
        return _inproj(x3, mod3, norm_pre[0, 0][None], w_main[0], w_main["lr"], w_gk, b_gk, tm_p)

    xp1, gla_p, ret_p = layer0(x_prompt, mod_p, tm_p, inproj_p, attn_p, mlp_p, True)

    sl = tm5 // SUBLANES
    xpp = jnp.swapaxes(xp1.reshape(bp, tp // tm5, SUBLANES, sl, d), 2, 3).reshape(bp, tp, d)
    zpp, re_p, im_p, (w_up[1], w_dn[1], w_ga, w_gb) = _s5_rows(
        xpp, mod_p[2], norm_pre[1, 0][None], bblk, cblk, tbl, dskip, tm5,
        casts=((w_mlp_up, 1), (w_mlp_down, 1), (w_glu_a, 0), (w_glu_b, 0)))
    zp3 = jnp.swapaxes(zpp.reshape(bp, tp // tm5, sl, SUBLANES, d), 2, 3).reshape(bp, tp, d)

    zs3, re_s, im_s = s5_s(xs1, mod_s[2])
    y_s = layer1_tail(xs1, zs3, mod_s, tm_s, (w_ga, w_gb), mlp_s, False)
    y_s = jnp.swapaxes(y_s.reshape(ts, bs, d), 0, 1)
    y_p = layer1_tail(xp1, zp3, mod_p, tm_p, (w_ga, w_gb), mlp_p, True)

    st = lambda a, b_: a.reshape(1, b_, ng, S5_STATE)
    return (y_p, y_s, gla_p[None], gla_s[None], ret_p[None], ret_s[None],
            st(re_p, bp), st(re_s, bs), st(im_p, bp), st(im_s, bs))
```
